```python
import math
import jax, jax.numpy as jnp
from jax import lax
import numpy as np

D_MODEL = 1024
BATCH = 8
SEQ = 2048
DEPTH = 2
DEC_BATCH = 32
DEC_SEQ = 64
PAST_LEN = 4096

CHUNK = 64
EPS = 1e-6
GLA_HEADS = 4
GLA_KEY_DIM = D_MODEL // 2
GLA_VALUE_DIM = D_MODEL
GLA_DK = GLA_KEY_DIM // GLA_HEADS
GLA_DV = GLA_VALUE_DIM // GLA_HEADS
GLA_GATE_RANK = 16
GLA_GATE_NORM = 16.0
SSD_INNER = 2 * D_MODEL
SSD_HEADDIM = 64
SSD_HEADS = SSD_INNER // SSD_HEADDIM
SSD_GROUPS = 4
SSD_DSTATE = 128
CONV_WIDTH = 4
SSD_CONV_DIM = SSD_INNER + 2 * SSD_GROUPS * SSD_DSTATE
N_BRANCH = 2
IN_SPLITS = (GLA_KEY_DIM, GLA_KEY_DIM, GLA_VALUE_DIM, GLA_VALUE_DIM, GLA_GATE_RANK,
             SSD_INNER, SSD_CONV_DIM, SSD_HEADS, N_BRANCH * D_MODEL)
D_IN_PROJ = sum(IN_SPLITS)
PEER_HEADS = 8
PEER_NKEYS = 128
PEER_EXPERTS = PEER_NKEYS * PEER_NKEYS
PEER_DQ = 256
PEER_TOPK = 16
PEER_TOKEN_BLOCK = 128

kernel_name = "hybrid_gla_ssd_peer_streaming_step"


def rmsnorm(x, w):
    xf = x.astype(jnp.float32)
    y = xf * lax.rsqrt(jnp.mean(xf * xf, axis=-1, keepdims=True) + EPS)
    return (y * w.astype(jnp.float32)).astype(x.dtype)


def chunk_len(T):
    return CHUNK if T >= CHUNK else T


def gla_chunked(q, k, v, gk, s0):
    Bsz, T, H, DK = q.shape
    DV = v.shape[-1]
    L = chunk_len(T)
    NC = T // L
    q = q.reshape(Bsz, NC, L, H, DK)
    k = k.reshape(Bsz, NC, L, H, DK)
    v = v.reshape(Bsz, NC, L, H, DV)
    b = jnp.cumsum(gk.reshape(Bsz, NC, L, H, DK), axis=2)
    b_last = b[:, :, -1]
    q_t = q * jnp.exp(b)
    k_t = k * jnp.exp(-b)
    causal = jnp.tril(jnp.ones((L, L), dtype=bool))
    att = jnp.where(causal, jnp.einsum('bnihd,bnjhd->bnhij', q_t, k_t), 0.0)
    o_intra = jnp.einsum('bnhij,bnjhe->bnihe', att, v)
    k_end = k * jnp.exp(b_last[:, :, None] - b)
    d_s = jnp.einsum('bnjhd,bnjhe->bnhde', k_end, v)

    def step(s, inp):
        dec, ds = inp
        return s * dec[..., None] + ds, s

    s_final, s_starts = lax.scan(step, s0, (jnp.moveaxis(jnp.exp(b_last), 1, 0), jnp.moveaxis(d_s, 1, 0)))
    s_starts = jnp.moveaxis(s_starts, 0, 1)
    o_inter = jnp.einsum('bnihd,bnhde->bnihe', q_t, s_starts)
    return (o_intra + o_inter).reshape(Bsz, T, H, DV), s_final


def ssd_chunked(x, dt, A, bm, cm, s0):
    Bsz, T, H, P = x.shape
    G, N = bm.shape[-2], bm.shape[-1]
    HG = H // G
    L = chunk_len(T)
    NC = T // L
    x = x.reshape(Bsz, NC, L, G, HG, P)
    dt = dt.reshape(Bsz, NC, L, G, HG)
    bm = bm.reshape(Bsz, NC, L, G, N)
    cm = cm.reshape(Bsz, NC, L, G, N)
    a = jnp.cumsum(dt * A.reshape(G, HG), axis=2)
    a_last = a[:, :, -1]
    a_t = jnp.moveaxis(a, 2, -1)
    causal = jnp.tril(jnp.ones((L, L), dtype=bool))
    decay = jnp.exp(jnp.where(causal, a_t[..., :, None] - a_t[..., None, :], -jnp.inf))
    cb = jnp.einsum('bnigs,bnjgs->bngij', cm, bm)
    w_intra = cb[:, :, :, None] * decay * jnp.moveaxis(dt, 2, -1)[..., None, :]
    y_intra = jnp.einsum('bnghij,bnjghp->bnighp', w_intra, x)
    xw = x * (dt * jnp.exp(a_last[:, :, None] - a))[..., None]
    d_s = jnp.einsum('bnjghp,bnjgs->bnghps', xw, bm)

    def step(s, inp):
        dec, ds = inp
        return s * dec[..., None, None] + ds, s

    s_final, s_starts = lax.scan(step, s0.reshape(Bsz, G, HG, P, N),
                                 (jnp.moveaxis(jnp.exp(a_last), 1, 0), jnp.moveaxis(d_s, 1, 0)))
    s_starts = jnp.moveaxis(s_starts, 0, 1)
    y_inter = jnp.einsum('bnigs,bnghps->bnighp', cm, s_starts) * jnp.exp(a)[..., None]
    y = (y_intra + y_inter).reshape(Bsz, T, H, P)
    return y, s_final.reshape(Bsz, H, P, N)


def token_mixer(h, conv_prev, gla_s0, ssd_s0, w_in, gla_gk_w2, gla_gk_b, gla_norm_w, gla_proj,
                ssd_conv_w, ssd_conv_b, ssd_dt_bias, ssd_A_log, ssd_D, ssd_norm_w, ssd_proj, w_out):
    f32 = jnp.float32
    Bsz, T, _ = h.shape
    offs = np.cumsum(IN_SPLITS)[:-1].tolist()
    q, k, v, g_out, gk_low, z, xbc, dt_raw, gate_logits = jnp.split(h @ w_in, offs, axis=-1)

    qh = q.reshape(Bsz, T, GLA_HEADS, GLA_DK).astype(f32) * (GLA_DK ** -0.5)
    kh = k.reshape(Bsz, T, GLA_HEADS, GLA_DK).astype(f32)
    vh = v.reshape(Bsz, T, GLA_HEADS, GLA_DV).astype(f32)
    gk = jax.nn.log_sigmoid((gk_low @ gla_gk_w2 + gla_gk_b).astype(f32)) / GLA_GATE_NORM
    o_a, gla_s = gla_chunked(qh, kh, vh, gk.reshape(Bsz, T, GLA_HEADS, GLA_DK), gla_s0.astype(f32))
    o_a = rmsnorm(o_a, gla_norm_w) * jax.nn.silu(g_out.reshape(Bsz, T, GLA_HEADS, GLA_DV).astype(f32))
    br_a = o_a.reshape(Bsz, T, GLA_VALUE_DIM).astype(h.dtype) @ gla_proj

    xbc_ext = jnp.concatenate([conv_prev.astype(xbc.dtype), xbc], axis=1)
    conv = ssd_conv_b + xbc_ext[:, 0:T] * ssd_conv_w[0]
    for i in range(1, CONV_WIDTH):
        conv = conv + xbc_ext[:, i:i + T] * ssd_conv_w[i]
    conv_new = xbc_ext[:, T:]
    xbc_act = jax.nn.silu(conv.astype(f32))
    xs, b_ssm, c_ssm = jnp.split(xbc_act, [SSD_INNER, SSD_INNER + SSD_GROUPS * SSD_DSTATE], axis=-1)
    xs = xs.reshape(Bsz, T, SSD_HEADS, SSD_HEADDIM)
    dt = jax.nn.softplus(dt_raw.astype(f32) + ssd_dt_bias.astype(f32))
    A = -jnp.exp(ssd_A_log.astype(f32))
    y, ssd_s = ssd_chunked(xs, dt, A,
                           b_ssm.reshape(Bsz, T, SSD_GROUPS, SSD_DSTATE),
                           c_ssm.reshape(Bsz, T, SSD_GROUPS, SSD_DSTATE),
                           ssd_s0.astype(f32))
    y = (y + ssd_D.astype(f32)[:, None] * xs).reshape(Bsz, T, SSD_INNER) * jax.nn.silu(z.astype(f32))
    y = rmsnorm(y.reshape(Bsz, T, SSD_GROUPS, SSD_INNER // SSD_GROUPS),
                ssd_norm_w.reshape(SSD_GROUPS, SSD_INNER // SSD_GROUPS)).reshape(Bsz, T, SSD_INNER)
    br_b = y.astype(h.dtype) @ ssd_proj

    g_a, g_b = jnp.split(jax.nn.sigmoid(gate_logits), N_BRANCH, axis=-1)
    return (g_a * br_a + g_b * br_b) @ w_out, conv_new, gla_s, ssd_s


def peer_ffn(h, wq, keys1, keys2, u_tab, v_tab):
    Bsz, T, D = h.shape
    xt = h.reshape(Bsz * T, D)
    n = Bsz * T
    npad = (-n) % PEER_TOKEN_BLOCK
    blocks = jnp.pad(xt, ((0, npad), (0, 0))).reshape(-1, PEER_TOKEN_BLOCK, D)
    half = PEER_DQ // 2

    def one_block(xb):
        q = (xb @ wq).reshape(PEER_TOKEN_BLOCK, PEER_HEADS, PEER_DQ).astype(jnp.float32)
        s1 = jnp.einsum('thd,hkd->thk', q[..., :half], keys1.astype(jnp.float32))
        s2 = jnp.einsum('thd,hkd->thk', q[..., half:], keys2.astype(jnp.float32))
        v1, i1 = lax.top_k(s1, PEER_TOPK)
        v2, i2 = lax.top_k(s2, PEER_TOPK)
        cand = (v1[..., :, None] + v2[..., None, :]).reshape(PEER_TOKEN_BLOCK, PEER_HEADS, PEER_TOPK * PEER_TOPK)
        cidx = (i1[..., :, None] * PEER_NKEYS + i2[..., None, :]).reshape(PEER_TOKEN_BLOCK, PEER_HEADS, PEER_TOPK * PEER_TOPK)
        sc, pos = lax.top_k(cand, PEER_TOPK)
        eidx = jnp.take_along_axis(cidx, pos, axis=-1)
        gsm = jax.nn.softmax(sc, axis=-1)
        u_sel = u_tab[eidx]
        v_sel = v_tab[eidx]
        act = jnp.einsum('td,thkd->thk', xb, u_sel).astype(jnp.float32)
        w = gsm * jax.nn.gelu(act, approximate=False)
        return jnp.einsum('thk,thkd->td', w.astype(xb.dtype), v_sel)

    out = lax.map(one_block, blocks).reshape(-1, D)[:n]
    return out.reshape(Bsz, T, D)


def run_trunk(x, c, conv_state, gla_state, ssd_state, w_ada, b_ada, norm1_w, w_in, gla_gk_w2, gla_gk_b,
              gla_norm_w, gla_proj, ssd_conv_w, ssd_conv_b, ssd_dt_bias, ssd_A_log, ssd_D, ssd_norm_w,
              ssd_proj, w_out, norm2_w, peer_wq, peer_keys1, peer_keys2, peer_u, peer_v, final_norm_w):
    new_gla, new_ssd, new_conv = [], [], []
    for l in range(DEPTH):
        mod = jax.nn.silu(c) @ w_ada[l] + b_ada[l]
        sh1, sc1, ga1, sh2, sc2, ga2 = jnp.split(mod[:, None, :], 6, axis=-1)
        h = rmsnorm(x, norm1_w[l]) * (1 + sc1) + sh1
        y, cv, sg, ss = token_mixer(h, conv_state[l], gla_state[l], ssd_state[l], w_in[l], gla_gk_w2[l],
                                    gla_gk_b[l], gla_norm_w[l], gla_proj[l], ssd_conv_w[l], ssd_conv_b[l],
                                    ssd_dt_bias[l], ssd_A_log[l], ssd_D[l], ssd_norm_w[l], ssd_proj[l], w_out[l])
        x = x + ga1 * y
        h = rmsnorm(x, norm2_w[l]) * (1 + sc2) + sh2
        x = x + ga2 * peer_ffn(h, peer_wq[l], peer_keys1[l], peer_keys2[l], peer_u[l], peer_v[l])
        new_gla.append(sg)
        new_ssd.append(ss)
        new_conv.append(cv)
    return rmsnorm(x, final_norm_w), jnp.stack(new_gla), jnp.stack(new_ssd), jnp.stack(new_conv)


def setup_inputs(seed: int = 0) -> dict:
    key = jax.random.key(seed)
    ks = iter(jax.random.split(key, 40))
    f32 = jnp.float32

    def nrm(shape, scale):
        return jax.random.normal(next(ks), shape, f32) * scale

    u_dt = jax.random.uniform(next(ks), (DEPTH, SSD_HEADS), f32)
    dt0 = jnp.exp(u_dt * (math.log(0.1) - math.log(1e-3)) + math.log(1e-3))
    return {
        "x_prompt": nrm((BATCH, SEQ, D_MODEL), 1.0),
        "x_sample": nrm((DEC_BATCH, DEC_SEQ, D_MODEL), 1.0),
        "state_gla": nrm((DEPTH, DEC_BATCH, GLA_HEADS, GLA_DK, GLA_DV), 1.0),
        "state_ssd": nrm((DEPTH, DEC_BATCH, SSD_HEADS, SSD_HEADDIM, SSD_DSTATE), 0.3),
        "state_conv": nrm((DEPTH, DEC_BATCH, CONV_WIDTH - 1, SSD_CONV_DIM), 1.0),
        "c_prompt": nrm((BATCH, D_MODEL), 1.0),
        "c_sample": nrm((DEC_BATCH, D_MODEL), 1.0),
        "w_ada": nrm((DEPTH, D_MODEL, 6 * D_MODEL), 0.5 * D_MODEL ** -0.5),
        "b_ada": nrm((DEPTH, 6 * D_MODEL), 0.02),
        "norm1_w": 1.0 + nrm((DEPTH, D_MODEL), 0.02),
        "w_in": nrm((DEPTH, D_MODEL, D_IN_PROJ), D_MODEL ** -0.5),
        "gla_gk_w2": nrm((DEPTH, GLA_GATE_RANK, GLA_KEY_DIM), GLA_GATE_RANK ** -0.5),
        "gla_gk_b": nrm((DEPTH, GLA_KEY_DIM), 0.1),
        "gla_norm_w": 1.0 + nrm((DEPTH, GLA_DV), 0.02),
        "gla_proj": nrm((DEPTH, GLA_VALUE_DIM, D_MODEL), GLA_VALUE_DIM ** -0.5),
        "ssd_conv_w": nrm((DEPTH, CONV_WIDTH, SSD_CONV_DIM), CONV_WIDTH ** -0.5),
        "ssd_conv_b": nrm((DEPTH, SSD_CONV_DIM), 0.02),
        "ssd_dt_bias": dt0 + jnp.log(-jnp.expm1(-dt0)),
        "ssd_A_log": jnp.log(jax.random.uniform(next(ks), (DEPTH, SSD_HEADS), f32, minval=1.0, maxval=16.0)),
        "ssd_D": 1.0 + nrm((DEPTH, SSD_HEADS), 0.1),
        "ssd_norm_w": 1.0 + nrm((DEPTH, SSD_INNER), 0.02),
        "ssd_proj": nrm((DEPTH, SSD_INNER, D_MODEL), SSD_INNER ** -0.5),
        "w_out": nrm((DEPTH, D_MODEL, D_MODEL), D_MODEL ** -0.5),
        "norm2_w": 1.0 + nrm((DEPTH, D_MODEL), 0.02),
        "peer_wq": nrm((DEPTH, D_MODEL, PEER_HEADS * PEER_DQ), D_MODEL ** -0.5),
        "peer_keys1": nrm((DEPTH, PEER_HEADS, PEER_NKEYS, PEER_DQ // 2), (PEER_DQ // 2) ** -0.5),
        "peer_keys2": nrm((DEPTH, PEER_HEADS, PEER_NKEYS, PEER_DQ // 2), (PEER_DQ // 2) ** -0.5),
        "peer_u": nrm((DEPTH, PEER_EXPERTS, D_MODEL), D_MODEL ** -0.5),
        "peer_v": nrm((DEPTH, PEER_EXPERTS, D_MODEL), (PEER_HEADS * PEER_TOPK) ** -0.5),
        "final_norm_w": 1.0 + nrm((D_MODEL,), 0.02),
    }


def reference(x_prompt, x_sample, state_gla, state_ssd, state_conv, c_prompt, c_sample, w_ada, b_ada,
              norm1_w, w_in, gla_gk_w2, gla_gk_b, gla_norm_w, gla_proj, ssd_conv_w, ssd_conv_b, ssd_dt_bias,
              ssd_A_log, ssd_D, ssd_norm_w, ssd_proj, w_out, norm2_w, peer_wq, peer_keys1, peer_keys2,
              peer_u, peer_v, final_norm_w):
    weights = (w_ada, b_ada, norm1_w, w_in, gla_gk_w2, gla_gk_b, gla_norm_w, gla_proj, ssd_conv_w,
               ssd_conv_b, ssd_dt_bias, ssd_A_log, ssd_D, ssd_norm_w, ssd_proj, w_out, norm2_w, peer_wq,
               peer_keys1, peer_keys2, peer_u, peer_v, final_norm_w)
    nb = x_prompt.shape[0]
    zero_conv = jnp.zeros((DEPTH, nb, CONV_WIDTH - 1, SSD_CONV_DIM), x_prompt.dtype)
    zero_gla = jnp.zeros((DEPTH, nb, GLA_HEADS, GLA_DK, GLA_DV), jnp.float32)
    zero_ssd = jnp.zeros((DEPTH, nb, SSD_HEADS, SSD_HEADDIM, SSD_DSTATE), jnp.float32)
    y_prompt, gla_p, ssd_p, conv_p = run_trunk(x_prompt, c_prompt, zero_conv, zero_gla, zero_ssd, *weights)
    y_sample, gla_s, ssd_s, conv_s = run_trunk(x_sample, c_sample, state_conv, state_gla, state_ssd, *weights)
    return (y_prompt, y_sample, gla_p, ssd_p, conv_p, gla_s, ssd_s, conv_s)
```

```python
import functools

import jax
import jax.numpy as jnp
import numpy as np
from jax import lax
from jax.experimental import pallas as pl
from jax.experimental.pallas import tpu as pltpu

f32 = jnp.float32
bf16 = jnp.bfloat16

D = 1024
DEPTH = 2
B_P, T_P = 8, 2048
B_S, T_S = 32, 64
N_P = B_P * T_P
N_S = B_S * T_S
N_TOK = N_P + N_S
SEG = 64
NSEG = N_TOK // SEG
NSEG_P = N_P // SEG
SEG_PER_PSEQ = T_P // SEG
EPS = 1e-6

GLA_H, GLA_DK, GLA_DV = 4, 128, 256
GLA_RANK = 16
GLA_GATE_NORM = 16.0
SSD_INNER = 2048
SSD_P = 64
SSD_H = 32
SSD_G = 4
SSD_N = 128
SSD_HG = SSD_H // SSD_G
SSD_GW = SSD_INNER // SSD_G
CONV_W = 4
SSD_CONV_DIM = SSD_INNER + 2 * SSD_G * SSD_N

PEER_H = 8
PEER_NK = 128
PEER_DQ = 256
PEER_TOPK = 16

C_Q, C_K, C_V, C_GO = 0, 512, 1024, 2048
C_Z, C_XBC, C_GATE = 3072, 5120, 8192
C_GKLOW = 10240
C_DT = 10368
P_COLS = C_DT + SSD_G * 128

LANE = 128
VMEM_LIMIT = 56 * 1024 * 1024


def _cparams(sem):
    return pltpu.CompilerParams(dimension_semantics=sem, vmem_limit_bytes=VMEM_LIMIT)


def _dot(a, b):
    return jnp.dot(a.astype(bf16), b.astype(bf16), preferred_element_type=f32)


def _dot_nt(a, b):
    return lax.dot_general(a.astype(bf16), b.astype(bf16), (((1,), (1,)), ((), ())), preferred_element_type=f32)


def _dot_tn(a, b):
    return lax.dot_general(a.astype(bf16), b.astype(bf16), (((0,), (0,)), ((), ())), preferred_element_type=f32)


def _split3(x):
    hi = x.astype(bf16)
    r = x - hi.astype(f32)
    mid = r.astype(bf16)
    lo = (r - mid.astype(f32)).astype(bf16)
    return hi, mid, lo


def _dot3(a, b):
    ah, am, _ = _split3(a)
    bh, bm, _ = _split3(b)
    d = functools.partial(jnp.dot, preferred_element_type=f32)
    return d(ah, bh) + (d(ah, bm) + d(am, bh))


def _dot3_nt(a, b):
    ah, am, _ = _split3(a)
    bh, bm, _ = _split3(b)
    d = functools.partial(lax.dot_general, dimension_numbers=(((1,), (1,)), ((), ())), preferred_element_type=f32)
    return d(ah, bh) + (d(ah, bm) + d(am, bh))


def _cumsum_rows(x):
    n = x.shape[0]
    tri = (lax.broadcasted_iota(jnp.int32, (n, n), 0) >= lax.broadcasted_iota(jnp.int32, (n, n), 1)).astype(bf16)
    hi, mid, lo = _split3(x)
    d = functools.partial(jnp.dot, preferred_element_type=f32)
    return d(tri, hi) + (d(tri, mid) + d(tri, lo))


def _silu(x):
    return x * jax.nn.sigmoid(x)


def _softplus(x):
    return jnp.maximum(x, 0.0) + jnp.log1p(jnp.exp(-jnp.abs(x)))


def _rms(x, w):
    return x * lax.rsqrt(jnp.mean(x * x, axis=-1, keepdims=True) + EPS) * w


def _ada_kernel(c_ref, w_ref, b_ref, o_ref):
    o_ref[0] = _dot3(_silu(c_ref[...]), w_ref[0]) + b_ref[0]


def _ada(c_all, w_ada, b_ada):
    nb = c_all.shape[0]
    tn = 1536
    return pl.pallas_call(
        _ada_kernel,
        grid=(DEPTH, 6 * D // tn),
        in_specs=[pl.BlockSpec((nb, D), lambda l, j: (0, 0)),
                  pl.BlockSpec((1, D, tn), lambda l, j: (l, 0, j)),
                  pl.BlockSpec((1, 1, tn), lambda l, j: (l, 0, j))],
        out_specs=pl.BlockSpec((1, nb, tn), lambda l, j: (l, 0, j)),
        out_shape=jax.ShapeDtypeStruct((DEPTH, nb, 6 * D), f32),
        compiler_params=_cparams(("parallel", "parallel")),
        name="ada_mod",
    )(c_all, w_ada, b_ada.reshape(DEPTH, 1, 6 * D))


NM_TB = 512


def _normmod_kernel(x_ref, w_ref, sc_ref, sh_ref, o_ref):
    for s in range(NM_TB // SEG):
        r = slice(s * SEG, (s + 1) * SEG)
        y = _rms(x_ref[r, :], w_ref[...])
        o_ref[r, :] = (y * (1.0 + sc_ref[s:s + 1, :]) + sh_ref[s:s + 1, :]).astype(bf16)


def _normmod(x, w, mod_seg, sc_col, sh_col):
    nseg_b = NM_TB // SEG
    return pl.pallas_call(
        _normmod_kernel,
        grid=(N_TOK // NM_TB,),
        in_specs=[pl.BlockSpec((NM_TB, D), lambda i: (i, 0)),
                  pl.BlockSpec((1, D), lambda i: (0, 0)),
                  pl.BlockSpec((nseg_b, D), lambda i: (i, sc_col)),
                  pl.BlockSpec((nseg_b, D), lambda i: (i, sh_col))],
        out_specs=pl.BlockSpec((NM_TB, D), lambda i: (i, 0)),
        out_shape=jax.ShapeDtypeStruct((N_TOK, D), bf16),
        compiler_params=_cparams(("parallel",)),
        name="norm_mod",
    )(x, w.reshape(1, D), mod_seg, mod_seg)


def _mm_kernel(a_ref, b_ref, o_ref):
    o_ref[...] = jnp.dot(a_ref[...], b_ref[...], preferred_element_type=f32)


def _mm(a, b, tm, tn, name):
    m, k = a.shape
    n = b.shape[1]
    return pl.pallas_call(
        _mm_kernel,
        grid=(n // tn, m // tm),
        in_specs=[pl.BlockSpec((tm, k), lambda j, i: (i, 0)),
                  pl.BlockSpec((k, tn), lambda j, i: (0, j))],
        out_specs=pl.BlockSpec((tm, tn), lambda j, i: (i, j)),
        out_shape=jax.ShapeDtypeStruct((m, n), f32),
        compiler_params=_cparams(("parallel", "parallel")),
        name=name,
    )(a, b)


def _is_prompt(c):
    return c < NSEG_P


def _sample_seq(c):
    return jnp.maximum(c - NSEG_P, 0)


def _prompt_seq(c):
    return jnp.minimum(c // SEG_PER_PSEQ, B_P - 1)


def _gla_kernel(q_ref, k_ref, v_ref, g_ref, low_ref, w2_ref, b2_ref, nw_ref, s0_ref,
                o_ref, stp_ref, sts_ref, st_ref):
    c = pl.program_id(1)
    is_p = _is_prompt(c)
    pos = c % SEG_PER_PSEQ

    @pl.when(jnp.logical_and(is_p, pos == 0))
    def _():
        st_ref[...] = jnp.zeros_like(st_ref)

    @pl.when(jnp.logical_not(is_p))
    def _():
        st_ref[...] = s0_ref[0, 0].T

    pre = _dot3(low_ref[...], w2_ref[...]) + b2_ref[...]
    gk = -_softplus(-pre) / GLA_GATE_NORM
    b = _cumsum_rows(gk)
    b_last = b[SEG - 1:SEG, :]
    k = k_ref[...]
    q_t = (q_ref[...] * (GLA_DK ** -0.5)) * jnp.exp(b)
    k_t = k * jnp.exp(-b)
    causal = lax.broadcasted_iota(jnp.int32, (SEG, SEG), 0) >= lax.broadcasted_iota(jnp.int32, (SEG, SEG), 1)
    att = jnp.where(causal, _dot_nt(q_t, k_t), 0.0)
    v = v_ref[...]
    st = st_ref[...]
    o = _dot(att, v) + _dot_nt(q_t, st)
    k_end = k * jnp.exp(b_last - b)
    st_new = st * jnp.exp(b_last) + _dot_tn(v, k_end)
    st_ref[...] = st_new

    o_ref[...] = (_rms(o, nw_ref[...]) * _silu(g_ref[...])).astype(bf16)

    @pl.when(jnp.logical_and(is_p, pos == SEG_PER_PSEQ - 1))
    def _():
        stp_ref[0, 0] = st_new.T

    @pl.when(jnp.logical_not(is_p))
    def _():
        sts_ref[0, 0] = st_new.T


def _gla(p, w2pad, b2, norm_w, s0):
    st_block = (1, 1, GLA_DK, GLA_DV)
    return pl.pallas_call(
        _gla_kernel,
        grid=(GLA_H, NSEG),
        in_specs=[pl.BlockSpec((SEG, GLA_DK), lambda h, c: (c, C_Q // GLA_DK + h)),
                  pl.BlockSpec((SEG, GLA_DK), lambda h, c: (c, C_K // GLA_DK + h)),
                  pl.BlockSpec((SEG, GLA_DV), lambda h, c: (c, C_V // GLA_DV + h)),
                  pl.BlockSpec((SEG, GLA_DV), lambda h, c: (c, C_GO // GLA_DV + h)),
                  pl.BlockSpec((SEG, LANE), lambda h, c: (c, C_GKLOW // LANE)),
                  pl.BlockSpec((LANE, GLA_DK), lambda h, c: (0, h)),
                  pl.BlockSpec((1, GLA_DK), lambda h, c: (0, h)),
                  pl.BlockSpec((1, GLA_DV), lambda h, c: (0, 0)),
                  pl.BlockSpec(st_block, lambda h, c: (_sample_seq(c), h, 0, 0))],
        out_specs=[pl.BlockSpec((SEG, GLA_DV), lambda h, c: (c, h)),
                   pl.BlockSpec(st_block, lambda h, c: (_prompt_seq(c), h, 0, 0)),
                   pl.BlockSpec(st_block, lambda h, c: (_sample_seq(c), h, 0, 0))],
        out_shape=[jax.ShapeDtypeStruct((N_TOK, GLA_H * GLA_DV), bf16),
                   jax.ShapeDtypeStruct((B_P, GLA_H, GLA_DK, GLA_DV), f32),
                   jax.ShapeDtypeStruct((B_S, GLA_H, GLA_DK, GLA_DV), f32)],
        scratch_shapes=[pltpu.VMEM((GLA_DV, GLA_DK), f32)],
        compiler_params=_cparams(("parallel", "arbitrary")),
        name="gla_mixer",
    )(p, p, p, p, p, w2pad, b2.reshape(1, -1), norm_w.reshape(1, -1), s0)


EXT = SEG + 8


def _ssd_kernel(x_ref, b_ref, c_ref, z_ref, dt_ref,
                cpx_ref, cpb_ref, cpc_ref, cwx_ref, cwb_ref, cwc_ref, cbx_ref, cbb_ref, cbc_ref,
                dtb_ref, alog_ref, dvec_ref, nw_ref, s0_ref,
                o_ref, stp_ref, sts_ref,
                extx, extb, extc, st_ref):
    c = pl.program_id(1)
    is_p = _is_prompt(c)
    pos = c % SEG_PER_PSEQ
    exts = (extx, extb, extc)

    @pl.when(jnp.logical_and(is_p, pos == 0))
    def _():
        for e in exts:
            e[0:8, :] = jnp.zeros((8, e.shape[1]), f32)
        st_ref[...] = jnp.zeros_like(st_ref)

    @pl.when(jnp.logical_not(is_p))
    def _():
        for e, cp in zip(exts, (cpx_ref, cpb_ref, cpc_ref)):
            e[0:8, :] = jnp.zeros((8, e.shape[1]), f32)
            e[8 - (CONV_W - 1):8, :] = cp[0]
        st_ref[...] = s0_ref[0].reshape(st_ref.shape)

    def conv(e, raw_ref, w_ref, bias_ref):
        e[8:EXT, :] = raw_ref[...]
        acc = bias_ref[...] + e[pl.ds(8 - (CONV_W - 1), SEG), :] * w_ref[0:1, :]
        for i in range(1, CONV_W):
            acc = acc + e[pl.ds(8 - (CONV_W - 1) + i, SEG), :] * w_ref[i:i + 1, :]
        e[0:8, :] = e[SEG:EXT, :]
        return _silu(acc)

    xs = conv(extx, x_ref, cwx_ref, cbx_ref)
    bc = conv(extb, b_ref, cwb_ref, cbb_ref)
    cc = conv(extc, c_ref, cwc_ref, cbc_ref)

    dt = _softplus(dt_ref[...] + dtb_ref[0])
    a = _cumsum_rows(dt * (-jnp.exp(alog_ref[0])))
    a_t = a.T
    dt_t = dt.T
    a_last = a[SEG - 1:SEG, :]
    e_a = jnp.exp(a)
    w_col = dt * jnp.exp(a_last - a)
    e_last = jnp.exp(a_last)
    causal = lax.broadcasted_iota(jnp.int32, (SEG, SEG), 0) >= lax.broadcasted_iota(jnp.int32, (SEG, SEG), 1)
    cb = _dot_nt(cc, bc)
    lane_lo = lax.broadcasted_iota(jnp.int32, (SEG, 2 * SSD_P), 1) < SSD_P
    sub_lo = lax.broadcasted_iota(jnp.int32, (2 * SSD_P, SSD_N), 0) < SSD_P

    def pair_cols(m, ja, jb):
        return jnp.where(lane_lo, jnp.broadcast_to(m[:, ja:ja + 1], (SEG, 2 * SSD_P)),
                         jnp.broadcast_to(m[:, jb:jb + 1], (SEG, 2 * SSD_P)))

    def w_intra(j):
        seg = a[:, j:j + 1] - a_t[j:j + 1, :]
        return cb * jnp.exp(jnp.where(causal, seg, -jnp.inf)) * dt_t[j:j + 1, :]

    ys = []
    for p in range(SSD_HG // 2):
        ja, jb = 2 * p, 2 * p + 1
        xp = xs[:, p * 2 * SSD_P:(p + 1) * 2 * SSD_P]
        y_intra = jnp.where(lane_lo, _dot(w_intra(ja), xp), _dot(w_intra(jb), xp))
        sp = st_ref[p]
        y_inter = _dot_nt(cc, sp) * pair_cols(e_a, ja, jb)
        ds = _dot_tn(xp * pair_cols(w_col, ja, jb), bc)
        rs = jnp.where(sub_lo, jnp.broadcast_to(e_last[:, ja:ja + 1], (2 * SSD_P, SSD_N)),
                       jnp.broadcast_to(e_last[:, jb:jb + 1], (2 * SSD_P, SSD_N)))
        st_ref[p] = sp * rs + ds
        ys.append(y_intra + y_inter + dvec_ref[0, :, p * 2 * SSD_P:(p + 1) * 2 * SSD_P] * xp)
    y = jnp.concatenate(ys, axis=1) * _silu(z_ref[...])
    o_ref[...] = _rms(y, nw_ref[...]).astype(bf16)

    @pl.when(jnp.logical_and(is_p, pos == SEG_PER_PSEQ - 1))
    def _():
        stp_ref[0] = st_ref[...].reshape(stp_ref.shape[1:])

    @pl.when(jnp.logical_not(is_p))
    def _():
        sts_ref[0] = st_ref[...].reshape(sts_ref.shape[1:])


def _ssd(p, conv_prev, conv_w, conv_b, dtb, alog, dvec, norm_w, s0):
    nb = SSD_G * SSD_N
    xo, bo, co = 0, SSD_INNER, SSD_INNER + nb
    st_block = (1, SSD_HG, SSD_P, SSD_N)

    def col(off, width):
        return lambda g, c: (c, (C_XBC + off) // width + g)

    def wcol(off, width):
        return lambda g, c: (0, off // width + g)

    def cpcol(off, width):
        return lambda g, c: (_sample_seq(c), 0, off // width + g)

    conv_b2 = conv_b.reshape(1, -1)
    return pl.pallas_call(
        _ssd_kernel,
        grid=(SSD_G, NSEG),
        in_specs=[pl.BlockSpec((SEG, SSD_GW), col(xo, SSD_GW)),
                  pl.BlockSpec((SEG, SSD_N), col(bo, SSD_N)),
                  pl.BlockSpec((SEG, SSD_N), col(co, SSD_N)),
                  pl.BlockSpec((SEG, SSD_GW), lambda g, c: (c, C_Z // SSD_GW + g)),
                  pl.BlockSpec((SEG, LANE), lambda g, c: (c, C_DT // LANE + g)),
                  pl.BlockSpec((1, CONV_W - 1, SSD_GW), cpcol(xo, SSD_GW)),
                  pl.BlockSpec((1, CONV_W - 1, SSD_N), cpcol(bo, SSD_N)),
                  pl.BlockSpec((1, CONV_W - 1, SSD_N), cpcol(co, SSD_N)),
                  pl.BlockSpec((CONV_W, SSD_GW), wcol(xo, SSD_GW)),
                  pl.BlockSpec((CONV_W, SSD_N), wcol(bo, SSD_N)),
                  pl.BlockSpec((CONV_W, SSD_N), wcol(co, SSD_N)),
                  pl.BlockSpec((1, SSD_GW), wcol(xo, SSD_GW)),
                  pl.BlockSpec((1, SSD_N), wcol(bo, SSD_N)),
                  pl.BlockSpec((1, SSD_N), wcol(co, SSD_N)),
                  pl.BlockSpec((1, 1, LANE), lambda g, c: (g, 0, 0)),
                  pl.BlockSpec((1, 1, LANE), lambda g, c: (g, 0, 0)),
                  pl.BlockSpec((1, 1, SSD_GW), lambda g, c: (g, 0, 0)),
                  pl.BlockSpec((1, SSD_GW), lambda g, c: (0, g)),
                  pl.BlockSpec(st_block, lambda g, c: (_sample_seq(c), g, 0, 0))],
        out_specs=[pl.BlockSpec((SEG, SSD_GW), lambda g, c: (c, g)),
                   pl.BlockSpec(st_block, lambda g, c: (_prompt_seq(c), g, 0, 0)),
                   pl.BlockSpec(st_block, lambda g, c: (_sample_seq(c), g, 0, 0))],
        out_shape=[jax.ShapeDtypeStruct((N_TOK, SSD_INNER), bf16),
                   jax.ShapeDtypeStruct((B_P, SSD_H, SSD_P, SSD_N), f32),
                   jax.ShapeDtypeStruct((B_S, SSD_H, SSD_P, SSD_N), f32)],
        scratch_shapes=[pltpu.VMEM((EXT, SSD_GW), f32), pltpu.VMEM((EXT, SSD_N), f32), pltpu.VMEM((EXT, SSD_N), f32),
                        pltpu.VMEM((SSD_HG // 2, 2 * SSD_P, SSD_N), f32)],
        compiler_params=_cparams(("parallel", "arbitrary")),
        name="ssd_mixer",
    )(p, p, p, p, p, conv_prev, conv_prev, conv_prev, conv_w, conv_w, conv_w, conv_b2, conv_b2, conv_b2,
      dtb, alog, dvec, norm_w.reshape(1, -1), s0)


MIX_TB = 512


def _mix_kernel(oa_ref, yb_ref, gt_ref, x_ref, ga_ref, sc_ref, sh_ref, nw_ref, wa_ref, wb_ref, wo_ref,
                x1_ref, h2_ref, h2t_ref):
    br_a = jnp.dot(oa_ref[...], wa_ref[...], preferred_element_type=f32)
    br_b = jnp.dot(yb_ref[...], wb_ref[...], preferred_element_type=f32)
    g_a = jax.nn.sigmoid(gt_ref[:, 0:D])
    g_b = jax.nn.sigmoid(gt_ref[:, D:2 * D])
    y = _dot(g_a * br_a + g_b * br_b, wo_ref[...])
    for s in range(MIX_TB // SEG):
        r = slice(s * SEG, (s + 1) * SEG)
        x1 = x_ref[r, :] + ga_ref[s:s + 1, :] * y[r, :]
        x1_ref[r, :] = x1
        h2_ref[r, :] = (_rms(x1, nw_ref[...]) * (1.0 + sc_ref[s:s + 1, :]) + sh_ref[s:s + 1, :]).astype(bf16)
    h2t_ref[...] = h2_ref[...].T


def _mix(oa, yb, p, x, mod_seg, norm2_w, wa, wb, wo):
    nseg_b = MIX_TB // SEG
    full = lambda i: (0, 0)
    return pl.pallas_call(
        _mix_kernel,
        grid=(N_TOK // MIX_TB,),
        in_specs=[pl.BlockSpec((MIX_TB, D), lambda i: (i, 0)),
                  pl.BlockSpec((MIX_TB, SSD_INNER), lambda i: (i, 0)),
                  pl.BlockSpec((MIX_TB, 2 * D), lambda i: (i, C_GATE // (2 * D))),
                  pl.BlockSpec((MIX_TB, D), lambda i: (i, 0)),
                  pl.BlockSpec((nseg_b, D), lambda i: (i, 2)),
                  pl.BlockSpec((nseg_b, D), lambda i: (i, 4)),
                  pl.BlockSpec((nseg_b, D), lambda i: (i, 3)),
                  pl.BlockSpec((1, D), full),
                  pl.BlockSpec((D, D), full),
                  pl.BlockSpec((SSD_INNER, D), full),
                  pl.BlockSpec((D, D), full)],
        out_specs=[pl.BlockSpec((MIX_TB, D), lambda i: (i, 0)),
                   pl.BlockSpec((MIX_TB, D), lambda i: (i, 0)),
                   pl.BlockSpec((D, MIX_TB), lambda i: (0, i))],
        out_shape=[jax.ShapeDtypeStruct((N_TOK, D), f32),
                   jax.ShapeDtypeStruct((N_TOK, D), bf16),
                   jax.ShapeDtypeStruct((D, N_TOK), bf16)],
        compiler_params=_cparams(("parallel",)),
        name="mix_out",
    )(oa, yb, p, x, mod_seg, mod_seg, mod_seg, norm2_w.reshape(1, D), wa, wb, wo)


RT_TL = 128
HALF = PEER_DQ // 2


def _top16_rows(x, dst):
    for j in range(PEER_TOPK):
        m = jnp.max(x, axis=0, keepdims=True)
        dst[j:j + 1, :] = m
        x = jnp.where(x == m, -jnp.inf, x)


def _route_kernel(q1_ref, q2_ref, k1_ref, k2_ref, s1_ref, s2_ref, f1_ref, f2_ref, tau_ref, v1s, v2s):
    s1 = _dot3_nt(k1_ref[0], q1_ref[...])
    s2 = _dot3_nt(k2_ref[0], q2_ref[...])
    s1_ref[0] = s1
    s2_ref[0] = s2
    _top16_rows(s1, v1s)
    _top16_rows(s2, v2s)
    row = lax.broadcasted_iota(jnp.int32, (8, RT_TL), 0)
    ninf = jnp.float32(-jnp.inf)
    v2lo = v2s[0:8, :]
    cands = [v1s[0:1, :] + v2lo, v1s[0:1, :] + v2s[8:16, :], v1s[1:2, :] + v2lo]
    for i in range(2, 8):
        cands.append(jnp.where(row < PEER_TOPK // (i + 1), v1s[i:i + 1, :] + v2lo, ninf))
    cands.append(v1s[8:16, :] + v2s[0:1, :])
    top = None
    zsum = None
    tau = None
    for j in range(PEER_TOPK):
        m = cands[0]
        for cnd in cands[1:]:
            m = jnp.maximum(m, cnd)
        m = jnp.max(m, axis=0, keepdims=True)
        if j == 0:
            top = m
            zsum = jnp.ones_like(m)
        else:
            zsum = zsum + jnp.exp(m - top)
        tau = m
        cands = [jnp.where(cnd == m, ninf, cnd) for cnd in cands]
    tau_ref[0] = tau
    f1_ref[0] = jnp.exp(s1 - v1s[0:1, :]) / zsum
    f2_ref[0] = jnp.exp(s2 - v2s[0:1, :])


def _route(q, keys1, keys2):
    tile = pl.BlockSpec((1, PEER_NK, RT_TL), lambda i, h: (h, 0, i))
    big = jax.ShapeDtypeStruct((PEER_H, PEER_NK, N_TOK), f32)
    return pl.pallas_call(
        _route_kernel,
        grid=(N_TOK // RT_TL, PEER_H),
        in_specs=[pl.BlockSpec((RT_TL, HALF), lambda i, h: (i, 2 * h)),
                  pl.BlockSpec((RT_TL, HALF), lambda i, h: (i, 2 * h + 1)),
                  pl.BlockSpec((1, PEER_NK, HALF), lambda i, h: (h, 0, 0)),
                  pl.BlockSpec((1, PEER_NK, HALF), lambda i, h: (h, 0, 0))],
        out_specs=[tile, tile, tile, tile, pl.BlockSpec((1, 1, RT_TL), lambda i, h: (h, 0, i))],
        out_shape=[big, big, big, big, jax.ShapeDtypeStruct((PEER_H, 1, N_TOK), f32)],
        scratch_shapes=[pltpu.VMEM((PEER_TOPK, RT_TL), f32), pltpu.VMEM((PEER_TOPK, RT_TL), f32)],
        compiler_params=_cparams(("parallel", "parallel")),
        name="peer_route",
    )(q, q, keys1, keys2)


PE_TB = 512
PE_CH = 2 * PEER_NK
PE_NCH = PEER_NK * PEER_NK // PE_CH
SQRT_HALF = 0.7071067811865476


def _peer_kernel(*refs, final):
    if final:
        (u_ref, vt_ref, xt_ref, s1_ref, f1_ref, s2_ref, f2_ref, tau_ref, x1_ref, ga_ref, fw_ref,
         o_ref, acc_ref) = refs
    else:
        (u_ref, vt_ref, xt_ref, s1_ref, f1_ref, s2_ref, f2_ref, tau_ref, x1_ref, ga_ref,
         o_ref, acc_ref) = refs
    j = pl.program_id(1)

    @pl.when(j == 0)
    def _():
        acc_ref[...] = jnp.zeros_like(acc_ref)

    act = jnp.dot(u_ref[0], xt_ref[...], preferred_element_type=f32)
    gel = 0.5 * act * (1.0 + lax.erf(act * SQRT_HALF))
    halves = []
    for i in range(PE_CH // PEER_NK):
        a = j * (PE_CH // PEER_NK) + i
        s1_rows = [s1_ref[h, pl.ds(a, 1), :] for h in range(PEER_H)]
        f1_rows = [f1_ref[h, pl.ds(a, 1), :] for h in range(PEER_H)]
        tiles = []
        for lt in range(PE_TB // LANE):
            ls = slice(lt * LANE, (lt + 1) * LANE)
            w = jnp.zeros((PEER_NK, LANE), f32)
            for h in range(PEER_H):
                cand = s2_ref[h, :, ls] + s1_rows[h][:, ls]
                val = f2_ref[h, :, ls] * f1_rows[h][:, ls]
                w = w + jnp.where(cand >= tau_ref[h, :, ls], val, 0.0)
            tiles.append(w)
        halves.append(jnp.concatenate(tiles, axis=1))
    pw = (jnp.concatenate(halves, axis=0) * gel).astype(bf16)
    acc_ref[...] += jnp.dot(vt_ref[0], pw, preferred_element_type=f32)

    @pl.when(j == PE_NCH - 1)
    def _():
        out = acc_ref[...].T
        for s in range(PE_TB // SEG):
            r = slice(s * SEG, (s + 1) * SEG)
            x2 = x1_ref[r, :] + ga_ref[s:s + 1, :] * out[r, :]
            o_ref[r, :] = _rms(x2, fw_ref[...]) if final else x2


def _peer(u_c, vt_c, h2t, s1t, f1t, s2t, f2t, tau, x1, mod_seg, final_w):
    final = final_w is not None
    nseg_b = PE_TB // SEG
    rt = pl.BlockSpec((PEER_H, PEER_NK, PE_TB), lambda i, j: (0, 0, i))
    in_specs = [pl.BlockSpec((1, PE_CH, D), lambda i, j: (j, 0, 0)),
                pl.BlockSpec((1, D, PE_CH), lambda i, j: (j, 0, 0)),
                pl.BlockSpec((D, PE_TB), lambda i, j: (0, i)),
                rt, rt, rt, rt,
                pl.BlockSpec((PEER_H, 1, PE_TB), lambda i, j: (0, 0, i)),
                pl.BlockSpec((PE_TB, D), lambda i, j: (i, 0)),
                pl.BlockSpec((nseg_b, D), lambda i, j: (i, 5))]
    args = [u_c, vt_c, h2t, s1t, f1t, s2t, f2t, tau, x1, mod_seg]
    if final:
        in_specs.append(pl.BlockSpec((1, D), lambda i, j: (0, 0)))
        args.append(final_w.reshape(1, D))
    return pl.pallas_call(
        functools.partial(_peer_kernel, final=final),
        grid=(N_TOK // PE_TB, PE_NCH),
        in_specs=in_specs,
        out_specs=pl.BlockSpec((PE_TB, D), lambda i, j: (i, 0)),
        out_shape=jax.ShapeDtypeStruct((N_TOK, D), f32),
        scratch_shapes=[pltpu.VMEM((D, PE_TB), f32)],
        compiler_params=_cparams(("parallel", "arbitrary")),
        name="peer_experts_final" if final else "peer_experts",
    )(*args)


def _reorder_w_in(w):
    o = np.cumsum([0, 512, 512, 1024, 1024, GLA_RANK, SSD_INNER, SSD_CONV_DIM, SSD_H, 2 * D])
    q_k_v_go = w[:, o[0]:o[4]]
    gk_low = w[:, o[4]:o[5]]
    z = w[:, o[5]:o[6]]
    xbc = w[:, o[6]:o[7]]
    dt = w[:, o[7]:o[8]]
    gates = w[:, o[8]:o[9]]
    pad = lambda m: jnp.pad(m, ((0, 0), (0, LANE - m.shape[1])))
    dts = [pad(dt[:, g * SSD_HG:(g + 1) * SSD_HG]) for g in range(SSD_G)]
    return jnp.concatenate([q_k_v_go, z, xbc, gates, pad(gk_low)] + dts, axis=1).astype(bf16)


def kernel(x_prompt, x_sample, state_gla, state_ssd, state_conv, c_prompt, c_sample, w_ada, b_ada, norm1_w, w_in, gla_gk_w2, gla_gk_b, gla_norm_w, gla_proj, ssd_conv_w, ssd_conv_b, ssd_dt_bias, ssd_A_log, ssd_D, ssd_norm_w, ssd_proj, w_out, norm2_w, peer_wq, peer_keys1, peer_keys2, peer_u, peer_v, final_norm_w):
    x = jnp.concatenate([x_prompt.reshape(N_P, D), x_sample.reshape(N_S, D)], axis=0)
    c_all = jnp.concatenate([c_prompt, c_sample], axis=0)
    mod = _ada(c_all, w_ada, b_ada)
    seg2seq = np.concatenate([np.repeat(np.arange(B_P), SEG_PER_PSEQ), B_P + np.arange(B_S)])

    pad_lane = lambda m: jnp.pad(m, ((0, 0), (0, LANE - m.shape[1])))
    gla_st, ssd_st, conv_st = [], [], []
    for l in range(DEPTH):
        mod_seg = mod[l][seg2seq]
        h = _normmod(x, norm1_w[l], mod_seg, 1, 0)
        p = _mm(h, _reorder_w_in(w_in[l]), 512, 2176, "in_proj")

        w2pad = jnp.pad(gla_gk_w2[l], ((0, LANE - GLA_RANK), (0, 0)))
        oa, gla_p, gla_s = _gla(p, w2pad, gla_gk_b[l], gla_norm_w[l], state_gla[l])

        dtb = pad_lane(ssd_dt_bias[l].reshape(SSD_G, SSD_HG)).reshape(SSD_G, 1, LANE)
        alog = pad_lane(ssd_A_log[l].reshape(SSD_G, SSD_HG)).reshape(SSD_G, 1, LANE)
        dvec = jnp.repeat(ssd_D[l], SSD_P).reshape(SSD_G, 1, SSD_GW)
        yb, ssd_p, ssd_s = _ssd(p, state_conv[l], ssd_conv_w[l], ssd_conv_b[l], dtb, alog, dvec,
                                ssd_norm_w[l], state_ssd[l])

        x1, h2, h2t = _mix(oa, yb, p, x, mod_seg, norm2_w[l], gla_proj[l].astype(bf16),
                           ssd_proj[l].astype(bf16), w_out[l].astype(bf16))
        q = _mm(h2, peer_wq[l].astype(bf16), 512, PEER_H * PEER_DQ, "peer_query")
        s1t, s2t, f1t, f2t, tau = _route(q, peer_keys1[l], peer_keys2[l])
        u_c = peer_u[l].astype(bf16).reshape(PE_NCH, PE_CH, D)
        vt_c = peer_v[l].astype(bf16).reshape(PE_NCH, PE_CH, D).transpose(0, 2, 1)
        x = _peer(u_c, vt_c, h2t, s1t, f1t, s2t, f2t, tau, x1, mod_seg,
                  final_norm_w if l == DEPTH - 1 else None)

        xbc = p[:, C_XBC:C_XBC + SSD_CONV_DIM]
        conv_p = xbc[:N_P].reshape(B_P, T_P, SSD_CONV_DIM)[:, T_P - (CONV_W - 1):]
        conv_s = xbc[N_P:].reshape(B_S, T_S, SSD_CONV_DIM)[:, T_S - (CONV_W - 1):]
        gla_st.append((gla_p, gla_s))
        ssd_st.append((ssd_p, ssd_s))
        conv_st.append((conv_p, conv_s))

    y_prompt = x[:N_P].reshape(B_P, T_P, D)
    y_sample = x[N_P:].reshape(B_S, T_S, D)
    stack = lambda pairs, k: jnp.stack([pr[k] for pr in pairs])
    return (y_prompt, y_sample, stack(gla_st, 0), stack(ssd_st, 0), stack(conv_st, 0),
            stack(gla_st, 1), stack(ssd_st, 1), stack(conv_st, 1))
```

```python
import functools

import jax
import jax.numpy as jnp
import numpy as np
from jax import lax
from jax.experimental import pallas as pl
from jax.experimental.pallas import tpu as pltpu

f32 = jnp.float32
bf16 = jnp.bfloat16

D = 1024
DEPTH = 2
B_P, T_P = 8, 2048
B_S, T_S = 32, 64
N_P = B_P * T_P
N_S = B_S * T_S
N_TOK = N_P + N_S
SEG = 64
NSEG = N_TOK // SEG
NSEG_P = N_P // SEG
SEG_PER_PSEQ = T_P // SEG
EPS = 1e-6

GLA_H, GLA_DK, GLA_DV = 4, 128, 256
GLA_RANK = 16
GLA_GATE_NORM = 16.0
SSD_INNER = 2048
SSD_P = 64
SSD_H = 32
SSD_G = 4
SSD_N = 128
SSD_HG = SSD_H // SSD_G
SSD_GW = SSD_INNER // SSD_G
CONV_W = 4
SSD_CONV_DIM = SSD_INNER + 2 * SSD_G * SSD_N

PEER_H = 8
PEER_NK = 128
PEER_DQ = 256
PEER_TOPK = 16

C_Q, C_K, C_V, C_GO = 0, 512, 1024, 2048
C_Z, C_XBC, C_GATE = 3072, 5120, 8192
C_GKLOW = 10240
C_DT = 10368
P_COLS = C_DT + SSD_G * 128

LANE = 128
VMEM_LIMIT = 56 * 1024 * 1024


def _cparams(sem):
    return pltpu.CompilerParams(dimension_semantics=sem, vmem_limit_bytes=VMEM_LIMIT)


def _dot(a, b):
    return jnp.dot(a.astype(bf16), b.astype(bf16), preferred_element_type=f32)


def _dot_nt(a, b):
    return lax.dot_general(a.astype(bf16), b.astype(bf16), (((1,), (1,)), ((), ())), preferred_element_type=f32)


def _dot_tn(a, b):
    return lax.dot_general(a.astype(bf16), b.astype(bf16), (((0,), (0,)), ((), ())), preferred_element_type=f32)


def _split3(x):
    hi = x.astype(bf16)
    r = x - hi.astype(f32)
    mid = r.astype(bf16)
    lo = (r - mid.astype(f32)).astype(bf16)
    return hi, mid, lo


def _dot3(a, b):
    ah, am, _ = _split3(a)
    bh, bm, _ = _split3(b)
    d = functools.partial(jnp.dot, preferred_element_type=f32)
    return d(ah, bh) + (d(ah, bm) + d(am, bh))


def _dot3_nt(a, b):
    ah, am, _ = _split3(a)
    bh, bm, _ = _split3(b)
    d = functools.partial(lax.dot_general, dimension_numbers=(((1,), (1,)), ((), ())), preferred_element_type=f32)
    return d(ah, bh) + (d(ah, bm) + d(am, bh))


def _cumsum_rows(x):
    n = x.shape[0]
    tri = (lax.broadcasted_iota(jnp.int32, (n, n), 0) >= lax.broadcasted_iota(jnp.int32, (n, n), 1)).astype(bf16)
    hi, mid, lo = _split3(x)
    d = functools.partial(jnp.dot, preferred_element_type=f32)
    return d(tri, hi) + (d(tri, mid) + d(tri, lo))


def _silu(x):
    return x * jax.nn.sigmoid(x)


def _softplus(x):
    return jnp.maximum(x, 0.0) + jnp.log1p(jnp.exp(-jnp.abs(x)))


def _rms(x, w):
    return x * lax.rsqrt(jnp.mean(x * x, axis=-1, keepdims=True) + EPS) * w


def _ada_kernel(c_ref, w_ref, b_ref, o_ref):
    o_ref[0] = _dot3(_silu(c_ref[...]), w_ref[0]) + b_ref[0]


def _ada(c_all, w_ada, b_ada):
    nb = c_all.shape[0]
    tn = 1536
    return pl.pallas_call(
        _ada_kernel,
        grid=(DEPTH, 6 * D // tn),
        in_specs=[pl.BlockSpec((nb, D), lambda l, j: (0, 0)),
                  pl.BlockSpec((1, D, tn), lambda l, j: (l, 0, j)),
                  pl.BlockSpec((1, 1, tn), lambda l, j: (l, 0, j))],
        out_specs=pl.BlockSpec((1, nb, tn), lambda l, j: (l, 0, j)),
        out_shape=jax.ShapeDtypeStruct((DEPTH, nb, 6 * D), f32),
        compiler_params=_cparams(("parallel", "parallel")),
        name="ada_mod",
    )(c_all, w_ada, b_ada.reshape(DEPTH, 1, 6 * D))


NM_TB = 512


def _normmod_kernel(x_ref, w_ref, sc_ref, sh_ref, o_ref):
    for s in range(NM_TB // SEG):
        r = slice(s * SEG, (s + 1) * SEG)
        y = _rms(x_ref[r, :], w_ref[...])
        o_ref[r, :] = (y * (1.0 + sc_ref[s:s + 1, :]) + sh_ref[s:s + 1, :]).astype(bf16)


def _normmod(x, w, mod_seg, sc_col, sh_col):
    nseg_b = NM_TB // SEG
    return pl.pallas_call(
        _normmod_kernel,
        grid=(N_TOK // NM_TB,),
        in_specs=[pl.BlockSpec((NM_TB, D), lambda i: (i, 0)),
                  pl.BlockSpec((1, D), lambda i: (0, 0)),
                  pl.BlockSpec((nseg_b, D), lambda i: (i, sc_col)),
                  pl.BlockSpec((nseg_b, D), lambda i: (i, sh_col))],
        out_specs=pl.BlockSpec((NM_TB, D), lambda i: (i, 0)),
        out_shape=jax.ShapeDtypeStruct((N_TOK, D), bf16),
        compiler_params=_cparams(("parallel",)),
        name="norm_mod",
    )(x, w.reshape(1, D), mod_seg, mod_seg)


def _mm_kernel(a_ref, b_ref, o_ref):
    o_ref[...] = jnp.dot(a_ref[...], b_ref[...], preferred_element_type=f32)


def _mm(a, b, tm, tn, name):
    m, k = a.shape
    n = b.shape[1]
    return pl.pallas_call(
        _mm_kernel,
        grid=(n // tn, m // tm),
        in_specs=[pl.BlockSpec((tm, k), lambda j, i: (i, 0)),
                  pl.BlockSpec((k, tn), lambda j, i: (0, j))],
        out_specs=pl.BlockSpec((tm, tn), lambda j, i: (i, j)),
        out_shape=jax.ShapeDtypeStruct((m, n), f32),
        compiler_params=_cparams(("parallel", "parallel")),
        name=name,
    )(a, b)


def _is_prompt(c):
    return c < NSEG_P


def _sample_seq(c):
    return jnp.maximum(c - NSEG_P, 0)


def _prompt_seq(c):
    return jnp.minimum(c // SEG_PER_PSEQ, B_P - 1)


def _gla_kernel(q_ref, k_ref, v_ref, g_ref, low_ref, w2_ref, b2_ref, nw_ref, s0_ref,
                o_ref, stp_ref, sts_ref, st_ref):
    c = pl.program_id(1)
    is_p = _is_prompt(c)
    pos = c % SEG_PER_PSEQ

    @pl.when(jnp.logical_and(is_p, pos == 0))
    def _():
        st_ref[...] = jnp.zeros_like(st_ref)

    @pl.when(jnp.logical_not(is_p))
    def _():
        st_ref[...] = s0_ref[0, 0].T

    pre = _dot3(low_ref[...], w2_ref[...]) + b2_ref[...]
    gk = -_softplus(-pre) / GLA_GATE_NORM
    b = _cumsum_rows(gk)
    b_last = b[SEG - 1:SEG, :]
    k = k_ref[...]
    q_t = (q_ref[...] * (GLA_DK ** -0.5)) * jnp.exp(b)
    k_t = k * jnp.exp(-b)
    causal = lax.broadcasted_iota(jnp.int32, (SEG, SEG), 0) >= lax.broadcasted_iota(jnp.int32, (SEG, SEG), 1)
    att = jnp.where(causal, _dot_nt(q_t, k_t), 0.0)
    v = v_ref[...]
    st = st_ref[...]
    o = _dot(att, v) + _dot_nt(q_t, st)
    k_end = k * jnp.exp(b_last - b)
    st_new = st * jnp.exp(b_last) + _dot_tn(v, k_end)
    st_ref[...] = st_new

    o_ref[...] = (_rms(o, nw_ref[...]) * _silu(g_ref[...])).astype(bf16)

    @pl.when(jnp.logical_and(is_p, pos == SEG_PER_PSEQ - 1))
    def _():
        stp_ref[0, 0] = st_new.T

    @pl.when(jnp.logical_not(is_p))
    def _():
        sts_ref[0, 0] = st_new.T


def _gla(p, w2pad, b2, norm_w, s0):
    st_block = (1, 1, GLA_DK, GLA_DV)
    return pl.pallas_call(
        _gla_kernel,
        grid=(GLA_H, NSEG),
        in_specs=[pl.BlockSpec((SEG, GLA_DK), lambda h, c: (c, C_Q // GLA_DK + h)),
                  pl.BlockSpec((SEG, GLA_DK), lambda h, c: (c, C_K // GLA_DK + h)),
                  pl.BlockSpec((SEG, GLA_DV), lambda h, c: (c, C_V // GLA_DV + h)),
                  pl.BlockSpec((SEG, GLA_DV), lambda h, c: (c, C_GO // GLA_DV + h)),
                  pl.BlockSpec((SEG, LANE), lambda h, c: (c, C_GKLOW // LANE)),
                  pl.BlockSpec((LANE, GLA_DK), lambda h, c: (0, h)),
                  pl.BlockSpec((1, GLA_DK), lambda h, c: (0, h)),
                  pl.BlockSpec((1, GLA_DV), lambda h, c: (0, 0)),
                  pl.BlockSpec(st_block, lambda h, c: (_sample_seq(c), h, 0, 0))],
        out_specs=[pl.BlockSpec((SEG, GLA_DV), lambda h, c: (c, h)),
                   pl.BlockSpec(st_block, lambda h, c: (_prompt_seq(c), h, 0, 0)),
                   pl.BlockSpec(st_block, lambda h, c: (_sample_seq(c), h, 0, 0))],
        out_shape=[jax.ShapeDtypeStruct((N_TOK, GLA_H * GLA_DV), bf16),
                   jax.ShapeDtypeStruct((B_P, GLA_H, GLA_DK, GLA_DV), f32),
                   jax.ShapeDtypeStruct((B_S, GLA_H, GLA_DK, GLA_DV), f32)],
        scratch_shapes=[pltpu.VMEM((GLA_DV, GLA_DK), f32)],
        compiler_params=_cparams(("parallel", "arbitrary")),
        name="gla_mixer",
    )(p, p, p, p, p, w2pad, b2.reshape(1, -1), norm_w.reshape(1, -1), s0)


EXT = SEG + 8


def _ssd_kernel(x_ref, b_ref, c_ref, z_ref, dt_ref,
                cpx_ref, cpb_ref, cpc_ref, cwx_ref, cwb_ref, cwc_ref, cbx_ref, cbb_ref, cbc_ref,
                dtb_ref, alog_ref, dvec_ref, nw_ref, s0_ref,
                o_ref, stp_ref, sts_ref,
                extx, extb, extc, st_ref):
    c = pl.program_id(1)
    is_p = _is_prompt(c)
    pos = c % SEG_PER_PSEQ
    exts = (extx, extb, extc)

    @pl.when(jnp.logical_and(is_p, pos == 0))
    def _():
        for e in exts:
            e[0:8, :] = jnp.zeros((8, e.shape[1]), f32)
        st_ref[...] = jnp.zeros_like(st_ref)

    @pl.when(jnp.logical_not(is_p))
    def _():
        for e, cp in zip(exts, (cpx_ref, cpb_ref, cpc_ref)):
            e[0:8, :] = jnp.zeros((8, e.shape[1]), f32)
            e[8 - (CONV_W - 1):8, :] = cp[0]
        st_ref[...] = s0_ref[0].reshape(st_ref.shape)

    def conv(e, raw_ref, w_ref, bias_ref):
        e[8:EXT, :] = raw_ref[...]
        acc = bias_ref[...] + e[pl.ds(8 - (CONV_W - 1), SEG), :] * w_ref[0:1, :]
        for i in range(1, CONV_W):
            acc = acc + e[pl.ds(8 - (CONV_W - 1) + i, SEG), :] * w_ref[i:i + 1, :]
        e[0:8, :] = e[SEG:EXT, :]
        return _silu(acc)

    xs = conv(extx, x_ref, cwx_ref, cbx_ref)
    bc = conv(extb, b_ref, cwb_ref, cbb_ref)
    cc = conv(extc, c_ref, cwc_ref, cbc_ref)

    dt = _softplus(dt_ref[...] + dtb_ref[0])
    a = _cumsum_rows(dt * (-jnp.exp(alog_ref[0])))
    a_t = a.T
    dt_t = dt.T
    a_last = a[SEG - 1:SEG, :]
    e_a = jnp.exp(a)
    w_col = dt * jnp.exp(a_last - a)
    e_last = jnp.exp(a_last)
    causal = lax.broadcasted_iota(jnp.int32, (SEG, SEG), 0) >= lax.broadcasted_iota(jnp.int32, (SEG, SEG), 1)
    cb = _dot_nt(cc, bc)
    lane_lo = lax.broadcasted_iota(jnp.int32, (SEG, 2 * SSD_P), 1) < SSD_P
    sub_lo = lax.broadcasted_iota(jnp.int32, (2 * SSD_P, SSD_N), 0) < SSD_P

    def pair_cols(m, ja, jb):
        return jnp.where(lane_lo, jnp.broadcast_to(m[:, ja:ja + 1], (SEG, 2 * SSD_P)),
                         jnp.broadcast_to(m[:, jb:jb + 1], (SEG, 2 * SSD_P)))

    def w_intra(j):
        seg = a[:, j:j + 1] - a_t[j:j + 1, :]
        return cb * jnp.exp(jnp.where(causal, seg, -jnp.inf)) * dt_t[j:j + 1, :]

    ys = []
    for p in range(SSD_HG // 2):
        ja, jb = 2 * p, 2 * p + 1
        xp = xs[:, p * 2 * SSD_P:(p + 1) * 2 * SSD_P]
        y_intra = jnp.where(lane_lo, _dot(w_intra(ja), xp), _dot(w_intra(jb), xp))
        sp = st_ref[p]
        y_inter = _dot_nt(cc, sp) * pair_cols(e_a, ja, jb)
        ds = _dot_tn(xp * pair_cols(w_col, ja, jb), bc)
        rs = jnp.where(sub_lo, jnp.broadcast_to(e_last[:, ja:ja + 1], (2 * SSD_P, SSD_N)),
                       jnp.broadcast_to(e_last[:, jb:jb + 1], (2 * SSD_P, SSD_N)))
        st_ref[p] = sp * rs + ds
        ys.append(y_intra + y_inter + dvec_ref[0, :, p * 2 * SSD_P:(p + 1) * 2 * SSD_P] * xp)
    y = jnp.concatenate(ys, axis=1) * _silu(z_ref[...])
    o_ref[...] = _rms(y, nw_ref[...]).astype(bf16)

    @pl.when(jnp.logical_and(is_p, pos == SEG_PER_PSEQ - 1))
    def _():
        stp_ref[0] = st_ref[...].reshape(stp_ref.shape[1:])

    @pl.when(jnp.logical_not(is_p))
    def _():
        sts_ref[0] = st_ref[...].reshape(sts_ref.shape[1:])


def _ssd(p, conv_prev, conv_w, conv_b, dtb, alog, dvec, norm_w, s0):
    nb = SSD_G * SSD_N
    xo, bo, co = 0, SSD_INNER, SSD_INNER + nb
    st_block = (1, SSD_HG, SSD_P, SSD_N)

    def col(off, width):
        return lambda g, c: (c, (C_XBC + off) // width + g)

    def wcol(off, width):
        return lambda g, c: (0, off // width + g)

    def cpcol(off, width):
        return lambda g, c: (_sample_seq(c), 0, off // width + g)

    conv_b2 = conv_b.reshape(1, -1)
    return pl.pallas_call(
        _ssd_kernel,
        grid=(SSD_G, NSEG),
        in_specs=[pl.BlockSpec((SEG, SSD_GW), col(xo, SSD_GW)),
                  pl.BlockSpec((SEG, SSD_N), col(bo, SSD_N)),
                  pl.BlockSpec((SEG, SSD_N), col(co, SSD_N)),
                  pl.BlockSpec((SEG, SSD_GW), lambda g, c: (c, C_Z // SSD_GW + g)),
                  pl.BlockSpec((SEG, LANE), lambda g, c: (c, C_DT // LANE + g)),
                  pl.BlockSpec((1, CONV_W - 1, SSD_GW), cpcol(xo, SSD_GW)),
                  pl.BlockSpec((1, CONV_W - 1, SSD_N), cpcol(bo, SSD_N)),
                  pl.BlockSpec((1, CONV_W - 1, SSD_N), cpcol(co, SSD_N)),
                  pl.BlockSpec((CONV_W, SSD_GW), wcol(xo, SSD_GW)),
                  pl.BlockSpec((CONV_W, SSD_N), wcol(bo, SSD_N)),
                  pl.BlockSpec((CONV_W, SSD_N), wcol(co, SSD_N)),
                  pl.BlockSpec((1, SSD_GW), wcol(xo, SSD_GW)),
                  pl.BlockSpec((1, SSD_N), wcol(bo, SSD_N)),
                  pl.BlockSpec((1, SSD_N), wcol(co, SSD_N)),
                  pl.BlockSpec((1, 1, LANE), lambda g, c: (g, 0, 0)),
                  pl.BlockSpec((1, 1, LANE), lambda g, c: (g, 0, 0)),
                  pl.BlockSpec((1, 1, SSD_GW), lambda g, c: (g, 0, 0)),
                  pl.BlockSpec((1, SSD_GW), lambda g, c: (0, g)),
                  pl.BlockSpec(st_block, lambda g, c: (_sample_seq(c), g, 0, 0))],
        out_specs=[pl.BlockSpec((SEG, SSD_GW), lambda g, c: (c, g)),
                   pl.BlockSpec(st_block, lambda g, c: (_prompt_seq(c), g, 0, 0)),
                   pl.BlockSpec(st_block, lambda g, c: (_sample_seq(c), g, 0, 0))],
        out_shape=[jax.ShapeDtypeStruct((N_TOK, SSD_INNER), bf16),
                   jax.ShapeDtypeStruct((B_P, SSD_H, SSD_P, SSD_N), f32),
                   jax.ShapeDtypeStruct((B_S, SSD_H, SSD_P, SSD_N), f32)],
        scratch_shapes=[pltpu.VMEM((EXT, SSD_GW), f32), pltpu.VMEM((EXT, SSD_N), f32), pltpu.VMEM((EXT, SSD_N), f32),
                        pltpu.VMEM((SSD_HG // 2, 2 * SSD_P, SSD_N), f32)],
        compiler_params=_cparams(("parallel", "arbitrary")),
        name="ssd_mixer",
    )(p, p, p, p, p, conv_prev, conv_prev, conv_prev, conv_w, conv_w, conv_w, conv_b2, conv_b2, conv_b2,
      dtb, alog, dvec, norm_w.reshape(1, -1), s0)


MIX_TB = 512


def _mix_kernel(oa_ref, yb_ref, gt_ref, x_ref, ga_ref, sc_ref, sh_ref, nw_ref, wa_ref, wb_ref, wo_ref,
                x1_ref, h2_ref, h2t_ref):
    br_a = jnp.dot(oa_ref[...], wa_ref[...], preferred_element_type=f32)
    br_b = jnp.dot(yb_ref[...], wb_ref[...], preferred_element_type=f32)
    g_a = jax.nn.sigmoid(gt_ref[:, 0:D])
    g_b = jax.nn.sigmoid(gt_ref[:, D:2 * D])
    y = _dot(g_a * br_a + g_b * br_b, wo_ref[...])
    for s in range(MIX_TB // SEG):
        r = slice(s * SEG, (s + 1) * SEG)
        x1 = x_ref[r, :] + ga_ref[s:s + 1, :] * y[r, :]
        x1_ref[r, :] = x1
        h2_ref[r, :] = (_rms(x1, nw_ref[...]) * (1.0 + sc_ref[s:s + 1, :]) + sh_ref[s:s + 1, :]).astype(bf16)
    h2t_ref[...] = h2_ref[...].T


def _mix(oa, yb, p, x, mod_seg, norm2_w, wa, wb, wo):
    nseg_b = MIX_TB // SEG
    full = lambda i: (0, 0)
    return pl.pallas_call(
        _mix_kernel,
        grid=(N_TOK // MIX_TB,),
        in_specs=[pl.BlockSpec((MIX_TB, D), lambda i: (i, 0)),
                  pl.BlockSpec((MIX_TB, SSD_INNER), lambda i: (i, 0)),
                  pl.BlockSpec((MIX_TB, 2 * D), lambda i: (i, C_GATE // (2 * D))),
                  pl.BlockSpec((MIX_TB, D), lambda i: (i, 0)),
                  pl.BlockSpec((nseg_b, D), lambda i: (i, 2)),
                  pl.BlockSpec((nseg_b, D), lambda i: (i, 4)),
                  pl.BlockSpec((nseg_b, D), lambda i: (i, 3)),
                  pl.BlockSpec((1, D), full),
                  pl.BlockSpec((D, D), full),
                  pl.BlockSpec((SSD_INNER, D), full),
                  pl.BlockSpec((D, D), full)],
        out_specs=[pl.BlockSpec((MIX_TB, D), lambda i: (i, 0)),
                   pl.BlockSpec((MIX_TB, D), lambda i: (i, 0)),
                   pl.BlockSpec((D, MIX_TB), lambda i: (0, i))],
        out_shape=[jax.ShapeDtypeStruct((N_TOK, D), f32),
                   jax.ShapeDtypeStruct((N_TOK, D), bf16),
                   jax.ShapeDtypeStruct((D, N_TOK), bf16)],
        compiler_params=_cparams(("parallel",)),
        name="mix_out",
    )(oa, yb, p, x, mod_seg, mod_seg, mod_seg, norm2_w.reshape(1, D), wa, wb, wo)


RT_TL = 128
HALF = PEER_DQ // 2


def _top16_rows(x, dst):
    for j in range(PEER_TOPK):
        m = jnp.max(x, axis=0, keepdims=True)
        dst[j:j + 1, :] = m
        x = jnp.where(x == m, -jnp.inf, x)


def _route_kernel(q1_ref, q2_ref, k1_ref, k2_ref, s1_ref, s2_ref, f1_ref, f2_ref, tau_ref, v1s, v2s):
    s1 = _dot3_nt(k1_ref[0], q1_ref[...])
    s2 = _dot3_nt(k2_ref[0], q2_ref[...])
    s1_ref[0] = s1
    s2_ref[0] = s2
    _top16_rows(s1, v1s)
    _top16_rows(s2, v2s)
    row = lax.broadcasted_iota(jnp.int32, (8, RT_TL), 0)
    ninf = jnp.float32(-jnp.inf)
    v2lo = v2s[0:8, :]
    cands = [v1s[0:1, :] + v2lo, v1s[0:1, :] + v2s[8:16, :], v1s[1:2, :] + v2lo]
    for i in range(2, 8):
        cands.append(jnp.where(row < PEER_TOPK // (i + 1), v1s[i:i + 1, :] + v2lo, ninf))
    cands.append(v1s[8:16, :] + v2s[0:1, :])
    top = None
    zsum = None
    tau = None
    for j in range(PEER_TOPK):
        m = cands[0]
        for cnd in cands[1:]:
            m = jnp.maximum(m, cnd)
        m = jnp.max(m, axis=0, keepdims=True)
        if j == 0:
            top = m
            zsum = jnp.ones_like(m)
        else:
            zsum = zsum + jnp.exp(m - top)
        tau = m
        cands = [jnp.where(cnd == m, ninf, cnd) for cnd in cands]
    tau_ref[0] = jnp.broadcast_to(tau, (8, RT_TL))
    f1_ref[0] = jnp.exp(s1 - v1s[0:1, :]) / zsum
    f2_ref[0] = jnp.exp(s2 - v2s[0:1, :])


def _route(q, keys1, keys2):
    tile = pl.BlockSpec((1, PEER_NK, RT_TL), lambda i, h: (h, 0, i))
    big = jax.ShapeDtypeStruct((PEER_H, PEER_NK, N_TOK), f32)
    return pl.pallas_call(
        _route_kernel,
        grid=(N_TOK // RT_TL, PEER_H),
        in_specs=[pl.BlockSpec((RT_TL, HALF), lambda i, h: (i, 2 * h)),
                  pl.BlockSpec((RT_TL, HALF), lambda i, h: (i, 2 * h + 1)),
                  pl.BlockSpec((1, PEER_NK, HALF), lambda i, h: (h, 0, 0)),
                  pl.BlockSpec((1, PEER_NK, HALF), lambda i, h: (h, 0, 0))],
        out_specs=[tile, tile, tile, tile, pl.BlockSpec((1, 8, RT_TL), lambda i, h: (h, 0, i))],
        out_shape=[big, big, big, big, jax.ShapeDtypeStruct((PEER_H, 8, N_TOK), f32)],
        scratch_shapes=[pltpu.VMEM((PEER_TOPK, RT_TL), f32), pltpu.VMEM((PEER_TOPK, RT_TL), f32)],
        compiler_params=_cparams(("parallel", "parallel")),
        name="peer_route",
    )(q, q, keys1, keys2)


PE_TB = 512
PE_NA = 4
PE_CH = PE_NA * PEER_NK
PE_NCH = PEER_NK * PEER_NK // PE_CH
PE_BH = 32
SQRT_HALF = 0.7071067811865476
MXU = 256


def _peer_kernel(*refs, final):
    if final:
        (u0_ref, u_ref, vt_ref, xt_ref, s1_ref, f1_ref, s2_ref, f2_ref, tau_ref, x1_ref, ga_ref, fw_ref,
         o_ref, acc_ref, pw_ref, gel_ref, rs1_ref, rf1_ref) = refs
    else:
        (u0_ref, u_ref, vt_ref, xt_ref, s1_ref, f1_ref, s2_ref, f2_ref, tau_ref, x1_ref, ga_ref,
         o_ref, acc_ref, pw_ref, gel_ref, rs1_ref, rf1_ref) = refs
    j = pl.program_id(1)

    @pl.when(j == 0)
    def _():
        acc_ref[...] = jnp.zeros_like(acc_ref)
        pw_ref[1] = jnp.zeros(pw_ref.shape[1:], bf16)
        act0 = jnp.dot(u0_ref[0], xt_ref[...], preferred_element_type=f32)
        gel_ref[0] = 0.5 * act0 * (1.0 + lax.erf(act0 * SQRT_HALF))

    cur = j % 2
    prev = 1 - cur
    jc = jnp.minimum(j, PE_NCH - 1)

    for i in range(PE_NA):
        for h in range(PEER_H):
            a = jc * PE_NA + i
            rs1_ref[i * PEER_H + h] = jnp.broadcast_to(s1_ref[h, pl.ds(a, 1), :], (8, PE_TB))
            rf1_ref[i * PEER_H + h] = jnp.broadcast_to(f1_ref[h, pl.ds(a, 1), :], (8, PE_TB))

    nsub = PE_BH // 8
    n_bh = PEER_NK // PE_BH

    def weight_tile(t):
        l0 = pl.multiple_of((t // n_bh) * LANE, LANE)
        r0 = pl.multiple_of((t % n_bh) * PE_BH, PE_BH)
        lanes = pl.ds(l0, LANE)
        ws = [jnp.zeros((nsub, 8, LANE), f32) for _ in range(PE_NA)]
        for h in range(PEER_H):
            s2t = s2_ref[h, pl.ds(r0, PE_BH), lanes].reshape(nsub, 8, LANE)
            f2t = f2_ref[h, pl.ds(r0, PE_BH), lanes].reshape(nsub, 8, LANE)
            tr = tau_ref[h, :, lanes][None]
            for i in range(PE_NA):
                sel = (s2t + rs1_ref[i * PEER_H + h, :, lanes][None]) >= tr
                ws[i] = ws[i] + jnp.where(sel, f2t * rf1_ref[i * PEER_H + h, :, lanes][None], 0.0)
        for i in range(PE_NA):
            er = pl.ds(pl.multiple_of(i * PEER_NK + r0, PE_BH), PE_BH)
            pw_ref[cur, er, lanes] = (ws[i].reshape(PE_BH, LANE) * gel_ref[cur, er, lanes]).astype(bf16)

    n_half = PE_TB // MXU
    n_tiles = (PE_TB // LANE) * n_bh
    tiles_per_piece = n_tiles // (2 * n_half)

    def stage_act(p, carry):
        n0 = pl.multiple_of(p * MXU, MXU)
        act = jnp.dot(u_ref[0], xt_ref[:, pl.ds(n0, MXU)], preferred_element_type=f32)
        gel_ref[prev, :, pl.ds(n0, MXU)] = 0.5 * act * (1.0 + lax.erf(act * SQRT_HALF))
        for q in range(tiles_per_piece):
            weight_tile(p * tiles_per_piece + q)
        return carry

    def stage_fold(p, carry):
        n0 = pl.multiple_of(p * MXU, MXU)
        acc_ref[:, pl.ds(n0, MXU)] += jnp.dot(vt_ref[0], pw_ref[prev, :, pl.ds(n0, MXU)],
                                              preferred_element_type=f32)
        for q in range(tiles_per_piece):
            weight_tile((n_half + p) * tiles_per_piece + q)
        return carry

    lax.fori_loop(0, n_half, stage_act, 0)
    lax.fori_loop(0, n_half, stage_fold, 0)

    @pl.when(j == PE_NCH)
    def _():
        out = acc_ref[...].T
        for s in range(PE_TB // SEG):
            r = slice(s * SEG, (s + 1) * SEG)
            x2 = x1_ref[r, :] + ga_ref[s:s + 1, :] * out[r, :]
            o_ref[r, :] = _rms(x2, fw_ref[...]) if final else x2


def _peer(u_c, vt_c, h2t, s1t, f1t, s2t, f2t, tau, x1, mod_seg, final_w):
    final = final_w is not None
    nseg_b = PE_TB // SEG
    rt = pl.BlockSpec((PEER_H, PEER_NK, PE_TB), lambda i, j: (0, 0, i))
    in_specs = [pl.BlockSpec((1, PE_CH, D), lambda i, j: (0, 0, 0)),
                pl.BlockSpec((1, PE_CH, D), lambda i, j: (jnp.minimum(j + 1, PE_NCH - 1), 0, 0)),
                pl.BlockSpec((1, D, PE_CH), lambda i, j: (jnp.maximum(j - 1, 0), 0, 0)),
                pl.BlockSpec((D, PE_TB), lambda i, j: (0, i)),
                rt, rt, rt, rt,
                pl.BlockSpec((PEER_H, 8, PE_TB), lambda i, j: (0, 0, i)),
                pl.BlockSpec((PE_TB, D), lambda i, j: (i, 0)),
                pl.BlockSpec((nseg_b, D), lambda i, j: (i, 5))]
    args = [u_c, u_c, vt_c, h2t, s1t, f1t, s2t, f2t, tau, x1, mod_seg]
    if final:
        in_specs.append(pl.BlockSpec((1, D), lambda i, j: (0, 0)))
        args.append(final_w.reshape(1, D))
    return pl.pallas_call(
        functools.partial(_peer_kernel, final=final),
        grid=(N_TOK // PE_TB, PE_NCH + 1),
        in_specs=in_specs,
        out_specs=pl.BlockSpec((PE_TB, D), lambda i, j: (i, 0)),
        out_shape=jax.ShapeDtypeStruct((N_TOK, D), f32),
        scratch_shapes=[pltpu.VMEM((D, PE_TB), f32), pltpu.VMEM((2, PE_CH, PE_TB), bf16),
                        pltpu.VMEM((2, PE_CH, PE_TB), f32),
                        pltpu.VMEM((PE_NA * PEER_H, 8, PE_TB), f32), pltpu.VMEM((PE_NA * PEER_H, 8, PE_TB), f32)],
        compiler_params=_cparams(("parallel", "arbitrary")),
        name="peer_experts_final" if final else "peer_experts",
    )(*args)


def _reorder_w_in(w):
    o = np.cumsum([0, 512, 512, 1024, 1024, GLA_RANK, SSD_INNER, SSD_CONV_DIM, SSD_H, 2 * D])
    q_k_v_go = w[:, o[0]:o[4]]
    gk_low = w[:, o[4]:o[5]]
    z = w[:, o[5]:o[6]]
    xbc = w[:, o[6]:o[7]]
    dt = w[:, o[7]:o[8]]
    gates = w[:, o[8]:o[9]]
    pad = lambda m: jnp.pad(m, ((0, 0), (0, LANE - m.shape[1])))
    dts = [pad(dt[:, g * SSD_HG:(g + 1) * SSD_HG]) for g in range(SSD_G)]
    return jnp.concatenate([q_k_v_go, z, xbc, gates, pad(gk_low)] + dts, axis=1).astype(bf16)


def kernel(x_prompt, x_sample, state_gla, state_ssd, state_conv, c_prompt, c_sample, w_ada, b_ada, norm1_w, w_in, gla_gk_w2, gla_gk_b, gla_norm_w, gla_proj, ssd_conv_w, ssd_conv_b, ssd_dt_bias, ssd_A_log, ssd_D, ssd_norm_w, ssd_proj, w_out, norm2_w, peer_wq, peer_keys1, peer_keys2, peer_u, peer_v, final_norm_w):
    x = jnp.concatenate([x_prompt.reshape(N_P, D), x_sample.reshape(N_S, D)], axis=0)
    c_all = jnp.concatenate([c_prompt, c_sample], axis=0)
    mod = _ada(c_all, w_ada, b_ada)
    seg2seq = np.concatenate([np.repeat(np.arange(B_P), SEG_PER_PSEQ), B_P + np.arange(B_S)])

    pad_lane = lambda m: jnp.pad(m, ((0, 0), (0, LANE - m.shape[1])))
    gla_st, ssd_st, conv_st = [], [], []
    for l in range(DEPTH):
        mod_seg = mod[l][seg2seq]
        h = _normmod(x, norm1_w[l], mod_seg, 1, 0)
        p = _mm(h, _reorder_w_in(w_in[l]), 512, 2176, "in_proj")

        w2pad = jnp.pad(gla_gk_w2[l], ((0, LANE - GLA_RANK), (0, 0)))
        oa, gla_p, gla_s = _gla(p, w2pad, gla_gk_b[l], gla_norm_w[l], state_gla[l])

        dtb = pad_lane(ssd_dt_bias[l].reshape(SSD_G, SSD_HG)).reshape(SSD_G, 1, LANE)
        alog = pad_lane(ssd_A_log[l].reshape(SSD_G, SSD_HG)).reshape(SSD_G, 1, LANE)
        dvec = jnp.repeat(ssd_D[l], SSD_P).reshape(SSD_G, 1, SSD_GW)
        yb, ssd_p, ssd_s = _ssd(p, state_conv[l], ssd_conv_w[l], ssd_conv_b[l], dtb, alog, dvec,
                                ssd_norm_w[l], state_ssd[l])

        x1, h2, h2t = _mix(oa, yb, p, x, mod_seg, norm2_w[l], gla_proj[l].astype(bf16),
                           ssd_proj[l].astype(bf16), w_out[l].astype(bf16))
        q = _mm(h2, peer_wq[l].astype(bf16), 512, PEER_H * PEER_DQ, "peer_query")
        s1t, s2t, f1t, f2t, tau = _route(q, peer_keys1[l], peer_keys2[l])
        u_c = peer_u[l].astype(bf16).reshape(PE_NCH, PE_CH, D)
        vt_c = peer_v[l].astype(bf16).reshape(PE_NCH, PE_CH, D).transpose(0, 2, 1)
        x = _peer(u_c, vt_c, h2t, s1t, f1t, s2t, f2t, tau, x1, mod_seg,
                  final_norm_w if l == DEPTH - 1 else None)

        xbc = p[:, C_XBC:C_XBC + SSD_CONV_DIM]
        conv_p = xbc[:N_P].reshape(B_P, T_P, SSD_CONV_DIM)[:, T_P - (CONV_W - 1):]
        conv_s = xbc[N_P:].reshape(B_S, T_S, SSD_CONV_DIM)[:, T_S - (CONV_W - 1):]
        gla_st.append((gla_p, gla_s))
        ssd_st.append((ssd_p, ssd_s))
        conv_st.append((conv_p, conv_s))

    y_prompt = x[:N_P].reshape(B_P, T_P, D)
    y_sample = x[N_P:].reshape(B_S, T_S, D)
    stack = lambda pairs, k: jnp.stack([pr[k] for pr in pairs])
    return (y_prompt, y_sample, stack(gla_st, 0), stack(ssd_st, 0), stack(conv_st, 0),
            stack(gla_st, 1), stack(ssd_st, 1), stack(conv_st, 1))
```

```python
import functools

import jax
import jax.numpy as jnp
import numpy as np
from jax import lax
from jax.experimental import pallas as pl
from jax.experimental.pallas import tpu as pltpu

f32 = jnp.float32
bf16 = jnp.bfloat16

D = 1024
DEPTH = 2
B_P, T_P = 8, 2048
B_S, T_S = 32, 64
N_P = B_P * T_P
N_S = B_S * T_S
N_TOK = N_P + N_S
SEG = 64
NSEG = N_TOK // SEG
NSEG_P = N_P // SEG
SEG_PER_PSEQ = T_P // SEG
EPS = 1e-6

GLA_H, GLA_DK, GLA_DV = 4, 128, 256
GLA_RANK = 16
GLA_GATE_NORM = 16.0
SSD_INNER = 2048
SSD_P = 64
SSD_H = 32
SSD_G = 4
SSD_N = 128
SSD_HG = SSD_H // SSD_G
SSD_GW = SSD_INNER // SSD_G
CONV_W = 4
SSD_CONV_DIM = SSD_INNER + 2 * SSD_G * SSD_N

PEER_H = 8
PEER_NK = 128
PEER_DQ = 256
PEER_TOPK = 16

C_Q, C_K, C_V, C_GO = 0, 512, 1024, 2048
C_XBC, C_Z, C_GATE = 3072, 6144, 8192
C_DT = 10240
C_GKLOW = C_DT + SSD_G * 128
P_COLS = C_GKLOW + 128

LANE = 128
VMEM_LIMIT = 56 * 1024 * 1024


def _cparams(sem):
    return pltpu.CompilerParams(dimension_semantics=sem, vmem_limit_bytes=VMEM_LIMIT)


def _dot(a, b):
    return jnp.dot(a.astype(bf16), b.astype(bf16), preferred_element_type=f32)


def _dot_nt(a, b):
    return lax.dot_general(a.astype(bf16), b.astype(bf16), (((1,), (1,)), ((), ())), preferred_element_type=f32)


def _dot_tn(a, b):
    return lax.dot_general(a.astype(bf16), b.astype(bf16), (((0,), (0,)), ((), ())), preferred_element_type=f32)


def _split3(x):
    hi = x.astype(bf16)
    r = x - hi.astype(f32)
    mid = r.astype(bf16)
    lo = (r - mid.astype(f32)).astype(bf16)
    return hi, mid, lo


def _dot3(a, b):
    ah, am, _ = _split3(a)
    bh, bm, _ = _split3(b)
    d = functools.partial(jnp.dot, preferred_element_type=f32)
    return d(ah, bh) + (d(ah, bm) + d(am, bh))


def _dot3_nt(a, b):
    ah, am, _ = _split3(a)
    bh, bm, _ = _split3(b)
    d = functools.partial(lax.dot_general, dimension_numbers=(((1,), (1,)), ((), ())), preferred_element_type=f32)
    return d(ah, bh) + (d(ah, bm) + d(am, bh))


def _cumsum_rows(x):
    n = x.shape[0]
    tri = (lax.broadcasted_iota(jnp.int32, (n, n), 0) >= lax.broadcasted_iota(jnp.int32, (n, n), 1)).astype(bf16)
    hi, mid, lo = _split3(x)
    d = functools.partial(jnp.dot, preferred_element_type=f32)
    return d(tri, hi) + (d(tri, mid) + d(tri, lo))


def _silu(x):
    return x * jax.nn.sigmoid(x)


def _softplus(x):
    return jnp.maximum(x, 0.0) + jnp.log1p(jnp.exp(-jnp.abs(x)))


def _rms(x, w):
    return x * lax.rsqrt(jnp.mean(x * x, axis=-1, keepdims=True) + EPS) * w


def _ada_kernel(c_ref, w_ref, b_ref, o_ref):
    o_ref[0] = _dot3(_silu(c_ref[...]), w_ref[0]) + b_ref[0]


def _ada(c_all, w_ada, b_ada):
    nb = c_all.shape[0]
    tn = 1536
    return pl.pallas_call(
        _ada_kernel,
        grid=(DEPTH, 6 * D // tn),
        in_specs=[pl.BlockSpec((nb, D), lambda l, j: (0, 0)),
                  pl.BlockSpec((1, D, tn), lambda l, j: (l, 0, j)),
                  pl.BlockSpec((1, 1, tn), lambda l, j: (l, 0, j))],
        out_specs=pl.BlockSpec((1, nb, tn), lambda l, j: (l, 0, j)),
        out_shape=jax.ShapeDtypeStruct((DEPTH, nb, 6 * D), f32),
        compiler_params=_cparams(("parallel", "parallel")),
        name="ada_mod",
    )(c_all, w_ada, b_ada.reshape(DEPTH, 1, 6 * D))


NM_TB = 512


def _normmod_kernel(x_ref, w_ref, sc_ref, sh_ref, o_ref):
    for s in range(NM_TB // SEG):
        r = slice(s * SEG, (s + 1) * SEG)
        y = _rms(x_ref[r, :], w_ref[...])
        o_ref[r, :] = (y * (1.0 + sc_ref[s:s + 1, :]) + sh_ref[s:s + 1, :]).astype(bf16)


def _normmod(x, w, mod_seg, sc_col, sh_col):
    nseg_b = NM_TB // SEG
    return pl.pallas_call(
        _normmod_kernel,
        grid=(N_TOK // NM_TB,),
        in_specs=[pl.BlockSpec((NM_TB, D), lambda i: (i, 0)),
                  pl.BlockSpec((1, D), lambda i: (0, 0)),
                  pl.BlockSpec((nseg_b, D), lambda i: (i, sc_col)),
                  pl.BlockSpec((nseg_b, D), lambda i: (i, sh_col))],
        out_specs=pl.BlockSpec((NM_TB, D), lambda i: (i, 0)),
        out_shape=jax.ShapeDtypeStruct((N_TOK, D), bf16),
        compiler_params=_cparams(("parallel",)),
        name="norm_mod",
    )(x, w.reshape(1, D), mod_seg, mod_seg)


def _mm_kernel(a_ref, b_ref, o_ref):
    o_ref[...] = jnp.dot(a_ref[...], b_ref[...], preferred_element_type=f32)


def _mm(a, b, tm, tn, name):
    m, k = a.shape
    n = b.shape[1]
    return pl.pallas_call(
        _mm_kernel,
        grid=(n // tn, m // tm),
        in_specs=[pl.BlockSpec((tm, k), lambda j, i: (i, 0)),
                  pl.BlockSpec((k, tn), lambda j, i: (0, j))],
        out_specs=pl.BlockSpec((tm, tn), lambda j, i: (i, j)),
        out_shape=jax.ShapeDtypeStruct((m, n), f32),
        compiler_params=_cparams(("parallel", "parallel")),
        name=name,
    )(a, b)


def _is_prompt(c):
    return c < NSEG_P


def _sample_seq(c):
    return jnp.maximum(c - NSEG_P, 0)


def _prompt_seq(c):
    return jnp.minimum(c // SEG_PER_PSEQ, B_P - 1)


def _gla_kernel(q_ref, k_ref, v_ref, g_ref, low_ref, w2_ref, b2_ref, nw_ref, s0_ref,
                o_ref, stp_ref, sts_ref, st_ref):
    c = pl.program_id(0)
    is_p = _is_prompt(c)
    pos = c % SEG_PER_PSEQ

    @pl.when(jnp.logical_and(is_p, pos == 0))
    def _():
        st_ref[...] = jnp.zeros_like(st_ref)

    @pl.when(jnp.logical_not(is_p))
    def _():
        for h in range(GLA_H):
            st_ref[h] = s0_ref[0, h].T

    pre = _dot3(low_ref[...], w2_ref[...]) + b2_ref[...]
    gk = -_softplus(-pre) / GLA_GATE_NORM
    b = _cumsum_rows(gk)
    b_last = b[SEG - 1:SEG, :]
    e_b = jnp.exp(b)
    e_nb = jnp.exp(-b)
    e_end = jnp.exp(b_last - b)
    e_last = jnp.exp(b_last)
    causal = lax.broadcasted_iota(jnp.int32, (SEG, SEG), 0) >= lax.broadcasted_iota(jnp.int32, (SEG, SEG), 1)
    for h in range(GLA_H):
        ks = slice(h * GLA_DK, (h + 1) * GLA_DK)
        vs = slice(h * GLA_DV, (h + 1) * GLA_DV)
        k = k_ref[:, ks]
        q_t = (q_ref[:, ks] * (GLA_DK ** -0.5)) * e_b[:, ks]
        att = jnp.where(causal, _dot_nt(q_t, k * e_nb[:, ks]), 0.0)
        v = v_ref[:, vs]
        st = st_ref[h]
        o = _dot(att, v) + _dot_nt(q_t, st)
        st_ref[h] = st * e_last[:, ks] + _dot_tn(v, k * e_end[:, ks])
        o_ref[:, vs] = (_rms(o, nw_ref[...]) * _silu(g_ref[:, vs])).astype(bf16)

    @pl.when(jnp.logical_and(is_p, pos == SEG_PER_PSEQ - 1))
    def _():
        for h in range(GLA_H):
            stp_ref[0, h] = st_ref[h].T

    @pl.when(jnp.logical_not(is_p))
    def _():
        for h in range(GLA_H):
            sts_ref[0, h] = st_ref[h].T


def _gla(p, w2pad, b2, norm_w, s0):
    st_block = (1, GLA_H, GLA_DK, GLA_DV)
    kd, vd = GLA_H * GLA_DK, GLA_H * GLA_DV
    return pl.pallas_call(
        _gla_kernel,
        grid=(NSEG,),
        in_specs=[pl.BlockSpec((SEG, kd), lambda c: (c, C_Q // kd)),
                  pl.BlockSpec((SEG, kd), lambda c: (c, C_K // kd)),
                  pl.BlockSpec((SEG, vd), lambda c: (c, C_V // vd)),
                  pl.BlockSpec((SEG, vd), lambda c: (c, C_GO // vd)),
                  pl.BlockSpec((SEG, LANE), lambda c: (c, C_GKLOW // LANE)),
                  pl.BlockSpec((LANE, kd), lambda c: (0, 0)),
                  pl.BlockSpec((1, kd), lambda c: (0, 0)),
                  pl.BlockSpec((1, GLA_DV), lambda c: (0, 0)),
                  pl.BlockSpec(st_block, lambda c: (_sample_seq(c), 0, 0, 0))],
        out_specs=[pl.BlockSpec((SEG, vd), lambda c: (c, 0)),
                   pl.BlockSpec(st_block, lambda c: (_prompt_seq(c), 0, 0, 0)),
                   pl.BlockSpec(st_block, lambda c: (_sample_seq(c), 0, 0, 0))],
        out_shape=[jax.ShapeDtypeStruct((N_TOK, vd), bf16),
                   jax.ShapeDtypeStruct((B_P, GLA_H, GLA_DK, GLA_DV), f32),
                   jax.ShapeDtypeStruct((B_S, GLA_H, GLA_DK, GLA_DV), f32)],
        scratch_shapes=[pltpu.VMEM((GLA_H, GLA_DV, GLA_DK), f32)],
        compiler_params=_cparams(("arbitrary",)),
        name="gla_mixer",
    )(p, p, p, p, p, w2pad, b2.reshape(1, -1), norm_w.reshape(1, -1), s0)


EXT = SEG + 8


def _ssd_kernel(xbc_ref, z_ref, dt_ref, cp_ref, cw_ref, cb_ref, dtb_ref, alog_ref, dvec_ref, nw_ref, s0_ref,
                o_ref, stp_ref, sts_ref, ext, st_ref):
    c = pl.program_id(0)
    is_p = _is_prompt(c)
    pos = c % SEG_PER_PSEQ

    @pl.when(jnp.logical_and(is_p, pos == 0))
    def _():
        ext[0:8, :] = jnp.zeros((8, SSD_CONV_DIM), f32)
        st_ref[...] = jnp.zeros_like(st_ref)

    @pl.when(jnp.logical_not(is_p))
    def _():
        ext[0:8, :] = jnp.zeros((8, SSD_CONV_DIM), f32)
        ext[8 - (CONV_W - 1):8, :] = cp_ref[0]
        st_ref[...] = s0_ref[0].reshape(st_ref.shape)

    ext[8:EXT, :] = xbc_ref[...]
    acc = cb_ref[...] + ext[pl.ds(8 - (CONV_W - 1), SEG), :] * cw_ref[0:1, :]
    for i in range(1, CONV_W):
        acc = acc + ext[pl.ds(8 - (CONV_W - 1) + i, SEG), :] * cw_ref[i:i + 1, :]
    ext[0:8, :] = ext[SEG:EXT, :]
    conv = _silu(acc)

    dt = _softplus(dt_ref[...] + dtb_ref[...])
    a = _cumsum_rows(dt * (-jnp.exp(alog_ref[...])))
    a_last = a[SEG - 1:SEG, :]
    e_a = jnp.exp(a)
    w_col = dt * jnp.exp(a_last - a)
    e_last = jnp.exp(a_last)
    causal = lax.broadcasted_iota(jnp.int32, (SEG, SEG), 0) >= lax.broadcasted_iota(jnp.int32, (SEG, SEG), 1)
    lane_lo = lax.broadcasted_iota(jnp.int32, (SEG, 2 * SSD_P), 1) < SSD_P
    sub_lo = lax.broadcasted_iota(jnp.int32, (2 * SSD_P, SSD_N), 0) < SSD_P
    nbc = SSD_G * SSD_N

    def pair_cols(m, ja, jb):
        return jnp.where(lane_lo, jnp.broadcast_to(m[:, ja:ja + 1], (SEG, 2 * SSD_P)),
                         jnp.broadcast_to(m[:, jb:jb + 1], (SEG, 2 * SSD_P)))

    for g in range(SSD_G):
        gl = slice(g * LANE, (g + 1) * LANE)
        a_g, dt_g, ea_g, wc_g, el_g = a[:, gl], dt[:, gl], e_a[:, gl], w_col[:, gl], e_last[:, gl]
        a_t = a_g.T
        dt_t = dt_g.T
        bc = conv[:, SSD_INNER + g * SSD_N:SSD_INNER + (g + 1) * SSD_N]
        cc = conv[:, SSD_INNER + nbc + g * SSD_N:SSD_INNER + nbc + (g + 1) * SSD_N]
        cbm = _dot_nt(cc, bc)

        def w_intra(j):
            seg = a_g[:, j:j + 1] - a_t[j:j + 1, :]
            return cbm * jnp.exp(jnp.where(causal, seg, -jnp.inf)) * dt_t[j:j + 1, :]

        ys = []
        for p in range(SSD_HG // 2):
            ja, jb = 2 * p, 2 * p + 1
            pp = g * (SSD_HG // 2) + p
            xp = conv[:, pp * 2 * SSD_P:(pp + 1) * 2 * SSD_P]
            y_intra = jnp.where(lane_lo, _dot(w_intra(ja), xp), _dot(w_intra(jb), xp))
            sp = st_ref[pp]
            y_inter = _dot_nt(cc, sp) * pair_cols(ea_g, ja, jb)
            ds = _dot_tn(xp * pair_cols(wc_g, ja, jb), bc)
            rs = jnp.where(sub_lo, jnp.broadcast_to(el_g[:, ja:ja + 1], (2 * SSD_P, SSD_N)),
                           jnp.broadcast_to(el_g[:, jb:jb + 1], (2 * SSD_P, SSD_N)))
            st_ref[pp] = sp * rs + ds
            ys.append(y_intra + y_inter + dvec_ref[:, pp * 2 * SSD_P:(pp + 1) * 2 * SSD_P] * xp)
        gw = slice(g * SSD_GW, (g + 1) * SSD_GW)
        y = jnp.concatenate(ys, axis=1) * _silu(z_ref[:, gw])
        o_ref[:, gw] = _rms(y, nw_ref[:, gw]).astype(bf16)

    @pl.when(jnp.logical_and(is_p, pos == SEG_PER_PSEQ - 1))
    def _():
        stp_ref[0] = st_ref[...].reshape(stp_ref.shape[1:])

    @pl.when(jnp.logical_not(is_p))
    def _():
        sts_ref[0] = st_ref[...].reshape(sts_ref.shape[1:])


def _ssd(p, conv_prev, conv_w, conv_b, dtb, alog, dvec, norm_w, s0):
    st_block = (1, SSD_H, SSD_P, SSD_N)
    gl = SSD_G * LANE
    full = lambda c: (0, 0)
    return pl.pallas_call(
        _ssd_kernel,
        grid=(NSEG,),
        in_specs=[pl.BlockSpec((SEG, SSD_CONV_DIM), lambda c: (c, C_XBC // SSD_CONV_DIM)),
                  pl.BlockSpec((SEG, SSD_INNER), lambda c: (c, C_Z // SSD_INNER)),
                  pl.BlockSpec((SEG, gl), lambda c: (c, C_DT // gl)),
                  pl.BlockSpec((1, CONV_W - 1, SSD_CONV_DIM), lambda c: (_sample_seq(c), 0, 0)),
                  pl.BlockSpec((CONV_W, SSD_CONV_DIM), full),
                  pl.BlockSpec((1, SSD_CONV_DIM), full),
                  pl.BlockSpec((1, gl), full),
                  pl.BlockSpec((1, gl), full),
                  pl.BlockSpec((1, SSD_INNER), full),
                  pl.BlockSpec((1, SSD_INNER), full),
                  pl.BlockSpec(st_block, lambda c: (_sample_seq(c), 0, 0, 0))],
        out_specs=[pl.BlockSpec((SEG, SSD_INNER), lambda c: (c, 0)),
                   pl.BlockSpec(st_block, lambda c: (_prompt_seq(c), 0, 0, 0)),
                   pl.BlockSpec(st_block, lambda c: (_sample_seq(c), 0, 0, 0))],
        out_shape=[jax.ShapeDtypeStruct((N_TOK, SSD_INNER), bf16),
                   jax.ShapeDtypeStruct((B_P, SSD_H, SSD_P, SSD_N), f32),
                   jax.ShapeDtypeStruct((B_S, SSD_H, SSD_P, SSD_N), f32)],
        scratch_shapes=[pltpu.VMEM((EXT, SSD_CONV_DIM), f32),
                        pltpu.VMEM((SSD_H // 2, 2 * SSD_P, SSD_N), f32)],
        compiler_params=_cparams(("arbitrary",)),
        name="ssd_mixer",
    )(p, p, p, conv_prev, conv_w, conv_b.reshape(1, -1), dtb, alog, dvec, norm_w.reshape(1, -1), s0)


MIX_TB = 512


def _mix_kernel(oa_ref, yb_ref, gt_ref, x_ref, ga_ref, sc_ref, sh_ref, nw_ref, wa_ref, wb_ref, wo_ref,
                x1_ref, h2_ref, h2t_ref):
    br_a = jnp.dot(oa_ref[...], wa_ref[...], preferred_element_type=f32)
    br_b = jnp.dot(yb_ref[...], wb_ref[...], preferred_element_type=f32)
    g_a = jax.nn.sigmoid(gt_ref[:, 0:D])
    g_b = jax.nn.sigmoid(gt_ref[:, D:2 * D])
    y = _dot(g_a * br_a + g_b * br_b, wo_ref[...])
    for s in range(MIX_TB // SEG):
        r = slice(s * SEG, (s + 1) * SEG)
        x1 = x_ref[r, :] + ga_ref[s:s + 1, :] * y[r, :]
        x1_ref[r, :] = x1
        h2_ref[r, :] = (_rms(x1, nw_ref[...]) * (1.0 + sc_ref[s:s + 1, :]) + sh_ref[s:s + 1, :]).astype(bf16)
    h2t_ref[...] = h2_ref[...].T


def _mix(oa, yb, p, x, mod_seg, norm2_w, wa, wb, wo):
    nseg_b = MIX_TB // SEG
    full = lambda i: (0, 0)
    return pl.pallas_call(
        _mix_kernel,
        grid=(N_TOK // MIX_TB,),
        in_specs=[pl.BlockSpec((MIX_TB, D), lambda i: (i, 0)),
                  pl.BlockSpec((MIX_TB, SSD_INNER), lambda i: (i, 0)),
                  pl.BlockSpec((MIX_TB, 2 * D), lambda i: (i, C_GATE // (2 * D))),
                  pl.BlockSpec((MIX_TB, D), lambda i: (i, 0)),
                  pl.BlockSpec((nseg_b, D), lambda i: (i, 2)),
                  pl.BlockSpec((nseg_b, D), lambda i: (i, 4)),
                  pl.BlockSpec((nseg_b, D), lambda i: (i, 3)),
                  pl.BlockSpec((1, D), full),
                  pl.BlockSpec((D, D), full),
                  pl.BlockSpec((SSD_INNER, D), full),
                  pl.BlockSpec((D, D), full)],
        out_specs=[pl.BlockSpec((MIX_TB, D), lambda i: (i, 0)),
                   pl.BlockSpec((MIX_TB, D), lambda i: (i, 0)),
                   pl.BlockSpec((D, MIX_TB), lambda i: (0, i))],
        out_shape=[jax.ShapeDtypeStruct((N_TOK, D), f32),
                   jax.ShapeDtypeStruct((N_TOK, D), bf16),
                   jax.ShapeDtypeStruct((D, N_TOK), bf16)],
        compiler_params=_cparams(("parallel",)),
        name="mix_out",
    )(oa, yb, p, x, mod_seg, mod_seg, mod_seg, norm2_w.reshape(1, D), wa, wb, wo)


RT_TL = 128
HALF = PEER_DQ // 2


def _top16_rows(x, dst):
    for j in range(PEER_TOPK):
        m = jnp.max(x, axis=0, keepdims=True)
        dst[j:j + 1, :] = m
        x = jnp.where(x == m, -jnp.inf, x)


def _route_kernel(q1_ref, q2_ref, k1_ref, k2_ref, s1_ref, s2_ref, f1_ref, f2_ref, tau_ref, v1s, v2s):
    s1 = _dot3_nt(k1_ref[0], q1_ref[...])
    s2 = _dot3_nt(k2_ref[0], q2_ref[...])
    s1_ref[0] = s1
    s2_ref[0] = s2
    _top16_rows(s1, v1s)
    _top16_rows(s2, v2s)
    row = lax.broadcasted_iota(jnp.int32, (8, RT_TL), 0)
    ninf = jnp.float32(-jnp.inf)
    v2lo = v2s[0:8, :]
    cands = [v1s[0:1, :] + v2lo, v1s[0:1, :] + v2s[8:16, :], v1s[1:2, :] + v2lo]
    for i in range(2, 8):
        cands.append(jnp.where(row < PEER_TOPK // (i + 1), v1s[i:i + 1, :] + v2lo, ninf))
    cands.append(v1s[8:16, :] + v2s[0:1, :])
    top = None
    zsum = None
    tau = None
    for j in range(PEER_TOPK):
        m = cands[0]
        for cnd in cands[1:]:
            m = jnp.maximum(m, cnd)
        m = jnp.max(m, axis=0, keepdims=True)
        if j == 0:
            top = m
            zsum = jnp.ones_like(m)
        else:
            zsum = zsum + jnp.exp(m - top)
        tau = m
        cands = [jnp.where(cnd == m, ninf, cnd) for cnd in cands]
    tau_ref[0] = jnp.broadcast_to(tau, (8, RT_TL))
    f1_ref[0] = jnp.exp(s1 - v1s[0:1, :]) / zsum
    f2_ref[0] = jnp.exp(s2 - v2s[0:1, :])


def _route(q, keys1, keys2):
    tile = pl.BlockSpec((1, PEER_NK, RT_TL), lambda i, h: (h, 0, i))
    big = jax.ShapeDtypeStruct((PEER_H, PEER_NK, N_TOK), f32)
    return pl.pallas_call(
        _route_kernel,
        grid=(N_TOK // RT_TL, PEER_H),
        in_specs=[pl.BlockSpec((RT_TL, HALF), lambda i, h: (i, 2 * h)),
                  pl.BlockSpec((RT_TL, HALF), lambda i, h: (i, 2 * h + 1)),
                  pl.BlockSpec((1, PEER_NK, HALF), lambda i, h: (h, 0, 0)),
                  pl.BlockSpec((1, PEER_NK, HALF), lambda i, h: (h, 0, 0))],
        out_specs=[tile, tile, tile, tile, pl.BlockSpec((1, 8, RT_TL), lambda i, h: (h, 0, i))],
        out_shape=[big, big, big, big, jax.ShapeDtypeStruct((PEER_H, 8, N_TOK), f32)],
        scratch_shapes=[pltpu.VMEM((PEER_TOPK, RT_TL), f32), pltpu.VMEM((PEER_TOPK, RT_TL), f32)],
        compiler_params=_cparams(("parallel", "parallel")),
        name="peer_route",
    )(q, q, keys1, keys2)


PE_TB = 512
PE_NA = 4
PE_CH = PE_NA * PEER_NK
PE_NCH = PEER_NK * PEER_NK // PE_CH
PE_BH = 32
SQRT_HALF = 0.7071067811865476
MXU = 256


def _peer_kernel(*refs, final):
    if final:
        (u0_ref, u_ref, vt_ref, xt_ref, s1_ref, f1_ref, s2_ref, f2_ref, tau_ref, x1_ref, ga_ref, fw_ref,
         o_ref, acc_ref, pw_ref, gel_ref, rs1_ref, rf1_ref) = refs
    else:
        (u0_ref, u_ref, vt_ref, xt_ref, s1_ref, f1_ref, s2_ref, f2_ref, tau_ref, x1_ref, ga_ref,
         o_ref, acc_ref, pw_ref, gel_ref, rs1_ref, rf1_ref) = refs
    j = pl.program_id(1)

    @pl.when(j == 0)
    def _():
        acc_ref[...] = jnp.zeros_like(acc_ref)
        pw_ref[1] = jnp.zeros(pw_ref.shape[1:], bf16)
        act0 = jnp.dot(u0_ref[0], xt_ref[...], preferred_element_type=f32)
        gel_ref[0] = 0.5 * act0 * (1.0 + lax.erf(act0 * SQRT_HALF))

    cur = j % 2
    prev = 1 - cur
    jc = jnp.minimum(j, PE_NCH - 1)

    for i in range(PE_NA):
        for h in range(PEER_H):
            a = jc * PE_NA + i
            rs1_ref[i * PEER_H + h] = jnp.broadcast_to(s1_ref[h, pl.ds(a, 1), :], (8, PE_TB))
            rf1_ref[i * PEER_H + h] = jnp.broadcast_to(f1_ref[h, pl.ds(a, 1), :], (8, PE_TB))

    nsub = PE_BH // 8
    n_bh = PEER_NK // PE_BH

    def weight_tile(t):
        l0 = pl.multiple_of((t // n_bh) * LANE, LANE)
        r0 = pl.multiple_of((t % n_bh) * PE_BH, PE_BH)
        lanes = pl.ds(l0, LANE)
        ws = [jnp.zeros((nsub, 8, LANE), f32) for _ in range(PE_NA)]
        for h in range(PEER_H):
            s2t = s2_ref[h, pl.ds(r0, PE_BH), lanes].reshape(nsub, 8, LANE)
            f2t = f2_ref[h, pl.ds(r0, PE_BH), lanes].reshape(nsub, 8, LANE)
            tr = tau_ref[h, :, lanes][None]
            for i in range(PE_NA):
                sel = (s2t + rs1_ref[i * PEER_H + h, :, lanes][None]) >= tr
                ws[i] = ws[i] + jnp.where(sel, f2t * rf1_ref[i * PEER_H + h, :, lanes][None], 0.0)
        for i in range(PE_NA):
            er = pl.ds(pl.multiple_of(i * PEER_NK + r0, PE_BH), PE_BH)
            pw_ref[cur, er, lanes] = (ws[i].reshape(PE_BH, LANE) * gel_ref[cur, er, lanes]).astype(bf16)

    n_half = PE_TB // MXU
    n_tiles = (PE_TB // LANE) * n_bh
    tiles_per_piece = n_tiles // (2 * n_half)

    def stage_act(p, carry):
        n0 = pl.multiple_of(p * MXU, MXU)
        act = jnp.dot(u_ref[0], xt_ref[:, pl.ds(n0, MXU)], preferred_element_type=f32)
        gel_ref[prev, :, pl.ds(n0, MXU)] = 0.5 * act * (1.0 + lax.erf(act * SQRT_HALF))
        for q in range(tiles_per_piece):
            weight_tile(p * tiles_per_piece + q)
        return carry

    def stage_fold(p, carry):
        n0 = pl.multiple_of(p * MXU, MXU)
        acc_ref[:, pl.ds(n0, MXU)] += jnp.dot(vt_ref[0], pw_ref[prev, :, pl.ds(n0, MXU)],
                                              preferred_element_type=f32)
        for q in range(tiles_per_piece):
            weight_tile((n_half + p) * tiles_per_piece + q)
        return carry

    lax.fori_loop(0, n_half, stage_act, 0)
    lax.fori_loop(0, n_half, stage_fold, 0)

    @pl.when(j == PE_NCH)
    def _():
        out = acc_ref[...].T
        for s in range(PE_TB // SEG):
            r = slice(s * SEG, (s + 1) * SEG)
            x2 = x1_ref[r, :] + ga_ref[s:s + 1, :] * out[r, :]
            o_ref[r, :] = _rms(x2, fw_ref[...]) if final else x2


def _peer(u_c, vt_c, h2t, s1t, f1t, s2t, f2t, tau, x1, mod_seg, final_w):
    final = final_w is not None
    nseg_b = PE_TB // SEG
    rt = pl.BlockSpec((PEER_H, PEER_NK, PE_TB), lambda i, j: (0, 0, i))
    in_specs = [pl.BlockSpec((1, PE_CH, D), lambda i, j: (0, 0, 0)),
                pl.BlockSpec((1, PE_CH, D), lambda i, j: (jnp.minimum(j + 1, PE_NCH - 1), 0, 0)),
                pl.BlockSpec((1, D, PE_CH), lambda i, j: (jnp.maximum(j - 1, 0), 0, 0)),
                pl.BlockSpec((D, PE_TB), lambda i, j: (0, i)),
                rt, rt, rt, rt,
                pl.BlockSpec((PEER_H, 8, PE_TB), lambda i, j: (0, 0, i)),
                pl.BlockSpec((PE_TB, D), lambda i, j: (i, 0)),
                pl.BlockSpec((nseg_b, D), lambda i, j: (i, 5))]
    args = [u_c, u_c, vt_c, h2t, s1t, f1t, s2t, f2t, tau, x1, mod_seg]
    if final:
        in_specs.append(pl.BlockSpec((1, D), lambda i, j: (0, 0)))
        args.append(final_w.reshape(1, D))
    return pl.pallas_call(
        functools.partial(_peer_kernel, final=final),
        grid=(N_TOK // PE_TB, PE_NCH + 1),
        in_specs=in_specs,
        out_specs=pl.BlockSpec((PE_TB, D), lambda i, j: (i, 0)),
        out_shape=jax.ShapeDtypeStruct((N_TOK, D), f32),
        scratch_shapes=[pltpu.VMEM((D, PE_TB), f32), pltpu.VMEM((2, PE_CH, PE_TB), bf16),
                        pltpu.VMEM((2, PE_CH, PE_TB), f32),
                        pltpu.VMEM((PE_NA * PEER_H, 8, PE_TB), f32), pltpu.VMEM((PE_NA * PEER_H, 8, PE_TB), f32)],
        compiler_params=_cparams(("parallel", "arbitrary")),
        name="peer_experts_final" if final else "peer_experts",
    )(*args)


def _reorder_w_in(w):
    o = np.cumsum([0, 512, 512, 1024, 1024, GLA_RANK, SSD_INNER, SSD_CONV_DIM, SSD_H, 2 * D])
    q_k_v_go = w[:, o[0]:o[4]]
    gk_low = w[:, o[4]:o[5]]
    z = w[:, o[5]:o[6]]
    xbc = w[:, o[6]:o[7]]
    dt = w[:, o[7]:o[8]]
    gates = w[:, o[8]:o[9]]
    pad = lambda m: jnp.pad(m, ((0, 0), (0, LANE - m.shape[1])))
    dts = [pad(dt[:, g * SSD_HG:(g + 1) * SSD_HG]) for g in range(SSD_G)]
    return jnp.concatenate([q_k_v_go, xbc, z, gates] + dts + [pad(gk_low)], axis=1).astype(bf16)


def kernel(x_prompt, x_sample, state_gla, state_ssd, state_conv, c_prompt, c_sample, w_ada, b_ada, norm1_w, w_in, gla_gk_w2, gla_gk_b, gla_norm_w, gla_proj, ssd_conv_w, ssd_conv_b, ssd_dt_bias, ssd_A_log, ssd_D, ssd_norm_w, ssd_proj, w_out, norm2_w, peer_wq, peer_keys1, peer_keys2, peer_u, peer_v, final_norm_w):
    x = jnp.concatenate([x_prompt.reshape(N_P, D), x_sample.reshape(N_S, D)], axis=0)
    c_all = jnp.concatenate([c_prompt, c_sample], axis=0)
    mod = _ada(c_all, w_ada, b_ada)
    seg2seq = np.concatenate([np.repeat(np.arange(B_P), SEG_PER_PSEQ), B_P + np.arange(B_S)])

    pad_lane = lambda m: jnp.pad(m, ((0, 0), (0, LANE - m.shape[1])))
    gla_st, ssd_st, conv_st = [], [], []
    for l in range(DEPTH):
        mod_seg = mod[l][seg2seq]
        h = _normmod(x, norm1_w[l], mod_seg, 1, 0)
        p = _mm(h, _reorder_w_in(w_in[l]), 512, 2176, "in_proj")

        w2pad = jnp.pad(gla_gk_w2[l], ((0, LANE - GLA_RANK), (0, 0)))
        oa, gla_p, gla_s = _gla(p, w2pad, gla_gk_b[l], gla_norm_w[l], state_gla[l])

        dtb = pad_lane(ssd_dt_bias[l].reshape(SSD_G, SSD_HG)).reshape(1, SSD_G * LANE)
        alog = pad_lane(ssd_A_log[l].reshape(SSD_G, SSD_HG)).reshape(1, SSD_G * LANE)
        dvec = jnp.repeat(ssd_D[l], SSD_P).reshape(1, SSD_INNER)
        yb, ssd_p, ssd_s = _ssd(p, state_conv[l], ssd_conv_w[l], ssd_conv_b[l], dtb, alog, dvec,
                                ssd_norm_w[l], state_ssd[l])

        x1, h2, h2t = _mix(oa, yb, p, x, mod_seg, norm2_w[l], gla_proj[l].astype(bf16),
                           ssd_proj[l].astype(bf16), w_out[l].astype(bf16))
        q = _mm(h2, peer_wq[l].astype(bf16), 512, PEER_H * PEER_DQ, "peer_query")
        s1t, s2t, f1t, f2t, tau = _route(q, peer_keys1[l], peer_keys2[l])
        u_c = peer_u[l].astype(bf16).reshape(PE_NCH, PE_CH, D)
        vt_c = peer_v[l].astype(bf16).reshape(PE_NCH, PE_CH, D).transpose(0, 2, 1)
        x = _peer(u_c, vt_c, h2t, s1t, f1t, s2t, f2t, tau, x1, mod_seg,
                  final_norm_w if l == DEPTH - 1 else None)

        xbc = p[:, C_XBC:C_XBC + SSD_CONV_DIM]
        conv_p = xbc[:N_P].reshape(B_P, T_P, SSD_CONV_DIM)[:, T_P - (CONV_W - 1):]
        conv_s = xbc[N_P:].reshape(B_S, T_S, SSD_CONV_DIM)[:, T_S - (CONV_W - 1):]
        gla_st.append((gla_p, gla_s))
        ssd_st.append((ssd_p, ssd_s))
        conv_st.append((conv_p, conv_s))

    y_prompt = x[:N_P].reshape(B_P, T_P, D)
    y_sample = x[N_P:].reshape(B_S, T_S, D)
    stack = lambda pairs, k: jnp.stack([pr[k] for pr in pairs])
    return (y_prompt, y_sample, stack(gla_st, 0), stack(ssd_st, 0), stack(conv_st, 0),
            stack(gla_st, 1), stack(ssd_st, 1), stack(conv_st, 1))
```

```python
import functools

import jax
import jax.numpy as jnp
import numpy as np
from jax import lax
from jax.experimental import pallas as pl
from jax.experimental.pallas import tpu as pltpu

f32 = jnp.float32
bf16 = jnp.bfloat16

D = 1024
DEPTH = 2
B_P, T_P = 8, 2048
B_S, T_S = 32, 64
N_P = B_P * T_P
N_S = B_S * T_S
N_TOK = N_P + N_S
SEG = 64
NSEG = N_TOK // SEG
NSEG_P = N_P // SEG
SEG_PER_PSEQ = T_P // SEG
EPS = 1e-6

GLA_H, GLA_DK, GLA_DV = 4, 128, 256
GLA_RANK = 16
GLA_GATE_NORM = 16.0
SSD_INNER = 2048
SSD_P = 64
SSD_H = 32
SSD_G = 4
SSD_N = 128
SSD_HG = SSD_H // SSD_G
SSD_GW = SSD_INNER // SSD_G
CONV_W = 4
SSD_CONV_DIM = SSD_INNER + 2 * SSD_G * SSD_N

PEER_H = 8
PEER_NK = 128
PEER_DQ = 256
PEER_TOPK = 16

C_Q, C_K, C_V, C_GO = 0, 512, 1024, 2048
C_XBC, C_Z, C_GATE = 3072, 6144, 8192
C_DT = 10240
C_GKLOW = C_DT + SSD_G * 128
P_COLS = C_GKLOW + 128

LANE = 128
VMEM_LIMIT = 56 * 1024 * 1024


def _cparams(sem):
    return pltpu.CompilerParams(dimension_semantics=sem, vmem_limit_bytes=VMEM_LIMIT)


def _dot(a, b):
    return jnp.dot(a.astype(bf16), b.astype(bf16), preferred_element_type=f32)


def _dot_nt(a, b):
    return lax.dot_general(a.astype(bf16), b.astype(bf16), (((1,), (1,)), ((), ())), preferred_element_type=f32)


def _dot_tn(a, b):
    return lax.dot_general(a.astype(bf16), b.astype(bf16), (((0,), (0,)), ((), ())), preferred_element_type=f32)


def _split3(x):
    hi = x.astype(bf16)
    r = x - hi.astype(f32)
    mid = r.astype(bf16)
    lo = (r - mid.astype(f32)).astype(bf16)
    return hi, mid, lo


def _dot3(a, b):
    ah, am, _ = _split3(a)
    bh, bm, _ = _split3(b)
    d = functools.partial(jnp.dot, preferred_element_type=f32)
    return d(ah, bh) + (d(ah, bm) + d(am, bh))


def _dot3_nt(a, b):
    ah, am, _ = _split3(a)
    bh, bm, _ = _split3(b)
    d = functools.partial(lax.dot_general, dimension_numbers=(((1,), (1,)), ((), ())), preferred_element_type=f32)
    return d(ah, bh) + (d(ah, bm) + d(am, bh))


def _cumsum_rows(x):
    n = x.shape[0]
    tri = (lax.broadcasted_iota(jnp.int32, (n, n), 0) >= lax.broadcasted_iota(jnp.int32, (n, n), 1)).astype(bf16)
    hi, mid, lo = _split3(x)
    d = functools.partial(jnp.dot, preferred_element_type=f32)
    return d(tri, hi) + (d(tri, mid) + d(tri, lo))


def _silu(x):
    return x * jax.nn.sigmoid(x)


def _softplus(x):
    return jnp.maximum(x, 0.0) + jnp.log1p(jnp.exp(-jnp.abs(x)))


def _rms(x, w):
    return x * lax.rsqrt(jnp.mean(x * x, axis=-1, keepdims=True) + EPS) * w


def _ada_kernel(c_ref, w_ref, b_ref, o_ref):
    o_ref[0] = _dot3(_silu(c_ref[...]), w_ref[0]) + b_ref[0]


def _ada(c_all, w_ada, b_ada):
    nb = c_all.shape[0]
    tn = 1536
    return pl.pallas_call(
        _ada_kernel,
        grid=(DEPTH, 6 * D // tn),
        in_specs=[pl.BlockSpec((nb, D), lambda l, j: (0, 0)),
                  pl.BlockSpec((1, D, tn), lambda l, j: (l, 0, j)),
                  pl.BlockSpec((1, 1, tn), lambda l, j: (l, 0, j))],
        out_specs=pl.BlockSpec((1, nb, tn), lambda l, j: (l, 0, j)),
        out_shape=jax.ShapeDtypeStruct((DEPTH, nb, 6 * D), f32),
        compiler_params=_cparams(("parallel", "parallel")),
        name="ada_mod",
    )(c_all, w_ada, b_ada.reshape(DEPTH, 1, 6 * D))


NM_TB = 512


def _normmod_kernel(x_ref, w_ref, sc_ref, sh_ref, o_ref):
    for s in range(NM_TB // SEG):
        r = slice(s * SEG, (s + 1) * SEG)
        y = _rms(x_ref[r, :], w_ref[...])
        o_ref[r, :] = (y * (1.0 + sc_ref[s:s + 1, :]) + sh_ref[s:s + 1, :]).astype(bf16)


def _normmod(x, w, mod_seg, sc_col, sh_col):
    nseg_b = NM_TB // SEG
    return pl.pallas_call(
        _normmod_kernel,
        grid=(N_TOK // NM_TB,),
        in_specs=[pl.BlockSpec((NM_TB, D), lambda i: (i, 0)),
                  pl.BlockSpec((1, D), lambda i: (0, 0)),
                  pl.BlockSpec((nseg_b, D), lambda i: (i, sc_col)),
                  pl.BlockSpec((nseg_b, D), lambda i: (i, sh_col))],
        out_specs=pl.BlockSpec((NM_TB, D), lambda i: (i, 0)),
        out_shape=jax.ShapeDtypeStruct((N_TOK, D), bf16),
        compiler_params=_cparams(("parallel",)),
        name="norm_mod",
    )(x, w.reshape(1, D), mod_seg, mod_seg)


def _mm_kernel(a_ref, b_ref, o_ref):
    o_ref[...] = jnp.dot(a_ref[...], b_ref[...], preferred_element_type=f32)


def _mm(a, b, tm, tn, name):
    m, k = a.shape
    n = b.shape[1]
    return pl.pallas_call(
        _mm_kernel,
        grid=(n // tn, m // tm),
        in_specs=[pl.BlockSpec((tm, k), lambda j, i: (i, 0)),
                  pl.BlockSpec((k, tn), lambda j, i: (0, j))],
        out_specs=pl.BlockSpec((tm, tn), lambda j, i: (i, j)),
        out_shape=jax.ShapeDtypeStruct((m, n), f32),
        compiler_params=_cparams(("parallel", "parallel")),
        name=name,
    )(a, b)


def _is_prompt(c):
    return c < NSEG_P


def _sample_seq(c):
    return jnp.maximum(c - NSEG_P, 0)


def _prompt_seq(c):
    return jnp.minimum(c // SEG_PER_PSEQ, B_P - 1)


def _gla_kernel(q_ref, k_ref, v_ref, g_ref, low_ref, w2_ref, b2_ref, nw_ref, s0_ref,
                o_ref, stp_ref, sts_ref, st_ref):
    c = pl.program_id(0)
    is_p = _is_prompt(c)
    pos = c % SEG_PER_PSEQ

    @pl.when(jnp.logical_and(is_p, pos == 0))
    def _():
        st_ref[...] = jnp.zeros_like(st_ref)

    @pl.when(jnp.logical_not(is_p))
    def _():
        for h in range(GLA_H):
            st_ref[h] = s0_ref[0, h].T

    pre = _dot3(low_ref[...], w2_ref[...]) + b2_ref[...]
    gk = -_softplus(-pre) / GLA_GATE_NORM
    b = _cumsum_rows(gk)
    b_last = b[SEG - 1:SEG, :]
    e_b = jnp.exp(b)
    e_nb = jnp.exp(-b)
    e_end = jnp.exp(b_last - b)
    e_last = jnp.exp(b_last)
    causal = lax.broadcasted_iota(jnp.int32, (SEG, SEG), 0) >= lax.broadcasted_iota(jnp.int32, (SEG, SEG), 1)
    for h in range(GLA_H):
        ks = slice(h * GLA_DK, (h + 1) * GLA_DK)
        vs = slice(h * GLA_DV, (h + 1) * GLA_DV)
        k = k_ref[:, ks]
        q_t = (q_ref[:, ks] * (GLA_DK ** -0.5)) * e_b[:, ks]
        att = jnp.where(causal, _dot_nt(q_t, k * e_nb[:, ks]), 0.0)
        v = v_ref[:, vs]
        st = st_ref[h]
        o = _dot(att, v) + _dot_nt(q_t, st)
        st_ref[h] = st * e_last[:, ks] + _dot_tn(v, k * e_end[:, ks])
        o_ref[:, vs] = (_rms(o, nw_ref[...]) * _silu(g_ref[:, vs])).astype(bf16)

    @pl.when(jnp.logical_and(is_p, pos == SEG_PER_PSEQ - 1))
    def _():
        for h in range(GLA_H):
            stp_ref[0, h] = st_ref[h].T

    @pl.when(jnp.logical_not(is_p))
    def _():
        for h in range(GLA_H):
            sts_ref[0, h] = st_ref[h].T


def _gla(p, w2pad, b2, norm_w, s0):
    st_block = (1, GLA_H, GLA_DK, GLA_DV)
    kd, vd = GLA_H * GLA_DK, GLA_H * GLA_DV
    return pl.pallas_call(
        _gla_kernel,
        grid=(NSEG,),
        in_specs=[pl.BlockSpec((SEG, kd), lambda c: (c, C_Q // kd)),
                  pl.BlockSpec((SEG, kd), lambda c: (c, C_K // kd)),
                  pl.BlockSpec((SEG, vd), lambda c: (c, C_V // vd)),
                  pl.BlockSpec((SEG, vd), lambda c: (c, C_GO // vd)),
                  pl.BlockSpec((SEG, LANE), lambda c: (c, C_GKLOW // LANE)),
                  pl.BlockSpec((LANE, kd), lambda c: (0, 0)),
                  pl.BlockSpec((1, kd), lambda c: (0, 0)),
                  pl.BlockSpec((1, GLA_DV), lambda c: (0, 0)),
                  pl.BlockSpec(st_block, lambda c: (_sample_seq(c), 0, 0, 0))],
        out_specs=[pl.BlockSpec((SEG, vd), lambda c: (c, 0)),
                   pl.BlockSpec(st_block, lambda c: (_prompt_seq(c), 0, 0, 0)),
                   pl.BlockSpec(st_block, lambda c: (_sample_seq(c), 0, 0, 0))],
        out_shape=[jax.ShapeDtypeStruct((N_TOK, vd), bf16),
                   jax.ShapeDtypeStruct((B_P, GLA_H, GLA_DK, GLA_DV), f32),
                   jax.ShapeDtypeStruct((B_S, GLA_H, GLA_DK, GLA_DV), f32)],
        scratch_shapes=[pltpu.VMEM((GLA_H, GLA_DV, GLA_DK), f32)],
        compiler_params=_cparams(("arbitrary",)),
        name="gla_mixer",
    )(p, p, p, p, p, w2pad, b2.reshape(1, -1), norm_w.reshape(1, -1), s0)


EXT = SEG + 8


def _ssd_kernel(xbc_ref, z_ref, dt_ref, cp_ref, cw_ref, cb_ref, dtb_ref, alog_ref, dvec_ref, nw_ref, s0_ref,
                o_ref, stp_ref, sts_ref, ext, st_ref):
    c = pl.program_id(0)
    is_p = _is_prompt(c)
    pos = c % SEG_PER_PSEQ

    @pl.when(jnp.logical_and(is_p, pos == 0))
    def _():
        ext[0:8, :] = jnp.zeros((8, SSD_CONV_DIM), f32)
        st_ref[...] = jnp.zeros_like(st_ref)

    @pl.when(jnp.logical_not(is_p))
    def _():
        ext[0:8, :] = jnp.zeros((8, SSD_CONV_DIM), f32)
        ext[8 - (CONV_W - 1):8, :] = cp_ref[0]
        st_ref[...] = s0_ref[0].reshape(st_ref.shape)

    ext[8:EXT, :] = xbc_ref[...]
    acc = cb_ref[...] + ext[pl.ds(8 - (CONV_W - 1), SEG), :] * cw_ref[0:1, :]
    for i in range(1, CONV_W):
        acc = acc + ext[pl.ds(8 - (CONV_W - 1) + i, SEG), :] * cw_ref[i:i + 1, :]
    ext[0:8, :] = ext[SEG:EXT, :]
    conv = _silu(acc)

    dt = _softplus(dt_ref[...] + dtb_ref[...])
    a = _cumsum_rows(dt * (-jnp.exp(alog_ref[...])))
    a_last = a[SEG - 1:SEG, :]
    e_a = jnp.exp(a)
    w_col = dt * jnp.exp(a_last - a)
    e_last = jnp.exp(a_last)
    causal = lax.broadcasted_iota(jnp.int32, (SEG, SEG), 0) >= lax.broadcasted_iota(jnp.int32, (SEG, SEG), 1)
    lane_lo = lax.broadcasted_iota(jnp.int32, (SEG, 2 * SSD_P), 1) < SSD_P
    sub_lo = lax.broadcasted_iota(jnp.int32, (2 * SSD_P, SSD_N), 0) < SSD_P
    nbc = SSD_G * SSD_N

    def pair_cols(m, ja, jb):
        return jnp.where(lane_lo, jnp.broadcast_to(m[:, ja:ja + 1], (SEG, 2 * SSD_P)),
                         jnp.broadcast_to(m[:, jb:jb + 1], (SEG, 2 * SSD_P)))

    for g in range(SSD_G):
        gl = slice(g * LANE, (g + 1) * LANE)
        a_g, dt_g, ea_g, wc_g, el_g = a[:, gl], dt[:, gl], e_a[:, gl], w_col[:, gl], e_last[:, gl]
        a_t = a_g.T
        dt_t = dt_g.T
        bc = conv[:, SSD_INNER + g * SSD_N:SSD_INNER + (g + 1) * SSD_N]
        cc = conv[:, SSD_INNER + nbc + g * SSD_N:SSD_INNER + nbc + (g + 1) * SSD_N]
        cbm = _dot_nt(cc, bc)

        def w_intra(j):
            seg = a_g[:, j:j + 1] - a_t[j:j + 1, :]
            return cbm * jnp.exp(jnp.where(causal, seg, -jnp.inf)) * dt_t[j:j + 1, :]

        ys = []
        for p in range(SSD_HG // 2):
            ja, jb = 2 * p, 2 * p + 1
            pp = g * (SSD_HG // 2) + p
            xp = conv[:, pp * 2 * SSD_P:(pp + 1) * 2 * SSD_P]
            y_intra = jnp.where(lane_lo, _dot(w_intra(ja), xp), _dot(w_intra(jb), xp))
            sp = st_ref[pp]
            y_inter = _dot_nt(cc, sp) * pair_cols(ea_g, ja, jb)
            ds = _dot_tn(xp * pair_cols(wc_g, ja, jb), bc)
            rs = jnp.where(sub_lo, jnp.broadcast_to(el_g[:, ja:ja + 1], (2 * SSD_P, SSD_N)),
                           jnp.broadcast_to(el_g[:, jb:jb + 1], (2 * SSD_P, SSD_N)))
            st_ref[pp] = sp * rs + ds
            ys.append(y_intra + y_inter + dvec_ref[:, pp * 2 * SSD_P:(pp + 1) * 2 * SSD_P] * xp)
        gw = slice(g * SSD_GW, (g + 1) * SSD_GW)
        y = jnp.concatenate(ys, axis=1) * _silu(z_ref[:, gw])
        o_ref[:, gw] = _rms(y, nw_ref[:, gw]).astype(bf16)

    @pl.when(jnp.logical_and(is_p, pos == SEG_PER_PSEQ - 1))
    def _():
        stp_ref[0] = st_ref[...].reshape(stp_ref.shape[1:])

    @pl.when(jnp.logical_not(is_p))
    def _():
        sts_ref[0] = st_ref[...].reshape(sts_ref.shape[1:])


def _ssd(p, conv_prev, conv_w, conv_b, dtb, alog, dvec, norm_w, s0):
    st_block = (1, SSD_H, SSD_P, SSD_N)
    gl = SSD_G * LANE
    full = lambda c: (0, 0)
    return pl.pallas_call(
        _ssd_kernel,
        grid=(NSEG,),
        in_specs=[pl.BlockSpec((SEG, SSD_CONV_DIM), lambda c: (c, C_XBC // SSD_CONV_DIM)),
                  pl.BlockSpec((SEG, SSD_INNER), lambda c: (c, C_Z // SSD_INNER)),
                  pl.BlockSpec((SEG, gl), lambda c: (c, C_DT // gl)),
                  pl.BlockSpec((1, CONV_W - 1, SSD_CONV_DIM), lambda c: (_sample_seq(c), 0, 0)),
                  pl.BlockSpec((CONV_W, SSD_CONV_DIM), full),
                  pl.BlockSpec((1, SSD_CONV_DIM), full),
                  pl.BlockSpec((1, gl), full),
                  pl.BlockSpec((1, gl), full),
                  pl.BlockSpec((1, SSD_INNER), full),
                  pl.BlockSpec((1, SSD_INNER), full),
                  pl.BlockSpec(st_block, lambda c: (_sample_seq(c), 0, 0, 0))],
        out_specs=[pl.BlockSpec((SEG, SSD_INNER), lambda c: (c, 0)),
                   pl.BlockSpec(st_block, lambda c: (_prompt_seq(c), 0, 0, 0)),
                   pl.BlockSpec(st_block, lambda c: (_sample_seq(c), 0, 0, 0))],
        out_shape=[jax.ShapeDtypeStruct((N_TOK, SSD_INNER), bf16),
                   jax.ShapeDtypeStruct((B_P, SSD_H, SSD_P, SSD_N), f32),
                   jax.ShapeDtypeStruct((B_S, SSD_H, SSD_P, SSD_N), f32)],
        scratch_shapes=[pltpu.VMEM((EXT, SSD_CONV_DIM), f32),
                        pltpu.VMEM((SSD_H // 2, 2 * SSD_P, SSD_N), f32)],
        compiler_params=_cparams(("arbitrary",)),
        name="ssd_mixer",
    )(p, p, p, conv_prev, conv_w, conv_b.reshape(1, -1), dtb, alog, dvec, norm_w.reshape(1, -1), s0)


MIX_TB = 512


def _mix_kernel(oa_ref, yb_ref, gt_ref, x_ref, ga_ref, sc_ref, sh_ref, nw_ref, wa_ref, wb_ref, wo_ref,
                x1_ref, h2_ref, h2t_ref):
    br_a = jnp.dot(oa_ref[...], wa_ref[...], preferred_element_type=f32)
    br_b = jnp.dot(yb_ref[...], wb_ref[...], preferred_element_type=f32)
    g_a = jax.nn.sigmoid(gt_ref[:, 0:D])
    g_b = jax.nn.sigmoid(gt_ref[:, D:2 * D])
    y = _dot(g_a * br_a + g_b * br_b, wo_ref[...])
    for s in range(MIX_TB // SEG):
        r = slice(s * SEG, (s + 1) * SEG)
        x1 = x_ref[r, :] + ga_ref[s:s + 1, :] * y[r, :]
        x1_ref[r, :] = x1
        h2_ref[r, :] = (_rms(x1, nw_ref[...]) * (1.0 + sc_ref[s:s + 1, :]) + sh_ref[s:s + 1, :]).astype(bf16)
    h2t_ref[...] = h2_ref[...].T


def _mix(oa, yb, p, x, mod_seg, norm2_w, wa, wb, wo):
    nseg_b = MIX_TB // SEG
    full = lambda i: (0, 0)
    return pl.pallas_call(
        _mix_kernel,
        grid=(N_TOK // MIX_TB,),
        in_specs=[pl.BlockSpec((MIX_TB, D), lambda i: (i, 0)),
                  pl.BlockSpec((MIX_TB, SSD_INNER), lambda i: (i, 0)),
                  pl.BlockSpec((MIX_TB, 2 * D), lambda i: (i, C_GATE // (2 * D))),
                  pl.BlockSpec((MIX_TB, D), lambda i: (i, 0)),
                  pl.BlockSpec((nseg_b, D), lambda i: (i, 2)),
                  pl.BlockSpec((nseg_b, D), lambda i: (i, 4)),
                  pl.BlockSpec((nseg_b, D), lambda i: (i, 3)),
                  pl.BlockSpec((1, D), full),
                  pl.BlockSpec((D, D), full),
                  pl.BlockSpec((SSD_INNER, D), full),
                  pl.BlockSpec((D, D), full)],
        out_specs=[pl.BlockSpec((MIX_TB, D), lambda i: (i, 0)),
                   pl.BlockSpec((MIX_TB, D), lambda i: (i, 0)),
                   pl.BlockSpec((D, MIX_TB), lambda i: (0, i))],
        out_shape=[jax.ShapeDtypeStruct((N_TOK, D), f32),
                   jax.ShapeDtypeStruct((N_TOK, D), bf16),
                   jax.ShapeDtypeStruct((D, N_TOK), bf16)],
        compiler_params=_cparams(("parallel",)),
        name="mix_out",
    )(oa, yb, p, x, mod_seg, mod_seg, mod_seg, norm2_w.reshape(1, D), wa, wb, wo)


RT_TL = 128
HALF = PEER_DQ // 2


RT_HPS = 4
SUBLANES = 8


def _batcher_sort_net(n):
    def merge(lo, hi, r):
        step = 2 * r
        if step < hi - lo:
            yield from merge(lo, hi, step)
            yield from merge(lo + r, hi, step)
            yield from ((i, i + r) for i in range(lo + r, hi - r, step))
        else:
            yield (lo, lo + r)

    def sort(lo, hi):
        if hi - lo >= 1:
            mid = lo + (hi - lo) // 2
            yield from sort(lo, mid)
            yield from sort(mid + 1, hi)
            yield from merge(lo, hi, 1)

    return tuple(sort(0, n - 1))


def _bitonic_merge_net(n):
    net, d = [], n // 2
    while d >= 1:
        net += [(i, i + d) for i in range(n) if (i // d) % 2 == 0]
        d //= 2
    return tuple(net)


_SORT16 = _batcher_sort_net(PEER_TOPK)
_MERGE16 = _bitonic_merge_net(PEER_TOPK)
N_CAND_VREGS = 10


def _compare_exchange(x, net):
    for i, j in net:
        x[i], x[j] = jnp.maximum(x[i], x[j]), jnp.minimum(x[i], x[j])


def _merge_across_sublanes(x, n_valid):
    for shift in (4, 2, 1):
        y = [pltpu.roll(v, shift, 0) for v in x]
        merged = []
        for k in range(PEER_TOPK):
            a = x[k] if k < n_valid else None
            b = y[PEER_TOPK - 1 - k] if PEER_TOPK - 1 - k < n_valid else None
            merged.append(jnp.maximum(a, b) if (a is not None and b is not None) else (a if b is None else b))
        x = merged
        _compare_exchange(x, _MERGE16)
        n_valid = PEER_TOPK
    return x


def _top16_sorted(s):
    x = [s[SUBLANES * k:SUBLANES * (k + 1), :] for k in range(PEER_NK // SUBLANES)]
    _compare_exchange(x, _SORT16)
    return _merge_across_sublanes(x, PEER_TOPK)


def _route_kernel(q_ref, k1_ref, k2_ref, thr_ref, s2_ref, f1_ref, f2_ref):
    sub = lax.broadcasted_iota(jnp.int32, (SUBLANES, RT_TL), 0)
    ninf = jnp.float32(-jnp.inf)

    def by_sublane(vs):
        out = vs[0]
        for r in range(1, SUBLANES):
            out = jnp.where(sub == r, vs[r], out)
        return out

    for hh in range(RT_HPS):
        s1 = _dot3_nt(k1_ref[hh], q_ref[:, hh * PEER_DQ:hh * PEER_DQ + HALF])
        s2 = _dot3_nt(k2_ref[hh], q_ref[:, hh * PEER_DQ + HALF:(hh + 1) * PEER_DQ])
        s2_ref[hh] = s2
        v1 = _top16_sorted(s1)
        v2 = _top16_sorted(s2)
        v2lo, v2hi, v1hi = by_sublane(v2[:8]), by_sublane(v2[8:]), by_sublane(v1[8:])
        cands = [v1[0] + v2lo, v1[0] + v2hi, v1[1] + v2lo]
        for i in range(2, 8):
            cands.append(jnp.where(sub < PEER_TOPK // (i + 1), v1[i] + v2lo, ninf))
        cands.append(v1hi + v2[0])
        assert len(cands) == N_CAND_VREGS
        _compare_exchange(cands, tuple((i, j) for i, j in _SORT16 if j < N_CAND_VREGS))
        top = _merge_across_sublanes(cands, N_CAND_VREGS)
        zsum = jnp.ones_like(top[0])
        for k in range(1, PEER_TOPK):
            zsum = zsum + jnp.exp(top[k] - top[0])
        tau = top[PEER_TOPK - 1]
        for k in range(PEER_NK // SUBLANES):
            rows = slice(SUBLANES * k, SUBLANES * (k + 1))
            s1k = s1[rows, :]
            thr = jnp.full_like(s1k, jnp.inf)
            for j in range(PEER_TOPK):
                thr = jnp.where(s1k + v2[j] >= tau, v2[j], thr)
            thr_ref[hh, rows, :] = thr
        f1_ref[hh] = jnp.exp(s1 - v1[0][0:1, :]) * (0.5 / zsum[0:1, :])
        f2_ref[hh] = jnp.exp(s2 - v2[0][0:1, :])


def _route(q, keys1, keys2):
    tile = pl.BlockSpec((RT_HPS, PEER_NK, RT_TL), lambda i, h: (h, 0, i))
    big = jax.ShapeDtypeStruct((PEER_H, PEER_NK, N_TOK), f32)
    return pl.pallas_call(
        _route_kernel,
        grid=(N_TOK // RT_TL, PEER_H // RT_HPS),
        in_specs=[pl.BlockSpec((RT_TL, RT_HPS * PEER_DQ), lambda i, h: (i, h)),
                  pl.BlockSpec((RT_HPS, PEER_NK, HALF), lambda i, h: (h, 0, 0)),
                  pl.BlockSpec((RT_HPS, PEER_NK, HALF), lambda i, h: (h, 0, 0))],
        out_specs=[tile, tile, tile, tile],
        out_shape=[big, big, big, big],
        compiler_params=_cparams(("parallel", "parallel")),
        name="peer_route",
    )(q, keys1, keys2)


PE_TB = 512
PE_NA = 4
PE_CH = PE_NA * PEER_NK
PE_NCH = PEER_NK * PEER_NK // PE_CH
PE_BH = 32
SQRT_HALF = 0.7071067811865476
MXU = 256


def _peer_kernel(*refs, final):
    if final:
        (u0_ref, u_ref, vt_ref, xt_ref, thr_ref, f1_ref, s2_ref, f2_ref, x1_ref, ga_ref, fw_ref,
         o_ref, acc_ref, pw_ref, gel_ref, rthr_ref, rf1_ref) = refs
    else:
        (u0_ref, u_ref, vt_ref, xt_ref, thr_ref, f1_ref, s2_ref, f2_ref, x1_ref, ga_ref,
         o_ref, acc_ref, pw_ref, gel_ref, rthr_ref, rf1_ref) = refs
    j = pl.program_id(1)

    @pl.when(j == 0)
    def _():
        acc_ref[...] = jnp.zeros_like(acc_ref)
        pw_ref[1] = jnp.zeros(pw_ref.shape[1:], bf16)
        act0 = jnp.dot(u0_ref[0], xt_ref[...], preferred_element_type=f32)
        gel_ref[0] = act0 * (1.0 + lax.erf(act0 * SQRT_HALF))

    cur = j % 2
    prev = 1 - cur
    jc = jnp.minimum(j, PE_NCH - 1)

    for i in range(PE_NA):
        for h in range(PEER_H):
            a = jc * PE_NA + i
            rthr_ref[i * PEER_H + h] = jnp.broadcast_to(thr_ref[h, pl.ds(a, 1), :], (8, PE_TB))
            rf1_ref[i * PEER_H + h] = jnp.broadcast_to(f1_ref[h, pl.ds(a, 1), :], (8, PE_TB))

    nsub = PE_BH // 8
    n_bh = PEER_NK // PE_BH

    def weight_tile(t):
        l0 = pl.multiple_of((t // n_bh) * LANE, LANE)
        r0 = pl.multiple_of((t % n_bh) * PE_BH, PE_BH)
        lanes = pl.ds(l0, LANE)
        ws = [jnp.zeros((nsub, 8, LANE), f32) for _ in range(PE_NA)]
        for h in range(PEER_H):
            s2t = s2_ref[h, pl.ds(r0, PE_BH), lanes].reshape(nsub, 8, LANE)
            f2t = f2_ref[h, pl.ds(r0, PE_BH), lanes].reshape(nsub, 8, LANE)
            for i in range(PE_NA):
                sel = s2t >= rthr_ref[i * PEER_H + h, :, lanes][None]
                ws[i] = ws[i] + jnp.where(sel, f2t * rf1_ref[i * PEER_H + h, :, lanes][None], 0.0)
        for i in range(PE_NA):
            er = pl.ds(pl.multiple_of(i * PEER_NK + r0, PE_BH), PE_BH)
            pw_ref[cur, er, lanes] = (ws[i].reshape(PE_BH, LANE) * gel_ref[cur, er, lanes]).astype(bf16)

    n_half = PE_TB // MXU
    n_tiles = (PE_TB // LANE) * n_bh
    tiles_per_piece = n_tiles // (2 * n_half)

    def stage_act(p, carry):
        n0 = pl.multiple_of(p * MXU, MXU)
        act = jnp.dot(u_ref[0], xt_ref[:, pl.ds(n0, MXU)], preferred_element_type=f32)
        gel_ref[prev, :, pl.ds(n0, MXU)] = act * (1.0 + lax.erf(act * SQRT_HALF))
        for q in range(tiles_per_piece):
            weight_tile(p * tiles_per_piece + q)
        return carry

    def stage_fold(p, carry):
        n0 = pl.multiple_of(p * MXU, MXU)
        acc_ref[:, pl.ds(n0, MXU)] += jnp.dot(vt_ref[0], pw_ref[prev, :, pl.ds(n0, MXU)],
                                              preferred_element_type=f32)
        for q in range(tiles_per_piece):
            weight_tile((n_half + p) * tiles_per_piece + q)
        return carry

    lax.fori_loop(0, n_half, stage_act, 0)
    lax.fori_loop(0, n_half, stage_fold, 0)

    @pl.when(j == PE_NCH)
    def _():
        out = acc_ref[...].T
        for s in range(PE_TB // SEG):
            r = slice(s * SEG, (s + 1) * SEG)
            x2 = x1_ref[r, :] + ga_ref[s:s + 1, :] * out[r, :]
            o_ref[r, :] = _rms(x2, fw_ref[...]) if final else x2


def _peer(u_c, vt_c, h2t, thr, f1t, s2t, f2t, x1, mod_seg, final_w):
    final = final_w is not None
    nseg_b = PE_TB // SEG
    rt = pl.BlockSpec((PEER_H, PEER_NK, PE_TB), lambda i, j: (0, 0, i))
    in_specs = [pl.BlockSpec((1, PE_CH, D), lambda i, j: (0, 0, 0)),
                pl.BlockSpec((1, PE_CH, D), lambda i, j: (jnp.minimum(j + 1, PE_NCH - 1), 0, 0)),
                pl.BlockSpec((1, D, PE_CH), lambda i, j: (jnp.maximum(j - 1, 0), 0, 0)),
                pl.BlockSpec((D, PE_TB), lambda i, j: (0, i)),
                rt, rt, rt, rt,
                pl.BlockSpec((PE_TB, D), lambda i, j: (i, 0)),
                pl.BlockSpec((nseg_b, D), lambda i, j: (i, 5))]
    args = [u_c, u_c, vt_c, h2t, thr, f1t, s2t, f2t, x1, mod_seg]
    if final:
        in_specs.append(pl.BlockSpec((1, D), lambda i, j: (0, 0)))
        args.append(final_w.reshape(1, D))
    return pl.pallas_call(
        functools.partial(_peer_kernel, final=final),
        grid=(N_TOK // PE_TB, PE_NCH + 1),
        in_specs=in_specs,
        out_specs=pl.BlockSpec((PE_TB, D), lambda i, j: (i, 0)),
        out_shape=jax.ShapeDtypeStruct((N_TOK, D), f32),
        scratch_shapes=[pltpu.VMEM((D, PE_TB), f32), pltpu.VMEM((2, PE_CH, PE_TB), bf16),
                        pltpu.VMEM((2, PE_CH, PE_TB), f32),
                        pltpu.VMEM((PE_NA * PEER_H, 8, PE_TB), f32), pltpu.VMEM((PE_NA * PEER_H, 8, PE_TB), f32)],
        compiler_params=_cparams(("parallel", "arbitrary")),
        name="peer_experts_final" if final else "peer_experts",
    )(*args)


def _reorder_w_in(w):
    o = np.cumsum([0, 512, 512, 1024, 1024, GLA_RANK, SSD_INNER, SSD_CONV_DIM, SSD_H, 2 * D])
    q_k_v_go = w[:, o[0]:o[4]]
    gk_low = w[:, o[4]:o[5]]
    z = w[:, o[5]:o[6]]
    xbc = w[:, o[6]:o[7]]
    dt = w[:, o[7]:o[8]]
    gates = w[:, o[8]:o[9]]
    pad = lambda m: jnp.pad(m, ((0, 0), (0, LANE - m.shape[1])))
    dts = [pad(dt[:, g * SSD_HG:(g + 1) * SSD_HG]) for g in range(SSD_G)]
    return jnp.concatenate([q_k_v_go, xbc, z, gates] + dts + [pad(gk_low)], axis=1).astype(bf16)


def kernel(x_prompt, x_sample, state_gla, state_ssd, state_conv, c_prompt, c_sample, w_ada, b_ada, norm1_w, w_in, gla_gk_w2, gla_gk_b, gla_norm_w, gla_proj, ssd_conv_w, ssd_conv_b, ssd_dt_bias, ssd_A_log, ssd_D, ssd_norm_w, ssd_proj, w_out, norm2_w, peer_wq, peer_keys1, peer_keys2, peer_u, peer_v, final_norm_w):
    x = jnp.concatenate([x_prompt.reshape(N_P, D), x_sample.reshape(N_S, D)], axis=0)
    c_all = jnp.concatenate([c_prompt, c_sample], axis=0)
    mod = _ada(c_all, w_ada, b_ada)
    seg2seq = np.concatenate([np.repeat(np.arange(B_P), SEG_PER_PSEQ), B_P + np.arange(B_S)])

    pad_lane = lambda m: jnp.pad(m, ((0, 0), (0, LANE - m.shape[1])))
    gla_st, ssd_st, conv_st = [], [], []
    for l in range(DEPTH):
        mod_seg = mod[l][seg2seq]
        h = _normmod(x, norm1_w[l], mod_seg, 1, 0)
        p = _mm(h, _reorder_w_in(w_in[l]), 512, 2176, "in_proj")

        w2pad = jnp.pad(gla_gk_w2[l], ((0, LANE - GLA_RANK), (0, 0)))
        oa, gla_p, gla_s = _gla(p, w2pad, gla_gk_b[l], gla_norm_w[l], state_gla[l])

        dtb = pad_lane(ssd_dt_bias[l].reshape(SSD_G, SSD_HG)).reshape(1, SSD_G * LANE)
        alog = pad_lane(ssd_A_log[l].reshape(SSD_G, SSD_HG)).reshape(1, SSD_G * LANE)
        dvec = jnp.repeat(ssd_D[l], SSD_P).reshape(1, SSD_INNER)
        yb, ssd_p, ssd_s = _ssd(p, state_conv[l], ssd_conv_w[l], ssd_conv_b[l], dtb, alog, dvec,
                                ssd_norm_w[l], state_ssd[l])

        x1, h2, h2t = _mix(oa, yb, p, x, mod_seg, norm2_w[l], gla_proj[l].astype(bf16),
                           ssd_proj[l].astype(bf16), w_out[l].astype(bf16))
        q = _mm(h2, peer_wq[l].astype(bf16), 512, PEER_H * PEER_DQ, "peer_query")
        thr, s2t, f1t, f2t = _route(q, peer_keys1[l], peer_keys2[l])
        u_c = peer_u[l].astype(bf16).reshape(PE_NCH, PE_CH, D)
        vt_c = peer_v[l].astype(bf16).reshape(PE_NCH, PE_CH, D).transpose(0, 2, 1)
        x = _peer(u_c, vt_c, h2t, thr, f1t, s2t, f2t, x1, mod_seg,
                  final_norm_w if l == DEPTH - 1 else None)

        tail = lambda rows, nb, t: rows.reshape(nb, t, P_COLS)[:, t - (CONV_W - 1):, C_XBC:C_XBC + SSD_CONV_DIM]
        conv_p = tail(p[:N_P], B_P, T_P)
        conv_s = tail(p[N_P:], B_S, T_S)
        gla_st.append((gla_p, gla_s))
        ssd_st.append((ssd_p, ssd_s))
        conv_st.append((conv_p, conv_s))

    y_prompt = x[:N_P].reshape(B_P, T_P, D)
    y_sample = x[N_P:].reshape(B_S, T_S, D)
    stack = lambda pairs, k: jnp.stack([pr[k] for pr in pairs])
    return (y_prompt, y_sample, stack(gla_st, 0), stack(ssd_st, 0), stack(conv_st, 0),
            stack(gla_st, 1), stack(ssd_st, 1), stack(conv_st, 1))
```

```python
import functools

import jax
import jax.numpy as jnp
import numpy as np
from jax import lax
from jax.experimental import pallas as pl
from jax.experimental.pallas import tpu as pltpu

f32 = jnp.float32
bf16 = jnp.bfloat16

D = 1024
DEPTH = 2
B_P, T_P = 8, 2048
B_S, T_S = 32, 64
N_P = B_P * T_P
N_S = B_S * T_S
N_TOK = N_P + N_S
SEG = 64
NSEG = N_TOK // SEG
NSEG_P = N_P // SEG
SEG_PER_PSEQ = T_P // SEG
EPS = 1e-6

GLA_H, GLA_DK, GLA_DV = 4, 128, 256
GLA_RANK = 16
GLA_GATE_NORM = 16.0
SSD_INNER = 2048
SSD_P = 64
SSD_H = 32
SSD_G = 4
SSD_N = 128
SSD_HG = SSD_H // SSD_G
SSD_GW = SSD_INNER // SSD_G
CONV_W = 4
SSD_CONV_DIM = SSD_INNER + 2 * SSD_G * SSD_N

PEER_H = 8
PEER_NK = 128
PEER_DQ = 256
PEER_TOPK = 16

C_Q, C_K, C_V, C_GO = 0, 512, 1024, 2048
C_XBC, C_Z, C_GATE = 3072, 6144, 8192
C_DT = 10240
C_GKLOW = C_DT + SSD_G * 128
P_COLS = C_GKLOW + 128

LANE = 128
VMEM_LIMIT = 56 * 1024 * 1024


def _cparams(sem):
    return pltpu.CompilerParams(dimension_semantics=sem, vmem_limit_bytes=VMEM_LIMIT)


def _dot(a, b):
    return jnp.dot(a.astype(bf16), b.astype(bf16), preferred_element_type=f32)


def _dot_nt(a, b):
    return lax.dot_general(a.astype(bf16), b.astype(bf16), (((1,), (1,)), ((), ())), preferred_element_type=f32)


def _dot_tn(a, b):
    return lax.dot_general(a.astype(bf16), b.astype(bf16), (((0,), (0,)), ((), ())), preferred_element_type=f32)


def _split3(x):
    hi = x.astype(bf16)
    r = x - hi.astype(f32)
    mid = r.astype(bf16)
    lo = (r - mid.astype(f32)).astype(bf16)
    return hi, mid, lo


def _dot3(a, b):
    ah, am, _ = _split3(a)
    bh, bm, _ = _split3(b)
    d = functools.partial(jnp.dot, preferred_element_type=f32)
    return d(ah, bh) + (d(ah, bm) + d(am, bh))


def _dot3_nt(a, b):
    ah, am, _ = _split3(a)
    bh, bm, _ = _split3(b)
    d = functools.partial(lax.dot_general, dimension_numbers=(((1,), (1,)), ((), ())), preferred_element_type=f32)
    return d(ah, bh) + (d(ah, bm) + d(am, bh))


def _cumsum_rows(x):
    n = x.shape[0]
    tri = (lax.broadcasted_iota(jnp.int32, (n, n), 0) >= lax.broadcasted_iota(jnp.int32, (n, n), 1)).astype(bf16)
    hi, mid, lo = _split3(x)
    d = functools.partial(jnp.dot, preferred_element_type=f32)
    return d(tri, hi) + (d(tri, mid) + d(tri, lo))


def _silu(x):
    return x * jax.nn.sigmoid(x)


def _softplus(x):
    return jnp.maximum(x, 0.0) + jnp.log1p(jnp.exp(-jnp.abs(x)))


def _rms(x, w):
    return x * lax.rsqrt(jnp.mean(x * x, axis=-1, keepdims=True) + EPS) * w


def _ada_kernel(c_ref, w_ref, b_ref, o_ref):
    o_ref[0] = _dot3(_silu(c_ref[...]), w_ref[0]) + b_ref[0]


def _ada(c_all, w_ada, b_ada):
    nb = c_all.shape[0]
    tn = 1536
    return pl.pallas_call(
        _ada_kernel,
        grid=(DEPTH, 6 * D // tn),
        in_specs=[pl.BlockSpec((nb, D), lambda l, j: (0, 0)),
                  pl.BlockSpec((1, D, tn), lambda l, j: (l, 0, j)),
                  pl.BlockSpec((1, 1, tn), lambda l, j: (l, 0, j))],
        out_specs=pl.BlockSpec((1, nb, tn), lambda l, j: (l, 0, j)),
        out_shape=jax.ShapeDtypeStruct((DEPTH, nb, 6 * D), f32),
        compiler_params=_cparams(("parallel", "parallel")),
        name="ada_mod",
    )(c_all, w_ada, b_ada.reshape(DEPTH, 1, 6 * D))


NM_TB = 512


def _normmod_kernel(x_ref, w_ref, sc_ref, sh_ref, o_ref):
    for s in range(NM_TB // SEG):
        r = slice(s * SEG, (s + 1) * SEG)
        y = _rms(x_ref[r, :], w_ref[...])
        o_ref[r, :] = (y * (1.0 + sc_ref[s:s + 1, :]) + sh_ref[s:s + 1, :]).astype(bf16)


def _normmod(x, w, mod_seg, sc_col, sh_col):
    nseg_b = NM_TB // SEG
    return pl.pallas_call(
        _normmod_kernel,
        grid=(N_TOK // NM_TB,),
        in_specs=[pl.BlockSpec((NM_TB, D), lambda i: (i, 0)),
                  pl.BlockSpec((1, D), lambda i: (0, 0)),
                  pl.BlockSpec((nseg_b, D), lambda i: (i, sc_col)),
                  pl.BlockSpec((nseg_b, D), lambda i: (i, sh_col))],
        out_specs=pl.BlockSpec((NM_TB, D), lambda i: (i, 0)),
        out_shape=jax.ShapeDtypeStruct((N_TOK, D), bf16),
        compiler_params=_cparams(("parallel",)),
        name="norm_mod",
    )(x, w.reshape(1, D), mod_seg, mod_seg)


def _mm_kernel(a_ref, b_ref, o_ref):
    o_ref[...] = jnp.dot(a_ref[...], b_ref[...], preferred_element_type=f32)


def _mm(a, b, tm, tn, name):
    m, k = a.shape
    n = b.shape[1]
    return pl.pallas_call(
        _mm_kernel,
        grid=(n // tn, m // tm),
        in_specs=[pl.BlockSpec((tm, k), lambda j, i: (i, 0)),
                  pl.BlockSpec((k, tn), lambda j, i: (0, j))],
        out_specs=pl.BlockSpec((tm, tn), lambda j, i: (i, j)),
        out_shape=jax.ShapeDtypeStruct((m, n), f32),
        compiler_params=_cparams(("parallel", "parallel")),
        name=name,
    )(a, b)


def _is_prompt(c):
    return c < NSEG_P


def _sample_seq(c):
    return jnp.maximum(c - NSEG_P, 0)


def _prompt_seq(c):
    return jnp.minimum(c // SEG_PER_PSEQ, B_P - 1)


def _gla_kernel(q_ref, k_ref, v_ref, g_ref, low_ref, w2_ref, b2_ref, nw_ref, s0_ref,
                o_ref, stp_ref, sts_ref, st_ref):
    c = pl.program_id(0)
    is_p = _is_prompt(c)
    pos = c % SEG_PER_PSEQ

    @pl.when(jnp.logical_and(is_p, pos == 0))
    def _():
        st_ref[...] = jnp.zeros_like(st_ref)

    @pl.when(jnp.logical_not(is_p))
    def _():
        for h in range(GLA_H):
            st_ref[h] = s0_ref[0, h].T

    pre = _dot3(low_ref[...], w2_ref[...]) + b2_ref[...]
    gk = -_softplus(-pre) / GLA_GATE_NORM
    b = _cumsum_rows(gk)
    b_last = b[SEG - 1:SEG, :]
    e_b = jnp.exp(b)
    e_nb = jnp.exp(-b)
    e_end = jnp.exp(b_last - b)
    e_last = jnp.exp(b_last)
    causal = lax.broadcasted_iota(jnp.int32, (SEG, SEG), 0) >= lax.broadcasted_iota(jnp.int32, (SEG, SEG), 1)
    for h in range(GLA_H):
        ks = slice(h * GLA_DK, (h + 1) * GLA_DK)
        vs = slice(h * GLA_DV, (h + 1) * GLA_DV)
        k = k_ref[:, ks]
        q_t = (q_ref[:, ks] * (GLA_DK ** -0.5)) * e_b[:, ks]
        att = jnp.where(causal, _dot_nt(q_t, k * e_nb[:, ks]), 0.0)
        v = v_ref[:, vs]
        st = st_ref[h]
        o = _dot(att, v) + _dot_nt(q_t, st)
        st_ref[h] = st * e_last[:, ks] + _dot_tn(v, k * e_end[:, ks])
        o_ref[:, vs] = (_rms(o, nw_ref[...]) * _silu(g_ref[:, vs])).astype(bf16)

    @pl.when(jnp.logical_and(is_p, pos == SEG_PER_PSEQ - 1))
    def _():
        for h in range(GLA_H):
            stp_ref[0, h] = st_ref[h].T

    @pl.when(jnp.logical_not(is_p))
    def _():
        for h in range(GLA_H):
            sts_ref[0, h] = st_ref[h].T


def _gla(p, w2pad, b2, norm_w, s0):
    st_block = (1, GLA_H, GLA_DK, GLA_DV)
    kd, vd = GLA_H * GLA_DK, GLA_H * GLA_DV
    return pl.pallas_call(
        _gla_kernel,
        grid=(NSEG,),
        in_specs=[pl.BlockSpec((SEG, kd), lambda c: (c, C_Q // kd)),
                  pl.BlockSpec((SEG, kd), lambda c: (c, C_K // kd)),
                  pl.BlockSpec((SEG, vd), lambda c: (c, C_V // vd)),
                  pl.BlockSpec((SEG, vd), lambda c: (c, C_GO // vd)),
                  pl.BlockSpec((SEG, LANE), lambda c: (c, C_GKLOW // LANE)),
                  pl.BlockSpec((LANE, kd), lambda c: (0, 0)),
                  pl.BlockSpec((1, kd), lambda c: (0, 0)),
                  pl.BlockSpec((1, GLA_DV), lambda c: (0, 0)),
                  pl.BlockSpec(st_block, lambda c: (_sample_seq(c), 0, 0, 0))],
        out_specs=[pl.BlockSpec((SEG, vd), lambda c: (c, 0)),
                   pl.BlockSpec(st_block, lambda c: (_prompt_seq(c), 0, 0, 0)),
                   pl.BlockSpec(st_block, lambda c: (_sample_seq(c), 0, 0, 0))],
        out_shape=[jax.ShapeDtypeStruct((N_TOK, vd), bf16),
                   jax.ShapeDtypeStruct((B_P, GLA_H, GLA_DK, GLA_DV), f32),
                   jax.ShapeDtypeStruct((B_S, GLA_H, GLA_DK, GLA_DV), f32)],
        scratch_shapes=[pltpu.VMEM((GLA_H, GLA_DV, GLA_DK), f32)],
        compiler_params=_cparams(("arbitrary",)),
        name="gla_mixer",
    )(p, p, p, p, p, w2pad, b2.reshape(1, -1), norm_w.reshape(1, -1), s0)


EXT = SEG + 8


def _ssd_kernel(xbc_ref, z_ref, dt_ref, cp_ref, cw_ref, cb_ref, dtb_ref, alog_ref, dvec_ref, nw_ref, s0_ref,
                o_ref, stp_ref, sts_ref, cvp_ref, cvs_ref, ext, st_ref):
    c = pl.program_id(0)
    is_p = _is_prompt(c)
    pos = c % SEG_PER_PSEQ

    @pl.when(jnp.logical_and(is_p, pos == 0))
    def _():
        ext[0:8, :] = jnp.zeros((8, SSD_CONV_DIM), f32)
        st_ref[...] = jnp.zeros_like(st_ref)

    @pl.when(jnp.logical_not(is_p))
    def _():
        ext[0:8, :] = jnp.zeros((8, SSD_CONV_DIM), f32)
        ext[8 - (CONV_W - 1):8, :] = cp_ref[0]
        st_ref[...] = s0_ref[0].reshape(st_ref.shape)

    ext[8:EXT, :] = xbc_ref[...]
    acc = cb_ref[...] + ext[pl.ds(8 - (CONV_W - 1), SEG), :] * cw_ref[0:1, :]
    for i in range(1, CONV_W):
        acc = acc + ext[pl.ds(8 - (CONV_W - 1) + i, SEG), :] * cw_ref[i:i + 1, :]
    ext[0:8, :] = ext[SEG:EXT, :]
    conv = _silu(acc)

    dt = _softplus(dt_ref[...] + dtb_ref[...])
    a = _cumsum_rows(dt * (-jnp.exp(alog_ref[...])))
    a_last = a[SEG - 1:SEG, :]
    e_a = jnp.exp(a)
    w_col = dt * jnp.exp(a_last - a)
    e_last = jnp.exp(a_last)
    causal = lax.broadcasted_iota(jnp.int32, (SEG, SEG), 0) >= lax.broadcasted_iota(jnp.int32, (SEG, SEG), 1)
    lane_lo = lax.broadcasted_iota(jnp.int32, (SEG, 2 * SSD_P), 1) < SSD_P
    sub_lo = lax.broadcasted_iota(jnp.int32, (2 * SSD_P, SSD_N), 0) < SSD_P
    nbc = SSD_G * SSD_N

    def pair_cols(m, ja, jb):
        return jnp.where(lane_lo, jnp.broadcast_to(m[:, ja:ja + 1], (SEG, 2 * SSD_P)),
                         jnp.broadcast_to(m[:, jb:jb + 1], (SEG, 2 * SSD_P)))

    for g in range(SSD_G):
        gl = slice(g * LANE, (g + 1) * LANE)
        a_g, dt_g, ea_g, wc_g, el_g = a[:, gl], dt[:, gl], e_a[:, gl], w_col[:, gl], e_last[:, gl]
        a_t = a_g.T
        dt_t = dt_g.T
        bc = conv[:, SSD_INNER + g * SSD_N:SSD_INNER + (g + 1) * SSD_N]
        cc = conv[:, SSD_INNER + nbc + g * SSD_N:SSD_INNER + nbc + (g + 1) * SSD_N]
        cbm = _dot_nt(cc, bc)

        def w_intra(j):
            seg = a_g[:, j:j + 1] - a_t[j:j + 1, :]
            return cbm * jnp.exp(jnp.where(causal, seg, -jnp.inf)) * dt_t[j:j + 1, :]

        ys = []
        for p in range(SSD_HG // 2):
            ja, jb = 2 * p, 2 * p + 1
            pp = g * (SSD_HG // 2) + p
            xp = conv[:, pp * 2 * SSD_P:(pp + 1) * 2 * SSD_P]
            y_intra = jnp.where(lane_lo, _dot(w_intra(ja), xp), _dot(w_intra(jb), xp))
            sp = st_ref[pp]
            y_inter = _dot_nt(cc, sp) * pair_cols(ea_g, ja, jb)
            ds = _dot_tn(xp * pair_cols(wc_g, ja, jb), bc)
            rs = jnp.where(sub_lo, jnp.broadcast_to(el_g[:, ja:ja + 1], (2 * SSD_P, SSD_N)),
                           jnp.broadcast_to(el_g[:, jb:jb + 1], (2 * SSD_P, SSD_N)))
            st_ref[pp] = sp * rs + ds
            ys.append(y_intra + y_inter + dvec_ref[:, pp * 2 * SSD_P:(pp + 1) * 2 * SSD_P] * xp)
        gw = slice(g * SSD_GW, (g + 1) * SSD_GW)
        y = jnp.concatenate(ys, axis=1) * _silu(z_ref[:, gw])
        o_ref[:, gw] = _rms(y, nw_ref[:, gw]).astype(bf16)

    @pl.when(jnp.logical_and(is_p, pos == SEG_PER_PSEQ - 1))
    def _():
        stp_ref[0] = st_ref[...].reshape(stp_ref.shape[1:])
        cvp_ref[0] = xbc_ref[SEG - (CONV_W - 1):SEG, :]

    @pl.when(jnp.logical_not(is_p))
    def _():
        sts_ref[0] = st_ref[...].reshape(sts_ref.shape[1:])
        cvs_ref[0] = xbc_ref[SEG - (CONV_W - 1):SEG, :]


def _ssd(p, conv_prev, conv_w, conv_b, dtb, alog, dvec, norm_w, s0):
    st_block = (1, SSD_H, SSD_P, SSD_N)
    cv_block = (1, CONV_W - 1, SSD_CONV_DIM)
    gl = SSD_G * LANE
    full = lambda c: (0, 0)
    return pl.pallas_call(
        _ssd_kernel,
        grid=(NSEG,),
        in_specs=[pl.BlockSpec((SEG, SSD_CONV_DIM), lambda c: (c, C_XBC // SSD_CONV_DIM)),
                  pl.BlockSpec((SEG, SSD_INNER), lambda c: (c, C_Z // SSD_INNER)),
                  pl.BlockSpec((SEG, gl), lambda c: (c, C_DT // gl)),
                  pl.BlockSpec(cv_block, lambda c: (_sample_seq(c), 0, 0)),
                  pl.BlockSpec((CONV_W, SSD_CONV_DIM), full),
                  pl.BlockSpec((1, SSD_CONV_DIM), full),
                  pl.BlockSpec((1, gl), full),
                  pl.BlockSpec((1, gl), full),
                  pl.BlockSpec((1, SSD_INNER), full),
                  pl.BlockSpec((1, SSD_INNER), full),
                  pl.BlockSpec(st_block, lambda c: (_sample_seq(c), 0, 0, 0))],
        out_specs=[pl.BlockSpec((SEG, SSD_INNER), lambda c: (c, 0)),
                   pl.BlockSpec(st_block, lambda c: (_prompt_seq(c), 0, 0, 0)),
                   pl.BlockSpec(st_block, lambda c: (_sample_seq(c), 0, 0, 0)),
                   pl.BlockSpec(cv_block, lambda c: (_prompt_seq(c), 0, 0)),
                   pl.BlockSpec(cv_block, lambda c: (_sample_seq(c), 0, 0))],
        out_shape=[jax.ShapeDtypeStruct((N_TOK, SSD_INNER), bf16),
                   jax.ShapeDtypeStruct((B_P, SSD_H, SSD_P, SSD_N), f32),
                   jax.ShapeDtypeStruct((B_S, SSD_H, SSD_P, SSD_N), f32),
                   jax.ShapeDtypeStruct((B_P, CONV_W - 1, SSD_CONV_DIM), f32),
                   jax.ShapeDtypeStruct((B_S, CONV_W - 1, SSD_CONV_DIM), f32)],
        scratch_shapes=[pltpu.VMEM((EXT, SSD_CONV_DIM), f32),
                        pltpu.VMEM((SSD_H // 2, 2 * SSD_P, SSD_N), f32)],
        compiler_params=_cparams(("arbitrary",)),
        name="ssd_mixer",
    )(p, p, p, conv_prev, conv_w, conv_b.reshape(1, -1), dtb, alog, dvec, norm_w.reshape(1, -1), s0)


MIX_TB = 512


def _mix_kernel(oa_ref, yb_ref, gt_ref, x_ref, ga_ref, sc_ref, sh_ref, nw_ref, wa_ref, wb_ref, wo_ref,
                x1_ref, h2_ref, h2t_ref):
    br_a = jnp.dot(oa_ref[...], wa_ref[...], preferred_element_type=f32)
    br_b = jnp.dot(yb_ref[...], wb_ref[...], preferred_element_type=f32)
    g_a = jax.nn.sigmoid(gt_ref[:, 0:D])
    g_b = jax.nn.sigmoid(gt_ref[:, D:2 * D])
    y = _dot(g_a * br_a + g_b * br_b, wo_ref[...])
    for s in range(MIX_TB // SEG):
        r = slice(s * SEG, (s + 1) * SEG)
        x1 = x_ref[r, :] + ga_ref[s:s + 1, :] * y[r, :]
        x1_ref[r, :] = x1
        h2_ref[r, :] = (_rms(x1, nw_ref[...]) * (1.0 + sc_ref[s:s + 1, :]) + sh_ref[s:s + 1, :]).astype(bf16)
    h2t_ref[...] = h2_ref[...].T


def _mix(oa, yb, p, x, mod_seg, norm2_w, wa, wb, wo):
    nseg_b = MIX_TB // SEG
    full = lambda i: (0, 0)
    return pl.pallas_call(
        _mix_kernel,
        grid=(N_TOK // MIX_TB,),
        in_specs=[pl.BlockSpec((MIX_TB, D), lambda i: (i, 0)),
                  pl.BlockSpec((MIX_TB, SSD_INNER), lambda i: (i, 0)),
                  pl.BlockSpec((MIX_TB, 2 * D), lambda i: (i, C_GATE // (2 * D))),
                  pl.BlockSpec((MIX_TB, D), lambda i: (i, 0)),
                  pl.BlockSpec((nseg_b, D), lambda i: (i, 2)),
                  pl.BlockSpec((nseg_b, D), lambda i: (i, 4)),
                  pl.BlockSpec((nseg_b, D), lambda i: (i, 3)),
                  pl.BlockSpec((1, D), full),
                  pl.BlockSpec((D, D), full),
                  pl.BlockSpec((SSD_INNER, D), full),
                  pl.BlockSpec((D, D), full)],
        out_specs=[pl.BlockSpec((MIX_TB, D), lambda i: (i, 0)),
                   pl.BlockSpec((MIX_TB, D), lambda i: (i, 0)),
                   pl.BlockSpec((D, MIX_TB), lambda i: (0, i))],
        out_shape=[jax.ShapeDtypeStruct((N_TOK, D), f32),
                   jax.ShapeDtypeStruct((N_TOK, D), bf16),
                   jax.ShapeDtypeStruct((D, N_TOK), bf16)],
        compiler_params=_cparams(("parallel",)),
        name="mix_out",
    )(oa, yb, p, x, mod_seg, mod_seg, mod_seg, norm2_w.reshape(1, D), wa, wb, wo)


RT_TL = 128
HALF = PEER_DQ // 2


RT_HPS = 4
SUBLANES = 8


def _batcher_sort_net(n):
    def merge(lo, hi, r):
        step = 2 * r
        if step < hi - lo:
            yield from merge(lo, hi, step)
            yield from merge(lo + r, hi, step)
            yield from ((i, i + r) for i in range(lo + r, hi - r, step))
        else:
            yield (lo, lo + r)

    def sort(lo, hi):
        if hi - lo >= 1:
            mid = lo + (hi - lo) // 2
            yield from sort(lo, mid)
            yield from sort(mid + 1, hi)
            yield from merge(lo, hi, 1)

    return tuple(sort(0, n - 1))


def _bitonic_merge_net(n):
    net, d = [], n // 2
    while d >= 1:
        net += [(i, i + d) for i in range(n) if (i // d) % 2 == 0]
        d //= 2
    return tuple(net)


_SORT16 = _batcher_sort_net(PEER_TOPK)
_MERGE16 = _bitonic_merge_net(PEER_TOPK)
N_CAND_VREGS = 10


def _compare_exchange(x, net):
    for i, j in net:
        x[i], x[j] = jnp.maximum(x[i], x[j]), jnp.minimum(x[i], x[j])


def _merge_across_sublanes(x, n_valid):
    for shift in (4, 2, 1):
        y = [pltpu.roll(v, shift, 0) for v in x]
        merged = []
        for k in range(PEER_TOPK):
            a = x[k] if k < n_valid else None
            b = y[PEER_TOPK - 1 - k] if PEER_TOPK - 1 - k < n_valid else None
            merged.append(jnp.maximum(a, b) if (a is not None and b is not None) else (a if b is None else b))
        x = merged
        _compare_exchange(x, _MERGE16)
        n_valid = PEER_TOPK
    return x


def _top16_sorted(s):
    x = [s[SUBLANES * k:SUBLANES * (k + 1), :] for k in range(PEER_NK // SUBLANES)]
    _compare_exchange(x, _SORT16)
    return _merge_across_sublanes(x, PEER_TOPK)


def _route_kernel(q_ref, k1_ref, k2_ref, thr_ref, s2_ref, f1_ref, f2_ref):
    sub = lax.broadcasted_iota(jnp.int32, (SUBLANES, RT_TL), 0)
    ninf = jnp.float32(-jnp.inf)

    def by_sublane(vs):
        out = vs[0]
        for r in range(1, SUBLANES):
            out = jnp.where(sub == r, vs[r], out)
        return out

    for hh in range(RT_HPS):
        s1 = _dot3_nt(k1_ref[hh], q_ref[:, hh * PEER_DQ:hh * PEER_DQ + HALF])
        s2 = _dot3_nt(k2_ref[hh], q_ref[:, hh * PEER_DQ + HALF:(hh + 1) * PEER_DQ])
        s2_ref[hh] = s2
        v1 = _top16_sorted(s1)
        v2 = _top16_sorted(s2)
        v2lo, v2hi, v1hi = by_sublane(v2[:8]), by_sublane(v2[8:]), by_sublane(v1[8:])
        cands = [v1[0] + v2lo, v1[0] + v2hi, v1[1] + v2lo]
        for i in range(2, 8):
            cands.append(jnp.where(sub < PEER_TOPK // (i + 1), v1[i] + v2lo, ninf))
        cands.append(v1hi + v2[0])
        assert len(cands) == N_CAND_VREGS
        _compare_exchange(cands, tuple((i, j) for i, j in _SORT16 if j < N_CAND_VREGS))
        top = _merge_across_sublanes(cands, N_CAND_VREGS)
        zsum = jnp.ones_like(top[0])
        for k in range(1, PEER_TOPK):
            zsum = zsum + jnp.exp(top[k] - top[0])
        tau = top[PEER_TOPK - 1]
        for k in range(PEER_NK // SUBLANES):
            rows = slice(SUBLANES * k, SUBLANES * (k + 1))
            s1k = s1[rows, :]
            thr = jnp.full_like(s1k, jnp.inf)
            for j in range(PEER_TOPK):
                thr = jnp.where(s1k + v2[j] >= tau, v2[j], thr)
            thr_ref[hh, rows, :] = thr
        f1_ref[hh] = jnp.exp(s1 - v1[0][0:1, :]) * (0.5 / zsum[0:1, :])
        f2_ref[hh] = jnp.exp(s2 - v2[0][0:1, :])


def _route(q, keys1, keys2):
    tile = pl.BlockSpec((RT_HPS, PEER_NK, RT_TL), lambda i, h: (h, 0, i))
    big = jax.ShapeDtypeStruct((PEER_H, PEER_NK, N_TOK), f32)
    return pl.pallas_call(
        _route_kernel,
        grid=(N_TOK // RT_TL, PEER_H // RT_HPS),
        in_specs=[pl.BlockSpec((RT_TL, RT_HPS * PEER_DQ), lambda i, h: (i, h)),
                  pl.BlockSpec((RT_HPS, PEER_NK, HALF), lambda i, h: (h, 0, 0)),
                  pl.BlockSpec((RT_HPS, PEER_NK, HALF), lambda i, h: (h, 0, 0))],
        out_specs=[tile, tile, tile, tile],
        out_shape=[big, big, big, big],
        compiler_params=_cparams(("parallel", "parallel")),
        name="peer_route",
    )(q, keys1, keys2)


PE_TB = 512
PE_NA = 4
PE_CH = PE_NA * PEER_NK
PE_NCH = PEER_NK * PEER_NK // PE_CH
PE_BH = 32
SQRT_HALF = 0.7071067811865476
MXU = 256


def _peer_kernel(*refs, final):
    if final:
        (u0_ref, u_ref, vt_ref, xt_ref, thr_ref, f1_ref, s2_ref, f2_ref, x1_ref, ga_ref, fw_ref,
         o_ref, acc_ref, pw_ref, gel_ref, rthr_ref, rf1_ref) = refs
    else:
        (u0_ref, u_ref, vt_ref, xt_ref, thr_ref, f1_ref, s2_ref, f2_ref, x1_ref, ga_ref,
         o_ref, acc_ref, pw_ref, gel_ref, rthr_ref, rf1_ref) = refs
    j = pl.program_id(1)

    @pl.when(j == 0)
    def _():
        acc_ref[...] = jnp.zeros_like(acc_ref)
        pw_ref[1] = jnp.zeros(pw_ref.shape[1:], bf16)
        act0 = jnp.dot(u0_ref[0], xt_ref[...], preferred_element_type=f32)
        gel_ref[0] = act0 * (1.0 + lax.erf(act0 * SQRT_HALF))

    cur = j % 2
    prev = 1 - cur
    jc = jnp.minimum(j, PE_NCH - 1)

    for i in range(PE_NA):
        for h in range(PEER_H):
            a = jc * PE_NA + i
            rthr_ref[i * PEER_H + h] = jnp.broadcast_to(thr_ref[h, pl.ds(a, 1), :], (8, PE_TB))
            rf1_ref[i * PEER_H + h] = jnp.broadcast_to(f1_ref[h, pl.ds(a, 1), :], (8, PE_TB))

    nsub = PE_BH // 8
    n_bh = PEER_NK // PE_BH

    def weight_tile(t):
        l0 = pl.multiple_of((t // n_bh) * LANE, LANE)
        r0 = pl.multiple_of((t % n_bh) * PE_BH, PE_BH)
        lanes = pl.ds(l0, LANE)
        ws = [jnp.zeros((nsub, 8, LANE), f32) for _ in range(PE_NA)]
        for h in range(PEER_H):
            s2t = s2_ref[h, pl.ds(r0, PE_BH), lanes].reshape(nsub, 8, LANE)
            f2t = f2_ref[h, pl.ds(r0, PE_BH), lanes].reshape(nsub, 8, LANE)
            for i in range(PE_NA):
                sel = s2t >= rthr_ref[i * PEER_H + h, :, lanes][None]
                ws[i] = ws[i] + jnp.where(sel, f2t * rf1_ref[i * PEER_H + h, :, lanes][None], 0.0)
        for i in range(PE_NA):
            er = pl.ds(pl.multiple_of(i * PEER_NK + r0, PE_BH), PE_BH)
            pw_ref[cur, er, lanes] = (ws[i].reshape(PE_BH, LANE) * gel_ref[cur, er, lanes]).astype(bf16)

    n_half = PE_TB // MXU
    n_tiles = (PE_TB // LANE) * n_bh
    tiles_per_group = n_tiles // n_half

    def stage(p, carry):
        n0 = pl.multiple_of(p * MXU, MXU)
        act = jnp.dot(u_ref[0], xt_ref[:, pl.ds(n0, MXU)], preferred_element_type=f32)
        gel_ref[prev, :, pl.ds(n0, MXU)] = act * (1.0 + lax.erf(act * SQRT_HALF))
        acc_ref[:, pl.ds(n0, MXU)] += jnp.dot(vt_ref[0], pw_ref[prev, :, pl.ds(n0, MXU)],
                                              preferred_element_type=f32)
        for q in range(tiles_per_group):
            weight_tile(p * tiles_per_group + q)
        return carry

    lax.fori_loop(0, n_half, stage, 0)

    @pl.when(j == PE_NCH)
    def _():
        out = acc_ref[...].T
        for s in range(PE_TB // SEG):
            r = slice(s * SEG, (s + 1) * SEG)
            x2 = x1_ref[r, :] + ga_ref[s:s + 1, :] * out[r, :]
            o_ref[r, :] = _rms(x2, fw_ref[...]) if final else x2


def _peer(u_c, vt_c, h2t, thr, f1t, s2t, f2t, x1, mod_seg, final_w):
    final = final_w is not None
    nseg_b = PE_TB // SEG
    rt = pl.BlockSpec((PEER_H, PEER_NK, PE_TB), lambda i, j: (0, 0, i))
    in_specs = [pl.BlockSpec((1, PE_CH, D), lambda i, j: (0, 0, 0)),
                pl.BlockSpec((1, PE_CH, D), lambda i, j: (jnp.minimum(j + 1, PE_NCH - 1), 0, 0)),
                pl.BlockSpec((1, D, PE_CH), lambda i, j: (jnp.maximum(j - 1, 0), 0, 0)),
                pl.BlockSpec((D, PE_TB), lambda i, j: (0, i)),
                rt, rt, rt, rt,
                pl.BlockSpec((PE_TB, D), lambda i, j: (i, 0)),
                pl.BlockSpec((nseg_b, D), lambda i, j: (i, 5))]
    args = [u_c, u_c, vt_c, h2t, thr, f1t, s2t, f2t, x1, mod_seg]
    if final:
        in_specs.append(pl.BlockSpec((1, D), lambda i, j: (0, 0)))
        args.append(final_w.reshape(1, D))
    return pl.pallas_call(
        functools.partial(_peer_kernel, final=final),
        grid=(N_TOK // PE_TB, PE_NCH + 1),
        in_specs=in_specs,
        out_specs=pl.BlockSpec((PE_TB, D), lambda i, j: (i, 0)),
        out_shape=jax.ShapeDtypeStruct((N_TOK, D), f32),
        scratch_shapes=[pltpu.VMEM((D, PE_TB), f32), pltpu.VMEM((2, PE_CH, PE_TB), bf16),
                        pltpu.VMEM((2, PE_CH, PE_TB), f32),
                        pltpu.VMEM((PE_NA * PEER_H, 8, PE_TB), f32), pltpu.VMEM((PE_NA * PEER_H, 8, PE_TB), f32)],
        compiler_params=_cparams(("parallel", "arbitrary")),
        name="peer_experts_final" if final else "peer_experts",
    )(*args)


def _reorder_w_in(w):
    o = np.cumsum([0, 512, 512, 1024, 1024, GLA_RANK, SSD_INNER, SSD_CONV_DIM, SSD_H, 2 * D])
    q_k_v_go = w[:, o[0]:o[4]]
    gk_low = w[:, o[4]:o[5]]
    z = w[:, o[5]:o[6]]
    xbc = w[:, o[6]:o[7]]
    dt = w[:, o[7]:o[8]]
    gates = w[:, o[8]:o[9]]
    pad = lambda m: jnp.pad(m, ((0, 0), (0, LANE - m.shape[1])))
    dts = [pad(dt[:, g * SSD_HG:(g + 1) * SSD_HG]) for g in range(SSD_G)]
    return jnp.concatenate([q_k_v_go, xbc, z, gates] + dts + [pad(gk_low)], axis=1).astype(bf16)


def kernel(x_prompt, x_sample, state_gla, state_ssd, state_conv, c_prompt, c_sample, w_ada, b_ada, norm1_w, w_in, gla_gk_w2, gla_gk_b, gla_norm_w, gla_proj, ssd_conv_w, ssd_conv_b, ssd_dt_bias, ssd_A_log, ssd_D, ssd_norm_w, ssd_proj, w_out, norm2_w, peer_wq, peer_keys1, peer_keys2, peer_u, peer_v, final_norm_w):
    x = jnp.concatenate([x_prompt.reshape(N_P, D), x_sample.reshape(N_S, D)], axis=0)
    c_all = jnp.concatenate([c_prompt, c_sample], axis=0)
    mod = _ada(c_all, w_ada, b_ada)
    seg2seq = np.concatenate([np.repeat(np.arange(B_P), SEG_PER_PSEQ), B_P + np.arange(B_S)])

    pad_lane = lambda m: jnp.pad(m, ((0, 0), (0, LANE - m.shape[1])))
    gla_st, ssd_st, conv_st = [], [], []
    for l in range(DEPTH):
        mod_seg = mod[l][seg2seq]
        h = _normmod(x, norm1_w[l], mod_seg, 1, 0)
        p = _mm(h, _reorder_w_in(w_in[l]), 512, 2176, "in_proj")

        w2pad = jnp.pad(gla_gk_w2[l], ((0, LANE - GLA_RANK), (0, 0)))
        oa, gla_p, gla_s = _gla(p, w2pad, gla_gk_b[l], gla_norm_w[l], state_gla[l])

        dtb = pad_lane(ssd_dt_bias[l].reshape(SSD_G, SSD_HG)).reshape(1, SSD_G * LANE)
        alog = pad_lane(ssd_A_log[l].reshape(SSD_G, SSD_HG)).reshape(1, SSD_G * LANE)
        dvec = jnp.repeat(ssd_D[l], SSD_P).reshape(1, SSD_INNER)
        yb, ssd_p, ssd_s, conv_p, conv_s = _ssd(p, state_conv[l], ssd_conv_w[l], ssd_conv_b[l], dtb, alog, dvec,
                                ssd_norm_w[l], state_ssd[l])

        x1, h2, h2t = _mix(oa, yb, p, x, mod_seg, norm2_w[l], gla_proj[l].astype(bf16),
                           ssd_proj[l].astype(bf16), w_out[l].astype(bf16))
        q = _mm(h2, peer_wq[l].astype(bf16), 512, PEER_H * PEER_DQ, "peer_query")
        thr, s2t, f1t, f2t = _route(q, peer_keys1[l], peer_keys2[l])
        u_c = peer_u[l].astype(bf16).reshape(PE_NCH, PE_CH, D)
        vt_c = peer_v[l].astype(bf16).reshape(PE_NCH, PE_CH, D).transpose(0, 2, 1)
        x = _peer(u_c, vt_c, h2t, thr, f1t, s2t, f2t, x1, mod_seg,
                  final_norm_w if l == DEPTH - 1 else None)

        gla_st.append((gla_p, gla_s))
        ssd_st.append((ssd_p, ssd_s))
        conv_st.append((conv_p, conv_s))

    y_prompt = x[:N_P].reshape(B_P, T_P, D)
    y_sample = x[N_P:].reshape(B_S, T_S, D)
    stack = lambda pairs, k: jnp.stack([pr[k] for pr in pairs])
    return (y_prompt, y_sample, stack(gla_st, 0), stack(ssd_st, 0), stack(conv_st, 0),
            stack(gla_st, 1), stack(ssd_st, 1), stack(conv_st, 1))
```

```python
import functools

import jax
import jax.numpy as jnp
import numpy as np
from jax import lax
from jax.experimental import pallas as pl
from jax.experimental.pallas import tpu as pltpu

f32 = jnp.float32
bf16 = jnp.bfloat16

D = 1024
DEPTH = 2
B_P, T_P = 8, 2048
B_S, T_S = 32, 64
N_P = B_P * T_P
N_S = B_S * T_S
N_TOK = N_P + N_S
SEG = 64
NSEG = N_TOK // SEG
NSEG_P = N_P // SEG
SEG_PER_PSEQ = T_P // SEG
EPS = 1e-6

GLA_H, GLA_DK, GLA_DV = 4, 128, 256
GLA_RANK = 16
GLA_GATE_NORM = 16.0
SSD_INNER = 2048
SSD_P = 64
SSD_H = 32
SSD_G = 4
SSD_N = 128
SSD_HG = SSD_H // SSD_G
SSD_GW = SSD_INNER // SSD_G
CONV_W = 4
SSD_CONV_DIM = SSD_INNER + 2 * SSD_G * SSD_N

PEER_H = 8
PEER_NK = 128
PEER_DQ = 256
PEER_TOPK = 16

C_Q, C_K, C_V, C_GO = 0, 512, 1024, 2048
C_XBC, C_Z, C_GATE = 3072, 6144, 8192
C_LOW = 10240
LOW_DT0 = 32
P_COLS = C_LOW + 128

LANE = 128
VMEM_LIMIT = 56 * 1024 * 1024


def _cparams(sem):
    return pltpu.CompilerParams(dimension_semantics=sem, vmem_limit_bytes=VMEM_LIMIT)


def _dot(a, b):
    return jnp.dot(a.astype(bf16), b.astype(bf16), preferred_element_type=f32)


def _dot_nt(a, b):
    return lax.dot_general(a.astype(bf16), b.astype(bf16), (((1,), (1,)), ((), ())), preferred_element_type=f32)


def _dot_tn(a, b):
    return lax.dot_general(a.astype(bf16), b.astype(bf16), (((0,), (0,)), ((), ())), preferred_element_type=f32)


def _split3(x):
    hi = x.astype(bf16)
    r = x - hi.astype(f32)
    mid = r.astype(bf16)
    lo = (r - mid.astype(f32)).astype(bf16)
    return hi, mid, lo


def _dot3(a, b):
    ah, am, _ = _split3(a)
    bh, bm, _ = _split3(b)
    d = functools.partial(jnp.dot, preferred_element_type=f32)
    return d(ah, bh) + (d(ah, bm) + d(am, bh))


def _dot3_nt(a, b):
    ah, am, _ = _split3(a)
    bh, bm, _ = _split3(b)
    d = functools.partial(lax.dot_general, dimension_numbers=(((1,), (1,)), ((), ())), preferred_element_type=f32)
    return d(ah, bh) + (d(ah, bm) + d(am, bh))


def _cumsum_rows(x):
    n = x.shape[0]
    tri = (lax.broadcasted_iota(jnp.int32, (n, n), 0) >= lax.broadcasted_iota(jnp.int32, (n, n), 1)).astype(bf16)
    hi, mid, lo = _split3(x)
    d = functools.partial(jnp.dot, preferred_element_type=f32)
    return d(tri, hi) + (d(tri, mid) + d(tri, lo))


def _silu(x):
    return x * jax.nn.sigmoid(x)


def _softplus(x):
    return jnp.maximum(x, 0.0) + jnp.log1p(jnp.exp(-jnp.abs(x)))


def _rms(x, w):
    return x * lax.rsqrt(jnp.mean(x * x, axis=-1, keepdims=True) + EPS) * w


def _ada_kernel(c_ref, w_ref, b_ref, o_ref):
    o_ref[0] = _dot3(_silu(c_ref[...]), w_ref[0]) + b_ref[0]


def _ada(c_all, w_ada, b_ada):
    nb = c_all.shape[0]
    tn = 1536
    return pl.pallas_call(
        _ada_kernel,
        grid=(DEPTH, 6 * D // tn),
        in_specs=[pl.BlockSpec((nb, D), lambda l, j: (0, 0)),
                  pl.BlockSpec((1, D, tn), lambda l, j: (l, 0, j)),
                  pl.BlockSpec((1, 1, tn), lambda l, j: (l, 0, j))],
        out_specs=pl.BlockSpec((1, nb, tn), lambda l, j: (l, 0, j)),
        out_shape=jax.ShapeDtypeStruct((DEPTH, nb, 6 * D), f32),
        compiler_params=_cparams(("parallel", "parallel")),
        name="ada_mod",
    )(c_all, w_ada, b_ada.reshape(DEPTH, 1, 6 * D))


NM_TB = 512


def _normmod_kernel(x_ref, w_ref, sc_ref, sh_ref, o_ref):
    for s in range(NM_TB // SEG):
        r = slice(s * SEG, (s + 1) * SEG)
        y = _rms(x_ref[r, :], w_ref[...])
        o_ref[r, :] = (y * (1.0 + sc_ref[s:s + 1, :]) + sh_ref[s:s + 1, :]).astype(bf16)


def _normmod(x, w, mod_seg, sc_col, sh_col):
    nseg_b = NM_TB // SEG
    return pl.pallas_call(
        _normmod_kernel,
        grid=(N_TOK // NM_TB,),
        in_specs=[pl.BlockSpec((NM_TB, D), lambda i: (i, 0)),
                  pl.BlockSpec((1, D), lambda i: (0, 0)),
                  pl.BlockSpec((nseg_b, D), lambda i: (i, sc_col)),
                  pl.BlockSpec((nseg_b, D), lambda i: (i, sh_col))],
        out_specs=pl.BlockSpec((NM_TB, D), lambda i: (i, 0)),
        out_shape=jax.ShapeDtypeStruct((N_TOK, D), bf16),
        compiler_params=_cparams(("parallel",)),
        name="norm_mod",
    )(x, w.reshape(1, D), mod_seg, mod_seg)


def _mm_kernel(a_ref, b_ref, o_ref):
    o_ref[...] = jnp.dot(a_ref[...], b_ref[...], preferred_element_type=f32)


def _mm(a, b, tm, tn, name):
    m, k = a.shape
    n = b.shape[1]
    return pl.pallas_call(
        _mm_kernel,
        grid=(n // tn, m // tm),
        in_specs=[pl.BlockSpec((tm, k), lambda j, i: (i, 0)),
                  pl.BlockSpec((k, tn), lambda j, i: (0, j))],
        out_specs=pl.BlockSpec((tm, tn), lambda j, i: (i, j)),
        out_shape=jax.ShapeDtypeStruct((m, n), f32),
        compiler_params=_cparams(("parallel", "parallel")),
        name=name,
    )(a, b)


def _is_prompt(c):
    return c < NSEG_P


def _sample_seq(c):
    return jnp.maximum(c - NSEG_P, 0)


def _prompt_seq(c):
    return jnp.minimum(c // SEG_PER_PSEQ, B_P - 1)


def _gla_kernel(q_ref, k_ref, v_ref, g_ref, low_ref, w2_ref, b2_ref, nw_ref, s0_ref,
                o_ref, stp_ref, sts_ref, st_ref):
    c = pl.program_id(0)
    is_p = _is_prompt(c)
    pos = c % SEG_PER_PSEQ

    @pl.when(jnp.logical_and(is_p, pos == 0))
    def _():
        st_ref[...] = jnp.zeros_like(st_ref)

    @pl.when(jnp.logical_not(is_p))
    def _():
        for h in range(GLA_H):
            st_ref[h] = s0_ref[0, h].T

    pre = _dot3(low_ref[...], w2_ref[...]) + b2_ref[...]
    gk = -_softplus(-pre) / GLA_GATE_NORM
    b = _cumsum_rows(gk)
    b_last = b[SEG - 1:SEG, :]
    e_b = jnp.exp(b)
    e_nb = jnp.exp(-b)
    e_end = jnp.exp(b_last - b)
    e_last = jnp.exp(b_last)
    causal = lax.broadcasted_iota(jnp.int32, (SEG, SEG), 0) >= lax.broadcasted_iota(jnp.int32, (SEG, SEG), 1)
    for h in range(GLA_H):
        ks = slice(h * GLA_DK, (h + 1) * GLA_DK)
        vs = slice(h * GLA_DV, (h + 1) * GLA_DV)
        k = k_ref[:, ks]
        q_t = (q_ref[:, ks] * (GLA_DK ** -0.5)) * e_b[:, ks]
        att = jnp.where(causal, _dot_nt(q_t, k * e_nb[:, ks]), 0.0)
        v = v_ref[:, vs]
        st = st_ref[h]
        o = _dot(att, v) + _dot_nt(q_t, st)
        st_ref[h] = st * e_last[:, ks] + _dot_tn(v, k * e_end[:, ks])
        o_ref[:, vs] = (_rms(o, nw_ref[...]) * _silu(g_ref[:, vs])).astype(bf16)

    @pl.when(jnp.logical_and(is_p, pos == SEG_PER_PSEQ - 1))
    def _():
        for h in range(GLA_H):
            stp_ref[0, h] = st_ref[h].T

    @pl.when(jnp.logical_not(is_p))
    def _():
        for h in range(GLA_H):
            sts_ref[0, h] = st_ref[h].T


def _gla(p, w2pad, b2, norm_w, s0):
    st_block = (1, GLA_H, GLA_DK, GLA_DV)
    kd, vd = GLA_H * GLA_DK, GLA_H * GLA_DV
    return pl.pallas_call(
        _gla_kernel,
        grid=(NSEG,),
        in_specs=[pl.BlockSpec((SEG, kd), lambda c: (c, C_Q // kd)),
                  pl.BlockSpec((SEG, kd), lambda c: (c, C_K // kd)),
                  pl.BlockSpec((SEG, vd), lambda c: (c, C_V // vd)),
                  pl.BlockSpec((SEG, vd), lambda c: (c, C_GO // vd)),
                  pl.BlockSpec((SEG, LANE), lambda c: (c, C_LOW // LANE)),
                  pl.BlockSpec((LANE, kd), lambda c: (0, 0)),
                  pl.BlockSpec((1, kd), lambda c: (0, 0)),
                  pl.BlockSpec((1, GLA_DV), lambda c: (0, 0)),
                  pl.BlockSpec(st_block, lambda c: (_sample_seq(c), 0, 0, 0))],
        out_specs=[pl.BlockSpec((SEG, vd), lambda c: (c, 0)),
                   pl.BlockSpec(st_block, lambda c: (_prompt_seq(c), 0, 0, 0)),
                   pl.BlockSpec(st_block, lambda c: (_sample_seq(c), 0, 0, 0))],
        out_shape=[jax.ShapeDtypeStruct((N_TOK, vd), bf16),
                   jax.ShapeDtypeStruct((B_P, GLA_H, GLA_DK, GLA_DV), f32),
                   jax.ShapeDtypeStruct((B_S, GLA_H, GLA_DK, GLA_DV), f32)],
        scratch_shapes=[pltpu.VMEM((GLA_H, GLA_DV, GLA_DK), f32)],
        compiler_params=_cparams(("arbitrary",)),
        name="gla_mixer",
    )(p, p, p, p, p, w2pad, b2.reshape(1, -1), norm_w.reshape(1, -1), s0)


EXT = SEG + 8


def _ssd_kernel(xbc_ref, z_ref, dt_ref, cp_ref, cw_ref, cb_ref, dtb_ref, alog_ref, dvec_ref, nw_ref, s0_ref,
                o_ref, stp_ref, sts_ref, cvp_ref, cvs_ref, ext, st_ref):
    c = pl.program_id(0)
    is_p = _is_prompt(c)
    pos = c % SEG_PER_PSEQ

    @pl.when(jnp.logical_and(is_p, pos == 0))
    def _():
        ext[0:8, :] = jnp.zeros((8, SSD_CONV_DIM), f32)
        st_ref[...] = jnp.zeros_like(st_ref)

    @pl.when(jnp.logical_not(is_p))
    def _():
        ext[0:8, :] = jnp.zeros((8, SSD_CONV_DIM), f32)
        ext[8 - (CONV_W - 1):8, :] = cp_ref[0]
        st_ref[...] = s0_ref[0].reshape(st_ref.shape)

    ext[8:EXT, :] = xbc_ref[...]
    acc = cb_ref[...] + ext[pl.ds(8 - (CONV_W - 1), SEG), :] * cw_ref[0:1, :]
    for i in range(1, CONV_W):
        acc = acc + ext[pl.ds(8 - (CONV_W - 1) + i, SEG), :] * cw_ref[i:i + 1, :]
    ext[0:8, :] = ext[SEG:EXT, :]
    conv = _silu(acc)

    dt = _softplus(dt_ref[...] + dtb_ref[...])
    a = _cumsum_rows(dt * (-jnp.exp(alog_ref[...])))
    a_t = a.T
    dt_t = dt.T
    a_last = a[SEG - 1:SEG, :]
    e_a = jnp.exp(a)
    w_col = dt * jnp.exp(a_last - a)
    e_last = jnp.exp(a_last)
    causal = lax.broadcasted_iota(jnp.int32, (SEG, SEG), 0) >= lax.broadcasted_iota(jnp.int32, (SEG, SEG), 1)
    lane_lo = lax.broadcasted_iota(jnp.int32, (SEG, 2 * SSD_P), 1) < SSD_P
    sub_lo = lax.broadcasted_iota(jnp.int32, (2 * SSD_P, SSD_N), 0) < SSD_P
    nbc = SSD_G * SSD_N

    def pair_cols(m, ja, jb):
        return jnp.where(lane_lo, jnp.broadcast_to(m[:, ja:ja + 1], (SEG, 2 * SSD_P)),
                         jnp.broadcast_to(m[:, jb:jb + 1], (SEG, 2 * SSD_P)))

    for g in range(SSD_G):
        j0 = LOW_DT0 + g * SSD_HG
        bc = conv[:, SSD_INNER + g * SSD_N:SSD_INNER + (g + 1) * SSD_N]
        cc = conv[:, SSD_INNER + nbc + g * SSD_N:SSD_INNER + nbc + (g + 1) * SSD_N]
        cbm = _dot_nt(cc, bc)

        def w_intra(j):
            seg = a[:, j:j + 1] - a_t[j:j + 1, :]
            return cbm * jnp.exp(jnp.where(causal, seg, -jnp.inf)) * dt_t[j:j + 1, :]

        ys = []
        for p in range(SSD_HG // 2):
            ja, jb = j0 + 2 * p, j0 + 2 * p + 1
            pp = g * (SSD_HG // 2) + p
            xp = conv[:, pp * 2 * SSD_P:(pp + 1) * 2 * SSD_P]
            y_intra = jnp.where(lane_lo, _dot(w_intra(ja), xp), _dot(w_intra(jb), xp))
            sp = st_ref[pp]
            y_inter = _dot_nt(cc, sp) * pair_cols(e_a, ja, jb)
            ds = _dot_tn(xp * pair_cols(w_col, ja, jb), bc)
            rs = jnp.where(sub_lo, jnp.broadcast_to(e_last[:, ja:ja + 1], (2 * SSD_P, SSD_N)),
                           jnp.broadcast_to(e_last[:, jb:jb + 1], (2 * SSD_P, SSD_N)))
            st_ref[pp] = sp * rs + ds
            ys.append(y_intra + y_inter + dvec_ref[:, pp * 2 * SSD_P:(pp + 1) * 2 * SSD_P] * xp)
        gw = slice(g * SSD_GW, (g + 1) * SSD_GW)
        y = jnp.concatenate(ys, axis=1) * _silu(z_ref[:, gw])
        o_ref[:, gw] = _rms(y, nw_ref[:, gw]).astype(bf16)

    @pl.when(jnp.logical_and(is_p, pos == SEG_PER_PSEQ - 1))
    def _():
        stp_ref[0] = st_ref[...].reshape(stp_ref.shape[1:])
        cvp_ref[0] = xbc_ref[SEG - (CONV_W - 1):SEG, :]

    @pl.when(jnp.logical_not(is_p))
    def _():
        sts_ref[0] = st_ref[...].reshape(sts_ref.shape[1:])
        cvs_ref[0] = xbc_ref[SEG - (CONV_W - 1):SEG, :]


def _ssd(p, conv_prev, conv_w, conv_b, dtb, alog, dvec, norm_w, s0):
    st_block = (1, SSD_H, SSD_P, SSD_N)
    cv_block = (1, CONV_W - 1, SSD_CONV_DIM)
    full = lambda c: (0, 0)
    return pl.pallas_call(
        _ssd_kernel,
        grid=(NSEG,),
        in_specs=[pl.BlockSpec((SEG, SSD_CONV_DIM), lambda c: (c, C_XBC // SSD_CONV_DIM)),
                  pl.BlockSpec((SEG, SSD_INNER), lambda c: (c, C_Z // SSD_INNER)),
                  pl.BlockSpec((SEG, LANE), lambda c: (c, C_LOW // LANE)),
                  pl.BlockSpec(cv_block, lambda c: (_sample_seq(c), 0, 0)),
                  pl.BlockSpec((CONV_W, SSD_CONV_DIM), full),
                  pl.BlockSpec((1, SSD_CONV_DIM), full),
                  pl.BlockSpec((1, LANE), full),
                  pl.BlockSpec((1, LANE), full),
                  pl.BlockSpec((1, SSD_INNER), full),
                  pl.BlockSpec((1, SSD_INNER), full),
                  pl.BlockSpec(st_block, lambda c: (_sample_seq(c), 0, 0, 0))],
        out_specs=[pl.BlockSpec((SEG, SSD_INNER), lambda c: (c, 0)),
                   pl.BlockSpec(st_block, lambda c: (_prompt_seq(c), 0, 0, 0)),
                   pl.BlockSpec(st_block, lambda c: (_sample_seq(c), 0, 0, 0)),
                   pl.BlockSpec(cv_block, lambda c: (_prompt_seq(c), 0, 0)),
                   pl.BlockSpec(cv_block, lambda c: (_sample_seq(c), 0, 0))],
        out_shape=[jax.ShapeDtypeStruct((N_TOK, SSD_INNER), bf16),
                   jax.ShapeDtypeStruct((B_P, SSD_H, SSD_P, SSD_N), f32),
                   jax.ShapeDtypeStruct((B_S, SSD_H, SSD_P, SSD_N), f32),
                   jax.ShapeDtypeStruct((B_P, CONV_W - 1, SSD_CONV_DIM), f32),
                   jax.ShapeDtypeStruct((B_S, CONV_W - 1, SSD_CONV_DIM), f32)],
        scratch_shapes=[pltpu.VMEM((EXT, SSD_CONV_DIM), f32),
                        pltpu.VMEM((SSD_H // 2, 2 * SSD_P, SSD_N), f32)],
        compiler_params=_cparams(("arbitrary",)),
        name="ssd_mixer",
    )(p, p, p, conv_prev, conv_w, conv_b.reshape(1, -1), dtb, alog, dvec, norm_w.reshape(1, -1), s0)


MIX_TB = 512


def _mix_kernel(oa_ref, yb_ref, gt_ref, x_ref, ga_ref, sc_ref, sh_ref, nw_ref, wa_ref, wb_ref, wo_ref,
                x1_ref, h2_ref, h2t_ref):
    br_a = jnp.dot(oa_ref[...], wa_ref[...], preferred_element_type=f32)
    br_b = jnp.dot(yb_ref[...], wb_ref[...], preferred_element_type=f32)
    g_a = jax.nn.sigmoid(gt_ref[:, 0:D])
    g_b = jax.nn.sigmoid(gt_ref[:, D:2 * D])
    y = _dot(g_a * br_a + g_b * br_b, wo_ref[...])
    for s in range(MIX_TB // SEG):
        r = slice(s * SEG, (s + 1) * SEG)
        x1 = x_ref[r, :] + ga_ref[s:s + 1, :] * y[r, :]
        x1_ref[r, :] = x1
        h2_ref[r, :] = (_rms(x1, nw_ref[...]) * (1.0 + sc_ref[s:s + 1, :]) + sh_ref[s:s + 1, :]).astype(bf16)
    h2t_ref[...] = h2_ref[...].T


def _mix(oa, yb, p, x, mod_seg, norm2_w, wa, wb, wo):
    nseg_b = MIX_TB // SEG
    full = lambda i: (0, 0)
    return pl.pallas_call(
        _mix_kernel,
        grid=(N_TOK // MIX_TB,),
        in_specs=[pl.BlockSpec((MIX_TB, D), lambda i: (i, 0)),
                  pl.BlockSpec((MIX_TB, SSD_INNER), lambda i: (i, 0)),
                  pl.BlockSpec((MIX_TB, 2 * D), lambda i: (i, C_GATE // (2 * D))),
                  pl.BlockSpec((MIX_TB, D), lambda i: (i, 0)),
                  pl.BlockSpec((nseg_b, D), lambda i: (i, 2)),
                  pl.BlockSpec((nseg_b, D), lambda i: (i, 4)),
                  pl.BlockSpec((nseg_b, D), lambda i: (i, 3)),
                  pl.BlockSpec((1, D), full),
                  pl.BlockSpec((D, D), full),
                  pl.BlockSpec((SSD_INNER, D), full),
                  pl.BlockSpec((D, D), full)],
        out_specs=[pl.BlockSpec((MIX_TB, D), lambda i: (i, 0)),
                   pl.BlockSpec((MIX_TB, D), lambda i: (i, 0)),
                   pl.BlockSpec((D, MIX_TB), lambda i: (0, i))],
        out_shape=[jax.ShapeDtypeStruct((N_TOK, D), f32),
                   jax.ShapeDtypeStruct((N_TOK, D), bf16),
                   jax.ShapeDtypeStruct((D, N_TOK), bf16)],
        compiler_params=_cparams(("parallel",)),
        name="mix_out",
    )(oa, yb, p, x, mod_seg, mod_seg, mod_seg, norm2_w.reshape(1, D), wa, wb, wo)


RT_TL = 128
HALF = PEER_DQ // 2


RT_HPS = 4
SUBLANES = 8


def _batcher_sort_net(n):
    def merge(lo, hi, r):
        step = 2 * r
        if step < hi - lo:
            yield from merge(lo, hi, step)
            yield from merge(lo + r, hi, step)
            yield from ((i, i + r) for i in range(lo + r, hi - r, step))
        else:
            yield (lo, lo + r)

    def sort(lo, hi):
        if hi - lo >= 1:
            mid = lo + (hi - lo) // 2
            yield from sort(lo, mid)
            yield from sort(mid + 1, hi)
            yield from merge(lo, hi, 1)

    return tuple(sort(0, n - 1))


def _bitonic_merge_net(n):
    net, d = [], n // 2
    while d >= 1:
        net += [(i, i + d) for i in range(n) if (i // d) % 2 == 0]
        d //= 2
    return tuple(net)


_SORT16 = _batcher_sort_net(PEER_TOPK)
_MERGE16 = _bitonic_merge_net(PEER_TOPK)
N_CAND_VREGS = 10


def _compare_exchange(x, net):
    for i, j in net:
        x[i], x[j] = jnp.maximum(x[i], x[j]), jnp.minimum(x[i], x[j])


def _merge_across_sublanes(x, n_valid):
    for shift in (4, 2, 1):
        y = [pltpu.roll(v, shift, 0) for v in x]
        merged = []
        for k in range(PEER_TOPK):
            a = x[k] if k < n_valid else None
            b = y[PEER_TOPK - 1 - k] if PEER_TOPK - 1 - k < n_valid else None
            merged.append(jnp.maximum(a, b) if (a is not None and b is not None) else (a if b is None else b))
        x = merged
        _compare_exchange(x, _MERGE16)
        n_valid = PEER_TOPK
    return x


def _top16_sorted(s):
    x = [s[SUBLANES * k:SUBLANES * (k + 1), :] for k in range(PEER_NK // SUBLANES)]
    _compare_exchange(x, _SORT16)
    return _merge_across_sublanes(x, PEER_TOPK)


def _route_kernel(q_ref, k1_ref, k2_ref, thr_ref, s2_ref, f1_ref, f2_ref):
    sub = lax.broadcasted_iota(jnp.int32, (SUBLANES, RT_TL), 0)
    ninf = jnp.float32(-jnp.inf)

    def by_sublane(vs):
        out = vs[0]
        for r in range(1, SUBLANES):
            out = jnp.where(sub == r, vs[r], out)
        return out

    for hh in range(RT_HPS):
        s1 = _dot3_nt(k1_ref[hh], q_ref[:, hh * PEER_DQ:hh * PEER_DQ + HALF])
        s2 = _dot3_nt(k2_ref[hh], q_ref[:, hh * PEER_DQ + HALF:(hh + 1) * PEER_DQ])
        s2_ref[hh] = s2
        v1 = _top16_sorted(s1)
        v2 = _top16_sorted(s2)
        v2lo, v2hi, v1hi = by_sublane(v2[:8]), by_sublane(v2[8:]), by_sublane(v1[8:])
        cands = [v1[0] + v2lo, v1[0] + v2hi, v1[1] + v2lo]
        for i in range(2, 8):
            cands.append(jnp.where(sub < PEER_TOPK // (i + 1), v1[i] + v2lo, ninf))
        cands.append(v1hi + v2[0])
        assert len(cands) == N_CAND_VREGS
        _compare_exchange(cands, tuple((i, j) for i, j in _SORT16 if j < N_CAND_VREGS))
        top = _merge_across_sublanes(cands, N_CAND_VREGS)
        zsum = jnp.ones_like(top[0])
        for k in range(1, PEER_TOPK):
            zsum = zsum + jnp.exp(top[k] - top[0])
        tau = top[PEER_TOPK - 1]
        for k in range(PEER_NK // SUBLANES):
            rows = slice(SUBLANES * k, SUBLANES * (k + 1))
            s1k = s1[rows, :]
            thr = jnp.full_like(s1k, jnp.inf)
            for j in range(PEER_TOPK):
                thr = jnp.where(s1k + v2[j] >= tau, v2[j], thr)
            thr_ref[hh, rows, :] = thr
        f1_ref[hh] = jnp.exp(s1 - v1[0][0:1, :]) * (0.5 / zsum[0:1, :])
        f2_ref[hh] = jnp.exp(s2 - v2[0][0:1, :])


def _route(q, keys1, keys2):
    tile = pl.BlockSpec((RT_HPS, PEER_NK, RT_TL), lambda i, h: (h, 0, i))
    big = jax.ShapeDtypeStruct((PEER_H, PEER_NK, N_TOK), f32)
    return pl.pallas_call(
        _route_kernel,
        grid=(N_TOK // RT_TL, PEER_H // RT_HPS),
        in_specs=[pl.BlockSpec((RT_TL, RT_HPS * PEER_DQ), lambda i, h: (i, h)),
                  pl.BlockSpec((RT_HPS, PEER_NK, HALF), lambda i, h: (h, 0, 0)),
                  pl.BlockSpec((RT_HPS, PEER_NK, HALF), lambda i, h: (h, 0, 0))],
        out_specs=[tile, tile, tile, tile],
        out_shape=[big, big, big, big],
        compiler_params=_cparams(("parallel", "parallel")),
        name="peer_route",
    )(q, keys1, keys2)


PE_TB = 512
PE_NA = 4
PE_CH = PE_NA * PEER_NK
PE_NCH = PEER_NK * PEER_NK // PE_CH
PE_BH = 32
SQRT_HALF = 0.7071067811865476
MXU = 256


def _peer_kernel(*refs, final):
    if final:
        (u0_ref, u_ref, vt_ref, xt_ref, thr_ref, f1_ref, s2_ref, f2_ref, x1_ref, ga_ref, fw_ref,
         o_ref, acc_ref, pw_ref, gel_ref, rthr_ref, rf1_ref) = refs
    else:
        (u0_ref, u_ref, vt_ref, xt_ref, thr_ref, f1_ref, s2_ref, f2_ref, x1_ref, ga_ref,
         o_ref, acc_ref, pw_ref, gel_ref, rthr_ref, rf1_ref) = refs
    j = pl.program_id(1)

    @pl.when(j == 0)
    def _():
        acc_ref[...] = jnp.zeros_like(acc_ref)
        pw_ref[1] = jnp.zeros(pw_ref.shape[1:], bf16)
        act0 = jnp.dot(u0_ref[0], xt_ref[...], preferred_element_type=f32)
        gel_ref[0] = act0 * (1.0 + lax.erf(act0 * SQRT_HALF))

    cur = j % 2
    prev = 1 - cur
    jc = jnp.minimum(j, PE_NCH - 1)

    for i in range(PE_NA):
        for h in range(PEER_H):
            a = jc * PE_NA + i
            rthr_ref[i * PEER_H + h] = jnp.broadcast_to(thr_ref[h, pl.ds(a, 1), :], (8, PE_TB))
            rf1_ref[i * PEER_H + h] = jnp.broadcast_to(f1_ref[h, pl.ds(a, 1), :], (8, PE_TB))

    nsub = PE_BH // 8
    n_bh = PEER_NK // PE_BH

    def weight_tile(t):
        l0 = pl.multiple_of((t // n_bh) * LANE, LANE)
        r0 = pl.multiple_of((t % n_bh) * PE_BH, PE_BH)
        lanes = pl.ds(l0, LANE)
        ws = [jnp.zeros((nsub, 8, LANE), f32) for _ in range(PE_NA)]
        for h in range(PEER_H):
            s2t = s2_ref[h, pl.ds(r0, PE_BH), lanes].reshape(nsub, 8, LANE)
            f2t = f2_ref[h, pl.ds(r0, PE_BH), lanes].reshape(nsub, 8, LANE)
            for i in range(PE_NA):
                sel = s2t >= rthr_ref[i * PEER_H + h, :, lanes][None]
                ws[i] = ws[i] + jnp.where(sel, f2t * rf1_ref[i * PEER_H + h, :, lanes][None], 0.0)
        for i in range(PE_NA):
            er = pl.ds(pl.multiple_of(i * PEER_NK + r0, PE_BH), PE_BH)
            pw_ref[cur, er, lanes] = (ws[i].reshape(PE_BH, LANE) * gel_ref[cur, er, lanes]).astype(bf16)

    n_half = PE_TB // MXU
    n_tiles = (PE_TB // LANE) * n_bh
    tiles_per_group = n_tiles // n_half

    def stage(p, carry):
        n0 = pl.multiple_of(p * MXU, MXU)
        act = jnp.dot(u_ref[0], xt_ref[:, pl.ds(n0, MXU)], preferred_element_type=f32)
        gel_ref[prev, :, pl.ds(n0, MXU)] = act * (1.0 + lax.erf(act * SQRT_HALF))
        acc_ref[:, pl.ds(n0, MXU)] += jnp.dot(vt_ref[0], pw_ref[prev, :, pl.ds(n0, MXU)],
                                              preferred_element_type=f32)
        for q in range(tiles_per_group):
            weight_tile(p * tiles_per_group + q)
        return carry

    lax.fori_loop(0, n_half, stage, 0)

    @pl.when(j == PE_NCH)
    def _():
        out = acc_ref[...].T
        for s in range(PE_TB // SEG):
            r = slice(s * SEG, (s + 1) * SEG)
            x2 = x1_ref[r, :] + ga_ref[s:s + 1, :] * out[r, :]
            o_ref[r, :] = _rms(x2, fw_ref[...]) if final else x2


def _peer(u_c, vt_c, h2t, thr, f1t, s2t, f2t, x1, mod_seg, final_w):
    final = final_w is not None
    nseg_b = PE_TB // SEG
    rt = pl.BlockSpec((PEER_H, PEER_NK, PE_TB), lambda i, j: (0, 0, i))
    in_specs = [pl.BlockSpec((1, PE_CH, D), lambda i, j: (0, 0, 0)),
                pl.BlockSpec((1, PE_CH, D), lambda i, j: (jnp.minimum(j + 1, PE_NCH - 1), 0, 0)),
                pl.BlockSpec((1, D, PE_CH), lambda i, j: (jnp.maximum(j - 1, 0), 0, 0)),
                pl.BlockSpec((D, PE_TB), lambda i, j: (0, i)),
                rt, rt, rt, rt,
                pl.BlockSpec((PE_TB, D), lambda i, j: (i, 0)),
                pl.BlockSpec((nseg_b, D), lambda i, j: (i, 5))]
    args = [u_c, u_c, vt_c, h2t, thr, f1t, s2t, f2t, x1, mod_seg]
    if final:
        in_specs.append(pl.BlockSpec((1, D), lambda i, j: (0, 0)))
        args.append(final_w.reshape(1, D))
    return pl.pallas_call(
        functools.partial(_peer_kernel, final=final),
        grid=(N_TOK // PE_TB, PE_NCH + 1),
        in_specs=in_specs,
        out_specs=pl.BlockSpec((PE_TB, D), lambda i, j: (i, 0)),
        out_shape=jax.ShapeDtypeStruct((N_TOK, D), f32),
        scratch_shapes=[pltpu.VMEM((D, PE_TB), f32), pltpu.VMEM((2, PE_CH, PE_TB), bf16),
                        pltpu.VMEM((2, PE_CH, PE_TB), f32),
                        pltpu.VMEM((PE_NA * PEER_H, 8, PE_TB), f32), pltpu.VMEM((PE_NA * PEER_H, 8, PE_TB), f32)],
        compiler_params=_cparams(("parallel", "arbitrary")),
        name="peer_experts_final" if final else "peer_experts",
    )(*args)


def _reorder_w_in(w):
    o = np.cumsum([0, 512, 512, 1024, 1024, GLA_RANK, SSD_INNER, SSD_CONV_DIM, SSD_H, 2 * D])
    q_k_v_go = w[:, o[0]:o[4]]
    gk_low = w[:, o[4]:o[5]]
    z = w[:, o[5]:o[6]]
    xbc = w[:, o[6]:o[7]]
    dt = w[:, o[7]:o[8]]
    gates = w[:, o[8]:o[9]]
    pad = lambda m: jnp.pad(m, ((0, 0), (0, LANE - m.shape[1])))
    low = jnp.concatenate([gk_low, jnp.zeros((D, LOW_DT0 - GLA_RANK), w.dtype), dt], axis=1)
    return jnp.concatenate([q_k_v_go, xbc, z, gates, pad(low)], axis=1).astype(bf16)


def kernel(x_prompt, x_sample, state_gla, state_ssd, state_conv, c_prompt, c_sample, w_ada, b_ada, norm1_w, w_in, gla_gk_w2, gla_gk_b, gla_norm_w, gla_proj, ssd_conv_w, ssd_conv_b, ssd_dt_bias, ssd_A_log, ssd_D, ssd_norm_w, ssd_proj, w_out, norm2_w, peer_wq, peer_keys1, peer_keys2, peer_u, peer_v, final_norm_w):
    x = jnp.concatenate([x_prompt.reshape(N_P, D), x_sample.reshape(N_S, D)], axis=0)
    c_all = jnp.concatenate([c_prompt, c_sample], axis=0)
    mod = _ada(c_all, w_ada, b_ada)
    seg2seq = np.concatenate([np.repeat(np.arange(B_P), SEG_PER_PSEQ), B_P + np.arange(B_S)])

    gla_st, ssd_st, conv_st = [], [], []
    for l in range(DEPTH):
        mod_seg = mod[l][seg2seq]
        h = _normmod(x, norm1_w[l], mod_seg, 1, 0)
        p = _mm(h, _reorder_w_in(w_in[l]), 512, P_COLS // 3, "in_proj")

        w2pad = jnp.pad(gla_gk_w2[l], ((0, LANE - GLA_RANK), (0, 0)))
        oa, gla_p, gla_s = _gla(p, w2pad, gla_gk_b[l], gla_norm_w[l], state_gla[l])

        low_lanes = lambda m: jnp.pad(m.reshape(1, SSD_H), ((0, 0), (LOW_DT0, LANE - LOW_DT0 - SSD_H)))
        dtb = low_lanes(ssd_dt_bias[l])
        alog = low_lanes(ssd_A_log[l])
        dvec = jnp.repeat(ssd_D[l], SSD_P).reshape(1, SSD_INNER)
        yb, ssd_p, ssd_s, conv_p, conv_s = _ssd(p, state_conv[l], ssd_conv_w[l], ssd_conv_b[l], dtb, alog, dvec,
                                ssd_norm_w[l], state_ssd[l])

        x1, h2, h2t = _mix(oa, yb, p, x, mod_seg, norm2_w[l], gla_proj[l].astype(bf16),
                           ssd_proj[l].astype(bf16), w_out[l].astype(bf16))
        q = _mm(h2, peer_wq[l].astype(bf16), 512, PEER_H * PEER_DQ, "peer_query")
        thr, s2t, f1t, f2t = _route(q, peer_keys1[l], peer_keys2[l])
        u_c = peer_u[l].astype(bf16).reshape(PE_NCH, PE_CH, D)
        vt_c = peer_v[l].astype(bf16).reshape(PE_NCH, PE_CH, D).transpose(0, 2, 1)
        x = _peer(u_c, vt_c, h2t, thr, f1t, s2t, f2t, x1, mod_seg,
                  final_norm_w if l == DEPTH - 1 else None)

        gla_st.append((gla_p, gla_s))
        ssd_st.append((ssd_p, ssd_s))
        conv_st.append((conv_p, conv_s))

    y_prompt = x[:N_P].reshape(B_P, T_P, D)
    y_sample = x[N_P:].reshape(B_S, T_S, D)
    stack = lambda pairs, k: jnp.stack([pr[k] for pr in pairs])
    return (y_prompt, y_sample, stack(gla_st, 0), stack(ssd_st, 0), stack(conv_st, 0),
            stack(gla_st, 1), stack(ssd_st, 1), stack(conv_st, 1))
```

```python
import functools

import jax
import jax.numpy as jnp
import numpy as np
from jax import lax
from jax.experimental import pallas as pl
from jax.experimental.pallas import tpu as pltpu

f32 = jnp.float32
bf16 = jnp.bfloat16

D = 1024
DEPTH = 2
B_P, T_P = 8, 2048
B_S, T_S = 32, 64
N_P = B_P * T_P
N_S = B_S * T_S
N_TOK = N_P + N_S
SEG = 64
NSEG = N_TOK // SEG
NSEG_P = N_P // SEG
SEG_PER_PSEQ = T_P // SEG
EPS = 1e-6

GLA_H, GLA_DK, GLA_DV = 4, 128, 256
GLA_RANK = 16
GLA_GATE_NORM = 16.0
SSD_INNER = 2048
SSD_P = 64
SSD_H = 32
SSD_G = 4
SSD_N = 128
SSD_HG = SSD_H // SSD_G
SSD_GW = SSD_INNER // SSD_G
CONV_W = 4
SSD_CONV_DIM = SSD_INNER + 2 * SSD_G * SSD_N

PEER_H = 8
PEER_NK = 128
PEER_DQ = 256
PEER_TOPK = 16

C_Q, C_K, C_V, C_GO = 0, 512, 1024, 2048
C_XBC, C_Z, C_GATE = 3072, 6144, 8192
C_LOW = 10240
LOW_DT0 = 32
P_COLS = C_LOW + 128

LANE = 128
VMEM_LIMIT = 56 * 1024 * 1024


def _cparams(sem):
    return pltpu.CompilerParams(dimension_semantics=sem, vmem_limit_bytes=VMEM_LIMIT)


def _dot(a, b):
    return jnp.dot(a.astype(bf16), b.astype(bf16), preferred_element_type=f32)


def _dot_nt(a, b):
    return lax.dot_general(a.astype(bf16), b.astype(bf16), (((1,), (1,)), ((), ())), preferred_element_type=f32)


def _dot_tn(a, b):
    return lax.dot_general(a.astype(bf16), b.astype(bf16), (((0,), (0,)), ((), ())), preferred_element_type=f32)


def _split3(x):
    hi = x.astype(bf16)
    r = x - hi.astype(f32)
    mid = r.astype(bf16)
    lo = (r - mid.astype(f32)).astype(bf16)
    return hi, mid, lo


def _dot3(a, b):
    ah, am, _ = _split3(a)
    bh, bm, _ = _split3(b)
    d = functools.partial(jnp.dot, preferred_element_type=f32)
    return d(ah, bh) + (d(ah, bm) + d(am, bh))


def _dot3_nt(a, b):
    ah, am, _ = _split3(a)
    bh, bm, _ = _split3(b)
    d = functools.partial(lax.dot_general, dimension_numbers=(((1,), (1,)), ((), ())), preferred_element_type=f32)
    return d(ah, bh) + (d(ah, bm) + d(am, bh))


def _cumsum_rows(x):
    n = x.shape[0]
    tri = (lax.broadcasted_iota(jnp.int32, (n, n), 0) >= lax.broadcasted_iota(jnp.int32, (n, n), 1)).astype(bf16)
    hi, mid, lo = _split3(x)
    d = functools.partial(jnp.dot, preferred_element_type=f32)
    return d(tri, hi) + (d(tri, mid) + d(tri, lo))


def _silu(x):
    return x * jax.nn.sigmoid(x)


def _softplus(x):
    return jnp.maximum(x, 0.0) + jnp.log1p(jnp.exp(-jnp.abs(x)))


def _rms(x, w):
    return x * lax.rsqrt(jnp.mean(x * x, axis=-1, keepdims=True) + EPS) * w


def _ada_kernel(c_ref, w_ref, b_ref, o_ref):
    o_ref[0] = _dot3(_silu(c_ref[...]), w_ref[0]) + b_ref[0]


def _ada(c_all, w_ada, b_ada):
    nb = c_all.shape[0]
    tn = 1536
    return pl.pallas_call(
        _ada_kernel,
        grid=(DEPTH, 6 * D // tn),
        in_specs=[pl.BlockSpec((nb, D), lambda l, j: (0, 0)),
                  pl.BlockSpec((1, D, tn), lambda l, j: (l, 0, j)),
                  pl.BlockSpec((1, 1, tn), lambda l, j: (l, 0, j))],
        out_specs=pl.BlockSpec((1, nb, tn), lambda l, j: (l, 0, j)),
        out_shape=jax.ShapeDtypeStruct((DEPTH, nb, 6 * D), f32),
        compiler_params=_cparams(("parallel", "parallel")),
        name="ada_mod",
    )(c_all, w_ada, b_ada.reshape(DEPTH, 1, 6 * D))


NM_TB = 512


def _normmod_kernel(x_ref, w_ref, sc_ref, sh_ref, o_ref):
    for s in range(NM_TB // SEG):
        r = slice(s * SEG, (s + 1) * SEG)
        y = _rms(x_ref[r, :], w_ref[...])
        o_ref[r, :] = (y * (1.0 + sc_ref[s:s + 1, :]) + sh_ref[s:s + 1, :]).astype(bf16)


def _normmod(x, w, mod_seg, sc_col, sh_col):
    nseg_b = NM_TB // SEG
    return pl.pallas_call(
        _normmod_kernel,
        grid=(N_TOK // NM_TB,),
        in_specs=[pl.BlockSpec((NM_TB, D), lambda i: (i, 0)),
                  pl.BlockSpec((1, D), lambda i: (0, 0)),
                  pl.BlockSpec((nseg_b, D), lambda i: (i, sc_col)),
                  pl.BlockSpec((nseg_b, D), lambda i: (i, sh_col))],
        out_specs=pl.BlockSpec((NM_TB, D), lambda i: (i, 0)),
        out_shape=jax.ShapeDtypeStruct((N_TOK, D), bf16),
        compiler_params=_cparams(("parallel",)),
        name="norm_mod",
    )(x, w.reshape(1, D), mod_seg, mod_seg)


def _mm_kernel(a_ref, b_ref, o_ref):
    o_ref[...] = jnp.dot(a_ref[...], b_ref[...], preferred_element_type=f32)


def _mm(a, b, tm, tn, name):
    m, k = a.shape
    n = b.shape[1]
    return pl.pallas_call(
        _mm_kernel,
        grid=(n // tn, m // tm),
        in_specs=[pl.BlockSpec((tm, k), lambda j, i: (i, 0)),
                  pl.BlockSpec((k, tn), lambda j, i: (0, j))],
        out_specs=pl.BlockSpec((tm, tn), lambda j, i: (i, j)),
        out_shape=jax.ShapeDtypeStruct((m, n), f32),
        compiler_params=_cparams(("parallel", "parallel")),
        name=name,
    )(a, b)


def _is_prompt(c):
    return c < NSEG_P


def _sample_seq(c):
    return jnp.maximum(c - NSEG_P, 0)


def _prompt_seq(c):
    return jnp.minimum(c // SEG_PER_PSEQ, B_P - 1)


def _gla_kernel(q_ref, k_ref, v_ref, g_ref, low_ref, w2_ref, b2_ref, nw_ref, s0_ref,
                o_ref, stp_ref, sts_ref, st_ref):
    c = pl.program_id(0)
    is_p = _is_prompt(c)
    pos = c % SEG_PER_PSEQ

    @pl.when(jnp.logical_and(is_p, pos == 0))
    def _():
        st_ref[...] = jnp.zeros_like(st_ref)

    @pl.when(jnp.logical_not(is_p))
    def _():
        for h in range(GLA_H):
            st_ref[h] = s0_ref[0, h].T

    yield
    pre = _dot3(low_ref[...], w2_ref[...]) + b2_ref[...]
    gk = -_softplus(-pre) / GLA_GATE_NORM
    b = _cumsum_rows(gk)
    b_last = b[SEG - 1:SEG, :]
    e_b = jnp.exp(b)
    e_nb = jnp.exp(-b)
    e_end = jnp.exp(b_last - b)
    e_last = jnp.exp(b_last)
    causal = lax.broadcasted_iota(jnp.int32, (SEG, SEG), 0) >= lax.broadcasted_iota(jnp.int32, (SEG, SEG), 1)
    for h in range(GLA_H):
        ks = slice(h * GLA_DK, (h + 1) * GLA_DK)
        vs = slice(h * GLA_DV, (h + 1) * GLA_DV)
        k = k_ref[:, ks]
        q_t = (q_ref[:, ks] * (GLA_DK ** -0.5)) * e_b[:, ks]
        att = jnp.where(causal, _dot_nt(q_t, k * e_nb[:, ks]), 0.0)
        v = v_ref[:, vs]
        st = st_ref[h]
        o = _dot(att, v) + _dot_nt(q_t, st)
        st_ref[h] = st * e_last[:, ks] + _dot_tn(v, k * e_end[:, ks])
        o_ref[:, vs] = (_rms(o, nw_ref[...]) * _silu(g_ref[:, vs])).astype(bf16)

    yield
    @pl.when(jnp.logical_and(is_p, pos == SEG_PER_PSEQ - 1))
    def _():
        for h in range(GLA_H):
            stp_ref[0, h] = st_ref[h].T

    @pl.when(jnp.logical_not(is_p))
    def _():
        for h in range(GLA_H):
            sts_ref[0, h] = st_ref[h].T


def _gla_call(p, w2pad, b2, norm_w, s0):
    st_block = (1, GLA_H, GLA_DK, GLA_DV)
    kd, vd = GLA_H * GLA_DK, GLA_H * GLA_DV
    return dict(
        in_specs=[pl.BlockSpec((SEG, kd), lambda c: (c, C_Q // kd)),
                  pl.BlockSpec((SEG, kd), lambda c: (c, C_K // kd)),
                  pl.BlockSpec((SEG, vd), lambda c: (c, C_V // vd)),
                  pl.BlockSpec((SEG, vd), lambda c: (c, C_GO // vd)),
                  pl.BlockSpec((SEG, LANE), lambda c: (c, C_LOW // LANE)),
                  pl.BlockSpec((LANE, kd), lambda c: (0, 0)),
                  pl.BlockSpec((1, kd), lambda c: (0, 0)),
                  pl.BlockSpec((1, GLA_DV), lambda c: (0, 0)),
                  pl.BlockSpec(st_block, lambda c: (_sample_seq(c), 0, 0, 0))],
        out_specs=[pl.BlockSpec((SEG, vd), lambda c: (c, 0)),
                   pl.BlockSpec(st_block, lambda c: (_prompt_seq(c), 0, 0, 0)),
                   pl.BlockSpec(st_block, lambda c: (_sample_seq(c), 0, 0, 0))],
        out_shape=[jax.ShapeDtypeStruct((N_TOK, vd), bf16),
                   jax.ShapeDtypeStruct((B_P, GLA_H, GLA_DK, GLA_DV), f32),
                   jax.ShapeDtypeStruct((B_S, GLA_H, GLA_DK, GLA_DV), f32)],
        scratch_shapes=[pltpu.VMEM((GLA_H, GLA_DV, GLA_DK), f32)],
        args=(p, p, p, p, p, w2pad, b2.reshape(1, -1), norm_w.reshape(1, -1), s0))


EXT = SEG + 8


def _ssd_kernel(xbc_ref, z_ref, dt_ref, cp_ref, cw_ref, cb_ref, dtb_ref, alog_ref, dvec_ref, nw_ref, s0_ref,
                o_ref, stp_ref, sts_ref, cvp_ref, cvs_ref, ext, st_ref):
    c = pl.program_id(0)
    is_p = _is_prompt(c)
    pos = c % SEG_PER_PSEQ

    @pl.when(jnp.logical_and(is_p, pos == 0))
    def _():
        ext[0:8, :] = jnp.zeros((8, SSD_CONV_DIM), f32)
        st_ref[...] = jnp.zeros_like(st_ref)

    @pl.when(jnp.logical_not(is_p))
    def _():
        ext[0:8, :] = jnp.zeros((8, SSD_CONV_DIM), f32)
        ext[8 - (CONV_W - 1):8, :] = cp_ref[0]
        st_ref[...] = s0_ref[0].reshape(st_ref.shape)

    yield
    ext[8:EXT, :] = xbc_ref[...]
    acc = cb_ref[...] + ext[pl.ds(8 - (CONV_W - 1), SEG), :] * cw_ref[0:1, :]
    for i in range(1, CONV_W):
        acc = acc + ext[pl.ds(8 - (CONV_W - 1) + i, SEG), :] * cw_ref[i:i + 1, :]
    ext[0:8, :] = ext[SEG:EXT, :]
    conv = _silu(acc)

    dt = _softplus(dt_ref[...] + dtb_ref[...])
    a = _cumsum_rows(dt * (-jnp.exp(alog_ref[...])))
    a_t = a.T
    dt_t = dt.T
    a_last = a[SEG - 1:SEG, :]
    e_a = jnp.exp(a)
    w_col = dt * jnp.exp(a_last - a)
    e_last = jnp.exp(a_last)
    causal = lax.broadcasted_iota(jnp.int32, (SEG, SEG), 0) >= lax.broadcasted_iota(jnp.int32, (SEG, SEG), 1)
    lane_lo = lax.broadcasted_iota(jnp.int32, (SEG, 2 * SSD_P), 1) < SSD_P
    sub_lo = lax.broadcasted_iota(jnp.int32, (2 * SSD_P, SSD_N), 0) < SSD_P
    nbc = SSD_G * SSD_N

    def pair_cols(m, ja, jb):
        return jnp.where(lane_lo, jnp.broadcast_to(m[:, ja:ja + 1], (SEG, 2 * SSD_P)),
                         jnp.broadcast_to(m[:, jb:jb + 1], (SEG, 2 * SSD_P)))

    for g in range(SSD_G):
        j0 = LOW_DT0 + g * SSD_HG
        bc = conv[:, SSD_INNER + g * SSD_N:SSD_INNER + (g + 1) * SSD_N]
        cc = conv[:, SSD_INNER + nbc + g * SSD_N:SSD_INNER + nbc + (g + 1) * SSD_N]
        cbm = _dot_nt(cc, bc)

        def w_intra(j):
            seg = a[:, j:j + 1] - a_t[j:j + 1, :]
            return cbm * jnp.exp(jnp.where(causal, seg, -jnp.inf)) * dt_t[j:j + 1, :]

        ys = []
        for p in range(SSD_HG // 2):
            ja, jb = j0 + 2 * p, j0 + 2 * p + 1
            pp = g * (SSD_HG // 2) + p
            xp = conv[:, pp * 2 * SSD_P:(pp + 1) * 2 * SSD_P]
            y_intra = jnp.where(lane_lo, _dot(w_intra(ja), xp), _dot(w_intra(jb), xp))
            sp = st_ref[pp]
            y_inter = _dot_nt(cc, sp) * pair_cols(e_a, ja, jb)
            ds = _dot_tn(xp * pair_cols(w_col, ja, jb), bc)
            rs = jnp.where(sub_lo, jnp.broadcast_to(e_last[:, ja:ja + 1], (2 * SSD_P, SSD_N)),
                           jnp.broadcast_to(e_last[:, jb:jb + 1], (2 * SSD_P, SSD_N)))
            st_ref[pp] = sp * rs + ds
            ys.append(y_intra + y_inter + dvec_ref[:, pp * 2 * SSD_P:(pp + 1) * 2 * SSD_P] * xp)
        gw = slice(g * SSD_GW, (g + 1) * SSD_GW)
        y = jnp.concatenate(ys, axis=1) * _silu(z_ref[:, gw])
        o_ref[:, gw] = _rms(y, nw_ref[:, gw]).astype(bf16)

    yield
    @pl.when(jnp.logical_and(is_p, pos == SEG_PER_PSEQ - 1))
    def _():
        stp_ref[0] = st_ref[...].reshape(stp_ref.shape[1:])
        cvp_ref[0] = xbc_ref[SEG - (CONV_W - 1):SEG, :]

    @pl.when(jnp.logical_not(is_p))
    def _():
        sts_ref[0] = st_ref[...].reshape(sts_ref.shape[1:])
        cvs_ref[0] = xbc_ref[SEG - (CONV_W - 1):SEG, :]


def _ssd_call(p, conv_prev, conv_w, conv_b, dtb, alog, dvec, norm_w, s0):
    st_block = (1, SSD_H, SSD_P, SSD_N)
    cv_block = (1, CONV_W - 1, SSD_CONV_DIM)
    full = lambda c: (0, 0)
    return dict(
        in_specs=[pl.BlockSpec((SEG, SSD_CONV_DIM), lambda c: (c, C_XBC // SSD_CONV_DIM)),
                  pl.BlockSpec((SEG, SSD_INNER), lambda c: (c, C_Z // SSD_INNER)),
                  pl.BlockSpec((SEG, LANE), lambda c: (c, C_LOW // LANE)),
                  pl.BlockSpec(cv_block, lambda c: (_sample_seq(c), 0, 0)),
                  pl.BlockSpec((CONV_W, SSD_CONV_DIM), full),
                  pl.BlockSpec((1, SSD_CONV_DIM), full),
                  pl.BlockSpec((1, LANE), full),
                  pl.BlockSpec((1, LANE), full),
                  pl.BlockSpec((1, SSD_INNER), full),
                  pl.BlockSpec((1, SSD_INNER), full),
                  pl.BlockSpec(st_block, lambda c: (_sample_seq(c), 0, 0, 0))],
        out_specs=[pl.BlockSpec((SEG, SSD_INNER), lambda c: (c, 0)),
                   pl.BlockSpec(st_block, lambda c: (_prompt_seq(c), 0, 0, 0)),
                   pl.BlockSpec(st_block, lambda c: (_sample_seq(c), 0, 0, 0)),
                   pl.BlockSpec(cv_block, lambda c: (_prompt_seq(c), 0, 0)),
                   pl.BlockSpec(cv_block, lambda c: (_sample_seq(c), 0, 0))],
        out_shape=[jax.ShapeDtypeStruct((N_TOK, SSD_INNER), bf16),
                   jax.ShapeDtypeStruct((B_P, SSD_H, SSD_P, SSD_N), f32),
                   jax.ShapeDtypeStruct((B_S, SSD_H, SSD_P, SSD_N), f32),
                   jax.ShapeDtypeStruct((B_P, CONV_W - 1, SSD_CONV_DIM), f32),
                   jax.ShapeDtypeStruct((B_S, CONV_W - 1, SSD_CONV_DIM), f32)],
        scratch_shapes=[pltpu.VMEM((EXT, SSD_CONV_DIM), f32),
                        pltpu.VMEM((SSD_H // 2, 2 * SSD_P, SSD_N), f32)],
        args=(p, p, p, conv_prev, conv_w, conv_b.reshape(1, -1), dtb, alog, dvec, norm_w.reshape(1, -1), s0))


def _mixers(gla, ssd):
    n_in = (len(gla["in_specs"]), len(ssd["in_specs"]))
    n_out = (len(gla["out_specs"]), len(ssd["out_specs"]))
    n_scr = (len(gla["scratch_shapes"]), len(ssd["scratch_shapes"]))

    def body(*refs):
        ins, outs, scr = refs[:sum(n_in)], refs[sum(n_in):sum(n_in) + sum(n_out)], refs[sum(n_in) + sum(n_out):]
        parts = [_gla_kernel(*ins[:n_in[0]], *outs[:n_out[0]], *scr[:n_scr[0]]),
                 _ssd_kernel(*ins[n_in[0]:], *outs[n_out[0]:], *scr[n_scr[0]:])]
        for _ in range(3):
            for part in parts:
                next(part, None)

    outs = pl.pallas_call(
        body,
        grid=(NSEG,),
        in_specs=gla["in_specs"] + ssd["in_specs"],
        out_specs=gla["out_specs"] + ssd["out_specs"],
        out_shape=gla["out_shape"] + ssd["out_shape"],
        scratch_shapes=gla["scratch_shapes"] + ssd["scratch_shapes"],
        compiler_params=_cparams(("arbitrary",)),
        name="token_mixers",
    )(*gla["args"], *ssd["args"])
    return outs[:n_out[0]], outs[n_out[0]:]


MIX_TB = 512


def _mix_kernel(oa_ref, yb_ref, gt_ref, x_ref, ga_ref, sc_ref, sh_ref, nw_ref, wa_ref, wb_ref, wo_ref,
                x1_ref, h2_ref, h2t_ref):
    br_a = jnp.dot(oa_ref[...], wa_ref[...], preferred_element_type=f32)
    br_b = jnp.dot(yb_ref[...], wb_ref[...], preferred_element_type=f32)
    g_a = jax.nn.sigmoid(gt_ref[:, 0:D])
    g_b = jax.nn.sigmoid(gt_ref[:, D:2 * D])
    y = _dot(g_a * br_a + g_b * br_b, wo_ref[...])
    for s in range(MIX_TB // SEG):
        r = slice(s * SEG, (s + 1) * SEG)
        x1 = x_ref[r, :] + ga_ref[s:s + 1, :] * y[r, :]
        x1_ref[r, :] = x1
        h2_ref[r, :] = (_rms(x1, nw_ref[...]) * (1.0 + sc_ref[s:s + 1, :]) + sh_ref[s:s + 1, :]).astype(bf16)
    h2t_ref[...] = h2_ref[...].T


def _mix(oa, yb, p, x, mod_seg, norm2_w, wa, wb, wo):
    nseg_b = MIX_TB // SEG
    full = lambda i: (0, 0)
    return pl.pallas_call(
        _mix_kernel,
        grid=(N_TOK // MIX_TB,),
        in_specs=[pl.BlockSpec((MIX_TB, D), lambda i: (i, 0)),
                  pl.BlockSpec((MIX_TB, SSD_INNER), lambda i: (i, 0)),
                  pl.BlockSpec((MIX_TB, 2 * D), lambda i: (i, C_GATE // (2 * D))),
                  pl.BlockSpec((MIX_TB, D), lambda i: (i, 0)),
                  pl.BlockSpec((nseg_b, D), lambda i: (i, 2)),
                  pl.BlockSpec((nseg_b, D), lambda i: (i, 4)),
                  pl.BlockSpec((nseg_b, D), lambda i: (i, 3)),
                  pl.BlockSpec((1, D), full),
                  pl.BlockSpec((D, D), full),
                  pl.BlockSpec((SSD_INNER, D), full),
                  pl.BlockSpec((D, D), full)],
        out_specs=[pl.BlockSpec((MIX_TB, D), lambda i: (i, 0)),
                   pl.BlockSpec((MIX_TB, D), lambda i: (i, 0)),
                   pl.BlockSpec((D, MIX_TB), lambda i: (0, i))],
        out_shape=[jax.ShapeDtypeStruct((N_TOK, D), f32),
                   jax.ShapeDtypeStruct((N_TOK, D), bf16),
                   jax.ShapeDtypeStruct((D, N_TOK), bf16)],
        compiler_params=_cparams(("parallel",)),
        name="mix_out",
    )(oa, yb, p, x, mod_seg, mod_seg, mod_seg, norm2_w.reshape(1, D), wa, wb, wo)


RT_TL = 128
HALF = PEER_DQ // 2


RT_HPS = 4
SUBLANES = 8


def _batcher_sort_net(n):
    def merge(lo, hi, r):
        step = 2 * r
        if step < hi - lo:
            yield from merge(lo, hi, step)
            yield from merge(lo + r, hi, step)
            yield from ((i, i + r) for i in range(lo + r, hi - r, step))
        else:
            yield (lo, lo + r)

    def sort(lo, hi):
        if hi - lo >= 1:
            mid = lo + (hi - lo) // 2
            yield from sort(lo, mid)
            yield from sort(mid + 1, hi)
            yield from merge(lo, hi, 1)

    return tuple(sort(0, n - 1))


def _bitonic_merge_net(n):
    net, d = [], n // 2
    while d >= 1:
        net += [(i, i + d) for i in range(n) if (i // d) % 2 == 0]
        d //= 2
    return tuple(net)


_SORT16 = _batcher_sort_net(PEER_TOPK)
_MERGE16 = _bitonic_merge_net(PEER_TOPK)
N_CAND_VREGS = 10


def _compare_exchange(x, net):
    for i, j in net:
        x[i], x[j] = jnp.maximum(x[i], x[j]), jnp.minimum(x[i], x[j])


def _merge_across_sublanes(x, n_valid):
    for shift in (4, 2, 1):
        y = [pltpu.roll(v, shift, 0) for v in x]
        merged = []
        for k in range(PEER_TOPK):
            a = x[k] if k < n_valid else None
            b = y[PEER_TOPK - 1 - k] if PEER_TOPK - 1 - k < n_valid else None
            merged.append(jnp.maximum(a, b) if (a is not None and b is not None) else (a if b is None else b))
        x = merged
        _compare_exchange(x, _MERGE16)
        n_valid = PEER_TOPK
    return x


def _top16_sorted(s):
    x = [s[SUBLANES * k:SUBLANES * (k + 1), :] for k in range(PEER_NK // SUBLANES)]
    _compare_exchange(x, _SORT16)
    return _merge_across_sublanes(x, PEER_TOPK)


def _route_kernel(q_ref, k1_ref, k2_ref, thr_ref, s2_ref, f1_ref, f2_ref):
    sub = lax.broadcasted_iota(jnp.int32, (SUBLANES, RT_TL), 0)
    ninf = jnp.float32(-jnp.inf)

    def by_sublane(vs):
        out = vs[0]
        for r in range(1, SUBLANES):
            out = jnp.where(sub == r, vs[r], out)
        return out

    for hh in range(RT_HPS):
        s1 = _dot3_nt(k1_ref[hh], q_ref[:, hh * PEER_DQ:hh * PEER_DQ + HALF])
        s2 = _dot3_nt(k2_ref[hh], q_ref[:, hh * PEER_DQ + HALF:(hh + 1) * PEER_DQ])
        s2_ref[hh] = s2
        v1 = _top16_sorted(s1)
        v2 = _top16_sorted(s2)
        v2lo, v2hi, v1hi = by_sublane(v2[:8]), by_sublane(v2[8:]), by_sublane(v1[8:])
        cands = [v1[0] + v2lo, v1[0] + v2hi, v1[1] + v2lo]
        for i in range(2, 8):
            cands.append(jnp.where(sub < PEER_TOPK // (i + 1), v1[i] + v2lo, ninf))
        cands.append(v1hi + v2[0])
        assert len(cands) == N_CAND_VREGS
        _compare_exchange(cands, tuple((i, j) for i, j in _SORT16 if j < N_CAND_VREGS))
        top = _merge_across_sublanes(cands, N_CAND_VREGS)
        zsum = jnp.ones_like(top[0])
        for k in range(1, PEER_TOPK):
            zsum = zsum + jnp.exp(top[k] - top[0])
        tau = top[PEER_TOPK - 1]
        for k in range(PEER_NK // SUBLANES):
            rows = slice(SUBLANES * k, SUBLANES * (k + 1))
            s1k = s1[rows, :]
            thr = jnp.full_like(s1k, jnp.inf)
            for j in range(PEER_TOPK):
                thr = jnp.where(s1k + v2[j] >= tau, v2[j], thr)
            thr_ref[hh, rows, :] = thr
        f1_ref[hh] = jnp.exp(s1 - v1[0][0:1, :]) * (0.5 / zsum[0:1, :])
        f2_ref[hh] = jnp.exp(s2 - v2[0][0:1, :])


def _route(q, keys1, keys2):
    tile = pl.BlockSpec((RT_HPS, PEER_NK, RT_TL), lambda i, h: (h, 0, i))
    big = jax.ShapeDtypeStruct((PEER_H, PEER_NK, N_TOK), f32)
    return pl.pallas_call(
        _route_kernel,
        grid=(N_TOK // RT_TL, PEER_H // RT_HPS),
        in_specs=[pl.BlockSpec((RT_TL, RT_HPS * PEER_DQ), lambda i, h: (i, h)),
                  pl.BlockSpec((RT_HPS, PEER_NK, HALF), lambda i, h: (h, 0, 0)),
                  pl.BlockSpec((RT_HPS, PEER_NK, HALF), lambda i, h: (h, 0, 0))],
        out_specs=[tile, tile, tile, tile],
        out_shape=[big, big, big, big],
        compiler_params=_cparams(("parallel", "parallel")),
        name="peer_route",
    )(q, keys1, keys2)


PE_TB = 512
PE_NA = 4
PE_CH = PE_NA * PEER_NK
PE_NCH = PEER_NK * PEER_NK // PE_CH
PE_BH = 32
SQRT_HALF = 0.7071067811865476
MXU = 256


def _peer_kernel(*refs, final):
    if final:
        (u0_ref, u_ref, vt_ref, xt_ref, thr_ref, f1_ref, s2_ref, f2_ref, x1_ref, ga_ref, fw_ref,
         o_ref, acc_ref, pw_ref, gel_ref, rthr_ref, rf1_ref) = refs
    else:
        (u0_ref, u_ref, vt_ref, xt_ref, thr_ref, f1_ref, s2_ref, f2_ref, x1_ref, ga_ref,
         o_ref, acc_ref, pw_ref, gel_ref, rthr_ref, rf1_ref) = refs
    j = pl.program_id(1)

    @pl.when(j == 0)
    def _():
        acc_ref[...] = jnp.zeros_like(acc_ref)
        pw_ref[1] = jnp.zeros(pw_ref.shape[1:], bf16)
        act0 = jnp.dot(u0_ref[0], xt_ref[...], preferred_element_type=f32)
        gel_ref[0] = act0 * (1.0 + lax.erf(act0 * SQRT_HALF))

    cur = j % 2
    prev = 1 - cur
    jc = jnp.minimum(j, PE_NCH - 1)

    for i in range(PE_NA):
        for h in range(PEER_H):
            a = jc * PE_NA + i
            rthr_ref[i * PEER_H + h] = jnp.broadcast_to(thr_ref[h, pl.ds(a, 1), :], (8, PE_TB))
            rf1_ref[i * PEER_H + h] = jnp.broadcast_to(f1_ref[h, pl.ds(a, 1), :], (8, PE_TB))

    nsub = PE_BH // 8
    n_bh = PEER_NK // PE_BH

    def weight_tile(t):
        l0 = pl.multiple_of((t // n_bh) * LANE, LANE)
        r0 = pl.multiple_of((t % n_bh) * PE_BH, PE_BH)
        lanes = pl.ds(l0, LANE)
        ws = [jnp.zeros((nsub, 8, LANE), f32) for _ in range(PE_NA)]
        for h in range(PEER_H):
            s2t = s2_ref[h, pl.ds(r0, PE_BH), lanes].reshape(nsub, 8, LANE)
            f2t = f2_ref[h, pl.ds(r0, PE_BH), lanes].reshape(nsub, 8, LANE)
            for i in range(PE_NA):
                sel = s2t >= rthr_ref[i * PEER_H + h, :, lanes][None]
                ws[i] = ws[i] + jnp.where(sel, f2t * rf1_ref[i * PEER_H + h, :, lanes][None], 0.0)
        for i in range(PE_NA):
            er = pl.ds(pl.multiple_of(i * PEER_NK + r0, PE_BH), PE_BH)
            pw_ref[cur, er, lanes] = (ws[i].reshape(PE_BH, LANE) * gel_ref[cur, er, lanes]).astype(bf16)

    n_half = PE_TB // MXU
    n_tiles = (PE_TB // LANE) * n_bh
    tiles_per_group = n_tiles // n_half

    def stage(p, carry):
        n0 = pl.multiple_of(p * MXU, MXU)
        act = jnp.dot(u_ref[0], xt_ref[:, pl.ds(n0, MXU)], preferred_element_type=f32)
        gel_ref[prev, :, pl.ds(n0, MXU)] = act * (1.0 + lax.erf(act * SQRT_HALF))
        acc_ref[:, pl.ds(n0, MXU)] += jnp.dot(vt_ref[0], pw_ref[prev, :, pl.ds(n0, MXU)],
                                              preferred_element_type=f32)
        for q in range(tiles_per_group):
            weight_tile(p * tiles_per_group + q)
        return carry

    lax.fori_loop(0, n_half, stage, 0)

    @pl.when(j == PE_NCH)
    def _():
        out = acc_ref[...].T
        for s in range(PE_TB // SEG):
            r = slice(s * SEG, (s + 1) * SEG)
            x2 = x1_ref[r, :] + ga_ref[s:s + 1, :] * out[r, :]
            o_ref[r, :] = _rms(x2, fw_ref[...]) if final else x2


def _peer(u_c, vt_c, h2t, thr, f1t, s2t, f2t, x1, mod_seg, final_w):
    final = final_w is not None
    nseg_b = PE_TB // SEG
    rt = pl.BlockSpec((PEER_H, PEER_NK, PE_TB), lambda i, j: (0, 0, i))
    in_specs = [pl.BlockSpec((1, PE_CH, D), lambda i, j: (0, 0, 0)),
                pl.BlockSpec((1, PE_CH, D), lambda i, j: (jnp.minimum(j + 1, PE_NCH - 1), 0, 0)),
                pl.BlockSpec((1, D, PE_CH), lambda i, j: (jnp.maximum(j - 1, 0), 0, 0)),
                pl.BlockSpec((D, PE_TB), lambda i, j: (0, i)),
                rt, rt, rt, rt,
                pl.BlockSpec((PE_TB, D), lambda i, j: (i, 0)),
                pl.BlockSpec((nseg_b, D), lambda i, j: (i, 5))]
    args = [u_c, u_c, vt_c, h2t, thr, f1t, s2t, f2t, x1, mod_seg]
    if final:
        in_specs.append(pl.BlockSpec((1, D), lambda i, j: (0, 0)))
        args.append(final_w.reshape(1, D))
    return pl.pallas_call(
        functools.partial(_peer_kernel, final=final),
        grid=(N_TOK // PE_TB, PE_NCH + 1),
        in_specs=in_specs,
        out_specs=pl.BlockSpec((PE_TB, D), lambda i, j: (i, 0)),
        out_shape=jax.ShapeDtypeStruct((N_TOK, D), f32),
        scratch_shapes=[pltpu.VMEM((D, PE_TB), f32), pltpu.VMEM((2, PE_CH, PE_TB), bf16),
                        pltpu.VMEM((2, PE_CH, PE_TB), f32),
                        pltpu.VMEM((PE_NA * PEER_H, 8, PE_TB), f32), pltpu.VMEM((PE_NA * PEER_H, 8, PE_TB), f32)],
        compiler_params=_cparams(("parallel", "arbitrary")),
        name="peer_experts_final" if final else "peer_experts",
    )(*args)


def _reorder_w_in(w):
    o = np.cumsum([0, 512, 512, 1024, 1024, GLA_RANK, SSD_INNER, SSD_CONV_DIM, SSD_H, 2 * D])
    q_k_v_go = w[:, o[0]:o[4]]
    gk_low = w[:, o[4]:o[5]]
    z = w[:, o[5]:o[6]]
    xbc = w[:, o[6]:o[7]]
    dt = w[:, o[7]:o[8]]
    gates = w[:, o[8]:o[9]]
    pad = lambda m: jnp.pad(m, ((0, 0), (0, LANE - m.shape[1])))
    low = jnp.concatenate([gk_low, jnp.zeros((D, LOW_DT0 - GLA_RANK), w.dtype), dt], axis=1)
    return jnp.concatenate([q_k_v_go, xbc, z, gates, pad(low)], axis=1).astype(bf16)


def kernel(x_prompt, x_sample, state_gla, state_ssd, state_conv, c_prompt, c_sample, w_ada, b_ada, norm1_w, w_in, gla_gk_w2, gla_gk_b, gla_norm_w, gla_proj, ssd_conv_w, ssd_conv_b, ssd_dt_bias, ssd_A_log, ssd_D, ssd_norm_w, ssd_proj, w_out, norm2_w, peer_wq, peer_keys1, peer_keys2, peer_u, peer_v, final_norm_w):
    x = jnp.concatenate([x_prompt.reshape(N_P, D), x_sample.reshape(N_S, D)], axis=0)
    c_all = jnp.concatenate([c_prompt, c_sample], axis=0)
    mod = _ada(c_all, w_ada, b_ada)
    seg2seq = np.concatenate([np.repeat(np.arange(B_P), SEG_PER_PSEQ), B_P + np.arange(B_S)])

    gla_st, ssd_st, conv_st = [], [], []
    for l in range(DEPTH):
        mod_seg = mod[l][seg2seq]
        h = _normmod(x, norm1_w[l], mod_seg, 1, 0)
        p = _mm(h, _reorder_w_in(w_in[l]), 512, P_COLS // 3, "in_proj")

        w2pad = jnp.pad(gla_gk_w2[l], ((0, LANE - GLA_RANK), (0, 0)))
        low_lanes = lambda m: jnp.pad(m.reshape(1, SSD_H), ((0, 0), (LOW_DT0, LANE - LOW_DT0 - SSD_H)))
        dvec = jnp.repeat(ssd_D[l], SSD_P).reshape(1, SSD_INNER)
        (oa, gla_p, gla_s), (yb, ssd_p, ssd_s, conv_p, conv_s) = _mixers(
            _gla_call(p, w2pad, gla_gk_b[l], gla_norm_w[l], state_gla[l]),
            _ssd_call(p, state_conv[l], ssd_conv_w[l], ssd_conv_b[l], low_lanes(ssd_dt_bias[l]),
                      low_lanes(ssd_A_log[l]), dvec, ssd_norm_w[l], state_ssd[l]))

        x1, h2, h2t = _mix(oa, yb, p, x, mod_seg, norm2_w[l], gla_proj[l].astype(bf16),
                           ssd_proj[l].astype(bf16), w_out[l].astype(bf16))
        q = _mm(h2, peer_wq[l].astype(bf16), 512, PEER_H * PEER_DQ, "peer_query")
        thr, s2t, f1t, f2t = _route(q, peer_keys1[l], peer_keys2[l])
        u_c = peer_u[l].astype(bf16).reshape(PE_NCH, PE_CH, D)
        vt_c = peer_v[l].astype(bf16).reshape(PE_NCH, PE_CH, D).transpose(0, 2, 1)
        x = _peer(u_c, vt_c, h2t, thr, f1t, s2t, f2t, x1, mod_seg,
                  final_norm_w if l == DEPTH - 1 else None)

        gla_st.append((gla_p, gla_s))
        ssd_st.append((ssd_p, ssd_s))
        conv_st.append((conv_p, conv_s))

    y_prompt = x[:N_P].reshape(B_P, T_P, D)
    y_sample = x[N_P:].reshape(B_S, T_S, D)
    stack = lambda pairs, k: jnp.stack([pr[k] for pr in pairs])
    return (y_prompt, y_sample, stack(gla_st, 0), stack(ssd_st, 0), stack(conv_st, 0),
            stack(gla_st, 1), stack(ssd_st, 1), stack(conv_st, 1))
```

```python
import functools

import jax
import jax.numpy as jnp
import numpy as np
from jax import lax
from jax.experimental import pallas as pl
from jax.experimental.pallas import tpu as pltpu

f32 = jnp.float32
bf16 = jnp.bfloat16

D = 1024
DEPTH = 2
B_P, T_P = 8, 2048
B_S, T_S = 32, 64
N_P = B_P * T_P
N_S = B_S * T_S
N_TOK = N_P + N_S
SEG = 64
NSEG = N_TOK // SEG
NSEG_P = N_P // SEG
SEG_PER_PSEQ = T_P // SEG
EPS = 1e-6

GLA_H, GLA_DK, GLA_DV = 4, 128, 256
GLA_RANK = 16
GLA_GATE_NORM = 16.0
SSD_INNER = 2048
SSD_P = 64
SSD_H = 32
SSD_G = 4
SSD_N = 128
SSD_HG = SSD_H // SSD_G
SSD_GW = SSD_INNER // SSD_G
CONV_W = 4
SSD_CONV_DIM = SSD_INNER + 2 * SSD_G * SSD_N

PEER_H = 8
PEER_NK = 128
PEER_DQ = 256
PEER_TOPK = 16

C_Q, C_K, C_V, C_GO = 0, 512, 1024, 2048
C_XBC, C_Z, C_GATE = 3072, 6144, 8192
C_LOW = 10240
LOW_DT0 = 32
P_COLS = C_LOW + 128

LANE = 128
VMEM_LIMIT = 56 * 1024 * 1024


def _cparams(sem):
    return pltpu.CompilerParams(dimension_semantics=sem, vmem_limit_bytes=VMEM_LIMIT)


def _dot(a, b):
    return jnp.dot(a.astype(bf16), b.astype(bf16), preferred_element_type=f32)


def _dot_nt(a, b):
    return lax.dot_general(a.astype(bf16), b.astype(bf16), (((1,), (1,)), ((), ())), preferred_element_type=f32)


def _dot_tn(a, b):
    return lax.dot_general(a.astype(bf16), b.astype(bf16), (((0,), (0,)), ((), ())), preferred_element_type=f32)


def _split3(x):
    hi = x.astype(bf16)
    r = x - hi.astype(f32)
    mid = r.astype(bf16)
    lo = (r - mid.astype(f32)).astype(bf16)
    return hi, mid, lo


def _dot3(a, b):
    ah, am, _ = _split3(a)
    bh, bm, _ = _split3(b)
    d = functools.partial(jnp.dot, preferred_element_type=f32)
    return d(ah, bh) + (d(ah, bm) + d(am, bh))


def _dot3_nt(a, b):
    ah, am, _ = _split3(a)
    bh, bm, _ = _split3(b)
    d = functools.partial(lax.dot_general, dimension_numbers=(((1,), (1,)), ((), ())), preferred_element_type=f32)
    return d(ah, bh) + (d(ah, bm) + d(am, bh))


def _cumsum_rows(x):
    n = x.shape[0]
    tri = (lax.broadcasted_iota(jnp.int32, (n, n), 0) >= lax.broadcasted_iota(jnp.int32, (n, n), 1)).astype(bf16)
    hi, mid, lo = _split3(x)
    d = functools.partial(jnp.dot, preferred_element_type=f32)
    return d(tri, hi) + (d(tri, mid) + d(tri, lo))


def _silu(x):
    return x * jax.nn.sigmoid(x)


def _softplus(x):
    return jnp.maximum(x, 0.0) + jnp.log1p(jnp.exp(-jnp.abs(x)))


def _rms(x, w):
    return x * lax.rsqrt(jnp.mean(x * x, axis=-1, keepdims=True) + EPS) * w


def _ada_kernel(c_ref, w_ref, b_ref, o_ref):
    o_ref[0] = _dot3(_silu(c_ref[...]), w_ref[0]) + b_ref[0]


def _ada(c_all, w_ada, b_ada):
    nb = c_all.shape[0]
    tn = 1536
    return pl.pallas_call(
        _ada_kernel,
        grid=(DEPTH, 6 * D // tn),
        in_specs=[pl.BlockSpec((nb, D), lambda l, j: (0, 0)),
                  pl.BlockSpec((1, D, tn), lambda l, j: (l, 0, j)),
                  pl.BlockSpec((1, 1, tn), lambda l, j: (l, 0, j))],
        out_specs=pl.BlockSpec((1, nb, tn), lambda l, j: (l, 0, j)),
        out_shape=jax.ShapeDtypeStruct((DEPTH, nb, 6 * D), f32),
        compiler_params=_cparams(("parallel", "parallel")),
        name="ada_mod",
    )(c_all, w_ada, b_ada.reshape(DEPTH, 1, 6 * D))


NM_TB = 512


def _normmod_kernel(x_ref, w_ref, sc_ref, sh_ref, o_ref):
    for s in range(NM_TB // SEG):
        r = slice(s * SEG, (s + 1) * SEG)
        y = _rms(x_ref[r, :], w_ref[...])
        o_ref[r, :] = (y * (1.0 + sc_ref[s:s + 1, :]) + sh_ref[s:s + 1, :]).astype(bf16)


def _normmod(x, w, mod_seg, sc_col, sh_col):
    nseg_b = NM_TB // SEG
    return pl.pallas_call(
        _normmod_kernel,
        grid=(N_TOK // NM_TB,),
        in_specs=[pl.BlockSpec((NM_TB, D), lambda i: (i, 0)),
                  pl.BlockSpec((1, D), lambda i: (0, 0)),
                  pl.BlockSpec((nseg_b, D), lambda i: (i, sc_col)),
                  pl.BlockSpec((nseg_b, D), lambda i: (i, sh_col))],
        out_specs=pl.BlockSpec((NM_TB, D), lambda i: (i, 0)),
        out_shape=jax.ShapeDtypeStruct((N_TOK, D), bf16),
        compiler_params=_cparams(("parallel",)),
        name="norm_mod",
    )(x, w.reshape(1, D), mod_seg, mod_seg)


def _mm_kernel(a_ref, b_ref, o_ref):
    o_ref[...] = jnp.dot(a_ref[...], b_ref[...], preferred_element_type=f32)


def _mm(a, b, tm, tn, name):
    m, k = a.shape
    n = b.shape[1]
    return pl.pallas_call(
        _mm_kernel,
        grid=(n // tn, m // tm),
        in_specs=[pl.BlockSpec((tm, k), lambda j, i: (i, 0)),
                  pl.BlockSpec((k, tn), lambda j, i: (0, j))],
        out_specs=pl.BlockSpec((tm, tn), lambda j, i: (i, j)),
        out_shape=jax.ShapeDtypeStruct((m, n), f32),
        compiler_params=_cparams(("parallel", "parallel")),
        name=name,
    )(a, b)


def _is_prompt(c):
    return c < NSEG_P


def _sample_seq(c):
    return jnp.maximum(c - NSEG_P, 0)


def _prompt_seq(c):
    return jnp.minimum(c // SEG_PER_PSEQ, B_P - 1)


def _gla_kernel(q_ref, k_ref, v_ref, g_ref, low_ref, w2_ref, b2_ref, nw_ref, s0_ref,
                o_ref, stp_ref, sts_ref, st_ref):
    c = pl.program_id(0)
    is_p = _is_prompt(c)
    pos = c % SEG_PER_PSEQ

    @pl.when(jnp.logical_and(is_p, pos == 0))
    def _():
        st_ref[...] = jnp.zeros_like(st_ref)

    @pl.when(jnp.logical_not(is_p))
    def _():
        for h in range(GLA_H):
            st_ref[h] = s0_ref[0, h].T

    yield
    pre = _dot3(low_ref[...], w2_ref[...]) + b2_ref[...]
    gk = -_softplus(-pre) / GLA_GATE_NORM
    b = _cumsum_rows(gk)
    b_last = b[SEG - 1:SEG, :]
    e_b = jnp.exp(b)
    e_nb = jnp.exp(-b)
    e_end = jnp.exp(b_last - b)
    e_last = jnp.exp(b_last)
    causal = lax.broadcasted_iota(jnp.int32, (SEG, SEG), 0) >= lax.broadcasted_iota(jnp.int32, (SEG, SEG), 1)
    for h in range(GLA_H):
        ks = slice(h * GLA_DK, (h + 1) * GLA_DK)
        vs = slice(h * GLA_DV, (h + 1) * GLA_DV)
        k = k_ref[:, ks]
        q_t = (q_ref[:, ks] * (GLA_DK ** -0.5)) * e_b[:, ks]
        att = jnp.where(causal, _dot_nt(q_t, k * e_nb[:, ks]), 0.0)
        v = v_ref[:, vs]
        st = st_ref[h]
        o = _dot(att, v) + _dot_nt(q_t, st)
        st_ref[h] = st * e_last[:, ks] + _dot_tn(v, k * e_end[:, ks])
        o_ref[:, vs] = (_rms(o, nw_ref[...]) * _silu(g_ref[:, vs])).astype(bf16)

    yield
    @pl.when(jnp.logical_and(is_p, pos == SEG_PER_PSEQ - 1))
    def _():
        for h in range(GLA_H):
            stp_ref[0, h] = st_ref[h].T

    @pl.when(jnp.logical_not(is_p))
    def _():
        for h in range(GLA_H):
            sts_ref[0, h] = st_ref[h].T


def _gla_call(p, w2pad, b2, norm_w, s0):
    st_block = (1, GLA_H, GLA_DK, GLA_DV)
    kd, vd = GLA_H * GLA_DK, GLA_H * GLA_DV
    return dict(
        in_specs=[pl.BlockSpec((SEG, kd), lambda c: (c, C_Q // kd)),
                  pl.BlockSpec((SEG, kd), lambda c: (c, C_K // kd)),
                  pl.BlockSpec((SEG, vd), lambda c: (c, C_V // vd)),
                  pl.BlockSpec((SEG, vd), lambda c: (c, C_GO // vd)),
                  pl.BlockSpec((SEG, LANE), lambda c: (c, C_LOW // LANE)),
                  pl.BlockSpec((LANE, kd), lambda c: (0, 0)),
                  pl.BlockSpec((1, kd), lambda c: (0, 0)),
                  pl.BlockSpec((1, GLA_DV), lambda c: (0, 0)),
                  pl.BlockSpec(st_block, lambda c: (_sample_seq(c), 0, 0, 0))],
        out_specs=[pl.BlockSpec((SEG, vd), lambda c: (c, 0)),
                   pl.BlockSpec(st_block, lambda c: (_prompt_seq(c), 0, 0, 0)),
                   pl.BlockSpec(st_block, lambda c: (_sample_seq(c), 0, 0, 0))],
        out_shape=[jax.ShapeDtypeStruct((N_TOK, vd), bf16),
                   jax.ShapeDtypeStruct((B_P, GLA_H, GLA_DK, GLA_DV), f32),
                   jax.ShapeDtypeStruct((B_S, GLA_H, GLA_DK, GLA_DV), f32)],
        scratch_shapes=[pltpu.VMEM((GLA_H, GLA_DV, GLA_DK), f32)],
        args=(p, p, p, p, p, w2pad, b2.reshape(1, -1), norm_w.reshape(1, -1), s0))


EXT = SEG + 8


def _ssd_kernel(xbc_ref, z_ref, dt_ref, cp_ref, cw_ref, cb_ref, dtb_ref, alog_ref, dvec_ref, nw_ref, s0_ref,
                o_ref, stp_ref, sts_ref, cvp_ref, cvs_ref, ext, st_ref):
    c = pl.program_id(0)
    is_p = _is_prompt(c)
    pos = c % SEG_PER_PSEQ

    @pl.when(jnp.logical_and(is_p, pos == 0))
    def _():
        ext[0:8, :] = jnp.zeros((8, SSD_CONV_DIM), f32)
        st_ref[...] = jnp.zeros_like(st_ref)

    @pl.when(jnp.logical_not(is_p))
    def _():
        ext[0:8, :] = jnp.zeros((8, SSD_CONV_DIM), f32)
        ext[8 - (CONV_W - 1):8, :] = cp_ref[0]
        st_ref[...] = s0_ref[0].reshape(st_ref.shape)

    yield
    ext[8:EXT, :] = xbc_ref[...]
    acc = cb_ref[...] + ext[pl.ds(8 - (CONV_W - 1), SEG), :] * cw_ref[0:1, :]
    for i in range(1, CONV_W):
        acc = acc + ext[pl.ds(8 - (CONV_W - 1) + i, SEG), :] * cw_ref[i:i + 1, :]
    ext[0:8, :] = ext[SEG:EXT, :]
    conv = _silu(acc)

    dt = _softplus(dt_ref[...] + dtb_ref[...])
    a = _cumsum_rows(dt * (-jnp.exp(alog_ref[...])))
    a_t = a.T
    dt_t = dt.T
    a_last = a[SEG - 1:SEG, :]
    e_a = jnp.exp(a)
    w_col = dt * jnp.exp(a_last - a)
    e_last = jnp.exp(a_last)
    causal = lax.broadcasted_iota(jnp.int32, (SEG, SEG), 0) >= lax.broadcasted_iota(jnp.int32, (SEG, SEG), 1)
    lane_lo = lax.broadcasted_iota(jnp.int32, (SEG, 2 * SSD_P), 1) < SSD_P
    sub_lo = lax.broadcasted_iota(jnp.int32, (2 * SSD_P, SSD_N), 0) < SSD_P
    nbc = SSD_G * SSD_N

    def pair_cols(m, ja, jb):
        return jnp.where(lane_lo, jnp.broadcast_to(m[:, ja:ja + 1], (SEG, 2 * SSD_P)),
                         jnp.broadcast_to(m[:, jb:jb + 1], (SEG, 2 * SSD_P)))

    for g in range(SSD_G):
        j0 = LOW_DT0 + g * SSD_HG
        bc = conv[:, SSD_INNER + g * SSD_N:SSD_INNER + (g + 1) * SSD_N]
        cc = conv[:, SSD_INNER + nbc + g * SSD_N:SSD_INNER + nbc + (g + 1) * SSD_N]
        cbm = _dot_nt(cc, bc)

        def w_intra(j):
            seg = a[:, j:j + 1] - a_t[j:j + 1, :]
            return cbm * jnp.exp(jnp.where(causal, seg, -jnp.inf)) * dt_t[j:j + 1, :]

        ys = []
        for p in range(SSD_HG // 2):
            ja, jb = j0 + 2 * p, j0 + 2 * p + 1
            pp = g * (SSD_HG // 2) + p
            xp = conv[:, pp * 2 * SSD_P:(pp + 1) * 2 * SSD_P]
            y_intra = jnp.where(lane_lo, _dot(w_intra(ja), xp), _dot(w_intra(jb), xp))
            sp = st_ref[pp]
            y_inter = _dot_nt(cc, sp) * pair_cols(e_a, ja, jb)
            ds = _dot_tn(xp * pair_cols(w_col, ja, jb), bc)
            rs = jnp.where(sub_lo, jnp.broadcast_to(e_last[:, ja:ja + 1], (2 * SSD_P, SSD_N)),
                           jnp.broadcast_to(e_last[:, jb:jb + 1], (2 * SSD_P, SSD_N)))
            st_ref[pp] = sp * rs + ds
            ys.append(y_intra + y_inter + dvec_ref[:, pp * 2 * SSD_P:(pp + 1) * 2 * SSD_P] * xp)
        gw = slice(g * SSD_GW, (g + 1) * SSD_GW)
        y = jnp.concatenate(ys, axis=1) * _silu(z_ref[:, gw])
        o_ref[:, gw] = _rms(y, nw_ref[:, gw]).astype(bf16)

    yield
    @pl.when(jnp.logical_and(is_p, pos == SEG_PER_PSEQ - 1))
    def _():
        stp_ref[0] = st_ref[...].reshape(stp_ref.shape[1:])
        cvp_ref[0] = xbc_ref[SEG - (CONV_W - 1):SEG, :]

    @pl.when(jnp.logical_not(is_p))
    def _():
        sts_ref[0] = st_ref[...].reshape(sts_ref.shape[1:])
        cvs_ref[0] = xbc_ref[SEG - (CONV_W - 1):SEG, :]


def _ssd_call(p, conv_prev, conv_w, conv_b, dtb, alog, dvec, norm_w, s0):
    st_block = (1, SSD_H, SSD_P, SSD_N)
    cv_block = (1, CONV_W - 1, SSD_CONV_DIM)
    full = lambda c: (0, 0)
    return dict(
        in_specs=[pl.BlockSpec((SEG, SSD_CONV_DIM), lambda c: (c, C_XBC // SSD_CONV_DIM)),
                  pl.BlockSpec((SEG, SSD_INNER), lambda c: (c, C_Z // SSD_INNER)),
                  pl.BlockSpec((SEG, LANE), lambda c: (c, C_LOW // LANE)),
                  pl.BlockSpec(cv_block, lambda c: (_sample_seq(c), 0, 0)),
                  pl.BlockSpec((CONV_W, SSD_CONV_DIM), full),
                  pl.BlockSpec((1, SSD_CONV_DIM), full),
                  pl.BlockSpec((1, LANE), full),
                  pl.BlockSpec((1, LANE), full),
                  pl.BlockSpec((1, SSD_INNER), full),
                  pl.BlockSpec((1, SSD_INNER), full),
                  pl.BlockSpec(st_block, lambda c: (_sample_seq(c), 0, 0, 0))],
        out_specs=[pl.BlockSpec((SEG, SSD_INNER), lambda c: (c, 0)),
                   pl.BlockSpec(st_block, lambda c: (_prompt_seq(c), 0, 0, 0)),
                   pl.BlockSpec(st_block, lambda c: (_sample_seq(c), 0, 0, 0)),
                   pl.BlockSpec(cv_block, lambda c: (_prompt_seq(c), 0, 0)),
                   pl.BlockSpec(cv_block, lambda c: (_sample_seq(c), 0, 0))],
        out_shape=[jax.ShapeDtypeStruct((N_TOK, SSD_INNER), bf16),
                   jax.ShapeDtypeStruct((B_P, SSD_H, SSD_P, SSD_N), f32),
                   jax.ShapeDtypeStruct((B_S, SSD_H, SSD_P, SSD_N), f32),
                   jax.ShapeDtypeStruct((B_P, CONV_W - 1, SSD_CONV_DIM), f32),
                   jax.ShapeDtypeStruct((B_S, CONV_W - 1, SSD_CONV_DIM), f32)],
        scratch_shapes=[pltpu.VMEM((EXT, SSD_CONV_DIM), f32),
                        pltpu.VMEM((SSD_H // 2, 2 * SSD_P, SSD_N), f32)],
        args=(p, p, p, conv_prev, conv_w, conv_b.reshape(1, -1), dtb, alog, dvec, norm_w.reshape(1, -1), s0))


def _mixers(gla, ssd):
    n_in = (len(gla["in_specs"]), len(ssd["in_specs"]))
    n_out = (len(gla["out_specs"]), len(ssd["out_specs"]))
    n_scr = (len(gla["scratch_shapes"]), len(ssd["scratch_shapes"]))

    def body(*refs):
        ins, outs, scr = refs[:sum(n_in)], refs[sum(n_in):sum(n_in) + sum(n_out)], refs[sum(n_in) + sum(n_out):]
        parts = [_gla_kernel(*ins[:n_in[0]], *outs[:n_out[0]], *scr[:n_scr[0]]),
                 _ssd_kernel(*ins[n_in[0]:], *outs[n_out[0]:], *scr[n_scr[0]:])]
        for _ in range(3):
            for part in parts:
                next(part, None)

    outs = pl.pallas_call(
        body,
        grid=(NSEG,),
        in_specs=gla["in_specs"] + ssd["in_specs"],
        out_specs=gla["out_specs"] + ssd["out_specs"],
        out_shape=gla["out_shape"] + ssd["out_shape"],
        scratch_shapes=gla["scratch_shapes"] + ssd["scratch_shapes"],
        compiler_params=_cparams(("arbitrary",)),
        name="token_mixers",
    )(*gla["args"], *ssd["args"])
    return outs[:n_out[0]], outs[n_out[0]:]


MIX_TB = 512


def _mix_kernel(oa_ref, yb_ref, gt_ref, x_ref, ga_ref, sc_ref, sh_ref, nw_ref, wa_ref, wb_ref, wo_ref,
                x1_ref, h2_ref, h2t_ref):
    br_a = jnp.dot(oa_ref[...], wa_ref[...], preferred_element_type=f32)
    br_b = jnp.dot(yb_ref[...], wb_ref[...], preferred_element_type=f32)
    g_a = jax.nn.sigmoid(gt_ref[:, 0:D])
    g_b = jax.nn.sigmoid(gt_ref[:, D:2 * D])
    y = _dot(g_a * br_a + g_b * br_b, wo_ref[...])
    for s in range(MIX_TB // SEG):
        r = slice(s * SEG, (s + 1) * SEG)
        x1 = x_ref[r, :] + ga_ref[s:s + 1, :] * y[r, :]
        x1_ref[r, :] = x1
        h2_ref[r, :] = (_rms(x1, nw_ref[...]) * (1.0 + sc_ref[s:s + 1, :]) + sh_ref[s:s + 1, :]).astype(bf16)
    h2t_ref[...] = h2_ref[...].T


def _mix(oa, yb, p, x, mod_seg, norm2_w, wa, wb, wo):
    nseg_b = MIX_TB // SEG
    full = lambda i: (0, 0)
    return pl.pallas_call(
        _mix_kernel,
        grid=(N_TOK // MIX_TB,),
        in_specs=[pl.BlockSpec((MIX_TB, D), lambda i: (i, 0)),
                  pl.BlockSpec((MIX_TB, SSD_INNER), lambda i: (i, 0)),
                  pl.BlockSpec((MIX_TB, 2 * D), lambda i: (i, C_GATE // (2 * D))),
                  pl.BlockSpec((MIX_TB, D), lambda i: (i, 0)),
                  pl.BlockSpec((nseg_b, D), lambda i: (i, 2)),
                  pl.BlockSpec((nseg_b, D), lambda i: (i, 4)),
                  pl.BlockSpec((nseg_b, D), lambda i: (i, 3)),
                  pl.BlockSpec((1, D), full),
                  pl.BlockSpec((D, D), full),
                  pl.BlockSpec((SSD_INNER, D), full),
                  pl.BlockSpec((D, D), full)],
        out_specs=[pl.BlockSpec((MIX_TB, D), lambda i: (i, 0)),
                   pl.BlockSpec((MIX_TB, D), lambda i: (i, 0)),
                   pl.BlockSpec((D, MIX_TB), lambda i: (0, i))],
        out_shape=[jax.ShapeDtypeStruct((N_TOK, D), f32),
                   jax.ShapeDtypeStruct((N_TOK, D), bf16),
                   jax.ShapeDtypeStruct((D, N_TOK), bf16)],
        compiler_params=_cparams(("parallel",)),
        name="mix_out",
    )(oa, yb, p, x, mod_seg, mod_seg, mod_seg, norm2_w.reshape(1, D), wa, wb, wo)


RT_TL = 128
HALF = PEER_DQ // 2


RT_HPS = 4
SUBLANES = 8


def _batcher_sort_net(n):
    def merge(lo, hi, r):
        step = 2 * r
        if step < hi - lo:
            yield from merge(lo, hi, step)
            yield from merge(lo + r, hi, step)
            yield from ((i, i + r) for i in range(lo + r, hi - r, step))
        else:
            yield (lo, lo + r)

    def sort(lo, hi):
        if hi - lo >= 1:
            mid = lo + (hi - lo) // 2
            yield from sort(lo, mid)
            yield from sort(mid + 1, hi)
            yield from merge(lo, hi, 1)

    return tuple(sort(0, n - 1))


def _bitonic_merge_net(n):
    net, d = [], n // 2
    while d >= 1:
        net += [(i, i + d) for i in range(n) if (i // d) % 2 == 0]
        d //= 2
    return tuple(net)


_SORT16 = _batcher_sort_net(PEER_TOPK)
_MERGE16 = _bitonic_merge_net(PEER_TOPK)
N_CAND_VREGS = 10


def _compare_exchange(x, net):
    for i, j in net:
        x[i], x[j] = jnp.maximum(x[i], x[j]), jnp.minimum(x[i], x[j])


def _merge_across_sublanes(x, n_valid):
    for shift in (4, 2, 1):
        y = [pltpu.roll(v, shift, 0) for v in x]
        merged = []
        for k in range(PEER_TOPK):
            a = x[k] if k < n_valid else None
            b = y[PEER_TOPK - 1 - k] if PEER_TOPK - 1 - k < n_valid else None
            merged.append(jnp.maximum(a, b) if (a is not None and b is not None) else (a if b is None else b))
        x = merged
        _compare_exchange(x, _MERGE16)
        n_valid = PEER_TOPK
    return x


def _top16_sorted(s):
    x = [s[SUBLANES * k:SUBLANES * (k + 1), :] for k in range(PEER_NK // SUBLANES)]
    _compare_exchange(x, _SORT16)
    return _merge_across_sublanes(x, PEER_TOPK)


def _route_kernel(q_ref, k1_ref, k2_ref, thr_ref, s2_ref, f1_ref, f2_ref):
    sub = lax.broadcasted_iota(jnp.int32, (SUBLANES, RT_TL), 0)
    ninf = jnp.float32(-jnp.inf)

    def by_sublane(vs):
        out = vs[0]
        for r in range(1, SUBLANES):
            out = jnp.where(sub == r, vs[r], out)
        return out

    for hh in range(RT_HPS):
        s1 = _dot3_nt(k1_ref[hh], q_ref[:, hh * PEER_DQ:hh * PEER_DQ + HALF])
        s2 = _dot3_nt(k2_ref[hh], q_ref[:, hh * PEER_DQ + HALF:(hh + 1) * PEER_DQ])
        s2_ref[hh] = s2
        v1 = _top16_sorted(s1)
        v2 = _top16_sorted(s2)
        v2lo, v2hi, v1hi = by_sublane(v2[:8]), by_sublane(v2[8:]), by_sublane(v1[8:])
        cands = [v1[0] + v2lo, v1[0] + v2hi, v1[1] + v2lo]
        for i in range(2, 8):
            cands.append(jnp.where(sub < PEER_TOPK // (i + 1), v1[i] + v2lo, ninf))
        cands.append(v1hi + v2[0])
        assert len(cands) == N_CAND_VREGS
        _compare_exchange(cands, tuple((i, j) for i, j in _SORT16 if j < N_CAND_VREGS))
        top = _merge_across_sublanes(cands, N_CAND_VREGS)
        zsum = jnp.ones_like(top[0])
        for k in range(1, PEER_TOPK):
            zsum = zsum + jnp.exp(top[k] - top[0])
        tau = top[PEER_TOPK - 1]
        for k in range(PEER_NK // SUBLANES):
            rows = slice(SUBLANES * k, SUBLANES * (k + 1))
            s1k = s1[rows, :]
            thr = jnp.full_like(s1k, jnp.inf)
            for j in range(PEER_TOPK):
                thr = jnp.where(s1k + v2[j] >= tau, v2[j], thr)
            thr_ref[hh, rows, :] = thr
        f1_ref[hh] = jnp.exp(s1 - v1[0][0:1, :]) * (0.5 / zsum[0:1, :])
        f2_ref[hh] = jnp.exp(s2 - v2[0][0:1, :])


def _route(q, keys1, keys2):
    tile = pl.BlockSpec((RT_HPS, PEER_NK, RT_TL), lambda i, h: (h, 0, i))
    big = jax.ShapeDtypeStruct((PEER_H, PEER_NK, N_TOK), f32)
    return pl.pallas_call(
        _route_kernel,
        grid=(N_TOK // RT_TL, PEER_H // RT_HPS),
        in_specs=[pl.BlockSpec((RT_TL, RT_HPS * PEER_DQ), lambda i, h: (i, h)),
                  pl.BlockSpec((RT_HPS, PEER_NK, HALF), lambda i, h: (h, 0, 0)),
                  pl.BlockSpec((RT_HPS, PEER_NK, HALF), lambda i, h: (h, 0, 0))],
        out_specs=[tile, tile, tile, tile],
        out_shape=[big, big, big, big],
        compiler_params=_cparams(("parallel", "parallel")),
        name="peer_route",
    )(q, keys1, keys2)


PE_TB = 512
PE_NA = 4
PE_AG = 2
PE_CH = PE_AG * PE_NA * PEER_NK
PE_NCH = PEER_NK * PEER_NK // PE_CH
PE_BH = 32
SQRT_HALF = 0.7071067811865476
MXU = 256


def _peer_kernel(*refs, final):
    if final:
        (u0_ref, u_ref, vt_ref, xt_ref, thr_ref, f1_ref, s2_ref, f2_ref, x1_ref, ga_ref, fw_ref,
         o_ref, acc_ref, pw_ref, gel_ref, rthr_ref, rf1_ref) = refs
    else:
        (u0_ref, u_ref, vt_ref, xt_ref, thr_ref, f1_ref, s2_ref, f2_ref, x1_ref, ga_ref,
         o_ref, acc_ref, pw_ref, gel_ref, rthr_ref, rf1_ref) = refs
    j = pl.program_id(1)

    @pl.when(j == 0)
    def _():
        acc_ref[...] = jnp.zeros_like(acc_ref)
        pw_ref[1] = jnp.zeros(pw_ref.shape[1:], bf16)
        act0 = jnp.dot(u0_ref[0], xt_ref[...], preferred_element_type=f32)
        gel_ref[0] = act0 * (1.0 + lax.erf(act0 * SQRT_HALF))

    cur = j % 2
    prev = 1 - cur
    jc = jnp.minimum(j, PE_NCH - 1)

    for ia in range(PE_AG * PE_NA):
        for h in range(PEER_H):
            a = jc * (PE_AG * PE_NA) + ia
            rthr_ref[ia * PEER_H + h] = jnp.broadcast_to(thr_ref[h, pl.ds(a, 1), :], (8, PE_TB))
            rf1_ref[ia * PEER_H + h] = jnp.broadcast_to(f1_ref[h, pl.ds(a, 1), :], (8, PE_TB))

    nsub = PE_BH // 8
    n_bh = PEER_NK // PE_BH

    def weight_tile(t, ag):
        l0 = pl.multiple_of((t // n_bh) * LANE, LANE)
        r0 = pl.multiple_of((t % n_bh) * PE_BH, PE_BH)
        lanes = pl.ds(l0, LANE)
        ws = [jnp.zeros((nsub, 8, LANE), f32) for _ in range(PE_NA)]
        for h in range(PEER_H):
            s2t = s2_ref[h, pl.ds(r0, PE_BH), lanes].reshape(nsub, 8, LANE)
            f2t = f2_ref[h, pl.ds(r0, PE_BH), lanes].reshape(nsub, 8, LANE)
            for i in range(PE_NA):
                row = (ag * PE_NA + i) * PEER_H + h
                sel = s2t >= rthr_ref[row, :, lanes][None]
                ws[i] = ws[i] + jnp.where(sel, f2t * rf1_ref[row, :, lanes][None], 0.0)
        for i in range(PE_NA):
            er = pl.ds(pl.multiple_of((ag * PE_NA + i) * PEER_NK + r0, PE_BH), PE_BH)
            pw_ref[cur, er, lanes] = (ws[i].reshape(PE_BH, LANE) * gel_ref[cur, er, lanes]).astype(bf16)

    n_half = PE_TB // MXU
    n_tiles = (PE_TB // LANE) * n_bh
    tiles_per_group = n_tiles // n_half

    def stage(p, carry):
        n0 = pl.multiple_of(p * MXU, MXU)
        act = jnp.dot(u_ref[0], xt_ref[:, pl.ds(n0, MXU)], preferred_element_type=f32)
        gel_ref[prev, :, pl.ds(n0, MXU)] = act * (1.0 + lax.erf(act * SQRT_HALF))
        acc_ref[:, pl.ds(n0, MXU)] += jnp.dot(vt_ref[0], pw_ref[prev, :, pl.ds(n0, MXU)],
                                              preferred_element_type=f32)
        for q in range(tiles_per_group):
            for ag in range(PE_AG):
                weight_tile(p * tiles_per_group + q, ag)
        return carry

    lax.fori_loop(0, n_half, stage, 0)

    @pl.when(j == PE_NCH)
    def _():
        out = acc_ref[...].T
        for s in range(PE_TB // SEG):
            r = slice(s * SEG, (s + 1) * SEG)
            x2 = x1_ref[r, :] + ga_ref[s:s + 1, :] * out[r, :]
            o_ref[r, :] = _rms(x2, fw_ref[...]) if final else x2


def _peer(u_c, vt_c, h2t, thr, f1t, s2t, f2t, x1, mod_seg, final_w):
    final = final_w is not None
    nseg_b = PE_TB // SEG
    rt = pl.BlockSpec((PEER_H, PEER_NK, PE_TB), lambda i, j: (0, 0, i))
    in_specs = [pl.BlockSpec((1, PE_CH, D), lambda i, j: (0, 0, 0)),
                pl.BlockSpec((1, PE_CH, D), lambda i, j: (jnp.minimum(j + 1, PE_NCH - 1), 0, 0)),
                pl.BlockSpec((1, D, PE_CH), lambda i, j: (jnp.maximum(j - 1, 0), 0, 0)),
                pl.BlockSpec((D, PE_TB), lambda i, j: (0, i)),
                rt, rt, rt, rt,
                pl.BlockSpec((PE_TB, D), lambda i, j: (i, 0)),
                pl.BlockSpec((nseg_b, D), lambda i, j: (i, 5))]
    args = [u_c, u_c, vt_c, h2t, thr, f1t, s2t, f2t, x1, mod_seg]
    if final:
        in_specs.append(pl.BlockSpec((1, D), lambda i, j: (0, 0)))
        args.append(final_w.reshape(1, D))
    return pl.pallas_call(
        functools.partial(_peer_kernel, final=final),
        grid=(N_TOK // PE_TB, PE_NCH + 1),
        in_specs=in_specs,
        out_specs=pl.BlockSpec((PE_TB, D), lambda i, j: (i, 0)),
        out_shape=jax.ShapeDtypeStruct((N_TOK, D), f32),
        scratch_shapes=[pltpu.VMEM((D, PE_TB), f32), pltpu.VMEM((2, PE_CH, PE_TB), bf16),
                        pltpu.VMEM((2, PE_CH, PE_TB), f32),
                        pltpu.VMEM((PE_AG * PE_NA * PEER_H, 8, PE_TB), f32),
                        pltpu.VMEM((PE_AG * PE_NA * PEER_H, 8, PE_TB), f32)],
        compiler_params=_cparams(("parallel", "arbitrary")),
        name="peer_experts_final" if final else "peer_experts",
    )(*args)


def _reorder_w_in(w):
    o = np.cumsum([0, 512, 512, 1024, 1024, GLA_RANK, SSD_INNER, SSD_CONV_DIM, SSD_H, 2 * D])
    q_k_v_go = w[:, o[0]:o[4]]
    gk_low = w[:, o[4]:o[5]]
    z = w[:, o[5]:o[6]]
    xbc = w[:, o[6]:o[7]]
    dt = w[:, o[7]:o[8]]
    gates = w[:, o[8]:o[9]]
    pad = lambda m: jnp.pad(m, ((0, 0), (0, LANE - m.shape[1])))
    low = jnp.concatenate([gk_low, jnp.zeros((D, LOW_DT0 - GLA_RANK), w.dtype), dt], axis=1)
    return jnp.concatenate([q_k_v_go, xbc, z, gates, pad(low)], axis=1).astype(bf16)


def kernel(x_prompt, x_sample, state_gla, state_ssd, state_conv, c_prompt, c_sample, w_ada, b_ada, norm1_w, w_in, gla_gk_w2, gla_gk_b, gla_norm_w, gla_proj, ssd_conv_w, ssd_conv_b, ssd_dt_bias, ssd_A_log, ssd_D, ssd_norm_w, ssd_proj, w_out, norm2_w, peer_wq, peer_keys1, peer_keys2, peer_u, peer_v, final_norm_w):
    x = jnp.concatenate([x_prompt.reshape(N_P, D), x_sample.reshape(N_S, D)], axis=0)
    c_all = jnp.concatenate([c_prompt, c_sample], axis=0)
    mod = _ada(c_all, w_ada, b_ada)
    seg2seq = np.concatenate([np.repeat(np.arange(B_P), SEG_PER_PSEQ), B_P + np.arange(B_S)])

    gla_st, ssd_st, conv_st = [], [], []
    for l in range(DEPTH):
        mod_seg = mod[l][seg2seq]
        h = _normmod(x, norm1_w[l], mod_seg, 1, 0)
        p = _mm(h, _reorder_w_in(w_in[l]), 512, P_COLS // 3, "in_proj")

        w2pad = jnp.pad(gla_gk_w2[l], ((0, LANE - GLA_RANK), (0, 0)))
        low_lanes = lambda m: jnp.pad(m.reshape(1, SSD_H), ((0, 0), (LOW_DT0, LANE - LOW_DT0 - SSD_H)))
        dvec = jnp.repeat(ssd_D[l], SSD_P).reshape(1, SSD_INNER)
        (oa, gla_p, gla_s), (yb, ssd_p, ssd_s, conv_p, conv_s) = _mixers(
            _gla_call(p, w2pad, gla_gk_b[l], gla_norm_w[l], state_gla[l]),
            _ssd_call(p, state_conv[l], ssd_conv_w[l], ssd_conv_b[l], low_lanes(ssd_dt_bias[l]),
                      low_lanes(ssd_A_log[l]), dvec, ssd_norm_w[l], state_ssd[l]))

        x1, h2, h2t = _mix(oa, yb, p, x, mod_seg, norm2_w[l], gla_proj[l].astype(bf16),
                           ssd_proj[l].astype(bf16), w_out[l].astype(bf16))
        q = _mm(h2, peer_wq[l].astype(bf16), 512, PEER_H * PEER_DQ, "peer_query")
        thr, s2t, f1t, f2t = _route(q, peer_keys1[l], peer_keys2[l])
        u_c = peer_u[l].astype(bf16).reshape(PE_NCH, PE_CH, D)
        vt_c = peer_v[l].astype(bf16).reshape(PE_NCH, PE_CH, D).transpose(0, 2, 1)
        x = _peer(u_c, vt_c, h2t, thr, f1t, s2t, f2t, x1, mod_seg,
                  final_norm_w if l == DEPTH - 1 else None)

        gla_st.append((gla_p, gla_s))
        ssd_st.append((ssd_p, ssd_s))
        conv_st.append((conv_p, conv_s))

    y_prompt = x[:N_P].reshape(B_P, T_P, D)
    y_sample = x[N_P:].reshape(B_S, T_S, D)
    stack = lambda pairs, k: jnp.stack([pr[k] for pr in pairs])
    return (y_prompt, y_sample, stack(gla_st, 0), stack(ssd_st, 0), stack(conv_st, 0),
            stack(gla_st, 1), stack(ssd_st, 1), stack(conv_st, 1))
```

```python
import functools

import jax
import jax.numpy as jnp
import numpy as np
from jax import lax
from jax.experimental import pallas as pl
from jax.experimental.pallas import tpu as pltpu

f32 = jnp.float32
bf16 = jnp.bfloat16

D = 1024
DEPTH = 2
B_P, T_P = 8, 2048
B_S, T_S = 32, 64
N_P = B_P * T_P
N_S = B_S * T_S
N_TOK = N_P + N_S
SEG = 64
NSEG = N_TOK // SEG
NSEG_P = N_P // SEG
SEG_PER_PSEQ = T_P // SEG
EPS = 1e-6

GLA_H, GLA_DK, GLA_DV = 4, 128, 256
GLA_RANK = 16
GLA_GATE_NORM = 16.0
SSD_INNER = 2048
SSD_P = 64
SSD_H = 32
SSD_G = 4
SSD_N = 128
SSD_HG = SSD_H // SSD_G
SSD_GW = SSD_INNER // SSD_G
CONV_W = 4
SSD_CONV_DIM = SSD_INNER + 2 * SSD_G * SSD_N

PEER_H = 8
PEER_NK = 128
PEER_DQ = 256
PEER_TOPK = 16

C_Q, C_K, C_V, C_GO = 0, 512, 1024, 2048
C_XBC, C_Z, C_GATE = 3072, 6144, 8192
C_LOW = 10240
LOW_DT0 = 32
P_COLS = C_LOW + 128

LANE = 128
VMEM_LIMIT = 56 * 1024 * 1024


def _cparams(sem):
    return pltpu.CompilerParams(dimension_semantics=sem, vmem_limit_bytes=VMEM_LIMIT)


def _dot(a, b):
    return jnp.dot(a.astype(bf16), b.astype(bf16), preferred_element_type=f32)


def _dot_nt(a, b):
    return lax.dot_general(a.astype(bf16), b.astype(bf16), (((1,), (1,)), ((), ())), preferred_element_type=f32)


def _dot_tn(a, b):
    return lax.dot_general(a.astype(bf16), b.astype(bf16), (((0,), (0,)), ((), ())), preferred_element_type=f32)


def _split3(x):
    hi = x.astype(bf16)
    r = x - hi.astype(f32)
    mid = r.astype(bf16)
    lo = (r - mid.astype(f32)).astype(bf16)
    return hi, mid, lo


def _dot3(a, b):
    ah, am, _ = _split3(a)
    bh, bm, _ = _split3(b)
    d = functools.partial(jnp.dot, preferred_element_type=f32)
    return d(ah, bh) + (d(ah, bm) + d(am, bh))


def _dot3_nt(a, b):
    ah, am, _ = _split3(a)
    bh, bm, _ = _split3(b)
    d = functools.partial(lax.dot_general, dimension_numbers=(((1,), (1,)), ((), ())), preferred_element_type=f32)
    return d(ah, bh) + (d(ah, bm) + d(am, bh))


def _cumsum_rows(x):
    n = x.shape[0]
    tri = (lax.broadcasted_iota(jnp.int32, (n, n), 0) >= lax.broadcasted_iota(jnp.int32, (n, n), 1)).astype(bf16)
    hi, mid, lo = _split3(x)
    d = functools.partial(jnp.dot, preferred_element_type=f32)
    return d(tri, hi) + (d(tri, mid) + d(tri, lo))


def _silu(x):
    return x * jax.nn.sigmoid(x)


def _softplus(x):
    return jnp.maximum(x, 0.0) + jnp.log1p(jnp.exp(-jnp.abs(x)))


def _rms(x, w):
    return x * lax.rsqrt(jnp.mean(x * x, axis=-1, keepdims=True) + EPS) * w


def _ada_kernel(c_ref, w_ref, b_ref, o_ref):
    o_ref[0] = _dot3(_silu(c_ref[...]), w_ref[0]) + b_ref[0]


def _ada(c_all, w_ada, b_ada):
    nb = c_all.shape[0]
    tn = 1536
    return pl.pallas_call(
        _ada_kernel,
        grid=(DEPTH, 6 * D // tn),
        in_specs=[pl.BlockSpec((nb, D), lambda l, j: (0, 0)),
                  pl.BlockSpec((1, D, tn), lambda l, j: (l, 0, j)),
                  pl.BlockSpec((1, 1, tn), lambda l, j: (l, 0, j))],
        out_specs=pl.BlockSpec((1, nb, tn), lambda l, j: (l, 0, j)),
        out_shape=jax.ShapeDtypeStruct((DEPTH, nb, 6 * D), f32),
        compiler_params=_cparams(("parallel", "parallel")),
        name="ada_mod",
    )(c_all, w_ada, b_ada.reshape(DEPTH, 1, 6 * D))


NM_TB = 512


def _normmod_kernel(x_ref, w_ref, sc_ref, sh_ref, o_ref):
    for s in range(NM_TB // SEG):
        r = slice(s * SEG, (s + 1) * SEG)
        y = _rms(x_ref[r, :], w_ref[...])
        o_ref[r, :] = (y * (1.0 + sc_ref[s:s + 1, :]) + sh_ref[s:s + 1, :]).astype(bf16)


def _normmod(x, w, mod_seg, sc_col, sh_col):
    nseg_b = NM_TB // SEG
    return pl.pallas_call(
        _normmod_kernel,
        grid=(N_TOK // NM_TB,),
        in_specs=[pl.BlockSpec((NM_TB, D), lambda i: (i, 0)),
                  pl.BlockSpec((1, D), lambda i: (0, 0)),
                  pl.BlockSpec((nseg_b, D), lambda i: (i, sc_col)),
                  pl.BlockSpec((nseg_b, D), lambda i: (i, sh_col))],
        out_specs=pl.BlockSpec((NM_TB, D), lambda i: (i, 0)),
        out_shape=jax.ShapeDtypeStruct((N_TOK, D), bf16),
        compiler_params=_cparams(("parallel",)),
        name="norm_mod",
    )(x, w.reshape(1, D), mod_seg, mod_seg)


def _mm_kernel(a_ref, b_ref, o_ref):
    o_ref[...] = jnp.dot(a_ref[...], b_ref[...], preferred_element_type=f32)


def _mm(a, b, tm, tn, name):
    m, k = a.shape
    n = b.shape[1]
    return pl.pallas_call(
        _mm_kernel,
        grid=(n // tn, m // tm),
        in_specs=[pl.BlockSpec((tm, k), lambda j, i: (i, 0)),
                  pl.BlockSpec((k, tn), lambda j, i: (0, j))],
        out_specs=pl.BlockSpec((tm, tn), lambda j, i: (i, j)),
        out_shape=jax.ShapeDtypeStruct((m, n), f32),
        compiler_params=_cparams(("parallel", "parallel")),
        name=name,
    )(a, b)


def _is_prompt(c):
    return c < NSEG_P


def _sample_seq(c):
    return jnp.maximum(c - NSEG_P, 0)


def _prompt_seq(c):
    return jnp.minimum(c // SEG_PER_PSEQ, B_P - 1)


def _gla_kernel(q_ref, k_ref, v_ref, g_ref, low_ref, w2_ref, b2_ref, nw_ref, s0_ref,
                o_ref, stp_ref, sts_ref, st_ref):
    c = pl.program_id(0)
    is_p = _is_prompt(c)
    pos = c % SEG_PER_PSEQ

    @pl.when(jnp.logical_and(is_p, pos == 0))
    def _():
        st_ref[...] = jnp.zeros_like(st_ref)

    @pl.when(jnp.logical_not(is_p))
    def _():
        for h in range(GLA_H):
            st_ref[h] = s0_ref[0, h].T

    yield
    pre = _dot3(low_ref[...], w2_ref[...]) + b2_ref[...]
    gk = -_softplus(-pre) / GLA_GATE_NORM
    b = _cumsum_rows(gk)
    b_last = b[SEG - 1:SEG, :]
    e_b = jnp.exp(b)
    e_nb = jnp.exp(-b)
    e_end = jnp.exp(b_last - b)
    e_last = jnp.exp(b_last)
    causal = lax.broadcasted_iota(jnp.int32, (SEG, SEG), 0) >= lax.broadcasted_iota(jnp.int32, (SEG, SEG), 1)
    for h in range(GLA_H):
        ks = slice(h * GLA_DK, (h + 1) * GLA_DK)
        vs = slice(h * GLA_DV, (h + 1) * GLA_DV)
        k = k_ref[:, ks]
        q_t = (q_ref[:, ks] * (GLA_DK ** -0.5)) * e_b[:, ks]
        att = jnp.where(causal, _dot_nt(q_t, k * e_nb[:, ks]), 0.0)
        v = v_ref[:, vs]
        st = st_ref[h]
        o = _dot(att, v) + _dot_nt(q_t, st)
        st_ref[h] = st * e_last[:, ks] + _dot_tn(v, k * e_end[:, ks])
        o_ref[:, vs] = (_rms(o, nw_ref[...]) * _silu(g_ref[:, vs])).astype(bf16)

    yield
    @pl.when(jnp.logical_and(is_p, pos == SEG_PER_PSEQ - 1))
    def _():
        for h in range(GLA_H):
            stp_ref[0, h] = st_ref[h].T

    @pl.when(jnp.logical_not(is_p))
    def _():
        for h in range(GLA_H):
            sts_ref[0, h] = st_ref[h].T


def _gla_call(p, w2pad, b2, norm_w, s0):
    st_block = (1, GLA_H, GLA_DK, GLA_DV)
    kd, vd = GLA_H * GLA_DK, GLA_H * GLA_DV
    return dict(
        in_specs=[pl.BlockSpec((SEG, kd), lambda c: (c, C_Q // kd)),
                  pl.BlockSpec((SEG, kd), lambda c: (c, C_K // kd)),
                  pl.BlockSpec((SEG, vd), lambda c: (c, C_V // vd)),
                  pl.BlockSpec((SEG, vd), lambda c: (c, C_GO // vd)),
                  pl.BlockSpec((SEG, LANE), lambda c: (c, C_LOW // LANE)),
                  pl.BlockSpec((LANE, kd), lambda c: (0, 0)),
                  pl.BlockSpec((1, kd), lambda c: (0, 0)),
                  pl.BlockSpec((1, GLA_DV), lambda c: (0, 0)),
                  pl.BlockSpec(st_block, lambda c: (_sample_seq(c), 0, 0, 0))],
        out_specs=[pl.BlockSpec((SEG, vd), lambda c: (c, 0)),
                   pl.BlockSpec(st_block, lambda c: (_prompt_seq(c), 0, 0, 0)),
                   pl.BlockSpec(st_block, lambda c: (_sample_seq(c), 0, 0, 0))],
        out_shape=[jax.ShapeDtypeStruct((N_TOK, vd), bf16),
                   jax.ShapeDtypeStruct((B_P, GLA_H, GLA_DK, GLA_DV), f32),
                   jax.ShapeDtypeStruct((B_S, GLA_H, GLA_DK, GLA_DV), f32)],
        scratch_shapes=[pltpu.VMEM((GLA_H, GLA_DV, GLA_DK), f32)],
        args=(p, p, p, p, p, w2pad, b2.reshape(1, -1), norm_w.reshape(1, -1), s0))


EXT = SEG + 8


def _ssd_kernel(xbc_ref, z_ref, dt_ref, cp_ref, cw_ref, cb_ref, dtb_ref, alog_ref, dvec_ref, nw_ref, s0_ref,
                o_ref, stp_ref, sts_ref, cvp_ref, cvs_ref, ext, st_ref):
    c = pl.program_id(0)
    is_p = _is_prompt(c)
    pos = c % SEG_PER_PSEQ

    @pl.when(jnp.logical_and(is_p, pos == 0))
    def _():
        ext[0:8, :] = jnp.zeros((8, SSD_CONV_DIM), f32)
        st_ref[...] = jnp.zeros_like(st_ref)

    @pl.when(jnp.logical_not(is_p))
    def _():
        ext[0:8, :] = jnp.zeros((8, SSD_CONV_DIM), f32)
        ext[8 - (CONV_W - 1):8, :] = cp_ref[0]
        st_ref[...] = s0_ref[0].reshape(st_ref.shape)

    yield
    ext[8:EXT, :] = xbc_ref[...]
    acc = cb_ref[...] + ext[pl.ds(8 - (CONV_W - 1), SEG), :] * cw_ref[0:1, :]
    for i in range(1, CONV_W):
        acc = acc + ext[pl.ds(8 - (CONV_W - 1) + i, SEG), :] * cw_ref[i:i + 1, :]
    ext[0:8, :] = ext[SEG:EXT, :]
    conv = _silu(acc)

    dt = _softplus(dt_ref[...] + dtb_ref[...])
    a = _cumsum_rows(dt * (-jnp.exp(alog_ref[...])))
    a_t = a.T
    dt_t = dt.T
    a_last = a[SEG - 1:SEG, :]
    e_a = jnp.exp(a)
    w_col = dt * jnp.exp(a_last - a)
    e_last = jnp.exp(a_last)
    causal = lax.broadcasted_iota(jnp.int32, (SEG, SEG), 0) >= lax.broadcasted_iota(jnp.int32, (SEG, SEG), 1)
    lane_lo = lax.broadcasted_iota(jnp.int32, (SEG, 2 * SSD_P), 1) < SSD_P
    sub_lo = lax.broadcasted_iota(jnp.int32, (2 * SSD_P, SSD_N), 0) < SSD_P
    nbc = SSD_G * SSD_N

    def pair_cols(m, ja, jb):
        return jnp.where(lane_lo, jnp.broadcast_to(m[:, ja:ja + 1], (SEG, 2 * SSD_P)),
                         jnp.broadcast_to(m[:, jb:jb + 1], (SEG, 2 * SSD_P)))

    for g in range(SSD_G):
        j0 = LOW_DT0 + g * SSD_HG
        bc = conv[:, SSD_INNER + g * SSD_N:SSD_INNER + (g + 1) * SSD_N]
        cc = conv[:, SSD_INNER + nbc + g * SSD_N:SSD_INNER + nbc + (g + 1) * SSD_N]
        cbm = _dot_nt(cc, bc)

        def w_intra(j):
            seg = a[:, j:j + 1] - a_t[j:j + 1, :]
            return cbm * jnp.exp(jnp.where(causal, seg, -jnp.inf)) * dt_t[j:j + 1, :]

        ys = []
        for p in range(SSD_HG // 2):
            ja, jb = j0 + 2 * p, j0 + 2 * p + 1
            pp = g * (SSD_HG // 2) + p
            xp = conv[:, pp * 2 * SSD_P:(pp + 1) * 2 * SSD_P]
            y_intra = jnp.where(lane_lo, _dot(w_intra(ja), xp), _dot(w_intra(jb), xp))
            sp = st_ref[pp]
            y_inter = _dot_nt(cc, sp) * pair_cols(e_a, ja, jb)
            ds = _dot_tn(xp * pair_cols(w_col, ja, jb), bc)
            rs = jnp.where(sub_lo, jnp.broadcast_to(e_last[:, ja:ja + 1], (2 * SSD_P, SSD_N)),
                           jnp.broadcast_to(e_last[:, jb:jb + 1], (2 * SSD_P, SSD_N)))
            st_ref[pp] = sp * rs + ds
            ys.append(y_intra + y_inter + dvec_ref[:, pp * 2 * SSD_P:(pp + 1) * 2 * SSD_P] * xp)
        gw = slice(g * SSD_GW, (g + 1) * SSD_GW)
        y = jnp.concatenate(ys, axis=1) * _silu(z_ref[:, gw])
        o_ref[:, gw] = _rms(y, nw_ref[:, gw]).astype(bf16)

    yield
    @pl.when(jnp.logical_and(is_p, pos == SEG_PER_PSEQ - 1))
    def _():
        stp_ref[0] = st_ref[...].reshape(stp_ref.shape[1:])
        cvp_ref[0] = xbc_ref[SEG - (CONV_W - 1):SEG, :]

    @pl.when(jnp.logical_not(is_p))
    def _():
        sts_ref[0] = st_ref[...].reshape(sts_ref.shape[1:])
        cvs_ref[0] = xbc_ref[SEG - (CONV_W - 1):SEG, :]


def _ssd_call(p, conv_prev, conv_w, conv_b, dtb, alog, dvec, norm_w, s0):
    st_block = (1, SSD_H, SSD_P, SSD_N)
    cv_block = (1, CONV_W - 1, SSD_CONV_DIM)
    full = lambda c: (0, 0)
    return dict(
        in_specs=[pl.BlockSpec((SEG, SSD_CONV_DIM), lambda c: (c, C_XBC // SSD_CONV_DIM)),
                  pl.BlockSpec((SEG, SSD_INNER), lambda c: (c, C_Z // SSD_INNER)),
                  pl.BlockSpec((SEG, LANE), lambda c: (c, C_LOW // LANE)),
                  pl.BlockSpec(cv_block, lambda c: (_sample_seq(c), 0, 0)),
                  pl.BlockSpec((CONV_W, SSD_CONV_DIM), full),
                  pl.BlockSpec((1, SSD_CONV_DIM), full),
                  pl.BlockSpec((1, LANE), full),
                  pl.BlockSpec((1, LANE), full),
                  pl.BlockSpec((1, SSD_INNER), full),
                  pl.BlockSpec((1, SSD_INNER), full),
                  pl.BlockSpec(st_block, lambda c: (_sample_seq(c), 0, 0, 0))],
        out_specs=[pl.BlockSpec((SEG, SSD_INNER), lambda c: (c, 0)),
                   pl.BlockSpec(st_block, lambda c: (_prompt_seq(c), 0, 0, 0)),
                   pl.BlockSpec(st_block, lambda c: (_sample_seq(c), 0, 0, 0)),
                   pl.BlockSpec(cv_block, lambda c: (_prompt_seq(c), 0, 0)),
                   pl.BlockSpec(cv_block, lambda c: (_sample_seq(c), 0, 0))],
        out_shape=[jax.ShapeDtypeStruct((N_TOK, SSD_INNER), bf16),
                   jax.ShapeDtypeStruct((B_P, SSD_H, SSD_P, SSD_N), f32),
                   jax.ShapeDtypeStruct((B_S, SSD_H, SSD_P, SSD_N), f32),
                   jax.ShapeDtypeStruct((B_P, CONV_W - 1, SSD_CONV_DIM), f32),
                   jax.ShapeDtypeStruct((B_S, CONV_W - 1, SSD_CONV_DIM), f32)],
        scratch_shapes=[pltpu.VMEM((EXT, SSD_CONV_DIM), f32),
                        pltpu.VMEM((SSD_H // 2, 2 * SSD_P, SSD_N), f32)],
        args=(p, p, p, conv_prev, conv_w, conv_b.reshape(1, -1), dtb, alog, dvec, norm_w.reshape(1, -1), s0))


def _mixers(gla, ssd):
    n_in = (len(gla["in_specs"]), len(ssd["in_specs"]))
    n_out = (len(gla["out_specs"]), len(ssd["out_specs"]))
    n_scr = (len(gla["scratch_shapes"]), len(ssd["scratch_shapes"]))

    def body(*refs):
        ins, outs, scr = refs[:sum(n_in)], refs[sum(n_in):sum(n_in) + sum(n_out)], refs[sum(n_in) + sum(n_out):]
        parts = [_gla_kernel(*ins[:n_in[0]], *outs[:n_out[0]], *scr[:n_scr[0]]),
                 _ssd_kernel(*ins[n_in[0]:], *outs[n_out[0]:], *scr[n_scr[0]:])]
        for _ in range(3):
            for part in parts:
                next(part, None)

    outs = pl.pallas_call(
        body,
        grid=(NSEG,),
        in_specs=gla["in_specs"] + ssd["in_specs"],
        out_specs=gla["out_specs"] + ssd["out_specs"],
        out_shape=gla["out_shape"] + ssd["out_shape"],
        scratch_shapes=gla["scratch_shapes"] + ssd["scratch_shapes"],
        compiler_params=_cparams(("arbitrary",)),
        name="token_mixers",
    )(*gla["args"], *ssd["args"])
    return outs[:n_out[0]], outs[n_out[0]:]


MIX_TB = 512


def _mix_kernel(oa_ref, yb_ref, gt_ref, x_ref, ga_ref, sc_ref, sh_ref, nw_ref, wa_ref, wb_ref, wo_ref,
                x1_ref, h2_ref, h2t_ref):
    br_a = jnp.dot(oa_ref[...], wa_ref[...], preferred_element_type=f32)
    br_b = jnp.dot(yb_ref[...], wb_ref[...], preferred_element_type=f32)
    g_a = jax.nn.sigmoid(gt_ref[:, 0:D])
    g_b = jax.nn.sigmoid(gt_ref[:, D:2 * D])
    y = _dot(g_a * br_a + g_b * br_b, wo_ref[...])
    for s in range(MIX_TB // SEG):
        r = slice(s * SEG, (s + 1) * SEG)
        x1 = x_ref[r, :] + ga_ref[s:s + 1, :] * y[r, :]
        x1_ref[r, :] = x1
        h2_ref[r, :] = (_rms(x1, nw_ref[...]) * (1.0 + sc_ref[s:s + 1, :]) + sh_ref[s:s + 1, :]).astype(bf16)
    h2t_ref[...] = h2_ref[...].T


def _mix(oa, yb, p, x, mod_seg, norm2_w, wa, wb, wo):
    nseg_b = MIX_TB // SEG
    full = lambda i: (0, 0)
    return pl.pallas_call(
        _mix_kernel,
        grid=(N_TOK // MIX_TB,),
        in_specs=[pl.BlockSpec((MIX_TB, D), lambda i: (i, 0)),
                  pl.BlockSpec((MIX_TB, SSD_INNER), lambda i: (i, 0)),
                  pl.BlockSpec((MIX_TB, 2 * D), lambda i: (i, C_GATE // (2 * D))),
                  pl.BlockSpec((MIX_TB, D), lambda i: (i, 0)),
                  pl.BlockSpec((nseg_b, D), lambda i: (i, 2)),
                  pl.BlockSpec((nseg_b, D), lambda i: (i, 4)),
                  pl.BlockSpec((nseg_b, D), lambda i: (i, 3)),
                  pl.BlockSpec((1, D), full),
                  pl.BlockSpec((D, D), full),
                  pl.BlockSpec((SSD_INNER, D), full),
                  pl.BlockSpec((D, D), full)],
        out_specs=[pl.BlockSpec((MIX_TB, D), lambda i: (i, 0)),
                   pl.BlockSpec((MIX_TB, D), lambda i: (i, 0)),
                   pl.BlockSpec((D, MIX_TB), lambda i: (0, i))],
        out_shape=[jax.ShapeDtypeStruct((N_TOK, D), f32),
                   jax.ShapeDtypeStruct((N_TOK, D), bf16),
                   jax.ShapeDtypeStruct((D, N_TOK), bf16)],
        compiler_params=_cparams(("parallel",)),
        name="mix_out",
    )(oa, yb, p, x, mod_seg, mod_seg, mod_seg, norm2_w.reshape(1, D), wa, wb, wo)


RT_TL = 128
HALF = PEER_DQ // 2


RT_HPS = 8
SUBLANES = 8


def _batcher_sort_net(n):
    def merge(lo, hi, r):
        step = 2 * r
        if step < hi - lo:
            yield from merge(lo, hi, step)
            yield from merge(lo + r, hi, step)
            yield from ((i, i + r) for i in range(lo + r, hi - r, step))
        else:
            yield (lo, lo + r)

    def sort(lo, hi):
        if hi - lo >= 1:
            mid = lo + (hi - lo) // 2
            yield from sort(lo, mid)
            yield from sort(mid + 1, hi)
            yield from merge(lo, hi, 1)

    return tuple(sort(0, n - 1))


def _bitonic_merge_net(n):
    net, d = [], n // 2
    while d >= 1:
        net += [(i, i + d) for i in range(n) if (i // d) % 2 == 0]
        d //= 2
    return tuple(net)


_SORT16 = _batcher_sort_net(PEER_TOPK)
_MERGE16 = _bitonic_merge_net(PEER_TOPK)
N_CAND_VREGS = 10


def _compare_exchange(x, net):
    for i, j in net:
        x[i], x[j] = jnp.maximum(x[i], x[j]), jnp.minimum(x[i], x[j])


def _merge_across_sublanes(x, n_valid):
    for shift in (4, 2, 1):
        y = [pltpu.roll(v, shift, 0) for v in x]
        merged = []
        for k in range(PEER_TOPK):
            a = x[k] if k < n_valid else None
            b = y[PEER_TOPK - 1 - k] if PEER_TOPK - 1 - k < n_valid else None
            merged.append(jnp.maximum(a, b) if (a is not None and b is not None) else (a if b is None else b))
        x = merged
        _compare_exchange(x, _MERGE16)
        n_valid = PEER_TOPK
    return x


def _top16_sorted(s):
    x = [s[SUBLANES * k:SUBLANES * (k + 1), :] for k in range(PEER_NK // SUBLANES)]
    _compare_exchange(x, _SORT16)
    return _merge_across_sublanes(x, PEER_TOPK)


def _route_kernel(q_ref, k1_ref, k2_ref, thr_ref, s2_ref, f1_ref, f2_ref):
    sub = lax.broadcasted_iota(jnp.int32, (SUBLANES, RT_TL), 0)
    ninf = jnp.float32(-jnp.inf)

    def by_sublane(vs):
        out = vs[0]
        for r in range(1, SUBLANES):
            out = jnp.where(sub == r, vs[r], out)
        return out

    for hh in range(RT_HPS):
        s1 = _dot3_nt(k1_ref[hh], q_ref[:, hh * PEER_DQ:hh * PEER_DQ + HALF])
        s2 = _dot3_nt(k2_ref[hh], q_ref[:, hh * PEER_DQ + HALF:(hh + 1) * PEER_DQ])
        s2_ref[hh] = s2
        v1 = _top16_sorted(s1)
        v2 = _top16_sorted(s2)
        v2lo, v2hi, v1hi = by_sublane(v2[:8]), by_sublane(v2[8:]), by_sublane(v1[8:])
        cands = [v1[0] + v2lo, v1[0] + v2hi, v1[1] + v2lo]
        for i in range(2, 8):
            cands.append(jnp.where(sub < PEER_TOPK // (i + 1), v1[i] + v2lo, ninf))
        cands.append(v1hi + v2[0])
        assert len(cands) == N_CAND_VREGS
        _compare_exchange(cands, tuple((i, j) for i, j in _SORT16 if j < N_CAND_VREGS))
        top = _merge_across_sublanes(cands, N_CAND_VREGS)
        zsum = jnp.ones_like(top[0])
        for k in range(1, PEER_TOPK):
            zsum = zsum + jnp.exp(top[k] - top[0])
        tau = top[PEER_TOPK - 1]
        for k in range(PEER_NK // SUBLANES):
            rows = slice(SUBLANES * k, SUBLANES * (k + 1))
            s1k = s1[rows, :]
            thr = jnp.full_like(s1k, jnp.inf)
            for j in range(PEER_TOPK):
                thr = jnp.where(s1k + v2[j] >= tau, v2[j], thr)
            thr_ref[hh, rows, :] = thr
        f1_ref[hh] = jnp.exp(s1 - v1[0][0:1, :]) * (0.5 / zsum[0:1, :])
        f2_ref[hh] = jnp.exp(s2 - v2[0][0:1, :])


def _route(q, keys1, keys2):
    tile = pl.BlockSpec((RT_HPS, PEER_NK, RT_TL), lambda i, h: (h, 0, i))
    big = jax.ShapeDtypeStruct((PEER_H, PEER_NK, N_TOK), f32)
    return pl.pallas_call(
        _route_kernel,
        grid=(N_TOK // RT_TL, PEER_H // RT_HPS),
        in_specs=[pl.BlockSpec((RT_TL, RT_HPS * PEER_DQ), lambda i, h: (i, h)),
                  pl.BlockSpec((RT_HPS, PEER_NK, HALF), lambda i, h: (h, 0, 0)),
                  pl.BlockSpec((RT_HPS, PEER_NK, HALF), lambda i, h: (h, 0, 0))],
        out_specs=[tile, tile, tile, tile],
        out_shape=[big, big, big, big],
        compiler_params=_cparams(("parallel", "parallel")),
        name="peer_route",
    )(q, keys1, keys2)


PE_TB = 512
PE_NA = 4
PE_AG = 2
PE_CH = PE_AG * PE_NA * PEER_NK
PE_NCH = PEER_NK * PEER_NK // PE_CH
PE_BH = 32
SQRT_HALF = 0.7071067811865476
MXU = 256


def _peer_kernel(*refs, final):
    if final:
        (u0_ref, u_ref, vt_ref, xt_ref, thr_ref, f1_ref, s2_ref, f2_ref, x1_ref, ga_ref, fw_ref,
         o_ref, acc_ref, pw_ref, gel_ref, rthr_ref, rf1_ref) = refs
    else:
        (u0_ref, u_ref, vt_ref, xt_ref, thr_ref, f1_ref, s2_ref, f2_ref, x1_ref, ga_ref,
         o_ref, acc_ref, pw_ref, gel_ref, rthr_ref, rf1_ref) = refs
    j = pl.program_id(1)

    @pl.when(j == 0)
    def _():
        acc_ref[...] = jnp.zeros_like(acc_ref)
        pw_ref[1] = jnp.zeros(pw_ref.shape[1:], bf16)
        act0 = jnp.dot(u0_ref[0], xt_ref[...], preferred_element_type=f32)
        gel_ref[0] = act0 * (1.0 + lax.erf(act0 * SQRT_HALF))

    cur = j % 2
    prev = 1 - cur

    def prep_rows():
        for ia in range(PE_AG * PE_NA):
            for h in range(PEER_H):
                a = j * (PE_AG * PE_NA) + ia
                rthr_ref[ia * PEER_H + h] = jnp.broadcast_to(thr_ref[h, pl.ds(a, 1), :], (8, PE_TB))
                rf1_ref[ia * PEER_H + h] = jnp.broadcast_to(f1_ref[h, pl.ds(a, 1), :], (8, PE_TB))

    nsub = PE_BH // 8
    n_bh = PEER_NK // PE_BH

    def weight_tile(t, ag):
        l0 = pl.multiple_of((t // n_bh) * LANE, LANE)
        r0 = pl.multiple_of((t % n_bh) * PE_BH, PE_BH)
        lanes = pl.ds(l0, LANE)
        ws = [jnp.zeros((nsub, 8, LANE), f32) for _ in range(PE_NA)]
        for h in range(PEER_H):
            s2t = s2_ref[h, pl.ds(r0, PE_BH), lanes].reshape(nsub, 8, LANE)
            f2t = f2_ref[h, pl.ds(r0, PE_BH), lanes].reshape(nsub, 8, LANE)
            for i in range(PE_NA):
                row = (ag * PE_NA + i) * PEER_H + h
                sel = s2t >= rthr_ref[row, :, lanes][None]
                ws[i] = ws[i] + jnp.where(sel, f2t * rf1_ref[row, :, lanes][None], 0.0)
        for i in range(PE_NA):
            er = pl.ds(pl.multiple_of((ag * PE_NA + i) * PEER_NK + r0, PE_BH), PE_BH)
            pw_ref[cur, er, lanes] = (ws[i].reshape(PE_BH, LANE) * gel_ref[cur, er, lanes]).astype(bf16)

    n_half = PE_TB // MXU
    n_tiles = (PE_TB // LANE) * n_bh
    tiles_per_group = n_tiles // n_half

    def stage(p, carry):
        n0 = pl.multiple_of(p * MXU, MXU)
        act = jnp.dot(u_ref[0], xt_ref[:, pl.ds(n0, MXU)], preferred_element_type=f32)
        gel_ref[prev, :, pl.ds(n0, MXU)] = act * (1.0 + lax.erf(act * SQRT_HALF))
        acc_ref[:, pl.ds(n0, MXU)] += jnp.dot(vt_ref[0], pw_ref[prev, :, pl.ds(n0, MXU)],
                                              preferred_element_type=f32)
        for q in range(tiles_per_group):
            for ag in range(PE_AG):
                weight_tile(p * tiles_per_group + q, ag)
        return carry

    @pl.when(j < PE_NCH)
    def _():
        prep_rows()
        lax.fori_loop(0, n_half, stage, 0)

    @pl.when(j == PE_NCH)
    def _():
        acc_ref[...] += jnp.dot(vt_ref[0], pw_ref[prev], preferred_element_type=f32)
        out = acc_ref[...].T
        for s in range(PE_TB // SEG):
            r = slice(s * SEG, (s + 1) * SEG)
            x2 = x1_ref[r, :] + ga_ref[s:s + 1, :] * out[r, :]
            o_ref[r, :] = _rms(x2, fw_ref[...]) if final else x2


def _peer(u_c, vt_c, h2t, thr, f1t, s2t, f2t, x1, mod_seg, final_w):
    final = final_w is not None
    nseg_b = PE_TB // SEG
    rt = pl.BlockSpec((PEER_H, PEER_NK, PE_TB), lambda i, j: (0, 0, i))
    in_specs = [pl.BlockSpec((1, PE_CH, D), lambda i, j: (0, 0, 0)),
                pl.BlockSpec((1, PE_CH, D), lambda i, j: (jnp.minimum(j + 1, PE_NCH - 1), 0, 0)),
                pl.BlockSpec((1, D, PE_CH), lambda i, j: (jnp.maximum(j - 1, 0), 0, 0)),
                pl.BlockSpec((D, PE_TB), lambda i, j: (0, i)),
                rt, rt, rt, rt,
                pl.BlockSpec((PE_TB, D), lambda i, j: (i, 0)),
                pl.BlockSpec((nseg_b, D), lambda i, j: (i, 5))]
    args = [u_c, u_c, vt_c, h2t, thr, f1t, s2t, f2t, x1, mod_seg]
    if final:
        in_specs.append(pl.BlockSpec((1, D), lambda i, j: (0, 0)))
        args.append(final_w.reshape(1, D))
    return pl.pallas_call(
        functools.partial(_peer_kernel, final=final),
        grid=(N_TOK // PE_TB, PE_NCH + 1),
        in_specs=in_specs,
        out_specs=pl.BlockSpec((PE_TB, D), lambda i, j: (i, 0)),
        out_shape=jax.ShapeDtypeStruct((N_TOK, D), f32),
        scratch_shapes=[pltpu.VMEM((D, PE_TB), f32), pltpu.VMEM((2, PE_CH, PE_TB), bf16),
                        pltpu.VMEM((2, PE_CH, PE_TB), f32),
                        pltpu.VMEM((PE_AG * PE_NA * PEER_H, 8, PE_TB), f32),
                        pltpu.VMEM((PE_AG * PE_NA * PEER_H, 8, PE_TB), f32)],
        compiler_params=_cparams(("parallel", "arbitrary")),
        name="peer_experts_final" if final else "peer_experts",
    )(*args)


def _reorder_w_in(w):
    o = np.cumsum([0, 512, 512, 1024, 1024, GLA_RANK, SSD_INNER, SSD_CONV_DIM, SSD_H, 2 * D])
    q_k_v_go = w[:, o[0]:o[4]]
    gk_low = w[:, o[4]:o[5]]
    z = w[:, o[5]:o[6]]
    xbc = w[:, o[6]:o[7]]
    dt = w[:, o[7]:o[8]]
    gates = w[:, o[8]:o[9]]
    pad = lambda m: jnp.pad(m, ((0, 0), (0, LANE - m.shape[1])))
    low = jnp.concatenate([gk_low, jnp.zeros((D, LOW_DT0 - GLA_RANK), w.dtype), dt], axis=1)
    return jnp.concatenate([q_k_v_go, xbc, z, gates, pad(low)], axis=1).astype(bf16)


def kernel(x_prompt, x_sample, state_gla, state_ssd, state_conv, c_prompt, c_sample, w_ada, b_ada, norm1_w, w_in, gla_gk_w2, gla_gk_b, gla_norm_w, gla_proj, ssd_conv_w, ssd_conv_b, ssd_dt_bias, ssd_A_log, ssd_D, ssd_norm_w, ssd_proj, w_out, norm2_w, peer_wq, peer_keys1, peer_keys2, peer_u, peer_v, final_norm_w):
    x = jnp.concatenate([x_prompt.reshape(N_P, D), x_sample.reshape(N_S, D)], axis=0)
    c_all = jnp.concatenate([c_prompt, c_sample], axis=0)
    mod = _ada(c_all, w_ada, b_ada)
    seg2seq = np.concatenate([np.repeat(np.arange(B_P), SEG_PER_PSEQ), B_P + np.arange(B_S)])

    gla_st, ssd_st, conv_st = [], [], []
    for l in range(DEPTH):
        mod_seg = mod[l][seg2seq]
        h = _normmod(x, norm1_w[l], mod_seg, 1, 0)
        p = _mm(h, _reorder_w_in(w_in[l]), 512, P_COLS // 3, "in_proj")

        w2pad = jnp.pad(gla_gk_w2[l], ((0, LANE - GLA_RANK), (0, 0)))
        low_lanes = lambda m: jnp.pad(m.reshape(1, SSD_H), ((0, 0), (LOW_DT0, LANE - LOW_DT0 - SSD_H)))
        dvec = jnp.repeat(ssd_D[l], SSD_P).reshape(1, SSD_INNER)
        (oa, gla_p, gla_s), (yb, ssd_p, ssd_s, conv_p, conv_s) = _mixers(
            _gla_call(p, w2pad, gla_gk_b[l], gla_norm_w[l], state_gla[l]),
            _ssd_call(p, state_conv[l], ssd_conv_w[l], ssd_conv_b[l], low_lanes(ssd_dt_bias[l]),
                      low_lanes(ssd_A_log[l]), dvec, ssd_norm_w[l], state_ssd[l]))

        x1, h2, h2t = _mix(oa, yb, p, x, mod_seg, norm2_w[l], gla_proj[l].astype(bf16),
                           ssd_proj[l].astype(bf16), w_out[l].astype(bf16))
        q = _mm(h2, peer_wq[l].astype(bf16), 512, PEER_H * PEER_DQ, "peer_query")
        thr, s2t, f1t, f2t = _route(q, peer_keys1[l], peer_keys2[l])
        u_c = peer_u[l].astype(bf16).reshape(PE_NCH, PE_CH, D)
        vt_c = peer_v[l].astype(bf16).reshape(PE_NCH, PE_CH, D).transpose(0, 2, 1)
        x = _peer(u_c, vt_c, h2t, thr, f1t, s2t, f2t, x1, mod_seg,
                  final_norm_w if l == DEPTH - 1 else None)

        gla_st.append((gla_p, gla_s))
        ssd_st.append((ssd_p, ssd_s))
        conv_st.append((conv_p, conv_s))

    y_prompt = x[:N_P].reshape(B_P, T_P, D)
    y_sample = x[N_P:].reshape(B_S, T_S, D)
    stack = lambda pairs, k: jnp.stack([pr[k] for pr in pairs])
    return (y_prompt, y_sample, stack(gla_st, 0), stack(ssd_st, 0), stack(conv_st, 0),
            stack(gla_st, 1), stack(ssd_st, 1), stack(conv_st, 1))
```

```python
import functools

import jax
import jax.numpy as jnp
import numpy as np
from jax import lax
from jax.experimental import pallas as pl
from jax.experimental.pallas import tpu as pltpu

f32 = jnp.float32
bf16 = jnp.bfloat16

D = 1024
DEPTH = 2
B_P, T_P = 8, 2048
B_S, T_S = 32, 64
N_P = B_P * T_P
N_S = B_S * T_S
N_TOK = N_P + N_S
SEG = 64
NSEG = N_TOK // SEG
NSEG_P = N_P // SEG
SEG_PER_PSEQ = T_P // SEG
EPS = 1e-6

GLA_H, GLA_DK, GLA_DV = 4, 128, 256
GLA_RANK = 16
GLA_GATE_NORM = 16.0
SSD_INNER = 2048
SSD_P = 64
SSD_H = 32
SSD_G = 4
SSD_N = 128
SSD_HG = SSD_H // SSD_G
SSD_GW = SSD_INNER // SSD_G
CONV_W = 4
SSD_CONV_DIM = SSD_INNER + 2 * SSD_G * SSD_N

PEER_H = 8
PEER_NK = 128
PEER_DQ = 256
PEER_TOPK = 16

C_Q, C_K, C_V, C_GO = 0, 512, 1024, 2048
C_XBC, C_Z, C_GATE = 3072, 6144, 8192
C_LOW = 10240
LOW_DT0 = 32
P_COLS = C_LOW + 128

LANE = 128
VMEM_LIMIT = 56 * 1024 * 1024


def _cparams(sem):
    return pltpu.CompilerParams(dimension_semantics=sem, vmem_limit_bytes=VMEM_LIMIT)


def _dot(a, b):
    return jnp.dot(a.astype(bf16), b.astype(bf16), preferred_element_type=f32)


def _dot_nt(a, b):
    return lax.dot_general(a.astype(bf16), b.astype(bf16), (((1,), (1,)), ((), ())), preferred_element_type=f32)


def _dot_tn(a, b):
    return lax.dot_general(a.astype(bf16), b.astype(bf16), (((0,), (0,)), ((), ())), preferred_element_type=f32)


def _split3(x):
    hi = x.astype(bf16)
    r = x - hi.astype(f32)
    mid = r.astype(bf16)
    lo = (r - mid.astype(f32)).astype(bf16)
    return hi, mid, lo


def _dot3(a, b):
    ah, am, _ = _split3(a)
    bh, bm, _ = _split3(b)
    d = functools.partial(jnp.dot, preferred_element_type=f32)
    return d(ah, bh) + (d(ah, bm) + d(am, bh))


def _dot3_nt(a, b):
    ah, am, _ = _split3(a)
    bh, bm, _ = _split3(b)
    d = functools.partial(lax.dot_general, dimension_numbers=(((1,), (1,)), ((), ())), preferred_element_type=f32)
    return d(ah, bh) + (d(ah, bm) + d(am, bh))


def _cumsum_rows(x):
    n = x.shape[0]
    tri = (lax.broadcasted_iota(jnp.int32, (n, n), 0) >= lax.broadcasted_iota(jnp.int32, (n, n), 1)).astype(bf16)
    hi, mid, lo = _split3(x)
    d = functools.partial(jnp.dot, preferred_element_type=f32)
    return d(tri, hi) + (d(tri, mid) + d(tri, lo))


def _silu(x):
    return x * jax.nn.sigmoid(x)


def _softplus(x):
    return jnp.maximum(x, 0.0) + jnp.log1p(jnp.exp(-jnp.abs(x)))


def _rms(x, w):
    return x * lax.rsqrt(jnp.mean(x * x, axis=-1, keepdims=True) + EPS) * w


def _ada_kernel(c_ref, w_ref, b_ref, o_ref):
    o_ref[0] = _dot3(_silu(c_ref[...]), w_ref[0]) + b_ref[0]


def _ada(c_all, w_ada, b_ada):
    nb = c_all.shape[0]
    tn = 1536
    return pl.pallas_call(
        _ada_kernel,
        grid=(DEPTH, 6 * D // tn),
        in_specs=[pl.BlockSpec((nb, D), lambda l, j: (0, 0)),
                  pl.BlockSpec((1, D, tn), lambda l, j: (l, 0, j)),
                  pl.BlockSpec((1, 1, tn), lambda l, j: (l, 0, j))],
        out_specs=pl.BlockSpec((1, nb, tn), lambda l, j: (l, 0, j)),
        out_shape=jax.ShapeDtypeStruct((DEPTH, nb, 6 * D), f32),
        compiler_params=_cparams(("parallel", "parallel")),
        name="ada_mod",
    )(c_all, w_ada, b_ada.reshape(DEPTH, 1, 6 * D))


NM_TB = 512


def _normmod_kernel(x_ref, w_ref, sc_ref, sh_ref, o_ref):
    for s in range(NM_TB // SEG):
        r = slice(s * SEG, (s + 1) * SEG)
        y = _rms(x_ref[r, :], w_ref[...])
        o_ref[r, :] = (y * (1.0 + sc_ref[s:s + 1, :]) + sh_ref[s:s + 1, :]).astype(bf16)


def _normmod(x, w, mod_seg, sc_col, sh_col):
    nseg_b = NM_TB // SEG
    return pl.pallas_call(
        _normmod_kernel,
        grid=(N_TOK // NM_TB,),
        in_specs=[pl.BlockSpec((NM_TB, D), lambda i: (i, 0)),
                  pl.BlockSpec((1, D), lambda i: (0, 0)),
                  pl.BlockSpec((nseg_b, D), lambda i: (i, sc_col)),
                  pl.BlockSpec((nseg_b, D), lambda i: (i, sh_col))],
        out_specs=pl.BlockSpec((NM_TB, D), lambda i: (i, 0)),
        out_shape=jax.ShapeDtypeStruct((N_TOK, D), bf16),
        compiler_params=_cparams(("parallel",)),
        name="norm_mod",
    )(x, w.reshape(1, D), mod_seg, mod_seg)


def _mm_kernel(a_ref, b_ref, o_ref):
    o_ref[...] = jnp.dot(a_ref[...], b_ref[...], preferred_element_type=f32)


def _mm(a, b, tm, tn, name):
    m, k = a.shape
    n = b.shape[1]
    return pl.pallas_call(
        _mm_kernel,
        grid=(n // tn, m // tm),
        in_specs=[pl.BlockSpec((tm, k), lambda j, i: (i, 0)),
                  pl.BlockSpec((k, tn), lambda j, i: (0, j))],
        out_specs=pl.BlockSpec((tm, tn), lambda j, i: (i, j)),
        out_shape=jax.ShapeDtypeStruct((m, n), f32),
        compiler_params=_cparams(("parallel", "parallel")),
        name=name,
    )(a, b)


def _is_prompt(c):
    return c < NSEG_P


def _sample_seq(c):
    return jnp.maximum(c - NSEG_P, 0)


def _prompt_seq(c):
    return jnp.minimum(c // SEG_PER_PSEQ, B_P - 1)


def _gla_kernel(q_ref, k_ref, v_ref, g_ref, low_ref, w2_ref, b2_ref, nw_ref, s0_ref,
                o_ref, stp_ref, sts_ref, st_ref):
    c = pl.program_id(0)
    is_p = _is_prompt(c)
    pos = c % SEG_PER_PSEQ

    @pl.when(jnp.logical_and(is_p, pos == 0))
    def _():
        st_ref[...] = jnp.zeros_like(st_ref)

    @pl.when(jnp.logical_not(is_p))
    def _():
        for h in range(GLA_H):
            st_ref[h] = s0_ref[0, h].T

    yield
    pre = _dot3(low_ref[...], w2_ref[...]) + b2_ref[...]
    gk = -_softplus(-pre) / GLA_GATE_NORM
    b = _cumsum_rows(gk)
    b_last = b[SEG - 1:SEG, :]
    e_b = jnp.exp(b)
    e_nb = jnp.exp(-b)
    e_end = jnp.exp(b_last - b)
    e_last = jnp.exp(b_last)
    causal = lax.broadcasted_iota(jnp.int32, (SEG, SEG), 0) >= lax.broadcasted_iota(jnp.int32, (SEG, SEG), 1)
    for h in range(GLA_H):
        ks = slice(h * GLA_DK, (h + 1) * GLA_DK)
        vs = slice(h * GLA_DV, (h + 1) * GLA_DV)
        k = k_ref[:, ks]
        q_t = (q_ref[:, ks] * (GLA_DK ** -0.5)) * e_b[:, ks]
        att = jnp.where(causal, _dot_nt(q_t, k * e_nb[:, ks]), 0.0)
        v = v_ref[:, vs]
        st = st_ref[h]
        o = _dot(att, v) + _dot_nt(q_t, st)
        st_ref[h] = st * e_last[:, ks] + _dot_tn(v, k * e_end[:, ks])
        o_ref[:, vs] = (_rms(o, nw_ref[...]) * _silu(g_ref[:, vs])).astype(bf16)

    yield
    @pl.when(jnp.logical_and(is_p, pos == SEG_PER_PSEQ - 1))
    def _():
        for h in range(GLA_H):
            stp_ref[0, h] = st_ref[h].T

    @pl.when(jnp.logical_not(is_p))
    def _():
        for h in range(GLA_H):
            sts_ref[0, h] = st_ref[h].T


def _gla_call(p, w2pad, b2, norm_w, s0):
    st_block = (1, GLA_H, GLA_DK, GLA_DV)
    kd, vd = GLA_H * GLA_DK, GLA_H * GLA_DV
    return dict(
        in_specs=[pl.BlockSpec((SEG, kd), lambda c: (c, C_Q // kd)),
                  pl.BlockSpec((SEG, kd), lambda c: (c, C_K // kd)),
                  pl.BlockSpec((SEG, vd), lambda c: (c, C_V // vd)),
                  pl.BlockSpec((SEG, vd), lambda c: (c, C_GO // vd)),
                  pl.BlockSpec((SEG, LANE), lambda c: (c, C_LOW // LANE)),
                  pl.BlockSpec((LANE, kd), lambda c: (0, 0)),
                  pl.BlockSpec((1, kd), lambda c: (0, 0)),
                  pl.BlockSpec((1, GLA_DV), lambda c: (0, 0)),
                  pl.BlockSpec(st_block, lambda c: (_sample_seq(c), 0, 0, 0))],
        out_specs=[pl.BlockSpec((SEG, vd), lambda c: (c, 0)),
                   pl.BlockSpec(st_block, lambda c: (_prompt_seq(c), 0, 0, 0)),
                   pl.BlockSpec(st_block, lambda c: (_sample_seq(c), 0, 0, 0))],
        out_shape=[jax.ShapeDtypeStruct((N_TOK, vd), bf16),
                   jax.ShapeDtypeStruct((B_P, GLA_H, GLA_DK, GLA_DV), f32),
                   jax.ShapeDtypeStruct((B_S, GLA_H, GLA_DK, GLA_DV), f32)],
        scratch_shapes=[pltpu.VMEM((GLA_H, GLA_DV, GLA_DK), f32)],
        args=(p, p, p, p, p, w2pad, b2.reshape(1, -1), norm_w.reshape(1, -1), s0))


EXT = SEG + 8


def _ssd_kernel(xbc_ref, z_ref, dt_ref, cp_ref, cw_ref, cb_ref, dtb_ref, alog_ref, dvec_ref, nw_ref, s0_ref,
                o_ref, stp_ref, sts_ref, cvp_ref, cvs_ref, ext, st_ref):
    c = pl.program_id(0)
    is_p = _is_prompt(c)
    pos = c % SEG_PER_PSEQ

    @pl.when(jnp.logical_and(is_p, pos == 0))
    def _():
        ext[0:8, :] = jnp.zeros((8, SSD_CONV_DIM), f32)
        st_ref[...] = jnp.zeros_like(st_ref)

    @pl.when(jnp.logical_not(is_p))
    def _():
        ext[0:8, :] = jnp.zeros((8, SSD_CONV_DIM), f32)
        ext[8 - (CONV_W - 1):8, :] = cp_ref[0]
        st_ref[...] = s0_ref[0].reshape(st_ref.shape)

    yield
    ext[8:EXT, :] = xbc_ref[...]
    acc = cb_ref[...] + ext[pl.ds(8 - (CONV_W - 1), SEG), :] * cw_ref[0:1, :]
    for i in range(1, CONV_W):
        acc = acc + ext[pl.ds(8 - (CONV_W - 1) + i, SEG), :] * cw_ref[i:i + 1, :]
    ext[0:8, :] = ext[SEG:EXT, :]
    conv = _silu(acc)

    dt = _softplus(dt_ref[...] + dtb_ref[...])
    a = _cumsum_rows(dt * (-jnp.exp(alog_ref[...])))
    a_t = a.T
    dt_t = dt.T
    a_last = a[SEG - 1:SEG, :]
    e_a = jnp.exp(a)
    w_col = dt * jnp.exp(a_last - a)
    e_last = jnp.exp(a_last)
    causal = lax.broadcasted_iota(jnp.int32, (SEG, SEG), 0) >= lax.broadcasted_iota(jnp.int32, (SEG, SEG), 1)
    lane_lo = lax.broadcasted_iota(jnp.int32, (SEG, 2 * SSD_P), 1) < SSD_P
    sub_lo = lax.broadcasted_iota(jnp.int32, (2 * SSD_P, SSD_N), 0) < SSD_P
    nbc = SSD_G * SSD_N

    def pair_cols(m, ja, jb):
        return jnp.where(lane_lo, jnp.broadcast_to(m[:, ja:ja + 1], (SEG, 2 * SSD_P)),
                         jnp.broadcast_to(m[:, jb:jb + 1], (SEG, 2 * SSD_P)))

    for g in range(SSD_G):
        j0 = LOW_DT0 + g * SSD_HG
        bc = conv[:, SSD_INNER + g * SSD_N:SSD_INNER + (g + 1) * SSD_N]
        cc = conv[:, SSD_INNER + nbc + g * SSD_N:SSD_INNER + nbc + (g + 1) * SSD_N]
        cbm = _dot_nt(cc, bc)

        def w_intra(j):
            seg = a[:, j:j + 1] - a_t[j:j + 1, :]
            return cbm * jnp.exp(jnp.where(causal, seg, -jnp.inf)) * dt_t[j:j + 1, :]

        ys = []
        for p in range(SSD_HG // 2):
            ja, jb = j0 + 2 * p, j0 + 2 * p + 1
            pp = g * (SSD_HG // 2) + p
            xp = conv[:, pp * 2 * SSD_P:(pp + 1) * 2 * SSD_P]
            y_intra = jnp.where(lane_lo, _dot(w_intra(ja), xp), _dot(w_intra(jb), xp))
            sp = st_ref[pp]
            y_inter = _dot_nt(cc, sp) * pair_cols(e_a, ja, jb)
            ds = _dot_tn(xp * pair_cols(w_col, ja, jb), bc)
            rs = jnp.where(sub_lo, jnp.broadcast_to(e_last[:, ja:ja + 1], (2 * SSD_P, SSD_N)),
                           jnp.broadcast_to(e_last[:, jb:jb + 1], (2 * SSD_P, SSD_N)))
            st_ref[pp] = sp * rs + ds
            ys.append(y_intra + y_inter + dvec_ref[:, pp * 2 * SSD_P:(pp + 1) * 2 * SSD_P] * xp)
        gw = slice(g * SSD_GW, (g + 1) * SSD_GW)
        y = jnp.concatenate(ys, axis=1) * _silu(z_ref[:, gw])
        o_ref[:, gw] = _rms(y, nw_ref[:, gw]).astype(bf16)

    yield
    @pl.when(jnp.logical_and(is_p, pos == SEG_PER_PSEQ - 1))
    def _():
        stp_ref[0] = st_ref[...].reshape(stp_ref.shape[1:])
        cvp_ref[0] = xbc_ref[SEG - (CONV_W - 1):SEG, :]

    @pl.when(jnp.logical_not(is_p))
    def _():
        sts_ref[0] = st_ref[...].reshape(sts_ref.shape[1:])
        cvs_ref[0] = xbc_ref[SEG - (CONV_W - 1):SEG, :]


def _ssd_call(p, conv_prev, conv_w, conv_b, dtb, alog, dvec, norm_w, s0):
    st_block = (1, SSD_H, SSD_P, SSD_N)
    cv_block = (1, CONV_W - 1, SSD_CONV_DIM)
    full = lambda c: (0, 0)
    return dict(
        in_specs=[pl.BlockSpec((SEG, SSD_CONV_DIM), lambda c: (c, C_XBC // SSD_CONV_DIM)),
                  pl.BlockSpec((SEG, SSD_INNER), lambda c: (c, C_Z // SSD_INNER)),
                  pl.BlockSpec((SEG, LANE), lambda c: (c, C_LOW // LANE)),
                  pl.BlockSpec(cv_block, lambda c: (_sample_seq(c), 0, 0)),
                  pl.BlockSpec((CONV_W, SSD_CONV_DIM), full),
                  pl.BlockSpec((1, SSD_CONV_DIM), full),
                  pl.BlockSpec((1, LANE), full),
                  pl.BlockSpec((1, LANE), full),
                  pl.BlockSpec((1, SSD_INNER), full),
                  pl.BlockSpec((1, SSD_INNER), full),
                  pl.BlockSpec(st_block, lambda c: (_sample_seq(c), 0, 0, 0))],
        out_specs=[pl.BlockSpec((SEG, SSD_INNER), lambda c: (c, 0)),
                   pl.BlockSpec(st_block, lambda c: (_prompt_seq(c), 0, 0, 0)),
                   pl.BlockSpec(st_block, lambda c: (_sample_seq(c), 0, 0, 0)),
                   pl.BlockSpec(cv_block, lambda c: (_prompt_seq(c), 0, 0)),
                   pl.BlockSpec(cv_block, lambda c: (_sample_seq(c), 0, 0))],
        out_shape=[jax.ShapeDtypeStruct((N_TOK, SSD_INNER), bf16),
                   jax.ShapeDtypeStruct((B_P, SSD_H, SSD_P, SSD_N), f32),
                   jax.ShapeDtypeStruct((B_S, SSD_H, SSD_P, SSD_N), f32),
                   jax.ShapeDtypeStruct((B_P, CONV_W - 1, SSD_CONV_DIM), f32),
                   jax.ShapeDtypeStruct((B_S, CONV_W - 1, SSD_CONV_DIM), f32)],
        scratch_shapes=[pltpu.VMEM((EXT, SSD_CONV_DIM), f32),
                        pltpu.VMEM((SSD_H // 2, 2 * SSD_P, SSD_N), f32)],
        args=(p, p, p, conv_prev, conv_w, conv_b.reshape(1, -1), dtb, alog, dvec, norm_w.reshape(1, -1), s0))


def _mixers(gla, ssd):
    n_in = (len(gla["in_specs"]), len(ssd["in_specs"]))
    n_out = (len(gla["out_specs"]), len(ssd["out_specs"]))
    n_scr = (len(gla["scratch_shapes"]), len(ssd["scratch_shapes"]))

    def body(*refs):
        ins, outs, scr = refs[:sum(n_in)], refs[sum(n_in):sum(n_in) + sum(n_out)], refs[sum(n_in) + sum(n_out):]
        parts = [_gla_kernel(*ins[:n_in[0]], *outs[:n_out[0]], *scr[:n_scr[0]]),
                 _ssd_kernel(*ins[n_in[0]:], *outs[n_out[0]:], *scr[n_scr[0]:])]
        for _ in range(3):
            for part in parts:
                next(part, None)

    outs = pl.pallas_call(
        body,
        grid=(NSEG,),
        in_specs=gla["in_specs"] + ssd["in_specs"],
        out_specs=gla["out_specs"] + ssd["out_specs"],
        out_shape=gla["out_shape"] + ssd["out_shape"],
        scratch_shapes=gla["scratch_shapes"] + ssd["scratch_shapes"],
        compiler_params=_cparams(("arbitrary",)),
        name="token_mixers",
    )(*gla["args"], *ssd["args"])
    return outs[:n_out[0]], outs[n_out[0]:]


MIX_TB = 512


def _mix_kernel(oa_ref, yb_ref, gt_ref, x_ref, ga_ref, sc_ref, sh_ref, nw_ref, wa_ref, wb_ref, wo_ref,
                x1_ref, h2_ref, h2t_ref):
    br_a = jnp.dot(oa_ref[...], wa_ref[...], preferred_element_type=f32)
    br_b = jnp.dot(yb_ref[...], wb_ref[...], preferred_element_type=f32)
    g_a = jax.nn.sigmoid(gt_ref[:, 0:D])
    g_b = jax.nn.sigmoid(gt_ref[:, D:2 * D])
    y = _dot(g_a * br_a + g_b * br_b, wo_ref[...])
    for s in range(MIX_TB // SEG):
        r = slice(s * SEG, (s + 1) * SEG)
        x1 = x_ref[r, :] + ga_ref[s:s + 1, :] * y[r, :]
        x1_ref[r, :] = x1
        h2_ref[r, :] = (_rms(x1, nw_ref[...]) * (1.0 + sc_ref[s:s + 1, :]) + sh_ref[s:s + 1, :]).astype(bf16)
    for g in range(MIX_TB // MXU):
        h2t_ref[g] = h2_ref[g * MXU:(g + 1) * MXU, :].T


def _mix(oa, yb, p, x, mod_seg, norm2_w, wa, wb, wo):
    nseg_b = MIX_TB // SEG
    full = lambda i: (0, 0)
    return pl.pallas_call(
        _mix_kernel,
        grid=(N_TOK // MIX_TB,),
        in_specs=[pl.BlockSpec((MIX_TB, D), lambda i: (i, 0)),
                  pl.BlockSpec((MIX_TB, SSD_INNER), lambda i: (i, 0)),
                  pl.BlockSpec((MIX_TB, 2 * D), lambda i: (i, C_GATE // (2 * D))),
                  pl.BlockSpec((MIX_TB, D), lambda i: (i, 0)),
                  pl.BlockSpec((nseg_b, D), lambda i: (i, 2)),
                  pl.BlockSpec((nseg_b, D), lambda i: (i, 4)),
                  pl.BlockSpec((nseg_b, D), lambda i: (i, 3)),
                  pl.BlockSpec((1, D), full),
                  pl.BlockSpec((D, D), full),
                  pl.BlockSpec((SSD_INNER, D), full),
                  pl.BlockSpec((D, D), full)],
        out_specs=[pl.BlockSpec((MIX_TB, D), lambda i: (i, 0)),
                   pl.BlockSpec((MIX_TB, D), lambda i: (i, 0)),
                   pl.BlockSpec((MIX_TB // MXU, D, MXU), lambda i: (i, 0, 0))],
        out_shape=[jax.ShapeDtypeStruct((N_TOK, D), f32),
                   jax.ShapeDtypeStruct((N_TOK, D), bf16),
                   jax.ShapeDtypeStruct((N_TOK // MXU, D, MXU), bf16)],
        compiler_params=_cparams(("parallel",)),
        name="mix_out",
    )(oa, yb, p, x, mod_seg, mod_seg, mod_seg, norm2_w.reshape(1, D), wa, wb, wo)


RT_TL = 128
HALF = PEER_DQ // 2


RT_HPS = 8
SUBLANES = 8


def _batcher_sort_net(n):
    def merge(lo, hi, r):
        step = 2 * r
        if step < hi - lo:
            yield from merge(lo, hi, step)
            yield from merge(lo + r, hi, step)
            yield from ((i, i + r) for i in range(lo + r, hi - r, step))
        else:
            yield (lo, lo + r)

    def sort(lo, hi):
        if hi - lo >= 1:
            mid = lo + (hi - lo) // 2
            yield from sort(lo, mid)
            yield from sort(mid + 1, hi)
            yield from merge(lo, hi, 1)

    return tuple(sort(0, n - 1))


def _bitonic_merge_net(n):
    net, d = [], n // 2
    while d >= 1:
        net += [(i, i + d) for i in range(n) if (i // d) % 2 == 0]
        d //= 2
    return tuple(net)


_SORT16 = _batcher_sort_net(PEER_TOPK)
_MERGE16 = _bitonic_merge_net(PEER_TOPK)
N_CAND_VREGS = 10


def _compare_exchange(x, net):
    for i, j in net:
        x[i], x[j] = jnp.maximum(x[i], x[j]), jnp.minimum(x[i], x[j])


def _merge_across_sublanes(x, n_valid):
    for shift in (4, 2, 1):
        y = [pltpu.roll(v, shift, 0) for v in x]
        merged = []
        for k in range(PEER_TOPK):
            a = x[k] if k < n_valid else None
            b = y[PEER_TOPK - 1 - k] if PEER_TOPK - 1 - k < n_valid else None
            merged.append(jnp.maximum(a, b) if (a is not None and b is not None) else (a if b is None else b))
        x = merged
        _compare_exchange(x, _MERGE16)
        n_valid = PEER_TOPK
    return x


def _top16_sorted(s):
    x = [s[SUBLANES * k:SUBLANES * (k + 1), :] for k in range(PEER_NK // SUBLANES)]
    _compare_exchange(x, _SORT16)
    return _merge_across_sublanes(x, PEER_TOPK)


def _route_kernel(q_ref, k1_ref, k2_ref, thr_ref, s2_ref, f1_ref, f2_ref):
    sub = lax.broadcasted_iota(jnp.int32, (SUBLANES, RT_TL), 0)
    ninf = jnp.float32(-jnp.inf)

    def by_sublane(vs):
        out = vs[0]
        for r in range(1, SUBLANES):
            out = jnp.where(sub == r, vs[r], out)
        return out

    for hh in range(RT_HPS):
        s1 = _dot3_nt(k1_ref[hh], q_ref[:, hh * PEER_DQ:hh * PEER_DQ + HALF])
        s2 = _dot3_nt(k2_ref[hh], q_ref[:, hh * PEER_DQ + HALF:(hh + 1) * PEER_DQ])
        s2_ref[hh] = s2
        v1 = _top16_sorted(s1)
        v2 = _top16_sorted(s2)
        v2lo, v2hi, v1hi = by_sublane(v2[:8]), by_sublane(v2[8:]), by_sublane(v1[8:])
        cands = [v1[0] + v2lo, v1[0] + v2hi, v1[1] + v2lo]
        for i in range(2, 8):
            cands.append(jnp.where(sub < PEER_TOPK // (i + 1), v1[i] + v2lo, ninf))
        cands.append(v1hi + v2[0])
        assert len(cands) == N_CAND_VREGS
        _compare_exchange(cands, tuple((i, j) for i, j in _SORT16 if j < N_CAND_VREGS))
        top = _merge_across_sublanes(cands, N_CAND_VREGS)
        zsum = jnp.ones_like(top[0])
        for k in range(1, PEER_TOPK):
            zsum = zsum + jnp.exp(top[k] - top[0])
        tau = top[PEER_TOPK - 1]
        for k in range(PEER_NK // SUBLANES):
            rows = slice(SUBLANES * k, SUBLANES * (k + 1))
            s1k = s1[rows, :]
            thr = jnp.full_like(s1k, jnp.inf)
            for j in range(PEER_TOPK):
                thr = jnp.where(s1k + v2[j] >= tau, v2[j], thr)
            thr_ref[hh, rows, :] = thr
        f1_ref[hh] = jnp.exp(s1 - v1[0][0:1, :]) * (0.5 / zsum[0:1, :])
        f2_ref[hh] = jnp.exp(s2 - v2[0][0:1, :])


def _route(q, keys1, keys2):
    tile = pl.BlockSpec((RT_HPS, PEER_NK, RT_TL), lambda i, h: (h, 0, i))
    big = jax.ShapeDtypeStruct((PEER_H, PEER_NK, N_TOK), f32)
    return pl.pallas_call(
        _route_kernel,
        grid=(N_TOK // RT_TL, PEER_H // RT_HPS),
        in_specs=[pl.BlockSpec((RT_TL, RT_HPS * PEER_DQ), lambda i, h: (i, h)),
                  pl.BlockSpec((RT_HPS, PEER_NK, HALF), lambda i, h: (h, 0, 0)),
                  pl.BlockSpec((RT_HPS, PEER_NK, HALF), lambda i, h: (h, 0, 0))],
        out_specs=[tile, tile, tile, tile],
        out_shape=[big, big, big, big],
        compiler_params=_cparams(("parallel", "parallel")),
        name="peer_route",
    )(q, keys1, keys2)


PE_TB = 512
PE_NA = 4
PE_AG = 2
PE_CH = PE_AG * PE_NA * PEER_NK
PE_NCH = PEER_NK * PEER_NK // PE_CH
PE_BH = 32
SQRT_HALF = 0.7071067811865476
MXU = 256


def _peer_kernel(*refs, final):
    if final:
        (u0_ref, u_ref, vt_ref, xt_ref, thr_ref, f1_ref, s2_ref, f2_ref, x1_ref, ga_ref, fw_ref,
         o_ref, acc_ref, pw_ref, gel_ref, rthr_ref, rf1_ref) = refs
    else:
        (u0_ref, u_ref, vt_ref, xt_ref, thr_ref, f1_ref, s2_ref, f2_ref, x1_ref, ga_ref,
         o_ref, acc_ref, pw_ref, gel_ref, rthr_ref, rf1_ref) = refs
    j = pl.program_id(1)

    n_half = PE_TB // MXU

    def two_gelu(act):
        return act * (1.0 + lax.erf(act * SQRT_HALF))

    @pl.when(j == 0)
    def _():
        acc_ref[...] = jnp.zeros_like(acc_ref)
        pw_ref[1] = jnp.zeros(pw_ref.shape[1:], bf16)
        for g in range(n_half):
            gel_ref[0, g] = two_gelu(jnp.dot(u0_ref[0], xt_ref[g], preferred_element_type=f32))

    cur = j % 2
    prev = 1 - cur

    def prep_rows():
        for ia in range(PE_AG * PE_NA):
            for h in range(PEER_H):
                a = j * (PE_AG * PE_NA) + ia
                rthr_ref[ia * PEER_H + h] = jnp.broadcast_to(thr_ref[h, pl.ds(a, 1), :], (8, PE_TB))
                rf1_ref[ia * PEER_H + h] = jnp.broadcast_to(f1_ref[h, pl.ds(a, 1), :], (8, PE_TB))

    nsub = PE_BH // 8
    n_bh = PEER_NK // PE_BH

    def weight_tile(g, lq, bh, ag):
        lanes = pl.ds(pl.multiple_of(g * MXU + lq * LANE, LANE), LANE)
        lq_lanes = slice(lq * LANE, (lq + 1) * LANE)
        rows = slice(bh * PE_BH, (bh + 1) * PE_BH)
        ws = [jnp.zeros((nsub, 8, LANE), f32) for _ in range(PE_NA)]
        for h in range(PEER_H):
            s2t = s2_ref[h, rows, lanes].reshape(nsub, 8, LANE)
            f2t = f2_ref[h, rows, lanes].reshape(nsub, 8, LANE)
            for i in range(PE_NA):
                row = (ag * PE_NA + i) * PEER_H + h
                sel = s2t >= rthr_ref[row, :, lanes][None]
                ws[i] = ws[i] + jnp.where(sel, f2t * rf1_ref[row, :, lanes][None], 0.0)
        for i in range(PE_NA):
            e0 = (ag * PE_NA + i) * PEER_NK + bh * PE_BH
            pw_ref[cur, g, e0:e0 + PE_BH, lq_lanes] = (
                ws[i].reshape(PE_BH, LANE) * gel_ref[cur, g, e0:e0 + PE_BH, lq_lanes]).astype(bf16)

    def stage(p, carry, with_act):
        if with_act:
            gel_ref[prev, p] = two_gelu(jnp.dot(u_ref[0], xt_ref[p], preferred_element_type=f32))
        acc_ref[p] += jnp.dot(vt_ref[0], pw_ref[prev, p], preferred_element_type=f32)
        for lq in range(MXU // LANE):
            for bh in range(n_bh):
                for ag in range(PE_AG):
                    weight_tile(p, lq, bh, ag)
        return carry

    @pl.when(j < PE_NCH - 1)
    def _():
        prep_rows()
        lax.fori_loop(0, n_half, functools.partial(stage, with_act=True), 0)

    @pl.when(j == PE_NCH - 1)
    def _():
        prep_rows()
        lax.fori_loop(0, n_half, functools.partial(stage, with_act=False), 0)

    @pl.when(j == PE_NCH)
    def _():
        for g in range(n_half):
            out = (acc_ref[g] + jnp.dot(vt_ref[0], pw_ref[prev, g], preferred_element_type=f32)).T
            for s in range(MXU // SEG):
                r = slice(g * MXU + s * SEG, g * MXU + (s + 1) * SEG)
                seg = g * (MXU // SEG) + s
                x2 = x1_ref[r, :] + ga_ref[seg:seg + 1, :] * out[s * SEG:(s + 1) * SEG, :]
                o_ref[r, :] = _rms(x2, fw_ref[...]) if final else x2


def _peer(u_c, vt_c, h2t, thr, f1t, s2t, f2t, x1, mod_seg, final_w):
    final = final_w is not None
    nseg_b = PE_TB // SEG
    rt = pl.BlockSpec((PEER_H, PEER_NK, PE_TB), lambda i, j: (0, 0, i))
    in_specs = [pl.BlockSpec((1, PE_CH, D), lambda i, j: (0, 0, 0)),
                pl.BlockSpec((1, PE_CH, D), lambda i, j: (jnp.minimum(j + 1, PE_NCH - 1), 0, 0)),
                pl.BlockSpec((1, D, PE_CH), lambda i, j: (jnp.maximum(j - 1, 0), 0, 0)),
                pl.BlockSpec((PE_TB // MXU, D, MXU), lambda i, j: (i, 0, 0)),
                rt, rt, rt, rt,
                pl.BlockSpec((PE_TB, D), lambda i, j: (i, 0)),
                pl.BlockSpec((nseg_b, D), lambda i, j: (i, 5))]
    args = [u_c, u_c, vt_c, h2t, thr, f1t, s2t, f2t, x1, mod_seg]
    if final:
        in_specs.append(pl.BlockSpec((1, D), lambda i, j: (0, 0)))
        args.append(final_w.reshape(1, D))
    return pl.pallas_call(
        functools.partial(_peer_kernel, final=final),
        grid=(N_TOK // PE_TB, PE_NCH + 1),
        in_specs=in_specs,
        out_specs=pl.BlockSpec((PE_TB, D), lambda i, j: (i, 0)),
        out_shape=jax.ShapeDtypeStruct((N_TOK, D), f32),
        scratch_shapes=[pltpu.VMEM((PE_TB // MXU, D, MXU), f32), pltpu.VMEM((2, PE_TB // MXU, PE_CH, MXU), bf16),
                        pltpu.VMEM((2, PE_TB // MXU, PE_CH, MXU), f32),
                        pltpu.VMEM((PE_AG * PE_NA * PEER_H, 8, PE_TB), f32),
                        pltpu.VMEM((PE_AG * PE_NA * PEER_H, 8, PE_TB), f32)],
        compiler_params=_cparams(("parallel", "arbitrary")),
        name="peer_experts_final" if final else "peer_experts",
    )(*args)


def _reorder_w_in(w):
    o = np.cumsum([0, 512, 512, 1024, 1024, GLA_RANK, SSD_INNER, SSD_CONV_DIM, SSD_H, 2 * D])
    q_k_v_go = w[:, o[0]:o[4]]
    gk_low = w[:, o[4]:o[5]]
    z = w[:, o[5]:o[6]]
    xbc = w[:, o[6]:o[7]]
    dt = w[:, o[7]:o[8]]
    gates = w[:, o[8]:o[9]]
    pad = lambda m: jnp.pad(m, ((0, 0), (0, LANE - m.shape[1])))
    low = jnp.concatenate([gk_low, jnp.zeros((D, LOW_DT0 - GLA_RANK), w.dtype), dt], axis=1)
    return jnp.concatenate([q_k_v_go, xbc, z, gates, pad(low)], axis=1).astype(bf16)


def kernel(x_prompt, x_sample, state_gla, state_ssd, state_conv, c_prompt, c_sample, w_ada, b_ada, norm1_w, w_in, gla_gk_w2, gla_gk_b, gla_norm_w, gla_proj, ssd_conv_w, ssd_conv_b, ssd_dt_bias, ssd_A_log, ssd_D, ssd_norm_w, ssd_proj, w_out, norm2_w, peer_wq, peer_keys1, peer_keys2, peer_u, peer_v, final_norm_w):
    x = jnp.concatenate([x_prompt.reshape(N_P, D), x_sample.reshape(N_S, D)], axis=0)
    c_all = jnp.concatenate([c_prompt, c_sample], axis=0)
    mod = _ada(c_all, w_ada, b_ada)
    seg2seq = np.concatenate([np.repeat(np.arange(B_P), SEG_PER_PSEQ), B_P + np.arange(B_S)])

    gla_st, ssd_st, conv_st = [], [], []
    for l in range(DEPTH):
        mod_seg = mod[l][seg2seq]
        h = _normmod(x, norm1_w[l], mod_seg, 1, 0)
        p = _mm(h, _reorder_w_in(w_in[l]), 512, P_COLS // 3, "in_proj")

        w2pad = jnp.pad(gla_gk_w2[l], ((0, LANE - GLA_RANK), (0, 0)))
        low_lanes = lambda m: jnp.pad(m.reshape(1, SSD_H), ((0, 0), (LOW_DT0, LANE - LOW_DT0 - SSD_H)))
        dvec = jnp.repeat(ssd_D[l], SSD_P).reshape(1, SSD_INNER)
        (oa, gla_p, gla_s), (yb, ssd_p, ssd_s, conv_p, conv_s) = _mixers(
            _gla_call(p, w2pad, gla_gk_b[l], gla_norm_w[l], state_gla[l]),
            _ssd_call(p, state_conv[l], ssd_conv_w[l], ssd_conv_b[l], low_lanes(ssd_dt_bias[l]),
                      low_lanes(ssd_A_log[l]), dvec, ssd_norm_w[l], state_ssd[l]))

        x1, h2, h2t = _mix(oa, yb, p, x, mod_seg, norm2_w[l], gla_proj[l].astype(bf16),
                           ssd_proj[l].astype(bf16), w_out[l].astype(bf16))
        q = _mm(h2, peer_wq[l].astype(bf16), 512, PEER_H * PEER_DQ, "peer_query")
        thr, s2t, f1t, f2t = _route(q, peer_keys1[l], peer_keys2[l])
        u_c = peer_u[l].astype(bf16).reshape(PE_NCH, PE_CH, D)
        vt_c = peer_v[l].astype(bf16).reshape(PE_NCH, PE_CH, D).transpose(0, 2, 1)
        x = _peer(u_c, vt_c, h2t, thr, f1t, s2t, f2t, x1, mod_seg,
                  final_norm_w if l == DEPTH - 1 else None)

        gla_st.append((gla_p, gla_s))
        ssd_st.append((ssd_p, ssd_s))
        conv_st.append((conv_p, conv_s))

    y_prompt = x[:N_P].reshape(B_P, T_P, D)
    y_sample = x[N_P:].reshape(B_S, T_S, D)
    stack = lambda pairs, k: jnp.stack([pr[k] for pr in pairs])
    return (y_prompt, y_sample, stack(gla_st, 0), stack(ssd_st, 0), stack(conv_st, 0),
            stack(gla_st, 1), stack(ssd_st, 1), stack(conv_st, 1))
```

```python
import functools

import jax
import jax.numpy as jnp
import numpy as np
from jax import lax
from jax.experimental import pallas as pl
from jax.experimental.pallas import tpu as pltpu

f32 = jnp.float32
bf16 = jnp.bfloat16

D = 1024
DEPTH = 2
B_P, T_P = 8, 2048
B_S, T_S = 32, 64
N_P = B_P * T_P
N_S = B_S * T_S
N_TOK = N_P + N_S
SEG = 64
NSEG = N_TOK // SEG
NSEG_P = N_P // SEG
SEG_PER_PSEQ = T_P // SEG
EPS = 1e-6

GLA_H, GLA_DK, GLA_DV = 4, 128, 256
GLA_RANK = 16
GLA_GATE_NORM = 16.0
SSD_INNER = 2048
SSD_P = 64
SSD_H = 32
SSD_G = 4
SSD_N = 128
SSD_HG = SSD_H // SSD_G
SSD_GW = SSD_INNER // SSD_G
CONV_W = 4
SSD_CONV_DIM = SSD_INNER + 2 * SSD_G * SSD_N

PEER_H = 8
PEER_NK = 128
PEER_DQ = 256
PEER_TOPK = 16

C_Q, C_K, C_V, C_GO = 0, 512, 1024, 2048
C_XBC, C_Z, C_GATE = 3072, 6144, 8192
C_LOW = 10240
LOW_DT0 = 32
P_COLS = C_LOW + 128

LANE = 128
SUBLANES = 8
VMEM_LIMIT = 56 * 1024 * 1024


def _cparams(sem):
    return pltpu.CompilerParams(dimension_semantics=sem, vmem_limit_bytes=VMEM_LIMIT)


def _dot(a, b):
    return jnp.dot(a.astype(bf16), b.astype(bf16), preferred_element_type=f32)


def _dot_nt(a, b):
    return lax.dot_general(a.astype(bf16), b.astype(bf16), (((1,), (1,)), ((), ())), preferred_element_type=f32)


def _dot_tn(a, b):
    return lax.dot_general(a.astype(bf16), b.astype(bf16), (((0,), (0,)), ((), ())), preferred_element_type=f32)


def _split3(x):
    hi = x.astype(bf16)
    r = x - hi.astype(f32)
    mid = r.astype(bf16)
    lo = (r - mid.astype(f32)).astype(bf16)
    return hi, mid, lo


def _dot3(a, b):
    ah, am, _ = _split3(a)
    bh, bm, _ = _split3(b)
    d = functools.partial(jnp.dot, preferred_element_type=f32)
    return d(ah, bh) + (d(ah, bm) + d(am, bh))


def _dot3_nt(a, b):
    ah, am, _ = _split3(a)
    bh, bm, _ = _split3(b)
    d = functools.partial(lax.dot_general, dimension_numbers=(((1,), (1,)), ((), ())), preferred_element_type=f32)
    return d(ah, bh) + (d(ah, bm) + d(am, bh))


def _cumsum_rows(x):
    n = x.shape[0]
    tri = (lax.broadcasted_iota(jnp.int32, (n, n), 0) >= lax.broadcasted_iota(jnp.int32, (n, n), 1)).astype(bf16)
    hi, mid, lo = _split3(x)
    d = functools.partial(jnp.dot, preferred_element_type=f32)
    return d(tri, hi) + (d(tri, mid) + d(tri, lo))


def _silu(x):
    return x * jax.nn.sigmoid(x)


def _softplus(x):
    return jnp.maximum(x, 0.0) + jnp.log1p(jnp.exp(-jnp.abs(x)))


def _rms(x, w):
    return x * lax.rsqrt(jnp.mean(x * x, axis=-1, keepdims=True) + EPS) * w


def _ada_kernel(c_ref, w_ref, b_ref, o_ref):
    o_ref[0] = _dot3(_silu(c_ref[...]), w_ref[0]) + b_ref[0]


def _ada(c_all, w_ada, b_ada):
    nb = c_all.shape[0]
    tn = 1536
    return pl.pallas_call(
        _ada_kernel,
        grid=(DEPTH, 6 * D // tn),
        in_specs=[pl.BlockSpec((nb, D), lambda l, j: (0, 0)),
                  pl.BlockSpec((1, D, tn), lambda l, j: (l, 0, j)),
                  pl.BlockSpec((1, 1, tn), lambda l, j: (l, 0, j))],
        out_specs=pl.BlockSpec((1, nb, tn), lambda l, j: (l, 0, j)),
        out_shape=jax.ShapeDtypeStruct((DEPTH, nb, 6 * D), f32),
        compiler_params=_cparams(("parallel", "parallel")),
        name="ada_mod",
    )(c_all, w_ada, b_ada.reshape(DEPTH, 1, 6 * D))


NM_TB = 512


def _normmod_kernel(x_ref, w_ref, sc_ref, sh_ref, o_ref):
    for s in range(NM_TB // SEG):
        r = slice(s * SEG, (s + 1) * SEG)
        y = _rms(x_ref[r, :], w_ref[...])
        o_ref[r, :] = (y * (1.0 + sc_ref[s:s + 1, :]) + sh_ref[s:s + 1, :]).astype(bf16)


def _normmod(x, w, mod_seg, sc_col, sh_col):
    nseg_b = NM_TB // SEG
    return pl.pallas_call(
        _normmod_kernel,
        grid=(N_TOK // NM_TB,),
        in_specs=[pl.BlockSpec((NM_TB, D), lambda i: (i, 0)),
                  pl.BlockSpec((1, D), lambda i: (0, 0)),
                  pl.BlockSpec((nseg_b, D), lambda i: (i, sc_col)),
                  pl.BlockSpec((nseg_b, D), lambda i: (i, sh_col))],
        out_specs=pl.BlockSpec((NM_TB, D), lambda i: (i, 0)),
        out_shape=jax.ShapeDtypeStruct((N_TOK, D), bf16),
        compiler_params=_cparams(("parallel",)),
        name="norm_mod",
    )(x, w.reshape(1, D), mod_seg, mod_seg)


def _mm_kernel(a_ref, b_ref, o_ref):
    o_ref[...] = jnp.dot(a_ref[...], b_ref[...], preferred_element_type=f32)


def _mm(a, b, tm, tn, name):
    m, k = a.shape
    n = b.shape[1]
    return pl.pallas_call(
        _mm_kernel,
        grid=(n // tn, m // tm),
        in_specs=[pl.BlockSpec((tm, k), lambda j, i: (i, 0)),
                  pl.BlockSpec((k, tn), lambda j, i: (0, j))],
        out_specs=pl.BlockSpec((tm, tn), lambda j, i: (i, j)),
        out_shape=jax.ShapeDtypeStruct((m, n), f32),
        compiler_params=_cparams(("parallel", "parallel")),
        name=name,
    )(a, b)


def _is_prompt(c):
    return c < NSEG_P


def _sample_seq(c):
    return jnp.maximum(c - NSEG_P, 0)


def _prompt_seq(c):
    return jnp.minimum(c // SEG_PER_PSEQ, B_P - 1)


def _gla_kernel(q_ref, k_ref, v_ref, g_ref, low_ref, w2_ref, b2_ref, nw_ref, s0_ref,
                o_ref, stp_ref, sts_ref, st_ref):
    c = pl.program_id(0)
    is_p = _is_prompt(c)
    pos = c % SEG_PER_PSEQ

    @pl.when(jnp.logical_and(is_p, pos == 0))
    def _():
        st_ref[...] = jnp.zeros_like(st_ref)

    @pl.when(jnp.logical_not(is_p))
    def _():
        for h in range(GLA_H):
            st_ref[h] = s0_ref[0, h].T

    yield
    pre = _dot3(low_ref[...], w2_ref[...]) + b2_ref[...]
    gk = -_softplus(-pre) / GLA_GATE_NORM
    b = _cumsum_rows(gk)
    b_last = b[SEG - 1:SEG, :]
    e_b = jnp.exp(b)
    e_nb = jnp.exp(-b)
    e_end = jnp.exp(b_last - b)
    e_last = jnp.exp(b_last)
    causal = lax.broadcasted_iota(jnp.int32, (SEG, SEG), 0) >= lax.broadcasted_iota(jnp.int32, (SEG, SEG), 1)
    for h in range(GLA_H):
        ks = slice(h * GLA_DK, (h + 1) * GLA_DK)
        vs = slice(h * GLA_DV, (h + 1) * GLA_DV)
        k = k_ref[:, ks]
        q_t = (q_ref[:, ks] * (GLA_DK ** -0.5)) * e_b[:, ks]
        att = jnp.where(causal, _dot_nt(q_t, k * e_nb[:, ks]), 0.0)
        v = v_ref[:, vs]
        st = st_ref[h]
        o = _dot(att, v) + _dot_nt(q_t, st)
        st_ref[h] = st * e_last[:, ks] + _dot_tn(v, k * e_end[:, ks])
        o_ref[:, vs] = (_rms(o, nw_ref[...]) * _silu(g_ref[:, vs])).astype(bf16)

    yield
    @pl.when(jnp.logical_and(is_p, pos == SEG_PER_PSEQ - 1))
    def _():
        for h in range(GLA_H):
            stp_ref[0, h] = st_ref[h].T

    @pl.when(jnp.logical_not(is_p))
    def _():
        for h in range(GLA_H):
            sts_ref[0, h] = st_ref[h].T


def _gla_call(p, w2pad, b2, norm_w, s0):
    st_block = (1, GLA_H, GLA_DK, GLA_DV)
    kd, vd = GLA_H * GLA_DK, GLA_H * GLA_DV
    return dict(
        in_specs=[pl.BlockSpec((SEG, kd), lambda c: (c, C_Q // kd)),
                  pl.BlockSpec((SEG, kd), lambda c: (c, C_K // kd)),
                  pl.BlockSpec((SEG, vd), lambda c: (c, C_V // vd)),
                  pl.BlockSpec((SEG, vd), lambda c: (c, C_GO // vd)),
                  pl.BlockSpec((SEG, LANE), lambda c: (c, C_LOW // LANE)),
                  pl.BlockSpec((LANE, kd), lambda c: (0, 0)),
                  pl.BlockSpec((1, kd), lambda c: (0, 0)),
                  pl.BlockSpec((1, GLA_DV), lambda c: (0, 0)),
                  pl.BlockSpec(st_block, lambda c: (_sample_seq(c), 0, 0, 0))],
        out_specs=[pl.BlockSpec((SEG, vd), lambda c: (c, 0)),
                   pl.BlockSpec(st_block, lambda c: (_prompt_seq(c), 0, 0, 0)),
                   pl.BlockSpec(st_block, lambda c: (_sample_seq(c), 0, 0, 0))],
        out_shape=[jax.ShapeDtypeStruct((N_TOK, vd), bf16),
                   jax.ShapeDtypeStruct((B_P, GLA_H, GLA_DK, GLA_DV), f32),
                   jax.ShapeDtypeStruct((B_S, GLA_H, GLA_DK, GLA_DV), f32)],
        scratch_shapes=[pltpu.VMEM((GLA_H, GLA_DV, GLA_DK), f32)],
        args=(p, p, p, p, p, w2pad, b2.reshape(1, -1), norm_w.reshape(1, -1), s0))


CARRY = SUBLANES
EXT = SEG + CARRY


def _ssd_kernel(xbc_ref, z_ref, dt_ref, cp_ref, cw_ref, cb_ref, dtb_ref, alog_ref, dvec_ref, nw_ref, s0_ref,
                o_ref, stp_ref, sts_ref, cvp_ref, cvs_ref, ext, st_ref):
    c = pl.program_id(0)
    is_p = _is_prompt(c)
    pos = c % SEG_PER_PSEQ

    @pl.when(jnp.logical_and(is_p, pos == 0))
    def _():
        ext[0:CARRY, :] = jnp.zeros((CARRY, SSD_CONV_DIM), f32)
        st_ref[...] = jnp.zeros_like(st_ref)

    @pl.when(jnp.logical_not(is_p))
    def _():
        ext[0:CARRY, :] = jnp.zeros((CARRY, SSD_CONV_DIM), f32)
        ext[CARRY - (CONV_W - 1):CARRY, :] = cp_ref[0]
        st_ref[...] = s0_ref[0].reshape(st_ref.shape)

    yield
    ext[CARRY:EXT, :] = xbc_ref[...]
    acc = cb_ref[...] + ext[pl.ds(CARRY - (CONV_W - 1), SEG), :] * cw_ref[0:1, :]
    for i in range(1, CONV_W):
        acc = acc + ext[pl.ds(CARRY - (CONV_W - 1) + i, SEG), :] * cw_ref[i:i + 1, :]
    ext[0:CARRY, :] = ext[SEG:EXT, :]
    conv = _silu(acc)

    dt = _softplus(dt_ref[...] + dtb_ref[...])
    a = _cumsum_rows(dt * (-jnp.exp(alog_ref[...])))
    a_t = a.T
    dt_t = dt.T
    a_last = a[SEG - 1:SEG, :]
    e_a = jnp.exp(a)
    w_col = dt * jnp.exp(a_last - a)
    e_last = jnp.exp(a_last)
    causal = lax.broadcasted_iota(jnp.int32, (SEG, SEG), 0) >= lax.broadcasted_iota(jnp.int32, (SEG, SEG), 1)
    lane_lo = lax.broadcasted_iota(jnp.int32, (SEG, 2 * SSD_P), 1) < SSD_P
    sub_lo = lax.broadcasted_iota(jnp.int32, (2 * SSD_P, SSD_N), 0) < SSD_P
    nbc = SSD_G * SSD_N

    def pair_cols(m, ja, jb):
        return jnp.where(lane_lo, jnp.broadcast_to(m[:, ja:ja + 1], (SEG, 2 * SSD_P)),
                         jnp.broadcast_to(m[:, jb:jb + 1], (SEG, 2 * SSD_P)))

    for g in range(SSD_G):
        j0 = LOW_DT0 + g * SSD_HG
        bc = conv[:, SSD_INNER + g * SSD_N:SSD_INNER + (g + 1) * SSD_N]
        cc = conv[:, SSD_INNER + nbc + g * SSD_N:SSD_INNER + nbc + (g + 1) * SSD_N]
        cbm = _dot_nt(cc, bc)

        def w_intra(j):
            seg = a[:, j:j + 1] - a_t[j:j + 1, :]
            return cbm * jnp.exp(jnp.where(causal, seg, -jnp.inf)) * dt_t[j:j + 1, :]

        ys = []
        for p in range(SSD_HG // 2):
            ja, jb = j0 + 2 * p, j0 + 2 * p + 1
            pp = g * (SSD_HG // 2) + p
            xp = conv[:, pp * 2 * SSD_P:(pp + 1) * 2 * SSD_P]
            y_intra = jnp.where(lane_lo, _dot(w_intra(ja), xp), _dot(w_intra(jb), xp))
            sp = st_ref[pp]
            y_inter = _dot_nt(cc, sp) * pair_cols(e_a, ja, jb)
            ds = _dot_tn(xp * pair_cols(w_col, ja, jb), bc)
            rs = jnp.where(sub_lo, jnp.broadcast_to(e_last[:, ja:ja + 1], (2 * SSD_P, SSD_N)),
                           jnp.broadcast_to(e_last[:, jb:jb + 1], (2 * SSD_P, SSD_N)))
            st_ref[pp] = sp * rs + ds
            ys.append(y_intra + y_inter + dvec_ref[:, pp * 2 * SSD_P:(pp + 1) * 2 * SSD_P] * xp)
        gw = slice(g * SSD_GW, (g + 1) * SSD_GW)
        y = jnp.concatenate(ys, axis=1) * _silu(z_ref[:, gw])
        o_ref[:, gw] = _rms(y, nw_ref[:, gw]).astype(bf16)

    yield
    @pl.when(jnp.logical_and(is_p, pos == SEG_PER_PSEQ - 1))
    def _():
        stp_ref[0] = st_ref[...].reshape(stp_ref.shape[1:])
        cvp_ref[0] = xbc_ref[SEG - (CONV_W - 1):SEG, :]

    @pl.when(jnp.logical_not(is_p))
    def _():
        sts_ref[0] = st_ref[...].reshape(sts_ref.shape[1:])
        cvs_ref[0] = xbc_ref[SEG - (CONV_W - 1):SEG, :]


def _ssd_call(p, conv_prev, conv_w, conv_b, dtb, alog, dvec, norm_w, s0):
    st_block = (1, SSD_H, SSD_P, SSD_N)
    cv_block = (1, CONV_W - 1, SSD_CONV_DIM)
    full = lambda c: (0, 0)
    return dict(
        in_specs=[pl.BlockSpec((SEG, SSD_CONV_DIM), lambda c: (c, C_XBC // SSD_CONV_DIM)),
                  pl.BlockSpec((SEG, SSD_INNER), lambda c: (c, C_Z // SSD_INNER)),
                  pl.BlockSpec((SEG, LANE), lambda c: (c, C_LOW // LANE)),
                  pl.BlockSpec(cv_block, lambda c: (_sample_seq(c), 0, 0)),
                  pl.BlockSpec((CONV_W, SSD_CONV_DIM), full),
                  pl.BlockSpec((1, SSD_CONV_DIM), full),
                  pl.BlockSpec((1, LANE), full),
                  pl.BlockSpec((1, LANE), full),
                  pl.BlockSpec((1, SSD_INNER), full),
                  pl.BlockSpec((1, SSD_INNER), full),
                  pl.BlockSpec(st_block, lambda c: (_sample_seq(c), 0, 0, 0))],
        out_specs=[pl.BlockSpec((SEG, SSD_INNER), lambda c: (c, 0)),
                   pl.BlockSpec(st_block, lambda c: (_prompt_seq(c), 0, 0, 0)),
                   pl.BlockSpec(st_block, lambda c: (_sample_seq(c), 0, 0, 0)),
                   pl.BlockSpec(cv_block, lambda c: (_prompt_seq(c), 0, 0)),
                   pl.BlockSpec(cv_block, lambda c: (_sample_seq(c), 0, 0))],
        out_shape=[jax.ShapeDtypeStruct((N_TOK, SSD_INNER), bf16),
                   jax.ShapeDtypeStruct((B_P, SSD_H, SSD_P, SSD_N), f32),
                   jax.ShapeDtypeStruct((B_S, SSD_H, SSD_P, SSD_N), f32),
                   jax.ShapeDtypeStruct((B_P, CONV_W - 1, SSD_CONV_DIM), f32),
                   jax.ShapeDtypeStruct((B_S, CONV_W - 1, SSD_CONV_DIM), f32)],
        scratch_shapes=[pltpu.VMEM((EXT, SSD_CONV_DIM), f32),
                        pltpu.VMEM((SSD_H // 2, 2 * SSD_P, SSD_N), f32)],
        args=(p, p, p, conv_prev, conv_w, conv_b.reshape(1, -1), dtb, alog, dvec, norm_w.reshape(1, -1), s0))


def _mixers(gla, ssd):
    n_in = (len(gla["in_specs"]), len(ssd["in_specs"]))
    n_out = (len(gla["out_specs"]), len(ssd["out_specs"]))
    n_scr = (len(gla["scratch_shapes"]), len(ssd["scratch_shapes"]))

    def body(*refs):
        ins, outs, scr = refs[:sum(n_in)], refs[sum(n_in):sum(n_in) + sum(n_out)], refs[sum(n_in) + sum(n_out):]
        parts = [_gla_kernel(*ins[:n_in[0]], *outs[:n_out[0]], *scr[:n_scr[0]]),
                 _ssd_kernel(*ins[n_in[0]:], *outs[n_out[0]:], *scr[n_scr[0]:])]
        for _ in range(3):
            for part in parts:
                next(part, None)

    outs = pl.pallas_call(
        body,
        grid=(NSEG,),
        in_specs=gla["in_specs"] + ssd["in_specs"],
        out_specs=gla["out_specs"] + ssd["out_specs"],
        out_shape=gla["out_shape"] + ssd["out_shape"],
        scratch_shapes=gla["scratch_shapes"] + ssd["scratch_shapes"],
        compiler_params=_cparams(("arbitrary",)),
        name="token_mixers",
    )(*gla["args"], *ssd["args"])
    return outs[:n_out[0]], outs[n_out[0]:]


MIX_TB = 512


def _mix_kernel(oa_ref, yb_ref, gt_ref, x_ref, ga_ref, sc_ref, sh_ref, nw_ref, wa_ref, wb_ref, wo_ref,
                x1_ref, h2_ref, h2t_ref):
    br_a = jnp.dot(oa_ref[...], wa_ref[...], preferred_element_type=f32)
    br_b = jnp.dot(yb_ref[...], wb_ref[...], preferred_element_type=f32)
    g_a = jax.nn.sigmoid(gt_ref[:, 0:D])
    g_b = jax.nn.sigmoid(gt_ref[:, D:2 * D])
    y = _dot(g_a * br_a + g_b * br_b, wo_ref[...])
    for s in range(MIX_TB // SEG):
        r = slice(s * SEG, (s + 1) * SEG)
        x1 = x_ref[r, :] + ga_ref[s:s + 1, :] * y[r, :]
        x1_ref[r, :] = x1
        h2_ref[r, :] = (_rms(x1, nw_ref[...]) * (1.0 + sc_ref[s:s + 1, :]) + sh_ref[s:s + 1, :]).astype(bf16)
    for g in range(MIX_TB // MXU):
        h2t_ref[g] = h2_ref[g * MXU:(g + 1) * MXU, :].T


def _mix(oa, yb, p, x, mod_seg, norm2_w, wa, wb, wo):
    nseg_b = MIX_TB // SEG
    full = lambda i: (0, 0)
    return pl.pallas_call(
        _mix_kernel,
        grid=(N_TOK // MIX_TB,),
        in_specs=[pl.BlockSpec((MIX_TB, D), lambda i: (i, 0)),
                  pl.BlockSpec((MIX_TB, SSD_INNER), lambda i: (i, 0)),
                  pl.BlockSpec((MIX_TB, 2 * D), lambda i: (i, C_GATE // (2 * D))),
                  pl.BlockSpec((MIX_TB, D), lambda i: (i, 0)),
                  pl.BlockSpec((nseg_b, D), lambda i: (i, 2)),
                  pl.BlockSpec((nseg_b, D), lambda i: (i, 4)),
                  pl.BlockSpec((nseg_b, D), lambda i: (i, 3)),
                  pl.BlockSpec((1, D), full),
                  pl.BlockSpec((D, D), full),
                  pl.BlockSpec((SSD_INNER, D), full),
                  pl.BlockSpec((D, D), full)],
        out_specs=[pl.BlockSpec((MIX_TB, D), lambda i: (i, 0)),
                   pl.BlockSpec((MIX_TB, D), lambda i: (i, 0)),
                   pl.BlockSpec((MIX_TB // MXU, D, MXU), lambda i: (i, 0, 0))],
        out_shape=[jax.ShapeDtypeStruct((N_TOK, D), f32),
                   jax.ShapeDtypeStruct((N_TOK, D), bf16),
                   jax.ShapeDtypeStruct((N_TOK // MXU, D, MXU), bf16)],
        compiler_params=_cparams(("parallel",)),
        name="mix_out",
    )(oa, yb, p, x, mod_seg, mod_seg, mod_seg, norm2_w.reshape(1, D), wa, wb, wo)


RT_TL = 128
HALF = PEER_DQ // 2


RT_HPS = 8


def _batcher_sort_net(n):
    def merge(lo, hi, r):
        step = 2 * r
        if step < hi - lo:
            yield from merge(lo, hi, step)
            yield from merge(lo + r, hi, step)
            yield from ((i, i + r) for i in range(lo + r, hi - r, step))
        else:
            yield (lo, lo + r)

    def sort(lo, hi):
        if hi - lo >= 1:
            mid = lo + (hi - lo) // 2
            yield from sort(lo, mid)
            yield from sort(mid + 1, hi)
            yield from merge(lo, hi, 1)

    return tuple(sort(0, n - 1))


def _bitonic_merge_net(n):
    net, d = [], n // 2
    while d >= 1:
        net += [(i, i + d) for i in range(n) if (i // d) % 2 == 0]
        d //= 2
    return tuple(net)


_SORT16 = _batcher_sort_net(PEER_TOPK)
_MERGE16 = _bitonic_merge_net(PEER_TOPK)
N_CAND_VREGS = 10


def _compare_exchange(x, net):
    for i, j in net:
        x[i], x[j] = jnp.maximum(x[i], x[j]), jnp.minimum(x[i], x[j])


def _merge_across_sublanes(x, n_valid):
    for shift in (4, 2, 1):
        y = [pltpu.roll(v, shift, 0) for v in x]
        merged = []
        for k in range(PEER_TOPK):
            a = x[k] if k < n_valid else None
            b = y[PEER_TOPK - 1 - k] if PEER_TOPK - 1 - k < n_valid else None
            merged.append(jnp.maximum(a, b) if (a is not None and b is not None) else (a if b is None else b))
        x = merged
        _compare_exchange(x, _MERGE16)
        n_valid = PEER_TOPK
    return x


def _top16_sorted(s):
    x = [s[SUBLANES * k:SUBLANES * (k + 1), :] for k in range(PEER_NK // SUBLANES)]
    _compare_exchange(x, _SORT16)
    return _merge_across_sublanes(x, PEER_TOPK)


def _route_kernel(q_ref, k1_ref, k2_ref, thr_ref, s2_ref, f1_ref, f2_ref):
    sub = lax.broadcasted_iota(jnp.int32, (SUBLANES, RT_TL), 0)
    ninf = jnp.float32(-jnp.inf)

    def by_sublane(vs):
        out = vs[0]
        for r in range(1, SUBLANES):
            out = jnp.where(sub == r, vs[r], out)
        return out

    for hh in range(RT_HPS):
        s1 = _dot3_nt(k1_ref[hh], q_ref[:, hh * PEER_DQ:hh * PEER_DQ + HALF])
        s2 = _dot3_nt(k2_ref[hh], q_ref[:, hh * PEER_DQ + HALF:(hh + 1) * PEER_DQ])
        s2_ref[hh] = s2
        v1 = _top16_sorted(s1)
        v2 = _top16_sorted(s2)
        v2lo, v2hi, v1hi = by_sublane(v2[:8]), by_sublane(v2[8:]), by_sublane(v1[8:])
        cands = [v1[0] + v2lo, v1[0] + v2hi, v1[1] + v2lo]
        for i in range(2, 8):
            cands.append(jnp.where(sub < PEER_TOPK // (i + 1), v1[i] + v2lo, ninf))
        cands.append(v1hi + v2[0])
        assert len(cands) == N_CAND_VREGS
        _compare_exchange(cands, tuple((i, j) for i, j in _SORT16 if j < N_CAND_VREGS))
        top = _merge_across_sublanes(cands, N_CAND_VREGS)
        zsum = jnp.ones_like(top[0])
        for k in range(1, PEER_TOPK):
            zsum = zsum + jnp.exp(top[k] - top[0])
        tau = top[PEER_TOPK - 1]
        for k in range(PEER_NK // SUBLANES):
            rows = slice(SUBLANES * k, SUBLANES * (k + 1))
            s1k = s1[rows, :]
            thr = jnp.full_like(s1k, jnp.inf)
            for j in range(PEER_TOPK):
                thr = jnp.where(s1k + v2[j] >= tau, v2[j], thr)
            thr_ref[hh, rows, :] = thr
        f1_ref[hh] = jnp.exp(s1 - v1[0][0:1, :]) * (0.5 / zsum[0:1, :])
        f2_ref[hh] = jnp.exp(s2 - v2[0][0:1, :])


def _route(q, keys1, keys2):
    tile = pl.BlockSpec((RT_HPS, PEER_NK, RT_TL), lambda i, h: (h, 0, i))
    big = jax.ShapeDtypeStruct((PEER_H, PEER_NK, N_TOK), f32)
    return pl.pallas_call(
        _route_kernel,
        grid=(N_TOK // RT_TL, PEER_H // RT_HPS),
        in_specs=[pl.BlockSpec((RT_TL, RT_HPS * PEER_DQ), lambda i, h: (i, h)),
                  pl.BlockSpec((RT_HPS, PEER_NK, HALF), lambda i, h: (h, 0, 0)),
                  pl.BlockSpec((RT_HPS, PEER_NK, HALF), lambda i, h: (h, 0, 0))],
        out_specs=[tile, tile, tile, tile],
        out_shape=[big, big, big, big],
        compiler_params=_cparams(("parallel", "parallel")),
        name="peer_route",
    )(q, keys1, keys2)


PE_TB = 512
PE_NA = 4
PE_AG = 2
PE_CH = PE_AG * PE_NA * PEER_NK
PE_NCH = PEER_NK * PEER_NK // PE_CH
PE_BH = 32
SQRT_HALF = 0.7071067811865476
MXU = 256


def _peer_kernel(*refs, final):
    if final:
        (u0_ref, u_ref, vt_ref, xt_ref, thr_ref, f1_ref, s2_ref, f2_ref, x1_ref, ga_ref, fw_ref,
         o_ref, acc_ref, pw_ref, gel_ref, rthr_ref, rf1_ref) = refs
    else:
        (u0_ref, u_ref, vt_ref, xt_ref, thr_ref, f1_ref, s2_ref, f2_ref, x1_ref, ga_ref,
         o_ref, acc_ref, pw_ref, gel_ref, rthr_ref, rf1_ref) = refs
    j = pl.program_id(1)

    n_half = PE_TB // MXU

    def two_gelu(act):
        return act * (1.0 + lax.erf(act * SQRT_HALF))

    @pl.when(j == 0)
    def _():
        acc_ref[...] = jnp.zeros_like(acc_ref)
        pw_ref[1] = jnp.zeros(pw_ref.shape[1:], bf16)
        for g in range(n_half):
            gel_ref[0, g] = two_gelu(jnp.dot(u0_ref[0], xt_ref[g], preferred_element_type=f32))

    cur = j % 2
    prev = 1 - cur

    def prep_rows():
        for ia in range(PE_AG * PE_NA):
            for h in range(PEER_H):
                a = j * (PE_AG * PE_NA) + ia
                rthr_ref[ia * PEER_H + h] = jnp.broadcast_to(thr_ref[h, pl.ds(a, 1), :], (SUBLANES, PE_TB))
                rf1_ref[ia * PEER_H + h] = jnp.broadcast_to(f1_ref[h, pl.ds(a, 1), :], (SUBLANES, PE_TB))

    nsub = PE_BH // SUBLANES
    n_bh = PEER_NK // PE_BH

    def weight_tile(g, lq, bh, ag):
        lanes = pl.ds(pl.multiple_of(g * MXU + lq * LANE, LANE), LANE)
        lq_lanes = slice(lq * LANE, (lq + 1) * LANE)
        rows = slice(bh * PE_BH, (bh + 1) * PE_BH)
        ws = [jnp.zeros((nsub, SUBLANES, LANE), f32) for _ in range(PE_NA)]
        for h in range(PEER_H):
            s2t = s2_ref[h, rows, lanes].reshape(nsub, SUBLANES, LANE)
            f2t = f2_ref[h, rows, lanes].reshape(nsub, SUBLANES, LANE)
            for i in range(PE_NA):
                row = (ag * PE_NA + i) * PEER_H + h
                sel = s2t >= rthr_ref[row, :, lanes][None]
                ws[i] = ws[i] + jnp.where(sel, f2t * rf1_ref[row, :, lanes][None], 0.0)
        for i in range(PE_NA):
            e0 = (ag * PE_NA + i) * PEER_NK + bh * PE_BH
            pw_ref[cur, g, e0:e0 + PE_BH, lq_lanes] = (
                ws[i].reshape(PE_BH, LANE) * gel_ref[cur, g, e0:e0 + PE_BH, lq_lanes]).astype(bf16)

    def stage(p, carry, with_act):
        if with_act:
            gel_ref[prev, p] = two_gelu(jnp.dot(u_ref[0], xt_ref[p], preferred_element_type=f32))
        acc_ref[p] += jnp.dot(vt_ref[0], pw_ref[prev, p], preferred_element_type=f32)
        for lq in range(MXU // LANE):
            for bh in range(n_bh):
                for ag in range(PE_AG):
                    weight_tile(p, lq, bh, ag)
        return carry

    @pl.when(j < PE_NCH - 1)
    def _():
        prep_rows()
        lax.fori_loop(0, n_half, functools.partial(stage, with_act=True), 0)

    @pl.when(j == PE_NCH - 1)
    def _():
        prep_rows()
        lax.fori_loop(0, n_half, functools.partial(stage, with_act=False), 0)

    @pl.when(j == PE_NCH)
    def _():
        for g in range(n_half):
            out = (acc_ref[g] + jnp.dot(vt_ref[0], pw_ref[prev, g], preferred_element_type=f32)).T
            for s in range(MXU // SEG):
                r = slice(g * MXU + s * SEG, g * MXU + (s + 1) * SEG)
                seg = g * (MXU // SEG) + s
                x2 = x1_ref[r, :] + ga_ref[seg:seg + 1, :] * out[s * SEG:(s + 1) * SEG, :]
                o_ref[r, :] = _rms(x2, fw_ref[...]) if final else x2


def _peer(u_c, vt_c, h2t, thr, f1t, s2t, f2t, x1, mod_seg, final_w):
    final = final_w is not None
    nseg_b = PE_TB // SEG
    rt = pl.BlockSpec((PEER_H, PEER_NK, PE_TB), lambda i, j: (0, 0, i))
    in_specs = [pl.BlockSpec((1, PE_CH, D), lambda i, j: (0, 0, 0)),
                pl.BlockSpec((1, PE_CH, D), lambda i, j: (jnp.minimum(j + 1, PE_NCH - 1), 0, 0)),
                pl.BlockSpec((1, D, PE_CH), lambda i, j: (jnp.maximum(j - 1, 0), 0, 0)),
                pl.BlockSpec((PE_TB // MXU, D, MXU), lambda i, j: (i, 0, 0)),
                rt, rt, rt, rt,
                pl.BlockSpec((PE_TB, D), lambda i, j: (i, 0)),
                pl.BlockSpec((nseg_b, D), lambda i, j: (i, 5))]
    args = [u_c, u_c, vt_c, h2t, thr, f1t, s2t, f2t, x1, mod_seg]
    if final:
        in_specs.append(pl.BlockSpec((1, D), lambda i, j: (0, 0)))
        args.append(final_w.reshape(1, D))
    return pl.pallas_call(
        functools.partial(_peer_kernel, final=final),
        grid=(N_TOK // PE_TB, PE_NCH + 1),
        in_specs=in_specs,
        out_specs=pl.BlockSpec((PE_TB, D), lambda i, j: (i, 0)),
        out_shape=jax.ShapeDtypeStruct((N_TOK, D), f32),
        scratch_shapes=[pltpu.VMEM((PE_TB // MXU, D, MXU), f32), pltpu.VMEM((2, PE_TB // MXU, PE_CH, MXU), bf16),
                        pltpu.VMEM((2, PE_TB // MXU, PE_CH, MXU), f32),
                        pltpu.VMEM((PE_AG * PE_NA * PEER_H, SUBLANES, PE_TB), f32),
                        pltpu.VMEM((PE_AG * PE_NA * PEER_H, SUBLANES, PE_TB), f32)],
        compiler_params=_cparams(("parallel", "arbitrary")),
        name="peer_experts_final" if final else "peer_experts",
    )(*args)


def _reorder_w_in(w):
    o = np.cumsum([0, 512, 512, 1024, 1024, GLA_RANK, SSD_INNER, SSD_CONV_DIM, SSD_H, 2 * D])
    q_k_v_go = w[:, o[0]:o[4]]
    gk_low = w[:, o[4]:o[5]]
    z = w[:, o[5]:o[6]]
    xbc = w[:, o[6]:o[7]]
    dt = w[:, o[7]:o[8]]
    gates = w[:, o[8]:o[9]]
    pad = lambda m: jnp.pad(m, ((0, 0), (0, LANE - m.shape[1])))
    low = jnp.concatenate([gk_low, jnp.zeros((D, LOW_DT0 - GLA_RANK), w.dtype), dt], axis=1)
    return jnp.concatenate([q_k_v_go, xbc, z, gates, pad(low)], axis=1).astype(bf16)


def kernel(x_prompt, x_sample, state_gla, state_ssd, state_conv, c_prompt, c_sample, w_ada, b_ada, norm1_w, w_in, gla_gk_w2, gla_gk_b, gla_norm_w, gla_proj, ssd_conv_w, ssd_conv_b, ssd_dt_bias, ssd_A_log, ssd_D, ssd_norm_w, ssd_proj, w_out, norm2_w, peer_wq, peer_keys1, peer_keys2, peer_u, peer_v, final_norm_w):
    x = jnp.concatenate([x_prompt.reshape(N_P, D), x_sample.reshape(N_S, D)], axis=0)
    c_all = jnp.concatenate([c_prompt, c_sample], axis=0)
    mod = _ada(c_all, w_ada, b_ada)
    seg2seq = np.concatenate([np.repeat(np.arange(B_P), SEG_PER_PSEQ), B_P + np.arange(B_S)])

    gla_st, ssd_st, conv_st = [], [], []
    for l in range(DEPTH):
        mod_seg = mod[l][seg2seq]
        h = _normmod(x, norm1_w[l], mod_seg, 1, 0)
        p = _mm(h, _reorder_w_in(w_in[l]), 1024, P_COLS // 3, "in_proj")

        w2pad = jnp.pad(gla_gk_w2[l], ((0, LANE - GLA_RANK), (0, 0)))
        low_lanes = lambda m: jnp.pad(m.reshape(1, SSD_H), ((0, 0), (LOW_DT0, LANE - LOW_DT0 - SSD_H)))
        dvec = jnp.repeat(ssd_D[l], SSD_P).reshape(1, SSD_INNER)
        (oa, gla_p, gla_s), (yb, ssd_p, ssd_s, conv_p, conv_s) = _mixers(
            _gla_call(p, w2pad, gla_gk_b[l], gla_norm_w[l], state_gla[l]),
            _ssd_call(p, state_conv[l], ssd_conv_w[l], ssd_conv_b[l], low_lanes(ssd_dt_bias[l]),
                      low_lanes(ssd_A_log[l]), dvec, ssd_norm_w[l], state_ssd[l]))

        x1, h2, h2t = _mix(oa, yb, p, x, mod_seg, norm2_w[l], gla_proj[l].astype(bf16),
                           ssd_proj[l].astype(bf16), w_out[l].astype(bf16))
        q = _mm(h2, peer_wq[l].astype(bf16), 512, PEER_H * PEER_DQ, "peer_query")
        thr, s2t, f1t, f2t = _route(q, peer_keys1[l], peer_keys2[l])
        u_c = peer_u[l].astype(bf16).reshape(PE_NCH, PE_CH, D)
        vt_c = peer_v[l].astype(bf16).reshape(PE_NCH, PE_CH, D).transpose(0, 2, 1)
        x = _peer(u_c, vt_c, h2t, thr, f1t, s2t, f2t, x1, mod_seg,
                  final_norm_w if l == DEPTH - 1 else None)

        gla_st.append((gla_p, gla_s))
        ssd_st.append((ssd_p, ssd_s))
        conv_st.append((conv_p, conv_s))

    y_prompt = x[:N_P].reshape(B_P, T_P, D)
    y_sample = x[N_P:].reshape(B_S, T_S, D)
    stack = lambda pairs, k: jnp.stack([pr[k] for pr in pairs])
    return (y_prompt, y_sample, stack(gla_st, 0), stack(ssd_st, 0), stack(conv_st, 0),
            stack(gla_st, 1), stack(ssd_st, 1), stack(conv_st, 1))
```

```python
import functools

import jax
import jax.numpy as jnp
import numpy as np
from jax import lax
from jax.experimental import pallas as pl
from jax.experimental.pallas import tpu as pltpu

f32 = jnp.float32
bf16 = jnp.bfloat16

D = 1024
DEPTH = 2
B_P, T_P = 8, 2048
B_S, T_S = 32, 64
N_P = B_P * T_P
N_S = B_S * T_S
N_TOK = N_P + N_S
SEG = 64
NSEG = N_TOK // SEG
NSEG_P = N_P // SEG
SEG_PER_PSEQ = T_P // SEG
EPS = 1e-6

GLA_H, GLA_DK, GLA_DV = 4, 128, 256
GLA_RANK = 16
GLA_GATE_NORM = 16.0
SSD_INNER = 2048
SSD_P = 64
SSD_H = 32
SSD_G = 4
SSD_N = 128
SSD_HG = SSD_H // SSD_G
SSD_GW = SSD_INNER // SSD_G
CONV_W = 4
SSD_CONV_DIM = SSD_INNER + 2 * SSD_G * SSD_N

PEER_H = 8
PEER_NK = 128
PEER_DQ = 256
PEER_TOPK = 16

C_Q, C_K, C_V, C_GO = 0, 512, 1024, 2048
C_XBC, C_Z, C_GATE = 3072, 6144, 8192
C_LOW = 10240
LOW_DT0 = 32
P_COLS = C_LOW + 128

LANE = 128
SUBLANES = 8
VMEM_LIMIT = 56 * 1024 * 1024


def _cparams(sem):
    return pltpu.CompilerParams(dimension_semantics=sem, vmem_limit_bytes=VMEM_LIMIT)


def _dot(a, b):
    return jnp.dot(a.astype(bf16), b.astype(bf16), preferred_element_type=f32)


def _dot_nt(a, b):
    return lax.dot_general(a.astype(bf16), b.astype(bf16), (((1,), (1,)), ((), ())), preferred_element_type=f32)


def _dot_tn(a, b):
    return lax.dot_general(a.astype(bf16), b.astype(bf16), (((0,), (0,)), ((), ())), preferred_element_type=f32)


def _split3(x):
    hi = x.astype(bf16)
    r = x - hi.astype(f32)
    mid = r.astype(bf16)
    lo = (r - mid.astype(f32)).astype(bf16)
    return hi, mid, lo


def _dot3(a, b):
    ah, am, _ = _split3(a)
    bh, bm, _ = _split3(b)
    d = functools.partial(jnp.dot, preferred_element_type=f32)
    return d(ah, bh) + (d(ah, bm) + d(am, bh))


def _dot3_nt(a, b):
    ah, am, _ = _split3(a)
    bh, bm, _ = _split3(b)
    d = functools.partial(lax.dot_general, dimension_numbers=(((1,), (1,)), ((), ())), preferred_element_type=f32)
    return d(ah, bh) + (d(ah, bm) + d(am, bh))


def _cumsum_rows(x):
    n = x.shape[0]
    tri = (lax.broadcasted_iota(jnp.int32, (n, n), 0) >= lax.broadcasted_iota(jnp.int32, (n, n), 1)).astype(bf16)
    hi, mid, lo = _split3(x)
    d = functools.partial(jnp.dot, preferred_element_type=f32)
    return d(tri, hi) + (d(tri, mid) + d(tri, lo))


def _silu(x):
    return x * jax.nn.sigmoid(x)


def _softplus(x):
    return jnp.maximum(x, 0.0) + jnp.log1p(jnp.exp(-jnp.abs(x)))


def _rms(x, w):
    return x * lax.rsqrt(jnp.mean(x * x, axis=-1, keepdims=True) + EPS) * w


def _ada_kernel(c_ref, w_ref, b_ref, o_ref):
    o_ref[0] = _dot3(_silu(c_ref[...]), w_ref[0]) + b_ref[0]


def _ada(c_all, w_ada, b_ada):
    nb = c_all.shape[0]
    tn = 1536
    return pl.pallas_call(
        _ada_kernel,
        grid=(DEPTH, 6 * D // tn),
        in_specs=[pl.BlockSpec((nb, D), lambda l, j: (0, 0)),
                  pl.BlockSpec((1, D, tn), lambda l, j: (l, 0, j)),
                  pl.BlockSpec((1, 1, tn), lambda l, j: (l, 0, j))],
        out_specs=pl.BlockSpec((1, nb, tn), lambda l, j: (l, 0, j)),
        out_shape=jax.ShapeDtypeStruct((DEPTH, nb, 6 * D), f32),
        compiler_params=_cparams(("parallel", "parallel")),
        name="ada_mod",
    )(c_all, w_ada, b_ada.reshape(DEPTH, 1, 6 * D))


NM_TB = 512


def _normmod_kernel(x_ref, w_ref, sc_ref, sh_ref, o_ref):
    for s in range(NM_TB // SEG):
        r = slice(s * SEG, (s + 1) * SEG)
        y = _rms(x_ref[r, :], w_ref[...])
        o_ref[r, :] = (y * (1.0 + sc_ref[s:s + 1, :]) + sh_ref[s:s + 1, :]).astype(bf16)


def _normmod(x, w, mod_seg, sc_col, sh_col):
    nseg_b = NM_TB // SEG
    return pl.pallas_call(
        _normmod_kernel,
        grid=(N_TOK // NM_TB,),
        in_specs=[pl.BlockSpec((NM_TB, D), lambda i: (i, 0)),
                  pl.BlockSpec((1, D), lambda i: (0, 0)),
                  pl.BlockSpec((nseg_b, D), lambda i: (i, sc_col)),
                  pl.BlockSpec((nseg_b, D), lambda i: (i, sh_col))],
        out_specs=pl.BlockSpec((NM_TB, D), lambda i: (i, 0)),
        out_shape=jax.ShapeDtypeStruct((N_TOK, D), bf16),
        compiler_params=_cparams(("parallel",)),
        name="norm_mod",
    )(x, w.reshape(1, D), mod_seg, mod_seg)


def _mm_kernel(a_ref, b_ref, o_ref):
    o_ref[...] = jnp.dot(a_ref[...], b_ref[...], preferred_element_type=f32)


def _mm(a, b, tm, tn, name):
    m, k = a.shape
    n = b.shape[1]
    return pl.pallas_call(
        _mm_kernel,
        grid=(n // tn, m // tm),
        in_specs=[pl.BlockSpec((tm, k), lambda j, i: (i, 0)),
                  pl.BlockSpec((k, tn), lambda j, i: (0, j))],
        out_specs=pl.BlockSpec((tm, tn), lambda j, i: (i, j)),
        out_shape=jax.ShapeDtypeStruct((m, n), f32),
        compiler_params=_cparams(("parallel", "parallel")),
        name=name,
    )(a, b)


def _is_prompt(c):
    return c < NSEG_P


def _sample_seq(c):
    return jnp.maximum(c - NSEG_P, 0)


def _prompt_seq(c):
    return jnp.minimum(c // SEG_PER_PSEQ, B_P - 1)


def _gla_kernel(q_ref, k_ref, v_ref, g_ref, low_ref, w2_ref, b2_ref, nw_ref, s0_ref,
                o_ref, stp_ref, sts_ref, st_ref):
    c = pl.program_id(0)
    is_p = _is_prompt(c)
    pos = c % SEG_PER_PSEQ

    @pl.when(jnp.logical_and(is_p, pos == 0))
    def _():
        st_ref[...] = jnp.zeros_like(st_ref)

    @pl.when(jnp.logical_not(is_p))
    def _():
        for h in range(GLA_H):
            st_ref[h] = s0_ref[0, h].T

    yield
    pre = _dot3(low_ref[...], w2_ref[...]) + b2_ref[...]
    gk = -_softplus(-pre) / GLA_GATE_NORM
    b = _cumsum_rows(gk)
    b_last = b[SEG - 1:SEG, :]
    e_b = jnp.exp(b)
    e_nb = jnp.exp(-b)
    e_end = jnp.exp(b_last - b)
    e_last = jnp.exp(b_last)
    causal = lax.broadcasted_iota(jnp.int32, (SEG, SEG), 0) >= lax.broadcasted_iota(jnp.int32, (SEG, SEG), 1)
    for h in range(GLA_H):
        ks = slice(h * GLA_DK, (h + 1) * GLA_DK)
        vs = slice(h * GLA_DV, (h + 1) * GLA_DV)
        k = k_ref[:, ks]
        q_t = (q_ref[:, ks] * (GLA_DK ** -0.5)) * e_b[:, ks]
        att = jnp.where(causal, _dot_nt(q_t, k * e_nb[:, ks]), 0.0)
        v = v_ref[:, vs]
        st = st_ref[h]
        o = _dot(att, v) + _dot_nt(q_t, st)
        st_ref[h] = st * e_last[:, ks] + _dot_tn(v, k * e_end[:, ks])
        o_ref[:, vs] = (_rms(o, nw_ref[...]) * _silu(g_ref[:, vs])).astype(bf16)

    yield
    @pl.when(jnp.logical_and(is_p, pos == SEG_PER_PSEQ - 1))
    def _():
        for h in range(GLA_H):
            stp_ref[0, h] = st_ref[h].T

    @pl.when(jnp.logical_not(is_p))
    def _():
        for h in range(GLA_H):
            sts_ref[0, h] = st_ref[h].T


def _gla_call(p, w2pad, b2, norm_w, s0):
    st_block = (1, GLA_H, GLA_DK, GLA_DV)
    kd, vd = GLA_H * GLA_DK, GLA_H * GLA_DV
    return dict(
        in_specs=[pl.BlockSpec((SEG, kd), lambda c: (c, C_Q // kd)),
                  pl.BlockSpec((SEG, kd), lambda c: (c, C_K // kd)),
                  pl.BlockSpec((SEG, vd), lambda c: (c, C_V // vd)),
                  pl.BlockSpec((SEG, vd), lambda c: (c, C_GO // vd)),
                  pl.BlockSpec((SEG, LANE), lambda c: (c, C_LOW // LANE)),
                  pl.BlockSpec((LANE, kd), lambda c: (0, 0)),
                  pl.BlockSpec((1, kd), lambda c: (0, 0)),
                  pl.BlockSpec((1, GLA_DV), lambda c: (0, 0)),
                  pl.BlockSpec(st_block, lambda c: (_sample_seq(c), 0, 0, 0))],
        out_specs=[pl.BlockSpec((SEG, vd), lambda c: (c, 0)),
                   pl.BlockSpec(st_block, lambda c: (_prompt_seq(c), 0, 0, 0)),
                   pl.BlockSpec(st_block, lambda c: (_sample_seq(c), 0, 0, 0))],
        out_shape=[jax.ShapeDtypeStruct((N_TOK, vd), bf16),
                   jax.ShapeDtypeStruct((B_P, GLA_H, GLA_DK, GLA_DV), f32),
                   jax.ShapeDtypeStruct((B_S, GLA_H, GLA_DK, GLA_DV), f32)],
        scratch_shapes=[pltpu.VMEM((GLA_H, GLA_DV, GLA_DK), f32)],
        args=(p, p, p, p, p, w2pad, b2.reshape(1, -1), norm_w.reshape(1, -1), s0))


CARRY = SUBLANES
EXT = SEG + CARRY


def _ssd_kernel(xbc_ref, z_ref, dt_ref, cp_ref, cw_ref, cb_ref, dtb_ref, alog_ref, dvec_ref, nw_ref, s0_ref,
                o_ref, stp_ref, sts_ref, cvp_ref, cvs_ref, ext, st_ref):
    c = pl.program_id(0)
    is_p = _is_prompt(c)
    pos = c % SEG_PER_PSEQ

    @pl.when(jnp.logical_and(is_p, pos == 0))
    def _():
        ext[0:CARRY, :] = jnp.zeros((CARRY, SSD_CONV_DIM), f32)
        st_ref[...] = jnp.zeros_like(st_ref)

    @pl.when(jnp.logical_not(is_p))
    def _():
        ext[0:CARRY, :] = jnp.zeros((CARRY, SSD_CONV_DIM), f32)
        ext[CARRY - (CONV_W - 1):CARRY, :] = cp_ref[0]
        st_ref[...] = s0_ref[0].reshape(st_ref.shape)

    yield
    ext[CARRY:EXT, :] = xbc_ref[...]
    acc = cb_ref[...] + ext[pl.ds(CARRY - (CONV_W - 1), SEG), :] * cw_ref[0:1, :]
    for i in range(1, CONV_W):
        acc = acc + ext[pl.ds(CARRY - (CONV_W - 1) + i, SEG), :] * cw_ref[i:i + 1, :]
    ext[0:CARRY, :] = ext[SEG:EXT, :]
    conv = _silu(acc)

    dt = _softplus(dt_ref[...] + dtb_ref[...])
    a = _cumsum_rows(dt * (-jnp.exp(alog_ref[...])))
    a_t = a.T
    dt_t = dt.T
    a_last = a[SEG - 1:SEG, :]
    e_a = jnp.exp(a)
    w_col = dt * jnp.exp(a_last - a)
    e_last = jnp.exp(a_last)
    causal = lax.broadcasted_iota(jnp.int32, (SEG, SEG), 0) >= lax.broadcasted_iota(jnp.int32, (SEG, SEG), 1)
    lane_lo = lax.broadcasted_iota(jnp.int32, (SEG, 2 * SSD_P), 1) < SSD_P
    sub_lo = lax.broadcasted_iota(jnp.int32, (2 * SSD_P, SSD_N), 0) < SSD_P
    nbc = SSD_G * SSD_N

    def pair_cols(m, ja, jb):
        return jnp.where(lane_lo, jnp.broadcast_to(m[:, ja:ja + 1], (SEG, 2 * SSD_P)),
                         jnp.broadcast_to(m[:, jb:jb + 1], (SEG, 2 * SSD_P)))

    for g in range(SSD_G):
        j0 = LOW_DT0 + g * SSD_HG
        bc = conv[:, SSD_INNER + g * SSD_N:SSD_INNER + (g + 1) * SSD_N]
        cc = conv[:, SSD_INNER + nbc + g * SSD_N:SSD_INNER + nbc + (g + 1) * SSD_N]
        cbm = _dot_nt(cc, bc)

        def w_intra(j):
            seg = a[:, j:j + 1] - a_t[j:j + 1, :]
            return cbm * jnp.exp(jnp.where(causal, seg, -jnp.inf)) * dt_t[j:j + 1, :]

        ys = []
        for p in range(SSD_HG // 2):
            ja, jb = j0 + 2 * p, j0 + 2 * p + 1
            pp = g * (SSD_HG // 2) + p
            xp = conv[:, pp * 2 * SSD_P:(pp + 1) * 2 * SSD_P]
            y_intra = jnp.where(lane_lo, _dot(w_intra(ja), xp), _dot(w_intra(jb), xp))
            sp = st_ref[pp]
            y_inter = _dot_nt(cc, sp) * pair_cols(e_a, ja, jb)
            ds = _dot_tn(xp * pair_cols(w_col, ja, jb), bc)
            rs = jnp.where(sub_lo, jnp.broadcast_to(e_last[:, ja:ja + 1], (2 * SSD_P, SSD_N)),
                           jnp.broadcast_to(e_last[:, jb:jb + 1], (2 * SSD_P, SSD_N)))
            st_ref[pp] = sp * rs + ds
            ys.append(y_intra + y_inter + dvec_ref[:, pp * 2 * SSD_P:(pp + 1) * 2 * SSD_P] * xp)
        gw = slice(g * SSD_GW, (g + 1) * SSD_GW)
        y = jnp.concatenate(ys, axis=1) * _silu(z_ref[:, gw])
        o_ref[:, gw] = _rms(y, nw_ref[:, gw]).astype(bf16)

    yield
    @pl.when(jnp.logical_and(is_p, pos == SEG_PER_PSEQ - 1))
    def _():
        stp_ref[0] = st_ref[...].reshape(stp_ref.shape[1:])
        cvp_ref[0] = xbc_ref[SEG - (CONV_W - 1):SEG, :]

    @pl.when(jnp.logical_not(is_p))
    def _():
        sts_ref[0] = st_ref[...].reshape(sts_ref.shape[1:])
        cvs_ref[0] = xbc_ref[SEG - (CONV_W - 1):SEG, :]


def _ssd_call(p, conv_prev, conv_w, conv_b, dtb, alog, dvec, norm_w, s0):
    st_block = (1, SSD_H, SSD_P, SSD_N)
    cv_block = (1, CONV_W - 1, SSD_CONV_DIM)
    full = lambda c: (0, 0)
    return dict(
        in_specs=[pl.BlockSpec((SEG, SSD_CONV_DIM), lambda c: (c, C_XBC // SSD_CONV_DIM)),
                  pl.BlockSpec((SEG, SSD_INNER), lambda c: (c, C_Z // SSD_INNER)),
                  pl.BlockSpec((SEG, LANE), lambda c: (c, C_LOW // LANE)),
                  pl.BlockSpec(cv_block, lambda c: (_sample_seq(c), 0, 0)),
                  pl.BlockSpec((CONV_W, SSD_CONV_DIM), full),
                  pl.BlockSpec((1, SSD_CONV_DIM), full),
                  pl.BlockSpec((1, LANE), full),
                  pl.BlockSpec((1, LANE), full),
                  pl.BlockSpec((1, SSD_INNER), full),
                  pl.BlockSpec((1, SSD_INNER), full),
                  pl.BlockSpec(st_block, lambda c: (_sample_seq(c), 0, 0, 0))],
        out_specs=[pl.BlockSpec((SEG, SSD_INNER), lambda c: (c, 0)),
                   pl.BlockSpec(st_block, lambda c: (_prompt_seq(c), 0, 0, 0)),
                   pl.BlockSpec(st_block, lambda c: (_sample_seq(c), 0, 0, 0)),
                   pl.BlockSpec(cv_block, lambda c: (_prompt_seq(c), 0, 0)),
                   pl.BlockSpec(cv_block, lambda c: (_sample_seq(c), 0, 0))],
        out_shape=[jax.ShapeDtypeStruct((N_TOK, SSD_INNER), bf16),
                   jax.ShapeDtypeStruct((B_P, SSD_H, SSD_P, SSD_N), f32),
                   jax.ShapeDtypeStruct((B_S, SSD_H, SSD_P, SSD_N), f32),
                   jax.ShapeDtypeStruct((B_P, CONV_W - 1, SSD_CONV_DIM), f32),
                   jax.ShapeDtypeStruct((B_S, CONV_W - 1, SSD_CONV_DIM), f32)],
        scratch_shapes=[pltpu.VMEM((EXT, SSD_CONV_DIM), f32),
                        pltpu.VMEM((SSD_H // 2, 2 * SSD_P, SSD_N), f32)],
        args=(p, p, p, conv_prev, conv_w, conv_b.reshape(1, -1), dtb, alog, dvec, norm_w.reshape(1, -1), s0))


def _mixers(gla, ssd):
    n_in = (len(gla["in_specs"]), len(ssd["in_specs"]))
    n_out = (len(gla["out_specs"]), len(ssd["out_specs"]))
    n_scr = (len(gla["scratch_shapes"]), len(ssd["scratch_shapes"]))

    def body(*refs):
        ins, outs, scr = refs[:sum(n_in)], refs[sum(n_in):sum(n_in) + sum(n_out)], refs[sum(n_in) + sum(n_out):]
        parts = [_gla_kernel(*ins[:n_in[0]], *outs[:n_out[0]], *scr[:n_scr[0]]),
                 _ssd_kernel(*ins[n_in[0]:], *outs[n_out[0]:], *scr[n_scr[0]:])]
        for _ in range(3):
            for part in parts:
                next(part, None)

    outs = pl.pallas_call(
        body,
        grid=(NSEG,),
        in_specs=gla["in_specs"] + ssd["in_specs"],
        out_specs=gla["out_specs"] + ssd["out_specs"],
        out_shape=gla["out_shape"] + ssd["out_shape"],
        scratch_shapes=gla["scratch_shapes"] + ssd["scratch_shapes"],
        compiler_params=_cparams(("arbitrary",)),
        name="token_mixers",
    )(*gla["args"], *ssd["args"])
    return outs[:n_out[0]], outs[n_out[0]:]


MIX_TB = 512


def _mix_kernel(oa_ref, yb_ref, gt_ref, x_ref, ga_ref, sc_ref, sh_ref, nw_ref, wa_ref, wb_ref, wo_ref,
                x1_ref, h2_ref, h2t_ref):
    br_a = jnp.dot(oa_ref[...], wa_ref[...], preferred_element_type=f32)
    br_b = jnp.dot(yb_ref[...], wb_ref[...], preferred_element_type=f32)
    g_a = jax.nn.sigmoid(gt_ref[:, 0:D])
    g_b = jax.nn.sigmoid(gt_ref[:, D:2 * D])
    y = _dot(g_a * br_a + g_b * br_b, wo_ref[...])
    for s in range(MIX_TB // SEG):
        r = slice(s * SEG, (s + 1) * SEG)
        x1 = x_ref[r, :] + ga_ref[s:s + 1, :] * y[r, :]
        x1_ref[r, :] = x1
        h2_ref[r, :] = (_rms(x1, nw_ref[...]) * (1.0 + sc_ref[s:s + 1, :]) + sh_ref[s:s + 1, :]).astype(bf16)
    for g in range(MIX_TB // MXU):
        h2t_ref[g] = h2_ref[g * MXU:(g + 1) * MXU, :].T


def _mix(oa, yb, p, x, mod_seg, norm2_w, wa, wb, wo):
    nseg_b = MIX_TB // SEG
    full = lambda i: (0, 0)
    return pl.pallas_call(
        _mix_kernel,
        grid=(N_TOK // MIX_TB,),
        in_specs=[pl.BlockSpec((MIX_TB, D), lambda i: (i, 0)),
                  pl.BlockSpec((MIX_TB, SSD_INNER), lambda i: (i, 0)),
                  pl.BlockSpec((MIX_TB, 2 * D), lambda i: (i, C_GATE // (2 * D))),
                  pl.BlockSpec((MIX_TB, D), lambda i: (i, 0)),
                  pl.BlockSpec((nseg_b, D), lambda i: (i, 2)),
                  pl.BlockSpec((nseg_b, D), lambda i: (i, 4)),
                  pl.BlockSpec((nseg_b, D), lambda i: (i, 3)),
                  pl.BlockSpec((1, D), full),
                  pl.BlockSpec((D, D), full),
                  pl.BlockSpec((SSD_INNER, D), full),
                  pl.BlockSpec((D, D), full)],
        out_specs=[pl.BlockSpec((MIX_TB, D), lambda i: (i, 0)),
                   pl.BlockSpec((MIX_TB, D), lambda i: (i, 0)),
                   pl.BlockSpec((MIX_TB // MXU, D, MXU), lambda i: (i, 0, 0))],
        out_shape=[jax.ShapeDtypeStruct((N_TOK, D), f32),
                   jax.ShapeDtypeStruct((N_TOK, D), bf16),
                   jax.ShapeDtypeStruct((N_TOK // MXU, D, MXU), bf16)],
        compiler_params=_cparams(("parallel",)),
        name="mix_out",
    )(oa, yb, p, x, mod_seg, mod_seg, mod_seg, norm2_w.reshape(1, D), wa, wb, wo)


RT_TL = 128
HALF = PEER_DQ // 2


RT_HPS = 8


def _batcher_sort_net(n):
    def merge(lo, hi, r):
        step = 2 * r
        if step < hi - lo:
            yield from merge(lo, hi, step)
            yield from merge(lo + r, hi, step)
            yield from ((i, i + r) for i in range(lo + r, hi - r, step))
        else:
            yield (lo, lo + r)

    def sort(lo, hi):
        if hi - lo >= 1:
            mid = lo + (hi - lo) // 2
            yield from sort(lo, mid)
            yield from sort(mid + 1, hi)
            yield from merge(lo, hi, 1)

    return tuple(sort(0, n - 1))


def _bitonic_merge_net(n):
    net, d = [], n // 2
    while d >= 1:
        net += [(i, i + d) for i in range(n) if (i // d) % 2 == 0]
        d //= 2
    return tuple(net)


_SORT16 = _batcher_sort_net(PEER_TOPK)
_MERGE16 = _bitonic_merge_net(PEER_TOPK)
N_CAND_VREGS = 10


def _compare_exchange(x, net):
    for i, j in net:
        x[i], x[j] = jnp.maximum(x[i], x[j]), jnp.minimum(x[i], x[j])


def _merge_across_sublanes(x, n_valid):
    for shift in (4, 2, 1):
        y = [pltpu.roll(v, shift, 0) for v in x]
        merged = []
        for k in range(PEER_TOPK):
            a = x[k] if k < n_valid else None
            b = y[PEER_TOPK - 1 - k] if PEER_TOPK - 1 - k < n_valid else None
            merged.append(jnp.maximum(a, b) if (a is not None and b is not None) else (a if b is None else b))
        x = merged
        _compare_exchange(x, _MERGE16)
        n_valid = PEER_TOPK
    return x


def _top16_sorted(s):
    x = [s[SUBLANES * k:SUBLANES * (k + 1), :] for k in range(PEER_NK // SUBLANES)]
    _compare_exchange(x, _SORT16)
    return _merge_across_sublanes(x, PEER_TOPK)


def _route_kernel(q_ref, k1_ref, k2_ref, thr_ref, s2_ref, f1_ref, f2_ref):
    sub = lax.broadcasted_iota(jnp.int32, (SUBLANES, RT_TL), 0)
    ninf = jnp.float32(-jnp.inf)

    def by_sublane(vs):
        out = vs[0]
        for r in range(1, SUBLANES):
            out = jnp.where(sub == r, vs[r], out)
        return out

    for hh in range(RT_HPS):
        s1 = _dot3_nt(k1_ref[hh], q_ref[:, hh * PEER_DQ:hh * PEER_DQ + HALF])
        s2 = _dot3_nt(k2_ref[hh], q_ref[:, hh * PEER_DQ + HALF:(hh + 1) * PEER_DQ])
        s2_ref[hh] = s2
        v1 = _top16_sorted(s1)
        v2 = _top16_sorted(s2)
        v2lo, v2hi, v1hi = by_sublane(v2[:8]), by_sublane(v2[8:]), by_sublane(v1[8:])
        cands = [v1[0] + v2lo, v1[0] + v2hi, v1[1] + v2lo]
        for i in range(2, 8):
            cands.append(jnp.where(sub < PEER_TOPK // (i + 1), v1[i] + v2lo, ninf))
        cands.append(v1hi + v2[0])
        assert len(cands) == N_CAND_VREGS
        _compare_exchange(cands, tuple((i, j) for i, j in _SORT16 if j < N_CAND_VREGS))
        top = _merge_across_sublanes(cands, N_CAND_VREGS)
        zsum = jnp.ones_like(top[0])
        for k in range(1, PEER_TOPK):
            zsum = zsum + jnp.exp(top[k] - top[0])
        tau = top[PEER_TOPK - 1]
        for k in range(PEER_NK // SUBLANES):
            rows = slice(SUBLANES * k, SUBLANES * (k + 1))
            s1k = s1[rows, :]
            thr = jnp.full_like(s1k, jnp.inf)
            for j in range(PEER_TOPK):
                thr = jnp.where(s1k + v2[j] >= tau, v2[j], thr)
            thr_ref[hh, rows, :] = thr
        f1_ref[hh] = jnp.exp(s1 - v1[0][0:1, :]) * (0.5 / zsum[0:1, :])
        f2_ref[hh] = jnp.exp(s2 - v2[0][0:1, :])


def _route(q, keys1, keys2):
    tile = pl.BlockSpec((RT_HPS, PEER_NK, RT_TL), lambda i, h: (h, 0, i))
    big = jax.ShapeDtypeStruct((PEER_H, PEER_NK, N_TOK), f32)
    return pl.pallas_call(
        _route_kernel,
        grid=(N_TOK // RT_TL, PEER_H // RT_HPS),
        in_specs=[pl.BlockSpec((RT_TL, RT_HPS * PEER_DQ), lambda i, h: (i, h)),
                  pl.BlockSpec((RT_HPS, PEER_NK, HALF), lambda i, h: (h, 0, 0)),
                  pl.BlockSpec((RT_HPS, PEER_NK, HALF), lambda i, h: (h, 0, 0))],
        out_specs=[tile, tile, tile, tile],
        out_shape=[big, big, big, big],
        compiler_params=_cparams(("parallel", "parallel")),
        name="peer_route",
    )(q, keys1, keys2)


PE_TB = 512
PE_NA = 4
PE_AG = 2
PE_CH = PE_AG * PE_NA * PEER_NK
PE_NCH = PEER_NK * PEER_NK // PE_CH
PE_BH = 32
SQRT_HALF = 0.7071067811865476
MXU = 256


def _peer_kernel(*refs, final):
    if final:
        (u0_ref, u_ref, vt_ref, xt_ref, thr_ref, f1_ref, s2_ref, f2_ref, x1_ref, ga_ref, fw_ref,
         o_ref, acc_ref, pw_ref, gel_ref, rthr_ref, rf1_ref) = refs
    else:
        (u0_ref, u_ref, vt_ref, xt_ref, thr_ref, f1_ref, s2_ref, f2_ref, x1_ref, ga_ref,
         o_ref, acc_ref, pw_ref, gel_ref, rthr_ref, rf1_ref) = refs
    j = pl.program_id(1)

    n_half = PE_TB // MXU

    def two_gelu(act):
        return act * (1.0 + lax.erf(act * SQRT_HALF))

    rows_per_step = PE_AG * PE_NA

    def prep_rows(step, buf, ias):
        for ia in ias:
            a = jnp.minimum(step * rows_per_step + ia, PEER_NK - 1)
            for h in range(PEER_H):
                rthr_ref[buf, ia * PEER_H + h] = jnp.broadcast_to(thr_ref[h, pl.ds(a, 1), :], (SUBLANES, PE_TB))
                rf1_ref[buf, ia * PEER_H + h] = jnp.broadcast_to(f1_ref[h, pl.ds(a, 1), :], (SUBLANES, PE_TB))

    @pl.when(j == 0)
    def _():
        acc_ref[...] = jnp.zeros_like(acc_ref)
        pw_ref[1] = jnp.zeros(pw_ref.shape[1:], bf16)
        for g in range(n_half):
            gel_ref[0, g] = two_gelu(jnp.dot(u0_ref[0], xt_ref[g], preferred_element_type=f32))
        prep_rows(0, 0, range(rows_per_step))

    cur = j % 2
    prev = 1 - cur

    nsub = PE_BH // SUBLANES
    n_bh = PEER_NK // PE_BH

    def weight_tile(g, lq, bh, ag):
        lanes = pl.ds(pl.multiple_of(g * MXU + lq * LANE, LANE), LANE)
        lq_lanes = slice(lq * LANE, (lq + 1) * LANE)
        rows = slice(bh * PE_BH, (bh + 1) * PE_BH)
        ws = [jnp.zeros((nsub, SUBLANES, LANE), f32) for _ in range(PE_NA)]
        for h in range(PEER_H):
            s2t = s2_ref[h, rows, lanes].reshape(nsub, SUBLANES, LANE)
            f2t = f2_ref[h, rows, lanes].reshape(nsub, SUBLANES, LANE)
            for i in range(PE_NA):
                row = (ag * PE_NA + i) * PEER_H + h
                sel = s2t >= rthr_ref[cur, row, :, lanes][None]
                ws[i] = ws[i] + jnp.where(sel, f2t * rf1_ref[cur, row, :, lanes][None], 0.0)
        for i in range(PE_NA):
            e0 = (ag * PE_NA + i) * PEER_NK + bh * PE_BH
            pw_ref[cur, g, e0:e0 + PE_BH, lq_lanes] = (
                ws[i].reshape(PE_BH, LANE) * gel_ref[cur, g, e0:e0 + PE_BH, lq_lanes]).astype(bf16)

    def stage(p, carry, with_act):
        if with_act:
            gel_ref[prev, p] = two_gelu(jnp.dot(u_ref[0], xt_ref[p], preferred_element_type=f32))
            per_stage = rows_per_step // n_half
            prep_rows(j + 1, prev, [p * per_stage + k for k in range(per_stage)])
        acc_ref[p] += jnp.dot(vt_ref[0], pw_ref[prev, p], preferred_element_type=f32)
        for lq in range(MXU // LANE):
            for bh in range(n_bh):
                for ag in range(PE_AG):
                    weight_tile(p, lq, bh, ag)
        return carry

    @pl.when(j < PE_NCH - 1)
    def _():
        lax.fori_loop(0, n_half, functools.partial(stage, with_act=True), 0)

    @pl.when(j == PE_NCH - 1)
    def _():
        lax.fori_loop(0, n_half, functools.partial(stage, with_act=False), 0)

    @pl.when(j == PE_NCH)
    def _():
        for g in range(n_half):
            out = (acc_ref[g] + jnp.dot(vt_ref[0], pw_ref[prev, g], preferred_element_type=f32)).T
            for s in range(MXU // SEG):
                r = slice(g * MXU + s * SEG, g * MXU + (s + 1) * SEG)
                seg = g * (MXU // SEG) + s
                x2 = x1_ref[r, :] + ga_ref[seg:seg + 1, :] * out[s * SEG:(s + 1) * SEG, :]
                o_ref[r, :] = _rms(x2, fw_ref[...]) if final else x2


def _peer(u_c, vt_c, h2t, thr, f1t, s2t, f2t, x1, mod_seg, final_w):
    final = final_w is not None
    nseg_b = PE_TB // SEG
    rt = pl.BlockSpec((PEER_H, PEER_NK, PE_TB), lambda i, j: (0, 0, i))
    in_specs = [pl.BlockSpec((1, PE_CH, D), lambda i, j: (0, 0, 0)),
                pl.BlockSpec((1, PE_CH, D), lambda i, j: (jnp.minimum(j + 1, PE_NCH - 1), 0, 0)),
                pl.BlockSpec((1, D, PE_CH), lambda i, j: (jnp.maximum(j - 1, 0), 0, 0)),
                pl.BlockSpec((PE_TB // MXU, D, MXU), lambda i, j: (i, 0, 0)),
                rt, rt, rt, rt,
                pl.BlockSpec((PE_TB, D), lambda i, j: (i, 0)),
                pl.BlockSpec((nseg_b, D), lambda i, j: (i, 5))]
    args = [u_c, u_c, vt_c, h2t, thr, f1t, s2t, f2t, x1, mod_seg]
    if final:
        in_specs.append(pl.BlockSpec((1, D), lambda i, j: (0, 0)))
        args.append(final_w.reshape(1, D))
    return pl.pallas_call(
        functools.partial(_peer_kernel, final=final),
        grid=(N_TOK // PE_TB, PE_NCH + 1),
        in_specs=in_specs,
        out_specs=pl.BlockSpec((PE_TB, D), lambda i, j: (i, 0)),
        out_shape=jax.ShapeDtypeStruct((N_TOK, D), f32),
        scratch_shapes=[pltpu.VMEM((PE_TB // MXU, D, MXU), f32), pltpu.VMEM((2, PE_TB // MXU, PE_CH, MXU), bf16),
                        pltpu.VMEM((2, PE_TB // MXU, PE_CH, MXU), f32),
                        pltpu.VMEM((2, PE_AG * PE_NA * PEER_H, SUBLANES, PE_TB), f32),
                        pltpu.VMEM((2, PE_AG * PE_NA * PEER_H, SUBLANES, PE_TB), f32)],
        compiler_params=_cparams(("parallel", "arbitrary")),
        name="peer_experts_final" if final else "peer_experts",
    )(*args)


def _reorder_w_in(w):
    o = np.cumsum([0, 512, 512, 1024, 1024, GLA_RANK, SSD_INNER, SSD_CONV_DIM, SSD_H, 2 * D])
    q_k_v_go = w[:, o[0]:o[4]]
    gk_low = w[:, o[4]:o[5]]
    z = w[:, o[5]:o[6]]
    xbc = w[:, o[6]:o[7]]
    dt = w[:, o[7]:o[8]]
    gates = w[:, o[8]:o[9]]
    pad = lambda m: jnp.pad(m, ((0, 0), (0, LANE - m.shape[1])))
    low = jnp.concatenate([gk_low, jnp.zeros((D, LOW_DT0 - GLA_RANK), w.dtype), dt], axis=1)
    return jnp.concatenate([q_k_v_go, xbc, z, gates, pad(low)], axis=1).astype(bf16)


def kernel(x_prompt, x_sample, state_gla, state_ssd, state_conv, c_prompt, c_sample, w_ada, b_ada, norm1_w, w_in, gla_gk_w2, gla_gk_b, gla_norm_w, gla_proj, ssd_conv_w, ssd_conv_b, ssd_dt_bias, ssd_A_log, ssd_D, ssd_norm_w, ssd_proj, w_out, norm2_w, peer_wq, peer_keys1, peer_keys2, peer_u, peer_v, final_norm_w):
    x = jnp.concatenate([x_prompt.reshape(N_P, D), x_sample.reshape(N_S, D)], axis=0)
    c_all = jnp.concatenate([c_prompt, c_sample], axis=0)
    mod = _ada(c_all, w_ada, b_ada)
    seg2seq = np.concatenate([np.repeat(np.arange(B_P), SEG_PER_PSEQ), B_P + np.arange(B_S)])

    gla_st, ssd_st, conv_st = [], [], []
    for l in range(DEPTH):
        mod_seg = mod[l][seg2seq]
        h = _normmod(x, norm1_w[l], mod_seg, 1, 0)
        p = _mm(h, _reorder_w_in(w_in[l]), 1024, P_COLS // 3, "in_proj")

        w2pad = jnp.pad(gla_gk_w2[l], ((0, LANE - GLA_RANK), (0, 0)))
        low_lanes = lambda m: jnp.pad(m.reshape(1, SSD_H), ((0, 0), (LOW_DT0, LANE - LOW_DT0 - SSD_H)))
        dvec = jnp.repeat(ssd_D[l], SSD_P).reshape(1, SSD_INNER)
        (oa, gla_p, gla_s), (yb, ssd_p, ssd_s, conv_p, conv_s) = _mixers(
            _gla_call(p, w2pad, gla_gk_b[l], gla_norm_w[l], state_gla[l]),
            _ssd_call(p, state_conv[l], ssd_conv_w[l], ssd_conv_b[l], low_lanes(ssd_dt_bias[l]),
                      low_lanes(ssd_A_log[l]), dvec, ssd_norm_w[l], state_ssd[l]))

        x1, h2, h2t = _mix(oa, yb, p, x, mod_seg, norm2_w[l], gla_proj[l].astype(bf16),
                           ssd_proj[l].astype(bf16), w_out[l].astype(bf16))
        q = _mm(h2, peer_wq[l].astype(bf16), 512, PEER_H * PEER_DQ, "peer_query")
        thr, s2t, f1t, f2t = _route(q, peer_keys1[l], peer_keys2[l])
        u_c = peer_u[l].astype(bf16).reshape(PE_NCH, PE_CH, D)
        vt_c = peer_v[l].astype(bf16).reshape(PE_NCH, PE_CH, D).transpose(0, 2, 1)
        x = _peer(u_c, vt_c, h2t, thr, f1t, s2t, f2t, x1, mod_seg,
                  final_norm_w if l == DEPTH - 1 else None)

        gla_st.append((gla_p, gla_s))
        ssd_st.append((ssd_p, ssd_s))
        conv_st.append((conv_p, conv_s))

    y_prompt = x[:N_P].reshape(B_P, T_P, D)
    y_sample = x[N_P:].reshape(B_S, T_S, D)
    stack = lambda pairs, k: jnp.stack([pr[k] for pr in pairs])
    return (y_prompt, y_sample, stack(gla_st, 0), stack(ssd_st, 0), stack(conv_st, 0),
            stack(gla_st, 1), stack(ssd_st, 1), stack(conv_st, 1))
```

```python
import functools

import jax
import jax.numpy as jnp
import numpy as np
from jax import lax
from jax.experimental import pallas as pl
from jax.experimental.pallas import tpu as pltpu

f32 = jnp.float32
bf16 = jnp.bfloat16

D = 1024
DEPTH = 2
B_P, T_P = 8, 2048
B_S, T_S = 32, 64
N_P = B_P * T_P
N_S = B_S * T_S
N_TOK = N_P + N_S
SEG = 64
NSEG = N_TOK // SEG
NSEG_P = N_P // SEG
SEG_PER_PSEQ = T_P // SEG
EPS = 1e-6

GLA_H, GLA_DK, GLA_DV = 4, 128, 256
GLA_RANK = 16
GLA_GATE_NORM = 16.0
SSD_INNER = 2048
SSD_P = 64
SSD_H = 32
SSD_G = 4
SSD_N = 128
SSD_HG = SSD_H // SSD_G
SSD_GW = SSD_INNER // SSD_G
CONV_W = 4
SSD_CONV_DIM = SSD_INNER + 2 * SSD_G * SSD_N

PEER_H = 8
PEER_NK = 128
PEER_DQ = 256
PEER_TOPK = 16

C_Q, C_K, C_V, C_GO = 0, 512, 1024, 2048
C_XBC, C_Z, C_GATE = 3072, 6144, 8192
C_LOW = 10240
LOW_DT0 = 32
P_COLS = C_LOW + 128

LANE = 128
SUBLANES = 8
VMEM_LIMIT = 56 * 1024 * 1024


def _cparams(sem):
    return pltpu.CompilerParams(dimension_semantics=sem, vmem_limit_bytes=VMEM_LIMIT)


def _dot(a, b):
    return jnp.dot(a.astype(bf16), b.astype(bf16), preferred_element_type=f32)


def _dot_nt(a, b):
    return lax.dot_general(a.astype(bf16), b.astype(bf16), (((1,), (1,)), ((), ())), preferred_element_type=f32)


def _dot_tn(a, b):
    return lax.dot_general(a.astype(bf16), b.astype(bf16), (((0,), (0,)), ((), ())), preferred_element_type=f32)


def _split3(x):
    hi = x.astype(bf16)
    r = x - hi.astype(f32)
    mid = r.astype(bf16)
    lo = (r - mid.astype(f32)).astype(bf16)
    return hi, mid, lo


def _dot3(a, b):
    ah, am, _ = _split3(a)
    bh, bm, _ = _split3(b)
    d = functools.partial(jnp.dot, preferred_element_type=f32)
    return d(ah, bh) + (d(ah, bm) + d(am, bh))


def _dot3_nt(a, b):
    ah, am, _ = _split3(a)
    bh, bm, _ = _split3(b)
    d = functools.partial(lax.dot_general, dimension_numbers=(((1,), (1,)), ((), ())), preferred_element_type=f32)
    return d(ah, bh) + (d(ah, bm) + d(am, bh))


def _cumsum_rows(x):
    n = x.shape[0]
    tri = (lax.broadcasted_iota(jnp.int32, (n, n), 0) >= lax.broadcasted_iota(jnp.int32, (n, n), 1)).astype(bf16)
    hi, mid, lo = _split3(x)
    d = functools.partial(jnp.dot, preferred_element_type=f32)
    return d(tri, hi) + (d(tri, mid) + d(tri, lo))


def _silu(x):
    return x * jax.nn.sigmoid(x)


def _softplus(x):
    return jnp.maximum(x, 0.0) + jnp.log1p(jnp.exp(-jnp.abs(x)))


def _rms(x, w):
    return x * lax.rsqrt(jnp.mean(x * x, axis=-1, keepdims=True) + EPS) * w


def _ada_kernel(c_ref, w_ref, b_ref, o_ref):
    o_ref[0] = _dot3(_silu(c_ref[...]), w_ref[0]) + b_ref[0]


def _ada(c_all, w_ada, b_ada):
    nb = c_all.shape[0]
    tn = 1536
    return pl.pallas_call(
        _ada_kernel,
        grid=(DEPTH, 6 * D // tn),
        in_specs=[pl.BlockSpec((nb, D), lambda l, j: (0, 0)),
                  pl.BlockSpec((1, D, tn), lambda l, j: (l, 0, j)),
                  pl.BlockSpec((1, 1, tn), lambda l, j: (l, 0, j))],
        out_specs=pl.BlockSpec((1, nb, tn), lambda l, j: (l, 0, j)),
        out_shape=jax.ShapeDtypeStruct((DEPTH, nb, 6 * D), f32),
        compiler_params=_cparams(("parallel", "parallel")),
        name="ada_mod",
    )(c_all, w_ada, b_ada.reshape(DEPTH, 1, 6 * D))


NM_TB = 512


def _normmod_kernel(x_ref, w_ref, sc_ref, sh_ref, o_ref):
    for s in range(NM_TB // SEG):
        r = slice(s * SEG, (s + 1) * SEG)
        y = _rms(x_ref[r, :], w_ref[...])
        o_ref[r, :] = (y * (1.0 + sc_ref[s:s + 1, :]) + sh_ref[s:s + 1, :]).astype(bf16)


def _normmod(x, w, mod_seg, sc_col, sh_col):
    nseg_b = NM_TB // SEG
    return pl.pallas_call(
        _normmod_kernel,
        grid=(N_TOK // NM_TB,),
        in_specs=[pl.BlockSpec((NM_TB, D), lambda i: (i, 0)),
                  pl.BlockSpec((1, D), lambda i: (0, 0)),
                  pl.BlockSpec((nseg_b, D), lambda i: (i, sc_col)),
                  pl.BlockSpec((nseg_b, D), lambda i: (i, sh_col))],
        out_specs=pl.BlockSpec((NM_TB, D), lambda i: (i, 0)),
        out_shape=jax.ShapeDtypeStruct((N_TOK, D), bf16),
        compiler_params=_cparams(("parallel",)),
        name="norm_mod",
    )(x, w.reshape(1, D), mod_seg, mod_seg)


def _mm_kernel(a_ref, b_ref, o_ref):
    o_ref[...] = jnp.dot(a_ref[...], b_ref[...], preferred_element_type=f32)


def _mm(a, b, tm, tn, name):
    m, k = a.shape
    n = b.shape[1]
    return pl.pallas_call(
        _mm_kernel,
        grid=(n // tn, m // tm),
        in_specs=[pl.BlockSpec((tm, k), lambda j, i: (i, 0)),
                  pl.BlockSpec((k, tn), lambda j, i: (0, j))],
        out_specs=pl.BlockSpec((tm, tn), lambda j, i: (i, j)),
        out_shape=jax.ShapeDtypeStruct((m, n), f32),
        compiler_params=_cparams(("parallel", "parallel")),
        name=name,
    )(a, b)


NSTREAM = 2
STEPS_P = NSEG_P // NSTREAM
STEPS_S = (NSEG - NSEG_P) // NSTREAM
HALF_TOK = N_TOK // NSTREAM


def _is_prompt(c):
    return c < STEPS_P


def _sample_seq(c):
    return jnp.maximum(c - STEPS_P, 0)


def _prompt_seq(c):
    return jnp.minimum(c // SEG_PER_PSEQ, B_P // NSTREAM - 1)


LOADED, STEP, COMPUTED = "loaded", "step", "computed"


def _segment(c, r):
    return jnp.where(c < STEPS_P, r * STEPS_P + c, NSEG_P + r * STEPS_S + (c - STEPS_P))


def _gla_kernel(r, q_ref, k_ref, v_ref, g_ref, low_ref, w2_ref, b2_ref, nw_ref, s0_ref,
                o_ref, stp_ref, sts_ref, st_ref):
    c = pl.program_id(0)
    is_p = _is_prompt(c)
    pos = c % SEG_PER_PSEQ

    @pl.when(jnp.logical_and(is_p, pos == 0))
    def _():
        st_ref[r] = jnp.zeros(st_ref.shape[1:], f32)

    @pl.when(jnp.logical_not(is_p))
    def _():
        for h in range(GLA_H):
            st_ref[r, h] = s0_ref[r, 0, h].T

    yield LOADED
    pre = _dot3(low_ref[...], w2_ref[...]) + b2_ref[...]
    gk = -_softplus(-pre) / GLA_GATE_NORM
    b = _cumsum_rows(gk)
    b_last = b[SEG - 1:SEG, :]
    e_b = jnp.exp(b)
    e_nb = jnp.exp(-b)
    e_end = jnp.exp(b_last - b)
    e_last = jnp.exp(b_last)
    causal = lax.broadcasted_iota(jnp.int32, (SEG, SEG), 0) >= lax.broadcasted_iota(jnp.int32, (SEG, SEG), 1)
    for h in range(GLA_H):
        ks = slice(h * GLA_DK, (h + 1) * GLA_DK)
        vs = slice(h * GLA_DV, (h + 1) * GLA_DV)
        k = k_ref[:, ks]
        q_t = (q_ref[:, ks] * (GLA_DK ** -0.5)) * e_b[:, ks]
        att = jnp.where(causal, _dot_nt(q_t, k * e_nb[:, ks]), 0.0)
        v = v_ref[:, vs]
        st = st_ref[r, h]
        o = _dot(att, v) + _dot_nt(q_t, st)
        st_ref[r, h] = st * e_last[:, ks] + _dot_tn(v, k * e_end[:, ks])
        o_ref[r, :, vs] = (_rms(o, nw_ref[...]) * _silu(g_ref[:, vs])).astype(bf16)
        yield STEP

    yield COMPUTED
    @pl.when(jnp.logical_and(is_p, pos == SEG_PER_PSEQ - 1))
    def _():
        for h in range(GLA_H):
            stp_ref[r, 0, h] = st_ref[r, h].T

    @pl.when(jnp.logical_not(is_p))
    def _():
        for h in range(GLA_H):
            sts_ref[r, 0, h] = st_ref[r, h].T


def _gla_call(p, w2pad, b2, norm_w, s0):
    st_dims = (GLA_H, GLA_DK, GLA_DV)
    st_block = (NSTREAM, 1) + st_dims
    kd, vd = GLA_H * GLA_DK, GLA_H * GLA_DV

    def stream_specs(r):
        return [pl.BlockSpec((SEG, kd), lambda c: (_segment(c, r), C_Q // kd)),
                pl.BlockSpec((SEG, kd), lambda c: (_segment(c, r), C_K // kd)),
                pl.BlockSpec((SEG, vd), lambda c: (_segment(c, r), C_V // vd)),
                pl.BlockSpec((SEG, vd), lambda c: (_segment(c, r), C_GO // vd)),
                pl.BlockSpec((SEG, LANE), lambda c: (_segment(c, r), C_LOW // LANE))]

    return dict(
        stream_specs=stream_specs, stream_args=(p, p, p, p, p),
        in_specs=[pl.BlockSpec((LANE, kd), lambda c: (0, 0)),
                  pl.BlockSpec((1, kd), lambda c: (0, 0)),
                  pl.BlockSpec((1, GLA_DV), lambda c: (0, 0)),
                  pl.BlockSpec(st_block, lambda c: (0, _sample_seq(c), 0, 0, 0))],
        args=(w2pad, b2.reshape(1, -1), norm_w.reshape(1, -1), s0.reshape((NSTREAM, B_S // NSTREAM) + st_dims)),
        out_specs=[pl.BlockSpec((NSTREAM, SEG, vd), lambda c: (0, c, 0)),
                   pl.BlockSpec(st_block, lambda c: (0, _prompt_seq(c), 0, 0, 0)),
                   pl.BlockSpec(st_block, lambda c: (0, _sample_seq(c), 0, 0, 0))],
        out_shape=[jax.ShapeDtypeStruct((NSTREAM, HALF_TOK, vd), bf16),
                   jax.ShapeDtypeStruct((NSTREAM, B_P // NSTREAM) + st_dims, f32),
                   jax.ShapeDtypeStruct((NSTREAM, B_S // NSTREAM) + st_dims, f32)],
        scratch_shapes=[pltpu.VMEM((NSTREAM, GLA_H, GLA_DV, GLA_DK), f32)])


CARRY = SUBLANES
EXT = SEG + CARRY


def _ssd_kernel(r, xbc_ref, z_ref, dt_ref, cw_ref, cb_ref, dtb_ref, alog_ref, dvec_ref, nw_ref, cp_all, s0_all,
                o_all, stp_all, sts_all, cvp_all, cvs_all, ext_all, st_all):
    cp_ref, s0_ref, o_ref = cp_all.at[r], s0_all.at[r], o_all.at[r]
    stp_ref, sts_ref, cvp_ref, cvs_ref = stp_all.at[r], sts_all.at[r], cvp_all.at[r], cvs_all.at[r]
    ext, st_ref = ext_all.at[r], st_all.at[r]
    c = pl.program_id(0)
    is_p = _is_prompt(c)
    pos = c % SEG_PER_PSEQ

    @pl.when(jnp.logical_and(is_p, pos == 0))
    def _():
        ext[0:CARRY, :] = jnp.zeros((CARRY, SSD_CONV_DIM), f32)
        st_ref[...] = jnp.zeros_like(st_ref)

    @pl.when(jnp.logical_not(is_p))
    def _():
        ext[0:CARRY, :] = jnp.zeros((CARRY, SSD_CONV_DIM), f32)
        ext[CARRY - (CONV_W - 1):CARRY, :] = cp_ref[0]
        st_ref[...] = s0_ref[0].reshape(st_ref.shape)

    yield LOADED
    ext[CARRY:EXT, :] = xbc_ref[...]
    acc = cb_ref[...] + ext[pl.ds(CARRY - (CONV_W - 1), SEG), :] * cw_ref[0:1, :]
    for i in range(1, CONV_W):
        acc = acc + ext[pl.ds(CARRY - (CONV_W - 1) + i, SEG), :] * cw_ref[i:i + 1, :]
    ext[0:CARRY, :] = ext[SEG:EXT, :]
    conv = _silu(acc)
    yield STEP

    dt = _softplus(dt_ref[...] + dtb_ref[...])
    a = _cumsum_rows(dt * (-jnp.exp(alog_ref[...])))
    a_t = a.T
    dt_t = dt.T
    a_last = a[SEG - 1:SEG, :]
    e_a = jnp.exp(a)
    w_col = dt * jnp.exp(a_last - a)
    e_last = jnp.exp(a_last)
    causal = lax.broadcasted_iota(jnp.int32, (SEG, SEG), 0) >= lax.broadcasted_iota(jnp.int32, (SEG, SEG), 1)
    lane_lo = lax.broadcasted_iota(jnp.int32, (SEG, 2 * SSD_P), 1) < SSD_P
    sub_lo = lax.broadcasted_iota(jnp.int32, (2 * SSD_P, SSD_N), 0) < SSD_P
    nbc = SSD_G * SSD_N

    def pair_cols(m, ja, jb):
        return jnp.where(lane_lo, jnp.broadcast_to(m[:, ja:ja + 1], (SEG, 2 * SSD_P)),
                         jnp.broadcast_to(m[:, jb:jb + 1], (SEG, 2 * SSD_P)))

    for g in range(SSD_G):
        j0 = LOW_DT0 + g * SSD_HG
        bc = conv[:, SSD_INNER + g * SSD_N:SSD_INNER + (g + 1) * SSD_N]
        cc = conv[:, SSD_INNER + nbc + g * SSD_N:SSD_INNER + nbc + (g + 1) * SSD_N]
        cbm = _dot_nt(cc, bc)

        def w_intra(j):
            seg = a[:, j:j + 1] - a_t[j:j + 1, :]
            return cbm * jnp.exp(jnp.where(causal, seg, -jnp.inf)) * dt_t[j:j + 1, :]

        ys = []
        for p in range(SSD_HG // 2):
            ja, jb = j0 + 2 * p, j0 + 2 * p + 1
            pp = g * (SSD_HG // 2) + p
            xp = conv[:, pp * 2 * SSD_P:(pp + 1) * 2 * SSD_P]
            y_intra = jnp.where(lane_lo, _dot(w_intra(ja), xp), _dot(w_intra(jb), xp))
            sp = st_ref[pp]
            y_inter = _dot_nt(cc, sp) * pair_cols(e_a, ja, jb)
            ds = _dot_tn(xp * pair_cols(w_col, ja, jb), bc)
            rs = jnp.where(sub_lo, jnp.broadcast_to(e_last[:, ja:ja + 1], (2 * SSD_P, SSD_N)),
                           jnp.broadcast_to(e_last[:, jb:jb + 1], (2 * SSD_P, SSD_N)))
            st_ref[pp] = sp * rs + ds
            ys.append(y_intra + y_inter + dvec_ref[:, pp * 2 * SSD_P:(pp + 1) * 2 * SSD_P] * xp)
            yield STEP
        gw = slice(g * SSD_GW, (g + 1) * SSD_GW)
        y = jnp.concatenate(ys, axis=1) * _silu(z_ref[:, gw])
        o_ref[:, gw] = _rms(y, nw_ref[:, gw]).astype(bf16)

    yield COMPUTED
    @pl.when(jnp.logical_and(is_p, pos == SEG_PER_PSEQ - 1))
    def _():
        stp_ref[0] = st_ref[...].reshape(stp_ref.shape[1:])
        cvp_ref[0] = xbc_ref[SEG - (CONV_W - 1):SEG, :]

    @pl.when(jnp.logical_not(is_p))
    def _():
        sts_ref[0] = st_ref[...].reshape(sts_ref.shape[1:])
        cvs_ref[0] = xbc_ref[SEG - (CONV_W - 1):SEG, :]


def _ssd_call(p, conv_prev, conv_w, conv_b, dtb, alog, dvec, norm_w, s0):
    st_dims = (SSD_H, SSD_P, SSD_N)
    cv_dims = (CONV_W - 1, SSD_CONV_DIM)
    st_block = (NSTREAM, 1) + st_dims
    cv_block = (NSTREAM, 1) + cv_dims
    full = lambda c: (0, 0)
    sample_blk = lambda c: (0, _sample_seq(c)) + (0,) * 3
    prompt_blk = lambda c: (0, _prompt_seq(c)) + (0,) * 3

    def stream_specs(r):
        return [pl.BlockSpec((SEG, SSD_CONV_DIM), lambda c: (_segment(c, r), C_XBC // SSD_CONV_DIM)),
                pl.BlockSpec((SEG, SSD_INNER), lambda c: (_segment(c, r), C_Z // SSD_INNER)),
                pl.BlockSpec((SEG, LANE), lambda c: (_segment(c, r), C_LOW // LANE))]

    return dict(
        stream_specs=stream_specs, stream_args=(p, p, p),
        in_specs=[pl.BlockSpec((CONV_W, SSD_CONV_DIM), full),
                  pl.BlockSpec((1, SSD_CONV_DIM), full),
                  pl.BlockSpec((1, LANE), full),
                  pl.BlockSpec((1, LANE), full),
                  pl.BlockSpec((1, SSD_INNER), full),
                  pl.BlockSpec((1, SSD_INNER), full),
                  pl.BlockSpec(cv_block, lambda c: sample_blk(c)[:4]),
                  pl.BlockSpec(st_block, sample_blk)],
        args=(conv_w, conv_b.reshape(1, -1), dtb, alog, dvec, norm_w.reshape(1, -1),
              conv_prev.reshape((NSTREAM, B_S // NSTREAM) + cv_dims), s0.reshape((NSTREAM, B_S // NSTREAM) + st_dims)),
        out_specs=[pl.BlockSpec((NSTREAM, SEG, SSD_INNER), lambda c: (0, c, 0)),
                   pl.BlockSpec(st_block, prompt_blk),
                   pl.BlockSpec(st_block, sample_blk),
                   pl.BlockSpec(cv_block, lambda c: prompt_blk(c)[:4]),
                   pl.BlockSpec(cv_block, lambda c: sample_blk(c)[:4])],
        out_shape=[jax.ShapeDtypeStruct((NSTREAM, HALF_TOK, SSD_INNER), bf16),
                   jax.ShapeDtypeStruct((NSTREAM, B_P // NSTREAM) + st_dims, f32),
                   jax.ShapeDtypeStruct((NSTREAM, B_S // NSTREAM) + st_dims, f32),
                   jax.ShapeDtypeStruct((NSTREAM, B_P // NSTREAM) + cv_dims, f32),
                   jax.ShapeDtypeStruct((NSTREAM, B_S // NSTREAM) + cv_dims, f32)],
        scratch_shapes=[pltpu.VMEM((NSTREAM, EXT, SSD_CONV_DIM), f32),
                        pltpu.VMEM((NSTREAM, SSD_H // 2, 2 * SSD_P, SSD_N), f32)])


def _mixers(gla, ssd):
    halves = (gla, ssd)
    kernels = (_gla_kernel, _ssd_kernel)
    n_str = [len(m["stream_args"]) for m in halves]
    n_in = [len(m["in_specs"]) for m in halves]
    n_out = [len(m["out_specs"]) for m in halves]
    n_scr = [len(m["scratch_shapes"]) for m in halves]

    def take(refs, counts):
        out, at = [], 0
        for n in counts:
            out.append(refs[at:at + n])
            at += n
        return out, refs[at:]

    def body(*refs):
        streams, rest = take(refs, [n for n in n_str for _ in range(NSTREAM)])
        shared, rest = take(rest, n_in)
        outs, rest = take(rest, n_out)
        scr, _ = take(rest, n_scr)
        parts = [kernels[m](r, *streams[m * NSTREAM + r], *shared[m], *outs[m], *scr[m])
                 for m in range(len(halves)) for r in range(NSTREAM)]
        for part in parts:
            assert next(part) == LOADED
        active = list(parts)
        while active:
            active = [part for part in active if next(part) != COMPUTED]
        for part in parts:
            next(part, None)

    outs = pl.pallas_call(
        body,
        grid=(STEPS_P + STEPS_S,),
        in_specs=([s for m in halves for r in range(NSTREAM) for s in m["stream_specs"](r)]
                  + gla["in_specs"] + ssd["in_specs"]),
        out_specs=gla["out_specs"] + ssd["out_specs"],
        out_shape=gla["out_shape"] + ssd["out_shape"],
        scratch_shapes=gla["scratch_shapes"] + ssd["scratch_shapes"],
        compiler_params=_cparams(("arbitrary",)),
        name="token_mixers",
    )(*[a for m in halves for _ in range(NSTREAM) for a in m["stream_args"]], *gla["args"], *ssd["args"])
    return outs[:n_out[0]], outs[n_out[0]:]


MIX_TB = 512


def _mix_kernel(oa_ref, yb_ref, gt_ref, x_ref, ga_ref, sc_ref, sh_ref, nw_ref, wa_ref, wb_ref, wo_ref,
                x1_ref, h2_ref, h2t_ref):
    br_a = jnp.dot(oa_ref[0], wa_ref[...], preferred_element_type=f32)
    br_b = jnp.dot(yb_ref[0], wb_ref[...], preferred_element_type=f32)
    g_a = jax.nn.sigmoid(gt_ref[:, 0:D])
    g_b = jax.nn.sigmoid(gt_ref[:, D:2 * D])
    y = _dot(g_a * br_a + g_b * br_b, wo_ref[...])
    for s in range(MIX_TB // SEG):
        r = slice(s * SEG, (s + 1) * SEG)
        x1 = x_ref[r, :] + ga_ref[s:s + 1, :] * y[r, :]
        x1_ref[r, :] = x1
        h2_ref[r, :] = (_rms(x1, nw_ref[...]) * (1.0 + sc_ref[s:s + 1, :]) + sh_ref[s:s + 1, :]).astype(bf16)
    for g in range(MIX_TB // MXU):
        h2t_ref[g] = h2_ref[g * MXU:(g + 1) * MXU, :].T


def _mix(oa, yb, p, x, mod_seg, norm2_w, wa, wb, wo):
    nseg_b = MIX_TB // SEG
    full = lambda i: (0, 0)
    n_pb = N_P // MIX_TB
    pb_per_stream, sb_per_stream = n_pb // NSTREAM, (N_S // MIX_TB) // NSTREAM

    def mixer_block(i):
        r = jnp.where(i < n_pb, i // pb_per_stream, (i - n_pb) // sb_per_stream)
        blk = jnp.where(i < n_pb, i % pb_per_stream, pb_per_stream + (i - n_pb) % sb_per_stream)
        return r, blk, 0

    return pl.pallas_call(
        _mix_kernel,
        grid=(N_TOK // MIX_TB,),
        in_specs=[pl.BlockSpec((1, MIX_TB, D), mixer_block),
                  pl.BlockSpec((1, MIX_TB, SSD_INNER), mixer_block),
                  pl.BlockSpec((MIX_TB, 2 * D), lambda i: (i, C_GATE // (2 * D))),
                  pl.BlockSpec((MIX_TB, D), lambda i: (i, 0)),
                  pl.BlockSpec((nseg_b, D), lambda i: (i, 2)),
                  pl.BlockSpec((nseg_b, D), lambda i: (i, 4)),
                  pl.BlockSpec((nseg_b, D), lambda i: (i, 3)),
                  pl.BlockSpec((1, D), full),
                  pl.BlockSpec((D, D), full),
                  pl.BlockSpec((SSD_INNER, D), full),
                  pl.BlockSpec((D, D), full)],
        out_specs=[pl.BlockSpec((MIX_TB, D), lambda i: (i, 0)),
                   pl.BlockSpec((MIX_TB, D), lambda i: (i, 0)),
                   pl.BlockSpec((MIX_TB // MXU, D, MXU), lambda i: (i, 0, 0))],
        out_shape=[jax.ShapeDtypeStruct((N_TOK, D), f32),
                   jax.ShapeDtypeStruct((N_TOK, D), bf16),
                   jax.ShapeDtypeStruct((N_TOK // MXU, D, MXU), bf16)],
        compiler_params=_cparams(("parallel",)),
        name="mix_out",
    )(oa, yb, p, x, mod_seg, mod_seg, mod_seg, norm2_w.reshape(1, D), wa, wb, wo)


RT_TL = 128
HALF = PEER_DQ // 2


RT_HPS = 8


def _batcher_sort_net(n):
    def merge(lo, hi, r):
        step = 2 * r
        if step < hi - lo:
            yield from merge(lo, hi, step)
            yield from merge(lo + r, hi, step)
            yield from ((i, i + r) for i in range(lo + r, hi - r, step))
        else:
            yield (lo, lo + r)

    def sort(lo, hi):
        if hi - lo >= 1:
            mid = lo + (hi - lo) // 2
            yield from sort(lo, mid)
            yield from sort(mid + 1, hi)
            yield from merge(lo, hi, 1)

    return tuple(sort(0, n - 1))


def _bitonic_merge_net(n):
    net, d = [], n // 2
    while d >= 1:
        net += [(i, i + d) for i in range(n) if (i // d) % 2 == 0]
        d //= 2
    return tuple(net)


_SORT16 = _batcher_sort_net(PEER_TOPK)
_MERGE16 = _bitonic_merge_net(PEER_TOPK)
N_CAND_VREGS = 10


def _compare_exchange(x, net):
    for i, j in net:
        x[i], x[j] = jnp.maximum(x[i], x[j]), jnp.minimum(x[i], x[j])


def _merge_across_sublanes(x, n_valid):
    for shift in (4, 2, 1):
        y = [pltpu.roll(v, shift, 0) for v in x]
        merged = []
        for k in range(PEER_TOPK):
            a = x[k] if k < n_valid else None
            b = y[PEER_TOPK - 1 - k] if PEER_TOPK - 1 - k < n_valid else None
            merged.append(jnp.maximum(a, b) if (a is not None and b is not None) else (a if b is None else b))
        x = merged
        _compare_exchange(x, _MERGE16)
        n_valid = PEER_TOPK
    return x


def _top16_sorted(s):
    x = [s[SUBLANES * k:SUBLANES * (k + 1), :] for k in range(PEER_NK // SUBLANES)]
    _compare_exchange(x, _SORT16)
    return _merge_across_sublanes(x, PEER_TOPK)


def _route_kernel(q_ref, k1_ref, k2_ref, thr_ref, s2_ref, f1_ref, f2_ref):
    sub = lax.broadcasted_iota(jnp.int32, (SUBLANES, RT_TL), 0)
    ninf = jnp.float32(-jnp.inf)

    def by_sublane(vs):
        out = vs[0]
        for r in range(1, SUBLANES):
            out = jnp.where(sub == r, vs[r], out)
        return out

    for hh in range(RT_HPS):
        s1 = _dot3_nt(k1_ref[hh], q_ref[:, hh * PEER_DQ:hh * PEER_DQ + HALF])
        s2 = _dot3_nt(k2_ref[hh], q_ref[:, hh * PEER_DQ + HALF:(hh + 1) * PEER_DQ])
        s2_ref[hh] = s2
        v1 = _top16_sorted(s1)
        v2 = _top16_sorted(s2)
        v2lo, v2hi, v1hi = by_sublane(v2[:8]), by_sublane(v2[8:]), by_sublane(v1[8:])
        cands = [v1[0] + v2lo, v1[0] + v2hi, v1[1] + v2lo]
        for i in range(2, 8):
            cands.append(jnp.where(sub < PEER_TOPK // (i + 1), v1[i] + v2lo, ninf))
        cands.append(v1hi + v2[0])
        assert len(cands) == N_CAND_VREGS
        _compare_exchange(cands, tuple((i, j) for i, j in _SORT16 if j < N_CAND_VREGS))
        top = _merge_across_sublanes(cands, N_CAND_VREGS)
        zsum = jnp.ones_like(top[0])
        for k in range(1, PEER_TOPK):
            zsum = zsum + jnp.exp(top[k] - top[0])
        tau = top[PEER_TOPK - 1]
        for k in range(PEER_NK // SUBLANES):
            rows = slice(SUBLANES * k, SUBLANES * (k + 1))
            s1k = s1[rows, :]
            thr = jnp.full_like(s1k, jnp.inf)
            for j in range(PEER_TOPK):
                thr = jnp.where(s1k + v2[j] >= tau, v2[j], thr)
            thr_ref[hh, rows, :] = thr
        f1_ref[hh] = jnp.exp(s1 - v1[0][0:1, :]) * (0.5 / zsum[0:1, :])
        f2_ref[hh] = jnp.exp(s2 - v2[0][0:1, :])


def _route(q, keys1, keys2):
    tile = pl.BlockSpec((RT_HPS, PEER_NK, RT_TL), lambda i, h: (h, 0, i))
    big = jax.ShapeDtypeStruct((PEER_H, PEER_NK, N_TOK), f32)
    return pl.pallas_call(
        _route_kernel,
        grid=(N_TOK // RT_TL, PEER_H // RT_HPS),
        in_specs=[pl.BlockSpec((RT_TL, RT_HPS * PEER_DQ), lambda i, h: (i, h)),
                  pl.BlockSpec((RT_HPS, PEER_NK, HALF), lambda i, h: (h, 0, 0)),
                  pl.BlockSpec((RT_HPS, PEER_NK, HALF), lambda i, h: (h, 0, 0))],
        out_specs=[tile, tile, tile, tile],
        out_shape=[big, big, big, big],
        compiler_params=_cparams(("parallel", "parallel")),
        name="peer_route",
    )(q, keys1, keys2)


PE_TB = 512
PE_NA = 4
PE_AG = 2
PE_CH = PE_AG * PE_NA * PEER_NK
PE_NCH = PEER_NK * PEER_NK // PE_CH
PE_BH = 32
SQRT_HALF = 0.7071067811865476
MXU = 256


def _peer_kernel(*refs, final):
    if final:
        (u0_ref, u_ref, vt_ref, xt_ref, thr_ref, f1_ref, s2_ref, f2_ref, x1_ref, ga_ref, fw_ref,
         o_ref, acc_ref, pw_ref, gel_ref, rthr_ref, rf1_ref) = refs
    else:
        (u0_ref, u_ref, vt_ref, xt_ref, thr_ref, f1_ref, s2_ref, f2_ref, x1_ref, ga_ref,
         o_ref, acc_ref, pw_ref, gel_ref, rthr_ref, rf1_ref) = refs
    j = pl.program_id(1)

    n_half = PE_TB // MXU

    def two_gelu(act):
        return act * (1.0 + lax.erf(act * SQRT_HALF))

    @pl.when(j == 0)
    def _():
        acc_ref[...] = jnp.zeros_like(acc_ref)
        pw_ref[1] = jnp.zeros(pw_ref.shape[1:], bf16)
        for g in range(n_half):
            gel_ref[0, g] = two_gelu(jnp.dot(u0_ref[0], xt_ref[g], preferred_element_type=f32))

    cur = j % 2
    prev = 1 - cur

    def prep_rows():
        for ia in range(PE_AG * PE_NA):
            for h in range(PEER_H):
                a = j * (PE_AG * PE_NA) + ia
                rthr_ref[ia * PEER_H + h] = jnp.broadcast_to(thr_ref[h, pl.ds(a, 1), :], (SUBLANES, PE_TB))
                rf1_ref[ia * PEER_H + h] = jnp.broadcast_to(f1_ref[h, pl.ds(a, 1), :], (SUBLANES, PE_TB))

    nsub = PE_BH // SUBLANES
    n_bh = PEER_NK // PE_BH

    def weight_tile(g, lq, bh, ag):
        lanes = pl.ds(pl.multiple_of(g * MXU + lq * LANE, LANE), LANE)
        lq_lanes = slice(lq * LANE, (lq + 1) * LANE)
        rows = slice(bh * PE_BH, (bh + 1) * PE_BH)
        ws = [jnp.zeros((nsub, SUBLANES, LANE), f32) for _ in range(PE_NA)]
        for h in range(PEER_H):
            s2t = s2_ref[h, rows, lanes].reshape(nsub, SUBLANES, LANE)
            f2t = f2_ref[h, rows, lanes].reshape(nsub, SUBLANES, LANE)
            for i in range(PE_NA):
                row = (ag * PE_NA + i) * PEER_H + h
                sel = s2t >= rthr_ref[row, :, lanes][None]
                ws[i] = ws[i] + jnp.where(sel, f2t * rf1_ref[row, :, lanes][None], 0.0)
        for i in range(PE_NA):
            e0 = (ag * PE_NA + i) * PEER_NK + bh * PE_BH
            pw_ref[cur, g, e0:e0 + PE_BH, lq_lanes] = (
                ws[i].reshape(PE_BH, LANE) * gel_ref[cur, g, e0:e0 + PE_BH, lq_lanes]).astype(bf16)

    def stage(p, carry, with_act):
        if with_act:
            gel_ref[prev, p] = two_gelu(jnp.dot(u_ref[0], xt_ref[p], preferred_element_type=f32))
        acc_ref[p] += jnp.dot(vt_ref[0], pw_ref[prev, p], preferred_element_type=f32)
        for lq in range(MXU // LANE):
            for bh in range(n_bh):
                for ag in range(PE_AG):
                    weight_tile(p, lq, bh, ag)
        return carry

    @pl.when(j < PE_NCH - 1)
    def _():
        prep_rows()
        lax.fori_loop(0, n_half, functools.partial(stage, with_act=True), 0)

    @pl.when(j == PE_NCH - 1)
    def _():
        prep_rows()
        lax.fori_loop(0, n_half, functools.partial(stage, with_act=False), 0)

    @pl.when(j == PE_NCH)
    def _():
        for g in range(n_half):
            out = (acc_ref[g] + jnp.dot(vt_ref[0], pw_ref[prev, g], preferred_element_type=f32)).T
            for s in range(MXU // SEG):
                r = slice(g * MXU + s * SEG, g * MXU + (s + 1) * SEG)
                seg = g * (MXU // SEG) + s
                x2 = x1_ref[r, :] + ga_ref[seg:seg + 1, :] * out[s * SEG:(s + 1) * SEG, :]
                o_ref[r, :] = _rms(x2, fw_ref[...]) if final else x2


def _peer(u_c, vt_c, h2t, thr, f1t, s2t, f2t, x1, mod_seg, final_w):
    final = final_w is not None
    nseg_b = PE_TB // SEG
    rt = pl.BlockSpec((PEER_H, PEER_NK, PE_TB), lambda i, j: (0, 0, i))
    in_specs = [pl.BlockSpec((1, PE_CH, D), lambda i, j: (0, 0, 0)),
                pl.BlockSpec((1, PE_CH, D), lambda i, j: (jnp.minimum(j + 1, PE_NCH - 1), 0, 0)),
                pl.BlockSpec((1, D, PE_CH), lambda i, j: (jnp.maximum(j - 1, 0), 0, 0)),
                pl.BlockSpec((PE_TB // MXU, D, MXU), lambda i, j: (i, 0, 0)),
                rt, rt, rt, rt,
                pl.BlockSpec((PE_TB, D), lambda i, j: (i, 0)),
                pl.BlockSpec((nseg_b, D), lambda i, j: (i, 5))]
    args = [u_c, u_c, vt_c, h2t, thr, f1t, s2t, f2t, x1, mod_seg]
    if final:
        in_specs.append(pl.BlockSpec((1, D), lambda i, j: (0, 0)))
        args.append(final_w.reshape(1, D))
    return pl.pallas_call(
        functools.partial(_peer_kernel, final=final),
        grid=(N_TOK // PE_TB, PE_NCH + 1),
        in_specs=in_specs,
        out_specs=pl.BlockSpec((PE_TB, D), lambda i, j: (i, 0)),
        out_shape=jax.ShapeDtypeStruct((N_TOK, D), f32),
        scratch_shapes=[pltpu.VMEM((PE_TB // MXU, D, MXU), f32), pltpu.VMEM((2, PE_TB // MXU, PE_CH, MXU), bf16),
                        pltpu.VMEM((2, PE_TB // MXU, PE_CH, MXU), f32),
                        pltpu.VMEM((PE_AG * PE_NA * PEER_H, SUBLANES, PE_TB), f32),
                        pltpu.VMEM((PE_AG * PE_NA * PEER_H, SUBLANES, PE_TB), f32)],
        compiler_params=_cparams(("parallel", "arbitrary")),
        name="peer_experts_final" if final else "peer_experts",
    )(*args)


def _reorder_w_in(w):
    o = np.cumsum([0, 512, 512, 1024, 1024, GLA_RANK, SSD_INNER, SSD_CONV_DIM, SSD_H, 2 * D])
    q_k_v_go = w[:, o[0]:o[4]]
    gk_low = w[:, o[4]:o[5]]
    z = w[:, o[5]:o[6]]
    xbc = w[:, o[6]:o[7]]
    dt = w[:, o[7]:o[8]]
    gates = w[:, o[8]:o[9]]
    pad = lambda m: jnp.pad(m, ((0, 0), (0, LANE - m.shape[1])))
    low = jnp.concatenate([gk_low, jnp.zeros((D, LOW_DT0 - GLA_RANK), w.dtype), dt], axis=1)
    return jnp.concatenate([q_k_v_go, xbc, z, gates, pad(low)], axis=1).astype(bf16)


def kernel(x_prompt, x_sample, state_gla, state_ssd, state_conv, c_prompt, c_sample, w_ada, b_ada, norm1_w, w_in, gla_gk_w2, gla_gk_b, gla_norm_w, gla_proj, ssd_conv_w, ssd_conv_b, ssd_dt_bias, ssd_A_log, ssd_D, ssd_norm_w, ssd_proj, w_out, norm2_w, peer_wq, peer_keys1, peer_keys2, peer_u, peer_v, final_norm_w):
    x = jnp.concatenate([x_prompt.reshape(N_P, D), x_sample.reshape(N_S, D)], axis=0)
    c_all = jnp.concatenate([c_prompt, c_sample], axis=0)
    mod = _ada(c_all, w_ada, b_ada)
    seg2seq = np.concatenate([np.repeat(np.arange(B_P), SEG_PER_PSEQ), B_P + np.arange(B_S)])

    gla_st, ssd_st, conv_st = [], [], []
    for l in range(DEPTH):
        mod_seg = mod[l][seg2seq]
        h = _normmod(x, norm1_w[l], mod_seg, 1, 0)
        p = _mm(h, _reorder_w_in(w_in[l]), 1024, P_COLS // 3, "in_proj")

        w2pad = jnp.pad(gla_gk_w2[l], ((0, LANE - GLA_RANK), (0, 0)))
        low_lanes = lambda m: jnp.pad(m.reshape(1, SSD_H), ((0, 0), (LOW_DT0, LANE - LOW_DT0 - SSD_H)))
        dvec = jnp.repeat(ssd_D[l], SSD_P).reshape(1, SSD_INNER)
        (oa, gla_p, gla_s), (yb, ssd_p, ssd_s, conv_p, conv_s) = _mixers(
            _gla_call(p, w2pad, gla_gk_b[l], gla_norm_w[l], state_gla[l]),
            _ssd_call(p, state_conv[l], ssd_conv_w[l], ssd_conv_b[l], low_lanes(ssd_dt_bias[l]),
                      low_lanes(ssd_A_log[l]), dvec, ssd_norm_w[l], state_ssd[l]))

        x1, h2, h2t = _mix(oa, yb, p, x, mod_seg, norm2_w[l], gla_proj[l].astype(bf16),
                           ssd_proj[l].astype(bf16), w_out[l].astype(bf16))
        q = _mm(h2, peer_wq[l].astype(bf16), 512, PEER_H * PEER_DQ, "peer_query")
        thr, s2t, f1t, f2t = _route(q, peer_keys1[l], peer_keys2[l])
        u_c = peer_u[l].astype(bf16).reshape(PE_NCH, PE_CH, D)
        vt_c = peer_v[l].astype(bf16).reshape(PE_NCH, PE_CH, D).transpose(0, 2, 1)
        x = _peer(u_c, vt_c, h2t, thr, f1t, s2t, f2t, x1, mod_seg,
                  final_norm_w if l == DEPTH - 1 else None)

        seqs = lambda st: st.reshape((-1,) + st.shape[2:])
        gla_st.append((seqs(gla_p), seqs(gla_s)))
        ssd_st.append((seqs(ssd_p), seqs(ssd_s)))
        conv_st.append((seqs(conv_p), seqs(conv_s)))

    y_prompt = x[:N_P].reshape(B_P, T_P, D)
    y_sample = x[N_P:].reshape(B_S, T_S, D)
    stack = lambda pairs, k: jnp.stack([pr[k] for pr in pairs])
    return (y_prompt, y_sample, stack(gla_st, 0), stack(ssd_st, 0), stack(conv_st, 0),
            stack(gla_st, 1), stack(ssd_st, 1), stack(conv_st, 1))
```

```python
import functools

import jax
import jax.numpy as jnp
import numpy as np
from jax import lax
from jax.experimental import pallas as pl
from jax.experimental.pallas import tpu as pltpu

f32 = jnp.float32
bf16 = jnp.bfloat16

D = 1024
DEPTH = 2
B_P, T_P = 8, 2048
B_S, T_S = 32, 64
N_P = B_P * T_P
N_S = B_S * T_S
N_TOK = N_P + N_S
SEG = 64
NSEG = N_TOK // SEG
NSEG_P = N_P // SEG
SEG_PER_PSEQ = T_P // SEG
EPS = 1e-6

GLA_H, GLA_DK, GLA_DV = 4, 128, 256
GLA_RANK = 16
GLA_GATE_NORM = 16.0
SSD_INNER = 2048
SSD_P = 64
SSD_H = 32
SSD_G = 4
SSD_N = 128
SSD_HG = SSD_H // SSD_G
SSD_GW = SSD_INNER // SSD_G
CONV_W = 4
SSD_CONV_DIM = SSD_INNER + 2 * SSD_G * SSD_N

PEER_H = 8
PEER_NK = 128
PEER_DQ = 256
PEER_TOPK = 16

C_Q, C_K, C_V, C_GO = 0, 512, 1024, 2048
C_XBC, C_Z, C_GATE = 3072, 6144, 8192
C_LOW = 10240
LOW_DT0 = 32
P_COLS = C_LOW + 128

LANE = 128
SUBLANES = 8
VMEM_LIMIT = 56 * 1024 * 1024


def _cparams(sem):
    return pltpu.CompilerParams(dimension_semantics=sem, vmem_limit_bytes=VMEM_LIMIT)


def _dot(a, b):
    return jnp.dot(a.astype(bf16), b.astype(bf16), preferred_element_type=f32)


def _dot_nt(a, b):
    return lax.dot_general(a.astype(bf16), b.astype(bf16), (((1,), (1,)), ((), ())), preferred_element_type=f32)


def _dot_tn(a, b):
    return lax.dot_general(a.astype(bf16), b.astype(bf16), (((0,), (0,)), ((), ())), preferred_element_type=f32)


def _split3(x):
    hi = x.astype(bf16)
    r = x - hi.astype(f32)
    mid = r.astype(bf16)
    lo = (r - mid.astype(f32)).astype(bf16)
    return hi, mid, lo


def _dot3(a, b):
    ah, am, _ = _split3(a)
    bh, bm, _ = _split3(b)
    d = functools.partial(jnp.dot, preferred_element_type=f32)
    return d(ah, bh) + (d(ah, bm) + d(am, bh))


def _dot3_nt(a, b):
    ah, am, _ = _split3(a)
    bh, bm, _ = _split3(b)
    d = functools.partial(lax.dot_general, dimension_numbers=(((1,), (1,)), ((), ())), preferred_element_type=f32)
    return d(ah, bh) + (d(ah, bm) + d(am, bh))


def _cumsum_rows(x):
    n = x.shape[0]
    tri = (lax.broadcasted_iota(jnp.int32, (n, n), 0) >= lax.broadcasted_iota(jnp.int32, (n, n), 1)).astype(bf16)
    hi, mid, lo = _split3(x)
    d = functools.partial(jnp.dot, preferred_element_type=f32)
    return d(tri, hi) + (d(tri, mid) + d(tri, lo))


def _silu(x):
    return x * jax.nn.sigmoid(x)


def _softplus(x):
    return jnp.maximum(x, 0.0) + jnp.log1p(jnp.exp(-jnp.abs(x)))


def _rms(x, w):
    return x * lax.rsqrt(jnp.mean(x * x, axis=-1, keepdims=True) + EPS) * w


def _ada_kernel(c_ref, w_ref, b_ref, o_ref):
    o_ref[0] = _dot3(_silu(c_ref[...]), w_ref[0]) + b_ref[0]


def _ada(c_all, w_ada, b_ada):
    nb = c_all.shape[0]
    tn = 1536
    return pl.pallas_call(
        _ada_kernel,
        grid=(DEPTH, 6 * D // tn),
        in_specs=[pl.BlockSpec((nb, D), lambda l, j: (0, 0)),
                  pl.BlockSpec((1, D, tn), lambda l, j: (l, 0, j)),
                  pl.BlockSpec((1, 1, tn), lambda l, j: (l, 0, j))],
        out_specs=pl.BlockSpec((1, nb, tn), lambda l, j: (l, 0, j)),
        out_shape=jax.ShapeDtypeStruct((DEPTH, nb, 6 * D), f32),
        compiler_params=_cparams(("parallel", "parallel")),
        name="ada_mod",
    )(c_all, w_ada, b_ada.reshape(DEPTH, 1, 6 * D))


NM_TB = 512


def _normmod_kernel(x_ref, w_ref, sc_ref, sh_ref, o_ref):
    for s in range(NM_TB // SEG):
        r = slice(s * SEG, (s + 1) * SEG)
        y = _rms(x_ref[r, :], w_ref[...])
        o_ref[r, :] = (y * (1.0 + sc_ref[s:s + 1, :]) + sh_ref[s:s + 1, :]).astype(bf16)


def _normmod(x, w, mod_seg, sc_col, sh_col):
    nseg_b = NM_TB // SEG
    return pl.pallas_call(
        _normmod_kernel,
        grid=(N_TOK // NM_TB,),
        in_specs=[pl.BlockSpec((NM_TB, D), lambda i: (i, 0)),
                  pl.BlockSpec((1, D), lambda i: (0, 0)),
                  pl.BlockSpec((nseg_b, D), lambda i: (i, sc_col)),
                  pl.BlockSpec((nseg_b, D), lambda i: (i, sh_col))],
        out_specs=pl.BlockSpec((NM_TB, D), lambda i: (i, 0)),
        out_shape=jax.ShapeDtypeStruct((N_TOK, D), bf16),
        compiler_params=_cparams(("parallel",)),
        name="norm_mod",
    )(x, w.reshape(1, D), mod_seg, mod_seg)


def _mm_kernel(a_ref, b_ref, o_ref):
    o_ref[...] = jnp.dot(a_ref[...], b_ref[...], preferred_element_type=f32)


def _mm(a, b, tm, tn, name):
    m, k = a.shape
    n = b.shape[1]
    return pl.pallas_call(
        _mm_kernel,
        grid=(n // tn, m // tm),
        in_specs=[pl.BlockSpec((tm, k), lambda j, i: (i, 0)),
                  pl.BlockSpec((k, tn), lambda j, i: (0, j))],
        out_specs=pl.BlockSpec((tm, tn), lambda j, i: (i, j)),
        out_shape=jax.ShapeDtypeStruct((m, n), f32),
        compiler_params=_cparams(("parallel", "parallel")),
        name=name,
    )(a, b)


NSTREAM = 2
STEPS_P = NSEG_P // NSTREAM
STEPS_S = (NSEG - NSEG_P) // NSTREAM
HALF_TOK = N_TOK // NSTREAM


def _is_prompt(c):
    return c < STEPS_P


def _sample_seq(c):
    return jnp.maximum(c - STEPS_P, 0)


def _prompt_seq(c):
    return jnp.minimum(c // SEG_PER_PSEQ, B_P // NSTREAM - 1)


LOADED, STEP, COMPUTED = "loaded", "step", "computed"


def _segment(c, r):
    return jnp.where(c < STEPS_P, r * STEPS_P + c, NSEG_P + r * STEPS_S + (c - STEPS_P))


def _gla_kernel(r, q_ref, k_ref, v_ref, g_ref, low_ref, w2_ref, b2_ref, nw_ref, s0_ref,
                o_ref, stp_ref, sts_ref, st_ref):
    c = pl.program_id(0)
    is_p = _is_prompt(c)
    pos = c % SEG_PER_PSEQ

    @pl.when(jnp.logical_and(is_p, pos == 0))
    def _():
        st_ref[r] = jnp.zeros(st_ref.shape[1:], f32)

    @pl.when(jnp.logical_not(is_p))
    def _():
        for h in range(GLA_H):
            st_ref[r, h] = s0_ref[r, 0, h].T

    yield LOADED
    pre = _dot3(low_ref[...], w2_ref[...]) + b2_ref[...]
    gk = -_softplus(-pre) / GLA_GATE_NORM
    b = _cumsum_rows(gk)
    b_last = b[SEG - 1:SEG, :]
    e_b = jnp.exp(b)
    e_nb = jnp.exp(-b)
    e_end = jnp.exp(b_last - b)
    e_last = jnp.exp(b_last)
    causal = lax.broadcasted_iota(jnp.int32, (SEG, SEG), 0) >= lax.broadcasted_iota(jnp.int32, (SEG, SEG), 1)
    for h in range(GLA_H):
        ks = slice(h * GLA_DK, (h + 1) * GLA_DK)
        vs = slice(h * GLA_DV, (h + 1) * GLA_DV)
        k = k_ref[:, ks]
        q_t = (q_ref[:, ks] * (GLA_DK ** -0.5)) * e_b[:, ks]
        att = jnp.where(causal, _dot_nt(q_t, k * e_nb[:, ks]), 0.0)
        v = v_ref[:, vs]
        st = st_ref[r, h]
        o = _dot(att, v) + _dot_nt(q_t, st)
        st_ref[r, h] = st * e_last[:, ks] + _dot_tn(v, k * e_end[:, ks])
        o_ref[r, :, vs] = (_rms(o, nw_ref[...]) * _silu(g_ref[:, vs])).astype(bf16)
        yield STEP

    yield COMPUTED
    @pl.when(jnp.logical_and(is_p, pos == SEG_PER_PSEQ - 1))
    def _():
        for h in range(GLA_H):
            stp_ref[r, 0, h] = st_ref[r, h].T

    @pl.when(jnp.logical_not(is_p))
    def _():
        for h in range(GLA_H):
            sts_ref[r, 0, h] = st_ref[r, h].T


def _gla_call(p, w2pad, b2, norm_w, s0):
    st_dims = (GLA_H, GLA_DK, GLA_DV)
    st_block = (NSTREAM, 1) + st_dims
    kd, vd = GLA_H * GLA_DK, GLA_H * GLA_DV

    def stream_specs(r):
        return [pl.BlockSpec((SEG, kd), lambda c: (_segment(c, r), C_Q // kd)),
                pl.BlockSpec((SEG, kd), lambda c: (_segment(c, r), C_K // kd)),
                pl.BlockSpec((SEG, vd), lambda c: (_segment(c, r), C_V // vd)),
                pl.BlockSpec((SEG, vd), lambda c: (_segment(c, r), C_GO // vd)),
                pl.BlockSpec((SEG, LANE), lambda c: (_segment(c, r), C_LOW // LANE))]

    return dict(
        stream_specs=stream_specs, stream_args=(p, p, p, p, p),
        in_specs=[pl.BlockSpec((LANE, kd), lambda c: (0, 0)),
                  pl.BlockSpec((1, kd), lambda c: (0, 0)),
                  pl.BlockSpec((1, GLA_DV), lambda c: (0, 0)),
                  pl.BlockSpec(st_block, lambda c: (0, _sample_seq(c), 0, 0, 0))],
        args=(w2pad, b2.reshape(1, -1), norm_w.reshape(1, -1), s0.reshape((NSTREAM, B_S // NSTREAM) + st_dims)),
        out_specs=[pl.BlockSpec((NSTREAM, SEG, vd), lambda c: (0, c, 0)),
                   pl.BlockSpec(st_block, lambda c: (0, _prompt_seq(c), 0, 0, 0)),
                   pl.BlockSpec(st_block, lambda c: (0, _sample_seq(c), 0, 0, 0))],
        out_shape=[jax.ShapeDtypeStruct((NSTREAM, HALF_TOK, vd), bf16),
                   jax.ShapeDtypeStruct((NSTREAM, B_P // NSTREAM) + st_dims, f32),
                   jax.ShapeDtypeStruct((NSTREAM, B_S // NSTREAM) + st_dims, f32)],
        scratch_shapes=[pltpu.VMEM((NSTREAM, GLA_H, GLA_DV, GLA_DK), f32)])


CARRY = SUBLANES
EXT = SEG + CARRY


def _ssd_kernel(r, xbc_ref, z_ref, dt_ref, cw_ref, cb_ref, dtb_ref, alog_ref, dvec_ref, nw_ref, cp_all, s0_all,
                o_all, stp_all, sts_all, cvp_all, cvs_all, ext_all, st_all):
    cp_ref, s0_ref, o_ref = cp_all.at[r], s0_all.at[r], o_all.at[r]
    stp_ref, sts_ref, cvp_ref, cvs_ref = stp_all.at[r], sts_all.at[r], cvp_all.at[r], cvs_all.at[r]
    ext, st_ref = ext_all.at[r], st_all.at[r]
    c = pl.program_id(0)
    is_p = _is_prompt(c)
    pos = c % SEG_PER_PSEQ

    @pl.when(jnp.logical_and(is_p, pos == 0))
    def _():
        ext[0:CARRY, :] = jnp.zeros((CARRY, SSD_CONV_DIM), f32)
        st_ref[...] = jnp.zeros_like(st_ref)

    @pl.when(jnp.logical_not(is_p))
    def _():
        ext[0:CARRY, :] = jnp.zeros((CARRY, SSD_CONV_DIM), f32)
        ext[CARRY - (CONV_W - 1):CARRY, :] = cp_ref[0]
        st_ref[...] = s0_ref[0].reshape(st_ref.shape)

    yield LOADED
    ext[CARRY:EXT, :] = xbc_ref[...]
    acc = cb_ref[...] + ext[pl.ds(CARRY - (CONV_W - 1), SEG), :] * cw_ref[0:1, :]
    for i in range(1, CONV_W):
        acc = acc + ext[pl.ds(CARRY - (CONV_W - 1) + i, SEG), :] * cw_ref[i:i + 1, :]
    ext[0:CARRY, :] = ext[SEG:EXT, :]
    conv = _silu(acc)
    yield STEP

    dt = _softplus(dt_ref[...] + dtb_ref[...])
    a = _cumsum_rows(dt * (-jnp.exp(alog_ref[...])))
    a_t = a.T
    dt_t = dt.T
    a_last = a[SEG - 1:SEG, :]
    e_a = jnp.exp(a)
    w_col = dt * jnp.exp(a_last - a)
    e_last = jnp.exp(a_last)
    causal = lax.broadcasted_iota(jnp.int32, (SEG, SEG), 0) >= lax.broadcasted_iota(jnp.int32, (SEG, SEG), 1)
    lane_lo = lax.broadcasted_iota(jnp.int32, (SEG, 2 * SSD_P), 1) < SSD_P
    sub_lo = lax.broadcasted_iota(jnp.int32, (2 * SSD_P, SSD_N), 0) < SSD_P
    nbc = SSD_G * SSD_N

    def pair_cols(m, ja, jb):
        return jnp.where(lane_lo, jnp.broadcast_to(m[:, ja:ja + 1], (SEG, 2 * SSD_P)),
                         jnp.broadcast_to(m[:, jb:jb + 1], (SEG, 2 * SSD_P)))

    for g in range(SSD_G):
        j0 = LOW_DT0 + g * SSD_HG
        bc = conv[:, SSD_INNER + g * SSD_N:SSD_INNER + (g + 1) * SSD_N]
        cc = conv[:, SSD_INNER + nbc + g * SSD_N:SSD_INNER + nbc + (g + 1) * SSD_N]
        cbm = _dot_nt(cc, bc)

        def w_intra(j):
            seg = a[:, j:j + 1] - a_t[j:j + 1, :]
            return cbm * jnp.exp(jnp.where(causal, seg, -jnp.inf)) * dt_t[j:j + 1, :]

        ys = []
        for p in range(SSD_HG // 2):
            ja, jb = j0 + 2 * p, j0 + 2 * p + 1
            pp = g * (SSD_HG // 2) + p
            xp = conv[:, pp * 2 * SSD_P:(pp + 1) * 2 * SSD_P]
            y_intra = jnp.where(lane_lo, _dot(w_intra(ja), xp), _dot(w_intra(jb), xp))
            sp = st_ref[pp]
            y_inter = _dot_nt(cc, sp) * pair_cols(e_a, ja, jb)
            ds = _dot_tn(xp * pair_cols(w_col, ja, jb), bc)
            rs = jnp.where(sub_lo, jnp.broadcast_to(e_last[:, ja:ja + 1], (2 * SSD_P, SSD_N)),
                           jnp.broadcast_to(e_last[:, jb:jb + 1], (2 * SSD_P, SSD_N)))
            st_ref[pp] = sp * rs + ds
            ys.append(y_intra + y_inter + dvec_ref[:, pp * 2 * SSD_P:(pp + 1) * 2 * SSD_P] * xp)
            yield STEP
        gw = slice(g * SSD_GW, (g + 1) * SSD_GW)
        y = jnp.concatenate(ys, axis=1) * _silu(z_ref[:, gw])
        o_ref[:, gw] = _rms(y, nw_ref[:, gw]).astype(bf16)

    yield COMPUTED
    @pl.when(jnp.logical_and(is_p, pos == SEG_PER_PSEQ - 1))
    def _():
        stp_ref[0] = st_ref[...].reshape(stp_ref.shape[1:])
        cvp_ref[0] = xbc_ref[SEG - (CONV_W - 1):SEG, :]

    @pl.when(jnp.logical_not(is_p))
    def _():
        sts_ref[0] = st_ref[...].reshape(sts_ref.shape[1:])
        cvs_ref[0] = xbc_ref[SEG - (CONV_W - 1):SEG, :]


def _ssd_call(p, conv_prev, conv_w, conv_b, dtb, alog, dvec, norm_w, s0):
    st_dims = (SSD_H, SSD_P, SSD_N)
    cv_dims = (CONV_W - 1, SSD_CONV_DIM)
    st_block = (NSTREAM, 1) + st_dims
    cv_block = (NSTREAM, 1) + cv_dims
    full = lambda c: (0, 0)
    sample_blk = lambda c: (0, _sample_seq(c)) + (0,) * 3
    prompt_blk = lambda c: (0, _prompt_seq(c)) + (0,) * 3

    def stream_specs(r):
        return [pl.BlockSpec((SEG, SSD_CONV_DIM), lambda c: (_segment(c, r), C_XBC // SSD_CONV_DIM)),
                pl.BlockSpec((SEG, SSD_INNER), lambda c: (_segment(c, r), C_Z // SSD_INNER)),
                pl.BlockSpec((SEG, LANE), lambda c: (_segment(c, r), C_LOW // LANE))]

    return dict(
        stream_specs=stream_specs, stream_args=(p, p, p),
        in_specs=[pl.BlockSpec((CONV_W, SSD_CONV_DIM), full),
                  pl.BlockSpec((1, SSD_CONV_DIM), full),
                  pl.BlockSpec((1, LANE), full),
                  pl.BlockSpec((1, LANE), full),
                  pl.BlockSpec((1, SSD_INNER), full),
                  pl.BlockSpec((1, SSD_INNER), full),
                  pl.BlockSpec(cv_block, lambda c: sample_blk(c)[:4]),
                  pl.BlockSpec(st_block, sample_blk)],
        args=(conv_w, conv_b.reshape(1, -1), dtb, alog, dvec, norm_w.reshape(1, -1),
              conv_prev.reshape((NSTREAM, B_S // NSTREAM) + cv_dims), s0.reshape((NSTREAM, B_S // NSTREAM) + st_dims)),
        out_specs=[pl.BlockSpec((NSTREAM, SEG, SSD_INNER), lambda c: (0, c, 0)),
                   pl.BlockSpec(st_block, prompt_blk),
                   pl.BlockSpec(st_block, sample_blk),
                   pl.BlockSpec(cv_block, lambda c: prompt_blk(c)[:4]),
                   pl.BlockSpec(cv_block, lambda c: sample_blk(c)[:4])],
        out_shape=[jax.ShapeDtypeStruct((NSTREAM, HALF_TOK, SSD_INNER), bf16),
                   jax.ShapeDtypeStruct((NSTREAM, B_P // NSTREAM) + st_dims, f32),
                   jax.ShapeDtypeStruct((NSTREAM, B_S // NSTREAM) + st_dims, f32),
                   jax.ShapeDtypeStruct((NSTREAM, B_P // NSTREAM) + cv_dims, f32),
                   jax.ShapeDtypeStruct((NSTREAM, B_S // NSTREAM) + cv_dims, f32)],
        scratch_shapes=[pltpu.VMEM((NSTREAM, EXT, SSD_CONV_DIM), f32),
                        pltpu.VMEM((NSTREAM, SSD_H // 2, 2 * SSD_P, SSD_N), f32)])


def _mixers(gla, ssd):
    halves = (gla, ssd)
    kernels = (_gla_kernel, _ssd_kernel)
    n_str = [len(m["stream_args"]) for m in halves]
    n_in = [len(m["in_specs"]) for m in halves]
    n_out = [len(m["out_specs"]) for m in halves]
    n_scr = [len(m["scratch_shapes"]) for m in halves]

    def take(refs, counts):
        out, at = [], 0
        for n in counts:
            out.append(refs[at:at + n])
            at += n
        return out, refs[at:]

    def body(*refs):
        streams, rest = take(refs, [n for n in n_str for _ in range(NSTREAM)])
        shared, rest = take(rest, n_in)
        outs, rest = take(rest, n_out)
        scr, _ = take(rest, n_scr)
        parts = [kernels[m](r, *streams[m * NSTREAM + r], *shared[m], *outs[m], *scr[m])
                 for m in range(len(halves)) for r in range(NSTREAM)]
        for part in parts:
            assert next(part) == LOADED
        active = list(parts)
        while active:
            active = [part for part in active if next(part) != COMPUTED]
        for part in parts:
            next(part, None)

    outs = pl.pallas_call(
        body,
        grid=(STEPS_P + STEPS_S,),
        in_specs=([s for m in halves for r in range(NSTREAM) for s in m["stream_specs"](r)]
                  + gla["in_specs"] + ssd["in_specs"]),
        out_specs=gla["out_specs"] + ssd["out_specs"],
        out_shape=gla["out_shape"] + ssd["out_shape"],
        scratch_shapes=gla["scratch_shapes"] + ssd["scratch_shapes"],
        compiler_params=_cparams(("arbitrary",)),
        name="token_mixers",
    )(*[a for m in halves for _ in range(NSTREAM) for a in m["stream_args"]], *gla["args"], *ssd["args"])
    return outs[:n_out[0]], outs[n_out[0]:]


MIX_TB = 512


def _mix_kernel(oa_ref, yb_ref, gt_ref, x_ref, ga_ref, sc_ref, sh_ref, nw_ref, wa_ref, wb_ref, wo_ref,
                x1_ref, h2_ref, h2t_ref):
    br_a = jnp.dot(oa_ref[0], wa_ref[...], preferred_element_type=f32)
    br_b = jnp.dot(yb_ref[0], wb_ref[...], preferred_element_type=f32)
    g_a = jax.nn.sigmoid(gt_ref[:, 0:D])
    g_b = jax.nn.sigmoid(gt_ref[:, D:2 * D])
    y = _dot(g_a * br_a + g_b * br_b, wo_ref[...])
    for s in range(MIX_TB // SEG):
        r = slice(s * SEG, (s + 1) * SEG)
        x1 = x_ref[r, :] + ga_ref[s:s + 1, :] * y[r, :]
        x1_ref[r, :] = x1
        h2_ref[r, :] = (_rms(x1, nw_ref[...]) * (1.0 + sc_ref[s:s + 1, :]) + sh_ref[s:s + 1, :]).astype(bf16)
    for g in range(MIX_TB // MXU):
        h2t_ref[g] = h2_ref[g * MXU:(g + 1) * MXU, :].T


def _mix(oa, yb, p, x, mod_seg, norm2_w, wa, wb, wo):
    nseg_b = MIX_TB // SEG
    full = lambda i: (0, 0)
    n_pb = N_P // MIX_TB
    pb_per_stream, sb_per_stream = n_pb // NSTREAM, (N_S // MIX_TB) // NSTREAM

    def mixer_block(i):
        r = jnp.where(i < n_pb, i // pb_per_stream, (i - n_pb) // sb_per_stream)
        blk = jnp.where(i < n_pb, i % pb_per_stream, pb_per_stream + (i - n_pb) % sb_per_stream)
        return r, blk, 0

    return pl.pallas_call(
        _mix_kernel,
        grid=(N_TOK // MIX_TB,),
        in_specs=[pl.BlockSpec((1, MIX_TB, D), mixer_block),
                  pl.BlockSpec((1, MIX_TB, SSD_INNER), mixer_block),
                  pl.BlockSpec((MIX_TB, 2 * D), lambda i: (i, C_GATE // (2 * D))),
                  pl.BlockSpec((MIX_TB, D), lambda i: (i, 0)),
                  pl.BlockSpec((nseg_b, D), lambda i: (i, 2)),
                  pl.BlockSpec((nseg_b, D), lambda i: (i, 4)),
                  pl.BlockSpec((nseg_b, D), lambda i: (i, 3)),
                  pl.BlockSpec((1, D), full),
                  pl.BlockSpec((D, D), full),
                  pl.BlockSpec((SSD_INNER, D), full),
                  pl.BlockSpec((D, D), full)],
        out_specs=[pl.BlockSpec((MIX_TB, D), lambda i: (i, 0)),
                   pl.BlockSpec((MIX_TB, D), lambda i: (i, 0)),
                   pl.BlockSpec((MIX_TB // MXU, D, MXU), lambda i: (i, 0, 0))],
        out_shape=[jax.ShapeDtypeStruct((N_TOK, D), f32),
                   jax.ShapeDtypeStruct((N_TOK, D), bf16),
                   jax.ShapeDtypeStruct((N_TOK // MXU, D, MXU), bf16)],
        compiler_params=_cparams(("parallel",)),
        name="mix_out",
    )(oa, yb, p, x, mod_seg, mod_seg, mod_seg, norm2_w.reshape(1, D), wa, wb, wo)


RT_TL = 128
HALF = PEER_DQ // 2


RT_HPS = 8


def _batcher_sort_net(n):
    def merge(lo, hi, r):
        step = 2 * r
        if step < hi - lo:
            yield from merge(lo, hi, step)
            yield from merge(lo + r, hi, step)
            yield from ((i, i + r) for i in range(lo + r, hi - r, step))
        else:
            yield (lo, lo + r)

    def sort(lo, hi):
        if hi - lo >= 1:
            mid = lo + (hi - lo) // 2
            yield from sort(lo, mid)
            yield from sort(mid + 1, hi)
            yield from merge(lo, hi, 1)

    return tuple(sort(0, n - 1))


def _bitonic_merge_net(n):
    net, d = [], n // 2
    while d >= 1:
        net += [(i, i + d) for i in range(n) if (i // d) % 2 == 0]
        d //= 2
    return tuple(net)


_SORT16 = _batcher_sort_net(PEER_TOPK)
_MERGE16 = _bitonic_merge_net(PEER_TOPK)
N_CAND_VREGS = 10


def _compare_exchange(x, net):
    for i, j in net:
        x[i], x[j] = jnp.maximum(x[i], x[j]), jnp.minimum(x[i], x[j])


def _merge_across_sublanes(x, n_valid):
    for shift in (4, 2, 1):
        y = [pltpu.roll(v, shift, 0) for v in x]
        merged = []
        for k in range(PEER_TOPK):
            a = x[k] if k < n_valid else None
            b = y[PEER_TOPK - 1 - k] if PEER_TOPK - 1 - k < n_valid else None
            merged.append(jnp.maximum(a, b) if (a is not None and b is not None) else (a if b is None else b))
        x = merged
        _compare_exchange(x, _MERGE16)
        n_valid = PEER_TOPK
    return x


def _top16_sorted(s):
    x = [s[SUBLANES * k:SUBLANES * (k + 1), :] for k in range(PEER_NK // SUBLANES)]
    _compare_exchange(x, _SORT16)
    return _merge_across_sublanes(x, PEER_TOPK)


def _last_passing(s1k, v2, tau):
    def passes(t):
        return s1k + t >= tau

    best = v2[0]
    path = []
    for half in (8, 4, 2, 1):
        t = _select_by_path(path, [v2[lo + half] for lo in range(0, PEER_TOPK, 2 * half)])
        c = passes(t)
        best = jnp.where(c, t, best)
        path.append(c)
    return jnp.where(passes(v2[0]), best, jnp.inf)


def _select_by_path(path, leaves):
    if not path:
        return leaves[0]
    n = len(leaves) // 2
    return jnp.where(path[0], _select_by_path(path[1:], leaves[n:]), _select_by_path(path[1:], leaves[:n]))


def _route_kernel(q_ref, k1_ref, k2_ref, thr_ref, s2_ref, f1_ref, f2_ref):
    sub = lax.broadcasted_iota(jnp.int32, (SUBLANES, RT_TL), 0)
    ninf = jnp.float32(-jnp.inf)

    def by_sublane(vs):
        out = vs[0]
        for r in range(1, SUBLANES):
            out = jnp.where(sub == r, vs[r], out)
        return out

    for hh in range(RT_HPS):
        s1 = _dot3_nt(k1_ref[hh], q_ref[:, hh * PEER_DQ:hh * PEER_DQ + HALF])
        s2 = _dot3_nt(k2_ref[hh], q_ref[:, hh * PEER_DQ + HALF:(hh + 1) * PEER_DQ])
        s2_ref[hh] = s2
        v1 = _top16_sorted(s1)
        v2 = _top16_sorted(s2)
        v2lo, v2hi, v1hi = by_sublane(v2[:8]), by_sublane(v2[8:]), by_sublane(v1[8:])
        cands = [v1[0] + v2lo, v1[0] + v2hi, v1[1] + v2lo]
        for i in range(2, 8):
            cands.append(jnp.where(sub < PEER_TOPK // (i + 1), v1[i] + v2lo, ninf))
        cands.append(v1hi + v2[0])
        assert len(cands) == N_CAND_VREGS
        _compare_exchange(cands, tuple((i, j) for i, j in _SORT16 if j < N_CAND_VREGS))
        top = _merge_across_sublanes(cands, N_CAND_VREGS)
        zsum = jnp.ones_like(top[0])
        for k in range(1, PEER_TOPK):
            zsum = zsum + jnp.exp(top[k] - top[0])
        tau = top[PEER_TOPK - 1]
        for k in range(PEER_NK // SUBLANES):
            rows = slice(SUBLANES * k, SUBLANES * (k + 1))
            s1k = s1[rows, :]
            thr_ref[hh, rows, :] = _last_passing(s1k, v2, tau)
        f1_ref[hh] = jnp.exp(s1 - v1[0][0:1, :]) * (0.5 / zsum[0:1, :])
        f2_ref[hh] = jnp.exp(s2 - v2[0][0:1, :])


def _route(q, keys1, keys2):
    tile = pl.BlockSpec((RT_HPS, PEER_NK, RT_TL), lambda i, h: (h, 0, i))
    big = jax.ShapeDtypeStruct((PEER_H, PEER_NK, N_TOK), f32)
    return pl.pallas_call(
        _route_kernel,
        grid=(N_TOK // RT_TL, PEER_H // RT_HPS),
        in_specs=[pl.BlockSpec((RT_TL, RT_HPS * PEER_DQ), lambda i, h: (i, h)),
                  pl.BlockSpec((RT_HPS, PEER_NK, HALF), lambda i, h: (h, 0, 0)),
                  pl.BlockSpec((RT_HPS, PEER_NK, HALF), lambda i, h: (h, 0, 0))],
        out_specs=[tile, tile, tile, tile],
        out_shape=[big, big, big, big],
        compiler_params=_cparams(("parallel", "parallel")),
        name="peer_route",
    )(q, keys1, keys2)


PE_TB = 512
PE_NA = 4
PE_AG = 2
PE_CH = PE_AG * PE_NA * PEER_NK
PE_NCH = PEER_NK * PEER_NK // PE_CH
PE_BH = 32
SQRT_HALF = 0.7071067811865476
MXU = 256


def _peer_kernel(*refs, final):
    if final:
        (u0_ref, u_ref, vt_ref, xt_ref, thr_ref, f1_ref, s2_ref, f2_ref, x1_ref, ga_ref, fw_ref,
         o_ref, acc_ref, pw_ref, gel_ref, rthr_ref, rf1_ref) = refs
    else:
        (u0_ref, u_ref, vt_ref, xt_ref, thr_ref, f1_ref, s2_ref, f2_ref, x1_ref, ga_ref,
         o_ref, acc_ref, pw_ref, gel_ref, rthr_ref, rf1_ref) = refs
    j = pl.program_id(1)

    n_half = PE_TB // MXU

    def two_gelu(act):
        return act * (1.0 + lax.erf(act * SQRT_HALF))

    @pl.when(j == 0)
    def _():
        acc_ref[...] = jnp.zeros_like(acc_ref)
        pw_ref[1] = jnp.zeros(pw_ref.shape[1:], bf16)
        for g in range(n_half):
            gel_ref[0, g] = two_gelu(jnp.dot(u0_ref[0], xt_ref[g], preferred_element_type=f32))

    cur = j % 2
    prev = 1 - cur

    def prep_rows():
        for ia in range(PE_AG * PE_NA):
            for h in range(PEER_H):
                a = j * (PE_AG * PE_NA) + ia
                rthr_ref[ia * PEER_H + h] = jnp.broadcast_to(thr_ref[h, pl.ds(a, 1), :], (SUBLANES, PE_TB))
                rf1_ref[ia * PEER_H + h] = jnp.broadcast_to(f1_ref[h, pl.ds(a, 1), :], (SUBLANES, PE_TB))

    nsub = PE_BH // SUBLANES
    n_bh = PEER_NK // PE_BH

    def weight_tile(g, lq, bh, ag):
        lanes = pl.ds(pl.multiple_of(g * MXU + lq * LANE, LANE), LANE)
        lq_lanes = slice(lq * LANE, (lq + 1) * LANE)
        rows = slice(bh * PE_BH, (bh + 1) * PE_BH)
        ws = [jnp.zeros((nsub, SUBLANES, LANE), f32) for _ in range(PE_NA)]
        for h in range(PEER_H):
            s2t = s2_ref[h, rows, lanes].reshape(nsub, SUBLANES, LANE)
            f2t = f2_ref[h, rows, lanes].reshape(nsub, SUBLANES, LANE)
            for i in range(PE_NA):
                row = (ag * PE_NA + i) * PEER_H + h
                sel = s2t >= rthr_ref[row, :, lanes][None]
                ws[i] = ws[i] + jnp.where(sel, f2t * rf1_ref[row, :, lanes][None], 0.0)
        for i in range(PE_NA):
            e0 = (ag * PE_NA + i) * PEER_NK + bh * PE_BH
            pw_ref[cur, g, e0:e0 + PE_BH, lq_lanes] = (
                ws[i].reshape(PE_BH, LANE) * gel_ref[cur, g, e0:e0 + PE_BH, lq_lanes]).astype(bf16)

    def stage(p, carry, with_act):
        if with_act:
            gel_ref[prev, p] = two_gelu(jnp.dot(u_ref[0], xt_ref[p], preferred_element_type=f32))
        acc_ref[p] += jnp.dot(vt_ref[0], pw_ref[prev, p], preferred_element_type=f32)
        for lq in range(MXU // LANE):
            for bh in range(n_bh):
                for ag in range(PE_AG):
                    weight_tile(p, lq, bh, ag)
        return carry

    @pl.when(j < PE_NCH - 1)
    def _():
        prep_rows()
        lax.fori_loop(0, n_half, functools.partial(stage, with_act=True), 0)

    @pl.when(j == PE_NCH - 1)
    def _():
        prep_rows()
        lax.fori_loop(0, n_half, functools.partial(stage, with_act=False), 0)

    @pl.when(j == PE_NCH)
    def _():
        for g in range(n_half):
            out = (acc_ref[g] + jnp.dot(vt_ref[0], pw_ref[prev, g], preferred_element_type=f32)).T
            for s in range(MXU // SEG):
                r = slice(g * MXU + s * SEG, g * MXU + (s + 1) * SEG)
                seg = g * (MXU // SEG) + s
                x2 = x1_ref[r, :] + ga_ref[seg:seg + 1, :] * out[s * SEG:(s + 1) * SEG, :]
                o_ref[r, :] = _rms(x2, fw_ref[...]) if final else x2


def _peer(u_c, vt_c, h2t, thr, f1t, s2t, f2t, x1, mod_seg, final_w):
    final = final_w is not None
    nseg_b = PE_TB // SEG
    rt = pl.BlockSpec((PEER_H, PEER_NK, PE_TB), lambda i, j: (0, 0, i))
    in_specs = [pl.BlockSpec((1, PE_CH, D), lambda i, j: (0, 0, 0)),
                pl.BlockSpec((1, PE_CH, D), lambda i, j: (jnp.minimum(j + 1, PE_NCH - 1), 0, 0)),
                pl.BlockSpec((1, D, PE_CH), lambda i, j: (jnp.maximum(j - 1, 0), 0, 0)),
                pl.BlockSpec((PE_TB // MXU, D, MXU), lambda i, j: (i, 0, 0)),
                rt, rt, rt, rt,
                pl.BlockSpec((PE_TB, D), lambda i, j: (i, 0)),
                pl.BlockSpec((nseg_b, D), lambda i, j: (i, 5))]
    args = [u_c, u_c, vt_c, h2t, thr, f1t, s2t, f2t, x1, mod_seg]
    if final:
        in_specs.append(pl.BlockSpec((1, D), lambda i, j: (0, 0)))
        args.append(final_w.reshape(1, D))
    return pl.pallas_call(
        functools.partial(_peer_kernel, final=final),
        grid=(N_TOK // PE_TB, PE_NCH + 1),
        in_specs=in_specs,
        out_specs=pl.BlockSpec((PE_TB, D), lambda i, j: (i, 0)),
        out_shape=jax.ShapeDtypeStruct((N_TOK, D), f32),
        scratch_shapes=[pltpu.VMEM((PE_TB // MXU, D, MXU), f32), pltpu.VMEM((2, PE_TB // MXU, PE_CH, MXU), bf16),
                        pltpu.VMEM((2, PE_TB // MXU, PE_CH, MXU), f32),
                        pltpu.VMEM((PE_AG * PE_NA * PEER_H, SUBLANES, PE_TB), f32),
                        pltpu.VMEM((PE_AG * PE_NA * PEER_H, SUBLANES, PE_TB), f32)],
        compiler_params=_cparams(("parallel", "arbitrary")),
        name="peer_experts_final" if final else "peer_experts",
    )(*args)


def _reorder_w_in(w):
    o = np.cumsum([0, 512, 512, 1024, 1024, GLA_RANK, SSD_INNER, SSD_CONV_DIM, SSD_H, 2 * D])
    q_k_v_go = w[:, o[0]:o[4]]
    gk_low = w[:, o[4]:o[5]]
    z = w[:, o[5]:o[6]]
    xbc = w[:, o[6]:o[7]]
    dt = w[:, o[7]:o[8]]
    gates = w[:, o[8]:o[9]]
    pad = lambda m: jnp.pad(m, ((0, 0), (0, LANE - m.shape[1])))
    low = jnp.concatenate([gk_low, jnp.zeros((D, LOW_DT0 - GLA_RANK), w.dtype), dt], axis=1)
    return jnp.concatenate([q_k_v_go, xbc, z, gates, pad(low)], axis=1).astype(bf16)


def kernel(x_prompt, x_sample, state_gla, state_ssd, state_conv, c_prompt, c_sample, w_ada, b_ada, norm1_w, w_in, gla_gk_w2, gla_gk_b, gla_norm_w, gla_proj, ssd_conv_w, ssd_conv_b, ssd_dt_bias, ssd_A_log, ssd_D, ssd_norm_w, ssd_proj, w_out, norm2_w, peer_wq, peer_keys1, peer_keys2, peer_u, peer_v, final_norm_w):
    x = jnp.concatenate([x_prompt.reshape(N_P, D), x_sample.reshape(N_S, D)], axis=0)
    c_all = jnp.concatenate([c_prompt, c_sample], axis=0)
    mod = _ada(c_all, w_ada, b_ada)
    seg2seq = np.concatenate([np.repeat(np.arange(B_P), SEG_PER_PSEQ), B_P + np.arange(B_S)])

    gla_st, ssd_st, conv_st = [], [], []
    for l in range(DEPTH):
        mod_seg = mod[l][seg2seq]
        h = _normmod(x, norm1_w[l], mod_seg, 1, 0)
        p = _mm(h, _reorder_w_in(w_in[l]), 1024, P_COLS // 3, "in_proj")

        w2pad = jnp.pad(gla_gk_w2[l], ((0, LANE - GLA_RANK), (0, 0)))
        low_lanes = lambda m: jnp.pad(m.reshape(1, SSD_H), ((0, 0), (LOW_DT0, LANE - LOW_DT0 - SSD_H)))
        dvec = jnp.repeat(ssd_D[l], SSD_P).reshape(1, SSD_INNER)
        (oa, gla_p, gla_s), (yb, ssd_p, ssd_s, conv_p, conv_s) = _mixers(
            _gla_call(p, w2pad, gla_gk_b[l], gla_norm_w[l], state_gla[l]),
            _ssd_call(p, state_conv[l], ssd_conv_w[l], ssd_conv_b[l], low_lanes(ssd_dt_bias[l]),
                      low_lanes(ssd_A_log[l]), dvec, ssd_norm_w[l], state_ssd[l]))

        x1, h2, h2t = _mix(oa, yb, p, x, mod_seg, norm2_w[l], gla_proj[l].astype(bf16),
                           ssd_proj[l].astype(bf16), w_out[l].astype(bf16))
        q = _mm(h2, peer_wq[l].astype(bf16), 512, PEER_H * PEER_DQ, "peer_query")
        thr, s2t, f1t, f2t = _route(q, peer_keys1[l], peer_keys2[l])
        u_c = peer_u[l].astype(bf16).reshape(PE_NCH, PE_CH, D)
        vt_c = peer_v[l].astype(bf16).reshape(PE_NCH, PE_CH, D).transpose(0, 2, 1)
        x = _peer(u_c, vt_c, h2t, thr, f1t, s2t, f2t, x1, mod_seg,
                  final_norm_w if l == DEPTH - 1 else None)

        seqs = lambda st: st.reshape((-1,) + st.shape[2:])
        gla_st.append((seqs(gla_p), seqs(gla_s)))
        ssd_st.append((seqs(ssd_p), seqs(ssd_s)))
        conv_st.append((seqs(conv_p), seqs(conv_s)))

    y_prompt = x[:N_P].reshape(B_P, T_P, D)
    y_sample = x[N_P:].reshape(B_S, T_S, D)
    stack = lambda pairs, k: jnp.stack([pr[k] for pr in pairs])
    return (y_prompt, y_sample, stack(gla_st, 0), stack(ssd_st, 0), stack(conv_st, 0),
            stack(gla_st, 1), stack(ssd_st, 1), stack(conv_st, 1))
```

```python
import functools

import jax
import jax.numpy as jnp
import numpy as np
from jax import lax
from jax.experimental import pallas as pl
from jax.experimental.pallas import tpu as pltpu

f32 = jnp.float32
bf16 = jnp.bfloat16

D = 1024
DEPTH = 2
B_P, T_P = 8, 2048
B_S, T_S = 32, 64
N_P = B_P * T_P
N_S = B_S * T_S
N_TOK = N_P + N_S
SEG = 64
NSEG = N_TOK // SEG
NSEG_P = N_P // SEG
SEG_PER_PSEQ = T_P // SEG
EPS = 1e-6

GLA_H, GLA_DK, GLA_DV = 4, 128, 256
GLA_RANK = 16
GLA_GATE_NORM = 16.0
SSD_INNER = 2048
SSD_P = 64
SSD_H = 32
SSD_G = 4
SSD_N = 128
SSD_HG = SSD_H // SSD_G
SSD_GW = SSD_INNER // SSD_G
CONV_W = 4
SSD_CONV_DIM = SSD_INNER + 2 * SSD_G * SSD_N

PEER_H = 8
PEER_NK = 128
PEER_DQ = 256
PEER_TOPK = 16

C_Q, C_K, C_V, C_GO = 0, 512, 1024, 2048
C_XBC, C_Z, C_GATE = 3072, 6144, 8192
C_LOW = 10240
LOW_DT0 = 32
P_COLS = C_LOW + 128

LANE = 128
SUBLANES = 8
VMEM_LIMIT = 56 * 1024 * 1024


def _cparams(sem):
    return pltpu.CompilerParams(dimension_semantics=sem, vmem_limit_bytes=VMEM_LIMIT)


def _dot(a, b):
    return jnp.dot(a.astype(bf16), b.astype(bf16), preferred_element_type=f32)


def _dot_nt(a, b):
    return lax.dot_general(a.astype(bf16), b.astype(bf16), (((1,), (1,)), ((), ())), preferred_element_type=f32)


def _dot_tn(a, b):
    return lax.dot_general(a.astype(bf16), b.astype(bf16), (((0,), (0,)), ((), ())), preferred_element_type=f32)


def _split3(x):
    hi = x.astype(bf16)
    r = x - hi.astype(f32)
    mid = r.astype(bf16)
    lo = (r - mid.astype(f32)).astype(bf16)
    return hi, mid, lo


def _dot3(a, b):
    ah, am, _ = _split3(a)
    bh, bm, _ = _split3(b)
    d = functools.partial(jnp.dot, preferred_element_type=f32)
    return d(ah, bh) + (d(ah, bm) + d(am, bh))


def _dot3_nt(a, b):
    ah, am, _ = _split3(a)
    bh, bm, _ = _split3(b)
    d = functools.partial(lax.dot_general, dimension_numbers=(((1,), (1,)), ((), ())), preferred_element_type=f32)
    return d(ah, bh) + (d(ah, bm) + d(am, bh))


def _cumsum_rows(x):
    n = x.shape[0]
    tri = (lax.broadcasted_iota(jnp.int32, (n, n), 0) >= lax.broadcasted_iota(jnp.int32, (n, n), 1)).astype(bf16)
    hi, mid, lo = _split3(x)
    d = functools.partial(jnp.dot, preferred_element_type=f32)
    return d(tri, hi) + (d(tri, mid) + d(tri, lo))


def _silu(x):
    return x * jax.nn.sigmoid(x)


def _softplus(x):
    return jnp.maximum(x, 0.0) + jnp.log1p(jnp.exp(-jnp.abs(x)))


def _rms(x, w):
    return x * lax.rsqrt(jnp.mean(x * x, axis=-1, keepdims=True) + EPS) * w


def _ada_kernel(c_ref, w_ref, b_ref, o_ref):
    o_ref[0] = _dot3(_silu(c_ref[...]), w_ref[0]) + b_ref[0]


def _ada(c_all, w_ada, b_ada):
    nb = c_all.shape[0]
    tn = 1536
    return pl.pallas_call(
        _ada_kernel,
        grid=(DEPTH, 6 * D // tn),
        in_specs=[pl.BlockSpec((nb, D), lambda l, j: (0, 0)),
                  pl.BlockSpec((1, D, tn), lambda l, j: (l, 0, j)),
                  pl.BlockSpec((1, 1, tn), lambda l, j: (l, 0, j))],
        out_specs=pl.BlockSpec((1, nb, tn), lambda l, j: (l, 0, j)),
        out_shape=jax.ShapeDtypeStruct((DEPTH, nb, 6 * D), f32),
        compiler_params=_cparams(("parallel", "parallel")),
        name="ada_mod",
    )(c_all, w_ada, b_ada.reshape(DEPTH, 1, 6 * D))


NM_TB = 512


def _normmod_kernel(x_ref, w_ref, sc_ref, sh_ref, o_ref):
    for s in range(NM_TB // SEG):
        r = slice(s * SEG, (s + 1) * SEG)
        y = _rms(x_ref[r, :], w_ref[...])
        o_ref[r, :] = (y * (1.0 + sc_ref[s:s + 1, :]) + sh_ref[s:s + 1, :]).astype(bf16)


def _normmod(x, w, mod_seg, sc_col, sh_col):
    nseg_b = NM_TB // SEG
    return pl.pallas_call(
        _normmod_kernel,
        grid=(N_TOK // NM_TB,),
        in_specs=[pl.BlockSpec((NM_TB, D), lambda i: (i, 0)),
                  pl.BlockSpec((1, D), lambda i: (0, 0)),
                  pl.BlockSpec((nseg_b, D), lambda i: (i, sc_col)),
                  pl.BlockSpec((nseg_b, D), lambda i: (i, sh_col))],
        out_specs=pl.BlockSpec((NM_TB, D), lambda i: (i, 0)),
        out_shape=jax.ShapeDtypeStruct((N_TOK, D), bf16),
        compiler_params=_cparams(("parallel",)),
        name="norm_mod",
    )(x, w.reshape(1, D), mod_seg, mod_seg)


def _mm_kernel(a_ref, b_ref, o_ref):
    o_ref[...] = jnp.dot(a_ref[...], b_ref[...], preferred_element_type=f32)


def _mm(a, b, tm, tn, name):
    m, k = a.shape
    n = b.shape[1]
    return pl.pallas_call(
        _mm_kernel,
        grid=(n // tn, m // tm),
        in_specs=[pl.BlockSpec((tm, k), lambda j, i: (i, 0)),
                  pl.BlockSpec((k, tn), lambda j, i: (0, j))],
        out_specs=pl.BlockSpec((tm, tn), lambda j, i: (i, j)),
        out_shape=jax.ShapeDtypeStruct((m, n), f32),
        compiler_params=_cparams(("parallel", "parallel")),
        name=name,
    )(a, b)


NSTREAM = 2
STEPS_P = NSEG_P // NSTREAM
STEPS_S = (NSEG - NSEG_P) // NSTREAM
HALF_TOK = N_TOK // NSTREAM


def _is_prompt(c):
    return c < STEPS_P


def _sample_seq(c):
    return jnp.maximum(c - STEPS_P, 0)


def _prompt_seq(c):
    return jnp.minimum(c // SEG_PER_PSEQ, B_P // NSTREAM - 1)


LOADED, STEP, COMPUTED = "loaded", "step", "computed"


def _segment(c, r):
    return jnp.where(c < STEPS_P, r * STEPS_P + c, NSEG_P + r * STEPS_S + (c - STEPS_P))


def _gla_kernel(r, q_ref, k_ref, v_ref, g_ref, low_ref, w2_ref, b2_ref, nw_ref, s0_ref,
                o_ref, stp_ref, sts_ref, st_ref):
    c = pl.program_id(0)
    is_p = _is_prompt(c)
    pos = c % SEG_PER_PSEQ

    @pl.when(jnp.logical_and(is_p, pos == 0))
    def _():
        st_ref[r] = jnp.zeros(st_ref.shape[1:], f32)

    @pl.when(jnp.logical_not(is_p))
    def _():
        for h in range(GLA_H):
            st_ref[r, h] = s0_ref[r, 0, h].T

    yield LOADED
    pre = _dot3(low_ref[...], w2_ref[...]) + b2_ref[...]
    gk = -_softplus(-pre) / GLA_GATE_NORM
    b = _cumsum_rows(gk)
    b_last = b[SEG - 1:SEG, :]
    e_b = jnp.exp(b)
    e_nb = jnp.exp(-b)
    e_end = jnp.exp(b_last - b)
    e_last = jnp.exp(b_last)
    causal = lax.broadcasted_iota(jnp.int32, (SEG, SEG), 0) >= lax.broadcasted_iota(jnp.int32, (SEG, SEG), 1)
    for h in range(GLA_H):
        ks = slice(h * GLA_DK, (h + 1) * GLA_DK)
        vs = slice(h * GLA_DV, (h + 1) * GLA_DV)
        k = k_ref[:, ks]
        q_t = (q_ref[:, ks] * (GLA_DK ** -0.5)) * e_b[:, ks]
        att = jnp.where(causal, _dot_nt(q_t, k * e_nb[:, ks]), 0.0)
        v = v_ref[:, vs]
        st = st_ref[r, h]
        o = _dot(att, v) + _dot_nt(q_t, st)
        st_ref[r, h] = st * e_last[:, ks] + _dot_tn(v, k * e_end[:, ks])
        o_ref[r, :, vs] = (_rms(o, nw_ref[...]) * _silu(g_ref[:, vs])).astype(bf16)
        yield STEP

    yield COMPUTED
    @pl.when(jnp.logical_and(is_p, pos == SEG_PER_PSEQ - 1))
    def _():
        for h in range(GLA_H):
            stp_ref[r, 0, h] = st_ref[r, h].T

    @pl.when(jnp.logical_not(is_p))
    def _():
        for h in range(GLA_H):
            sts_ref[r, 0, h] = st_ref[r, h].T


def _gla_call(p, w2pad, b2, norm_w, s0):
    st_dims = (GLA_H, GLA_DK, GLA_DV)
    st_block = (NSTREAM, 1) + st_dims
    kd, vd = GLA_H * GLA_DK, GLA_H * GLA_DV

    def stream_specs(r):
        return [pl.BlockSpec((SEG, kd), lambda c: (_segment(c, r), C_Q // kd)),
                pl.BlockSpec((SEG, kd), lambda c: (_segment(c, r), C_K // kd)),
                pl.BlockSpec((SEG, vd), lambda c: (_segment(c, r), C_V // vd)),
                pl.BlockSpec((SEG, vd), lambda c: (_segment(c, r), C_GO // vd)),
                pl.BlockSpec((SEG, LANE), lambda c: (_segment(c, r), C_LOW // LANE))]

    return dict(
        stream_specs=stream_specs, stream_args=(p, p, p, p, p),
        in_specs=[pl.BlockSpec((LANE, kd), lambda c: (0, 0)),
                  pl.BlockSpec((1, kd), lambda c: (0, 0)),
                  pl.BlockSpec((1, GLA_DV), lambda c: (0, 0)),
                  pl.BlockSpec(st_block, lambda c: (0, _sample_seq(c), 0, 0, 0))],
        args=(w2pad, b2.reshape(1, -1), norm_w.reshape(1, -1), s0.reshape((NSTREAM, B_S // NSTREAM) + st_dims)),
        out_specs=[pl.BlockSpec((NSTREAM, SEG, vd), lambda c: (0, c, 0)),
                   pl.BlockSpec(st_block, lambda c: (0, _prompt_seq(c), 0, 0, 0)),
                   pl.BlockSpec(st_block, lambda c: (0, _sample_seq(c), 0, 0, 0))],
        out_shape=[jax.ShapeDtypeStruct((NSTREAM, HALF_TOK, vd), bf16),
                   jax.ShapeDtypeStruct((NSTREAM, B_P // NSTREAM) + st_dims, f32),
                   jax.ShapeDtypeStruct((NSTREAM, B_S // NSTREAM) + st_dims, f32)],
        scratch_shapes=[pltpu.VMEM((NSTREAM, GLA_H, GLA_DV, GLA_DK), f32)])


CARRY = SUBLANES
EXT = SEG + CARRY


def _ssd_kernel(r, xbc_ref, z_ref, dt_ref, cw_ref, cb_ref, dtb_ref, alog_ref, dvec_ref, nw_ref, cp_all, s0_all,
                o_all, stp_all, sts_all, cvp_all, cvs_all, ext_all, st_all):
    cp_ref, s0_ref, o_ref = cp_all.at[r], s0_all.at[r], o_all.at[r]
    stp_ref, sts_ref, cvp_ref, cvs_ref = stp_all.at[r], sts_all.at[r], cvp_all.at[r], cvs_all.at[r]
    ext, st_ref = ext_all.at[r], st_all.at[r]
    c = pl.program_id(0)
    is_p = _is_prompt(c)
    pos = c % SEG_PER_PSEQ

    @pl.when(jnp.logical_and(is_p, pos == 0))
    def _():
        ext[0:CARRY, :] = jnp.zeros((CARRY, SSD_CONV_DIM), f32)
        st_ref[...] = jnp.zeros_like(st_ref)

    @pl.when(jnp.logical_not(is_p))
    def _():
        ext[0:CARRY, :] = jnp.zeros((CARRY, SSD_CONV_DIM), f32)
        ext[CARRY - (CONV_W - 1):CARRY, :] = cp_ref[0]
        st_ref[...] = s0_ref[0].reshape(st_ref.shape)

    yield LOADED
    ext[CARRY:EXT, :] = xbc_ref[...]
    acc = cb_ref[...] + ext[pl.ds(CARRY - (CONV_W - 1), SEG), :] * cw_ref[0:1, :]
    for i in range(1, CONV_W):
        acc = acc + ext[pl.ds(CARRY - (CONV_W - 1) + i, SEG), :] * cw_ref[i:i + 1, :]
    ext[0:CARRY, :] = ext[SEG:EXT, :]
    ext[CARRY:EXT, :] = _silu(acc)
    conv = ext.at[pl.ds(CARRY, SEG), :]
    yield STEP

    dt = _softplus(dt_ref[...] + dtb_ref[...])
    a = _cumsum_rows(dt * (-jnp.exp(alog_ref[...])))
    a_t = a.T
    dt_t = dt.T
    a_last = a[SEG - 1:SEG, :]
    e_a = jnp.exp(a)
    w_col = dt * jnp.exp(a_last - a)
    e_last = jnp.exp(a_last)
    causal = lax.broadcasted_iota(jnp.int32, (SEG, SEG), 0) >= lax.broadcasted_iota(jnp.int32, (SEG, SEG), 1)
    lane_lo = lax.broadcasted_iota(jnp.int32, (SEG, 2 * SSD_P), 1) < SSD_P
    sub_lo = lax.broadcasted_iota(jnp.int32, (2 * SSD_P, SSD_N), 0) < SSD_P
    nbc = SSD_G * SSD_N

    def pair_cols(m, ja, jb):
        return jnp.where(lane_lo, jnp.broadcast_to(m[:, ja:ja + 1], (SEG, 2 * SSD_P)),
                         jnp.broadcast_to(m[:, jb:jb + 1], (SEG, 2 * SSD_P)))

    for g in range(SSD_G):
        j0 = LOW_DT0 + g * SSD_HG
        bc = conv[:, SSD_INNER + g * SSD_N:SSD_INNER + (g + 1) * SSD_N]
        cc = conv[:, SSD_INNER + nbc + g * SSD_N:SSD_INNER + nbc + (g + 1) * SSD_N]
        cbm = _dot_nt(cc, bc)

        def w_intra(j):
            seg = a[:, j:j + 1] - a_t[j:j + 1, :]
            return cbm * jnp.exp(jnp.where(causal, seg, -jnp.inf)) * dt_t[j:j + 1, :]

        ys = []
        for p in range(SSD_HG // 2):
            ja, jb = j0 + 2 * p, j0 + 2 * p + 1
            pp = g * (SSD_HG // 2) + p
            xp = conv[:, pp * 2 * SSD_P:(pp + 1) * 2 * SSD_P]
            y_intra = jnp.where(lane_lo, _dot(w_intra(ja), xp), _dot(w_intra(jb), xp))
            sp = st_ref[pp]
            y_inter = _dot_nt(cc, sp) * pair_cols(e_a, ja, jb)
            ds = _dot_tn(xp * pair_cols(w_col, ja, jb), bc)
            rs = jnp.where(sub_lo, jnp.broadcast_to(e_last[:, ja:ja + 1], (2 * SSD_P, SSD_N)),
                           jnp.broadcast_to(e_last[:, jb:jb + 1], (2 * SSD_P, SSD_N)))
            st_ref[pp] = sp * rs + ds
            ys.append(y_intra + y_inter + dvec_ref[:, pp * 2 * SSD_P:(pp + 1) * 2 * SSD_P] * xp)
            yield STEP
        gw = slice(g * SSD_GW, (g + 1) * SSD_GW)
        y = jnp.concatenate(ys, axis=1) * _silu(z_ref[:, gw])
        o_ref[:, gw] = _rms(y, nw_ref[:, gw]).astype(bf16)

    yield COMPUTED
    @pl.when(jnp.logical_and(is_p, pos == SEG_PER_PSEQ - 1))
    def _():
        stp_ref[0] = st_ref[...].reshape(stp_ref.shape[1:])
        cvp_ref[0] = xbc_ref[SEG - (CONV_W - 1):SEG, :]

    @pl.when(jnp.logical_not(is_p))
    def _():
        sts_ref[0] = st_ref[...].reshape(sts_ref.shape[1:])
        cvs_ref[0] = xbc_ref[SEG - (CONV_W - 1):SEG, :]


def _ssd_call(p, conv_prev, conv_w, conv_b, dtb, alog, dvec, norm_w, s0):
    st_dims = (SSD_H, SSD_P, SSD_N)
    cv_dims = (CONV_W - 1, SSD_CONV_DIM)
    st_block = (NSTREAM, 1) + st_dims
    cv_block = (NSTREAM, 1) + cv_dims
    full = lambda c: (0, 0)
    sample_blk = lambda c: (0, _sample_seq(c)) + (0,) * 3
    prompt_blk = lambda c: (0, _prompt_seq(c)) + (0,) * 3

    def stream_specs(r):
        return [pl.BlockSpec((SEG, SSD_CONV_DIM), lambda c: (_segment(c, r), C_XBC // SSD_CONV_DIM)),
                pl.BlockSpec((SEG, SSD_INNER), lambda c: (_segment(c, r), C_Z // SSD_INNER)),
                pl.BlockSpec((SEG, LANE), lambda c: (_segment(c, r), C_LOW // LANE))]

    return dict(
        stream_specs=stream_specs, stream_args=(p, p, p),
        in_specs=[pl.BlockSpec((CONV_W, SSD_CONV_DIM), full),
                  pl.BlockSpec((1, SSD_CONV_DIM), full),
                  pl.BlockSpec((1, LANE), full),
                  pl.BlockSpec((1, LANE), full),
                  pl.BlockSpec((1, SSD_INNER), full),
                  pl.BlockSpec((1, SSD_INNER), full),
                  pl.BlockSpec(cv_block, lambda c: sample_blk(c)[:4]),
                  pl.BlockSpec(st_block, sample_blk)],
        args=(conv_w, conv_b.reshape(1, -1), dtb, alog, dvec, norm_w.reshape(1, -1),
              conv_prev.reshape((NSTREAM, B_S // NSTREAM) + cv_dims), s0.reshape((NSTREAM, B_S // NSTREAM) + st_dims)),
        out_specs=[pl.BlockSpec((NSTREAM, SEG, SSD_INNER), lambda c: (0, c, 0)),
                   pl.BlockSpec(st_block, prompt_blk),
                   pl.BlockSpec(st_block, sample_blk),
                   pl.BlockSpec(cv_block, lambda c: prompt_blk(c)[:4]),
                   pl.BlockSpec(cv_block, lambda c: sample_blk(c)[:4])],
        out_shape=[jax.ShapeDtypeStruct((NSTREAM, HALF_TOK, SSD_INNER), bf16),
                   jax.ShapeDtypeStruct((NSTREAM, B_P // NSTREAM) + st_dims, f32),
                   jax.ShapeDtypeStruct((NSTREAM, B_S // NSTREAM) + st_dims, f32),
                   jax.ShapeDtypeStruct((NSTREAM, B_P // NSTREAM) + cv_dims, f32),
                   jax.ShapeDtypeStruct((NSTREAM, B_S // NSTREAM) + cv_dims, f32)],
        scratch_shapes=[pltpu.VMEM((NSTREAM, EXT, SSD_CONV_DIM), f32),
                        pltpu.VMEM((NSTREAM, SSD_H // 2, 2 * SSD_P, SSD_N), f32)])


def _mixers(gla, ssd):
    halves = (gla, ssd)
    kernels = (_gla_kernel, _ssd_kernel)
    n_str = [len(m["stream_args"]) for m in halves]
    n_in = [len(m["in_specs"]) for m in halves]
    n_out = [len(m["out_specs"]) for m in halves]
    n_scr = [len(m["scratch_shapes"]) for m in halves]

    def take(refs, counts):
        out, at = [], 0
        for n in counts:
            out.append(refs[at:at + n])
            at += n
        return out, refs[at:]

    def body(*refs):
        streams, rest = take(refs, [n for n in n_str for _ in range(NSTREAM)])
        shared, rest = take(rest, n_in)
        outs, rest = take(rest, n_out)
        scr, _ = take(rest, n_scr)
        parts = [kernels[m](r, *streams[m * NSTREAM + r], *shared[m], *outs[m], *scr[m])
                 for m in range(len(halves)) for r in range(NSTREAM)]
        for part in parts:
            assert next(part) == LOADED
        active = list(parts)
        while active:
            active = [part for part in active if next(part) != COMPUTED]
        for part in parts:
            next(part, None)

    outs = pl.pallas_call(
        body,
        grid=(STEPS_P + STEPS_S,),
        in_specs=([s for m in halves for r in range(NSTREAM) for s in m["stream_specs"](r)]
                  + gla["in_specs"] + ssd["in_specs"]),
        out_specs=gla["out_specs"] + ssd["out_specs"],
        out_shape=gla["out_shape"] + ssd["out_shape"],
        scratch_shapes=gla["scratch_shapes"] + ssd["scratch_shapes"],
        compiler_params=_cparams(("arbitrary",)),
        name="token_mixers",
    )(*[a for m in halves for _ in range(NSTREAM) for a in m["stream_args"]], *gla["args"], *ssd["args"])
    return outs[:n_out[0]], outs[n_out[0]:]


MIX_TB = 512


def _mix_kernel(oa_ref, yb_ref, gt_ref, x_ref, ga_ref, sc_ref, sh_ref, nw_ref, wa_ref, wb_ref, wo_ref,
                x1_ref, h2_ref, h2t_ref):
    br_a = jnp.dot(oa_ref[0], wa_ref[...], preferred_element_type=f32)
    br_b = jnp.dot(yb_ref[0], wb_ref[...], preferred_element_type=f32)
    g_a = jax.nn.sigmoid(gt_ref[:, 0:D])
    g_b = jax.nn.sigmoid(gt_ref[:, D:2 * D])
    y = _dot(g_a * br_a + g_b * br_b, wo_ref[...])
    for s in range(MIX_TB // SEG):
        r = slice(s * SEG, (s + 1) * SEG)
        x1 = x_ref[r, :] + ga_ref[s:s + 1, :] * y[r, :]
        x1_ref[r, :] = x1
        h2_ref[r, :] = (_rms(x1, nw_ref[...]) * (1.0 + sc_ref[s:s + 1, :]) + sh_ref[s:s + 1, :]).astype(bf16)
    for g in range(MIX_TB // MXU):
        h2t_ref[g] = h2_ref[g * MXU:(g + 1) * MXU, :].T


def _mix(oa, yb, p, x, mod_seg, norm2_w, wa, wb, wo):
    nseg_b = MIX_TB // SEG
    full = lambda i: (0, 0)
    n_pb = N_P // MIX_TB
    pb_per_stream, sb_per_stream = n_pb // NSTREAM, (N_S // MIX_TB) // NSTREAM

    def mixer_block(i):
        r = jnp.where(i < n_pb, i // pb_per_stream, (i - n_pb) // sb_per_stream)
        blk = jnp.where(i < n_pb, i % pb_per_stream, pb_per_stream + (i - n_pb) % sb_per_stream)
        return r, blk, 0

    return pl.pallas_call(
        _mix_kernel,
        grid=(N_TOK // MIX_TB,),
        in_specs=[pl.BlockSpec((1, MIX_TB, D), mixer_block),
                  pl.BlockSpec((1, MIX_TB, SSD_INNER), mixer_block),
                  pl.BlockSpec((MIX_TB, 2 * D), lambda i: (i, C_GATE // (2 * D))),
                  pl.BlockSpec((MIX_TB, D), lambda i: (i, 0)),
                  pl.BlockSpec((nseg_b, D), lambda i: (i, 2)),
                  pl.BlockSpec((nseg_b, D), lambda i: (i, 4)),
                  pl.BlockSpec((nseg_b, D), lambda i: (i, 3)),
                  pl.BlockSpec((1, D), full),
                  pl.BlockSpec((D, D), full),
                  pl.BlockSpec((SSD_INNER, D), full),
                  pl.BlockSpec((D, D), full)],
        out_specs=[pl.BlockSpec((MIX_TB, D), lambda i: (i, 0)),
                   pl.BlockSpec((MIX_TB, D), lambda i: (i, 0)),
                   pl.BlockSpec((MIX_TB // MXU, D, MXU), lambda i: (i, 0, 0))],
        out_shape=[jax.ShapeDtypeStruct((N_TOK, D), f32),
                   jax.ShapeDtypeStruct((N_TOK, D), bf16),
                   jax.ShapeDtypeStruct((N_TOK // MXU, D, MXU), bf16)],
        compiler_params=_cparams(("parallel",)),
        name="mix_out",
    )(oa, yb, p, x, mod_seg, mod_seg, mod_seg, norm2_w.reshape(1, D), wa, wb, wo)


RT_TL = 128
HALF = PEER_DQ // 2


RT_HPS = 8


def _batcher_sort_net(n):
    def merge(lo, hi, r):
        step = 2 * r
        if step < hi - lo:
            yield from merge(lo, hi, step)
            yield from merge(lo + r, hi, step)
            yield from ((i, i + r) for i in range(lo + r, hi - r, step))
        else:
            yield (lo, lo + r)

    def sort(lo, hi):
        if hi - lo >= 1:
            mid = lo + (hi - lo) // 2
            yield from sort(lo, mid)
            yield from sort(mid + 1, hi)
            yield from merge(lo, hi, 1)

    return tuple(sort(0, n - 1))


def _bitonic_merge_net(n):
    net, d = [], n // 2
    while d >= 1:
        net += [(i, i + d) for i in range(n) if (i // d) % 2 == 0]
        d //= 2
    return tuple(net)


_SORT16 = _batcher_sort_net(PEER_TOPK)
_MERGE16 = _bitonic_merge_net(PEER_TOPK)
N_CAND_VREGS = 10


def _compare_exchange(x, net):
    for i, j in net:
        x[i], x[j] = jnp.maximum(x[i], x[j]), jnp.minimum(x[i], x[j])


def _merge_across_sublanes(x, n_valid):
    for shift in (4, 2, 1):
        y = [pltpu.roll(v, shift, 0) for v in x]
        merged = []
        for k in range(PEER_TOPK):
            a = x[k] if k < n_valid else None
            b = y[PEER_TOPK - 1 - k] if PEER_TOPK - 1 - k < n_valid else None
            merged.append(jnp.maximum(a, b) if (a is not None and b is not None) else (a if b is None else b))
        x = merged
        _compare_exchange(x, _MERGE16)
        n_valid = PEER_TOPK
    return x


def _top16_sorted(s):
    x = [s[SUBLANES * k:SUBLANES * (k + 1), :] for k in range(PEER_NK // SUBLANES)]
    _compare_exchange(x, _SORT16)
    return _merge_across_sublanes(x, PEER_TOPK)


def _last_passing(s1k, v2, tau):
    def passes(t):
        return s1k + t >= tau

    best = v2[0]
    path = []
    for half in (8, 4, 2, 1):
        t = _select_by_path(path, [v2[lo + half] for lo in range(0, PEER_TOPK, 2 * half)])
        c = passes(t)
        best = jnp.where(c, t, best)
        path.append(c)
    return jnp.where(passes(v2[0]), best, jnp.inf)


def _select_by_path(path, leaves):
    if not path:
        return leaves[0]
    n = len(leaves) // 2
    return jnp.where(path[0], _select_by_path(path[1:], leaves[n:]), _select_by_path(path[1:], leaves[:n]))


def _route_kernel(q_ref, k1_ref, k2_ref, thr_ref, s2_ref, f1_ref, f2_ref):
    sub = lax.broadcasted_iota(jnp.int32, (SUBLANES, RT_TL), 0)
    ninf = jnp.float32(-jnp.inf)

    def by_sublane(vs):
        out = vs[0]
        for r in range(1, SUBLANES):
            out = jnp.where(sub == r, vs[r], out)
        return out

    for hh in range(RT_HPS):
        s1 = _dot3_nt(k1_ref[hh], q_ref[:, hh * PEER_DQ:hh * PEER_DQ + HALF])
        s2 = _dot3_nt(k2_ref[hh], q_ref[:, hh * PEER_DQ + HALF:(hh + 1) * PEER_DQ])
        s2_ref[hh] = s2
        v1 = _top16_sorted(s1)
        v2 = _top16_sorted(s2)
        v2lo, v2hi, v1hi = by_sublane(v2[:8]), by_sublane(v2[8:]), by_sublane(v1[8:])
        cands = [v1[0] + v2lo, v1[0] + v2hi, v1[1] + v2lo]
        for i in range(2, 8):
            cands.append(jnp.where(sub < PEER_TOPK // (i + 1), v1[i] + v2lo, ninf))
        cands.append(v1hi + v2[0])
        assert len(cands) == N_CAND_VREGS
        _compare_exchange(cands, tuple((i, j) for i, j in _SORT16 if j < N_CAND_VREGS))
        top = _merge_across_sublanes(cands, N_CAND_VREGS)
        zsum = jnp.ones_like(top[0])
        for k in range(1, PEER_TOPK):
            zsum = zsum + jnp.exp(top[k] - top[0])
        tau = top[PEER_TOPK - 1]
        for k in range(PEER_NK // SUBLANES):
            rows = slice(SUBLANES * k, SUBLANES * (k + 1))
            s1k = s1[rows, :]
            thr_ref[hh, rows, :] = _last_passing(s1k, v2, tau)
        f1_ref[hh] = jnp.exp(s1 - v1[0][0:1, :]) * (0.5 / zsum[0:1, :])
        f2_ref[hh] = jnp.exp(s2 - v2[0][0:1, :])


def _route(q, keys1, keys2):
    tile = pl.BlockSpec((RT_HPS, PEER_NK, RT_TL), lambda i, h: (h, 0, i))
    big = jax.ShapeDtypeStruct((PEER_H, PEER_NK, N_TOK), f32)
    return pl.pallas_call(
        _route_kernel,
        grid=(N_TOK // RT_TL, PEER_H // RT_HPS),
        in_specs=[pl.BlockSpec((RT_TL, RT_HPS * PEER_DQ), lambda i, h: (i, h)),
                  pl.BlockSpec((RT_HPS, PEER_NK, HALF), lambda i, h: (h, 0, 0)),
                  pl.BlockSpec((RT_HPS, PEER_NK, HALF), lambda i, h: (h, 0, 0))],
        out_specs=[tile, tile, tile, tile],
        out_shape=[big, big, big, big],
        compiler_params=_cparams(("parallel", "parallel")),
        name="peer_route",
    )(q, keys1, keys2)


PE_TB = 512
PE_NA = 4
PE_AG = 2
PE_CH = PE_AG * PE_NA * PEER_NK
PE_NCH = PEER_NK * PEER_NK // PE_CH
PE_BH = 32
SQRT_HALF = 0.7071067811865476
MXU = 256


def _peer_kernel(*refs, final):
    if final:
        (u0_ref, u_ref, vt_ref, xt_ref, thr_ref, f1_ref, s2_ref, f2_ref, x1_ref, ga_ref, fw_ref,
         o_ref, acc_ref, pw_ref, gel_ref, rthr_ref, rf1_ref) = refs
    else:
        (u0_ref, u_ref, vt_ref, xt_ref, thr_ref, f1_ref, s2_ref, f2_ref, x1_ref, ga_ref,
         o_ref, acc_ref, pw_ref, gel_ref, rthr_ref, rf1_ref) = refs
    j = pl.program_id(1)

    n_half = PE_TB // MXU

    def two_gelu(act):
        return act * (1.0 + lax.erf(act * SQRT_HALF))

    @pl.when(j == 0)
    def _():
        acc_ref[...] = jnp.zeros_like(acc_ref)
        pw_ref[1] = jnp.zeros(pw_ref.shape[1:], bf16)
        for g in range(n_half):
            gel_ref[0, g] = two_gelu(jnp.dot(u0_ref[0], xt_ref[g], preferred_element_type=f32))

    cur = j % 2
    prev = 1 - cur

    def prep_rows():
        for ia in range(PE_AG * PE_NA):
            for h in range(PEER_H):
                a = j * (PE_AG * PE_NA) + ia
                rthr_ref[ia * PEER_H + h] = jnp.broadcast_to(thr_ref[h, pl.ds(a, 1), :], (SUBLANES, PE_TB))
                rf1_ref[ia * PEER_H + h] = jnp.broadcast_to(f1_ref[h, pl.ds(a, 1), :], (SUBLANES, PE_TB))

    nsub = PE_BH // SUBLANES
    n_bh = PEER_NK // PE_BH

    def weight_tile(g, lq, bh, ag):
        lanes = pl.ds(pl.multiple_of(g * MXU + lq * LANE, LANE), LANE)
        lq_lanes = slice(lq * LANE, (lq + 1) * LANE)
        rows = slice(bh * PE_BH, (bh + 1) * PE_BH)
        ws = [jnp.zeros((nsub, SUBLANES, LANE), f32) for _ in range(PE_NA)]
        for h in range(PEER_H):
            s2t = s2_ref[h, rows, lanes].reshape(nsub, SUBLANES, LANE)
            f2t = f2_ref[h, rows, lanes].reshape(nsub, SUBLANES, LANE)
            for i in range(PE_NA):
                row = (ag * PE_NA + i) * PEER_H + h
                sel = s2t >= rthr_ref[row, :, lanes][None]
                ws[i] = ws[i] + jnp.where(sel, f2t * rf1_ref[row, :, lanes][None], 0.0)
        for i in range(PE_NA):
            e0 = (ag * PE_NA + i) * PEER_NK + bh * PE_BH
            pw_ref[cur, g, e0:e0 + PE_BH, lq_lanes] = (
                ws[i].reshape(PE_BH, LANE) * gel_ref[cur, g, e0:e0 + PE_BH, lq_lanes]).astype(bf16)

    def stage(p, carry, with_act):
        if with_act:
            gel_ref[prev, p] = two_gelu(jnp.dot(u_ref[0], xt_ref[p], preferred_element_type=f32))
        acc_ref[p] += jnp.dot(vt_ref[0], pw_ref[prev, p], preferred_element_type=f32)
        for lq in range(MXU // LANE):
            for bh in range(n_bh):
                for ag in range(PE_AG):
                    weight_tile(p, lq, bh, ag)
        return carry

    @pl.when(j < PE_NCH - 1)
    def _():
        prep_rows()
        lax.fori_loop(0, n_half, functools.partial(stage, with_act=True), 0)

    @pl.when(j == PE_NCH - 1)
    def _():
        prep_rows()
        lax.fori_loop(0, n_half, functools.partial(stage, with_act=False), 0)

    @pl.when(j == PE_NCH)
    def _():
        for g in range(n_half):
            out = (acc_ref[g] + jnp.dot(vt_ref[0], pw_ref[prev, g], preferred_element_type=f32)).T
            for s in range(MXU // SEG):
                r = slice(g * MXU + s * SEG, g * MXU + (s + 1) * SEG)
                seg = g * (MXU // SEG) + s
                x2 = x1_ref[r, :] + ga_ref[seg:seg + 1, :] * out[s * SEG:(s + 1) * SEG, :]
                o_ref[r, :] = _rms(x2, fw_ref[...]) if final else x2


def _peer(u_c, vt_c, h2t, thr, f1t, s2t, f2t, x1, mod_seg, final_w):
    final = final_w is not None
    nseg_b = PE_TB // SEG
    rt = pl.BlockSpec((PEER_H, PEER_NK, PE_TB), lambda i, j: (0, 0, i))
    in_specs = [pl.BlockSpec((1, PE_CH, D), lambda i, j: (0, 0, 0)),
                pl.BlockSpec((1, PE_CH, D), lambda i, j: (jnp.minimum(j + 1, PE_NCH - 1), 0, 0)),
                pl.BlockSpec((1, D, PE_CH), lambda i, j: (jnp.maximum(j - 1, 0), 0, 0)),
                pl.BlockSpec((PE_TB // MXU, D, MXU), lambda i, j: (i, 0, 0)),
                rt, rt, rt, rt,
                pl.BlockSpec((PE_TB, D), lambda i, j: (i, 0)),
                pl.BlockSpec((nseg_b, D), lambda i, j: (i, 5))]
    args = [u_c, u_c, vt_c, h2t, thr, f1t, s2t, f2t, x1, mod_seg]
    if final:
        in_specs.append(pl.BlockSpec((1, D), lambda i, j: (0, 0)))
        args.append(final_w.reshape(1, D))
    return pl.pallas_call(
        functools.partial(_peer_kernel, final=final),
        grid=(N_TOK // PE_TB, PE_NCH + 1),
        in_specs=in_specs,
        out_specs=pl.BlockSpec((PE_TB, D), lambda i, j: (i, 0)),
        out_shape=jax.ShapeDtypeStruct((N_TOK, D), f32),
        scratch_shapes=[pltpu.VMEM((PE_TB // MXU, D, MXU), f32), pltpu.VMEM((2, PE_TB // MXU, PE_CH, MXU), bf16),
                        pltpu.VMEM((2, PE_TB // MXU, PE_CH, MXU), f32),
                        pltpu.VMEM((PE_AG * PE_NA * PEER_H, SUBLANES, PE_TB), f32),
                        pltpu.VMEM((PE_AG * PE_NA * PEER_H, SUBLANES, PE_TB), f32)],
        compiler_params=_cparams(("parallel", "arbitrary")),
        name="peer_experts_final" if final else "peer_experts",
    )(*args)


def _reorder_w_in(w):
    o = np.cumsum([0, 512, 512, 1024, 1024, GLA_RANK, SSD_INNER, SSD_CONV_DIM, SSD_H, 2 * D])
    q_k_v_go = w[:, o[0]:o[4]]
    gk_low = w[:, o[4]:o[5]]
    z = w[:, o[5]:o[6]]
    xbc = w[:, o[6]:o[7]]
    dt = w[:, o[7]:o[8]]
    gates = w[:, o[8]:o[9]]
    pad = lambda m: jnp.pad(m, ((0, 0), (0, LANE - m.shape[1])))
    low = jnp.concatenate([gk_low, jnp.zeros((D, LOW_DT0 - GLA_RANK), w.dtype), dt], axis=1)
    return jnp.concatenate([q_k_v_go, xbc, z, gates, pad(low)], axis=1).astype(bf16)


def kernel(x_prompt, x_sample, state_gla, state_ssd, state_conv, c_prompt, c_sample, w_ada, b_ada, norm1_w, w_in, gla_gk_w2, gla_gk_b, gla_norm_w, gla_proj, ssd_conv_w, ssd_conv_b, ssd_dt_bias, ssd_A_log, ssd_D, ssd_norm_w, ssd_proj, w_out, norm2_w, peer_wq, peer_keys1, peer_keys2, peer_u, peer_v, final_norm_w):
    x = jnp.concatenate([x_prompt.reshape(N_P, D), x_sample.reshape(N_S, D)], axis=0)
    c_all = jnp.concatenate([c_prompt, c_sample], axis=0)
    mod = _ada(c_all, w_ada, b_ada)
    seg2seq = np.concatenate([np.repeat(np.arange(B_P), SEG_PER_PSEQ), B_P + np.arange(B_S)])

    gla_st, ssd_st, conv_st = [], [], []
    for l in range(DEPTH):
        mod_seg = mod[l][seg2seq]
        h = _normmod(x, norm1_w[l], mod_seg, 1, 0)
        p = _mm(h, _reorder_w_in(w_in[l]), 1024, P_COLS // 3, "in_proj")

        w2pad = jnp.pad(gla_gk_w2[l], ((0, LANE - GLA_RANK), (0, 0)))
        low_lanes = lambda m: jnp.pad(m.reshape(1, SSD_H), ((0, 0), (LOW_DT0, LANE - LOW_DT0 - SSD_H)))
        dvec = jnp.repeat(ssd_D[l], SSD_P).reshape(1, SSD_INNER)
        (oa, gla_p, gla_s), (yb, ssd_p, ssd_s, conv_p, conv_s) = _mixers(
            _gla_call(p, w2pad, gla_gk_b[l], gla_norm_w[l], state_gla[l]),
            _ssd_call(p, state_conv[l], ssd_conv_w[l], ssd_conv_b[l], low_lanes(ssd_dt_bias[l]),
                      low_lanes(ssd_A_log[l]), dvec, ssd_norm_w[l], state_ssd[l]))

        x1, h2, h2t = _mix(oa, yb, p, x, mod_seg, norm2_w[l], gla_proj[l].astype(bf16),
                           ssd_proj[l].astype(bf16), w_out[l].astype(bf16))
        q = _mm(h2, peer_wq[l].astype(bf16), 512, PEER_H * PEER_DQ, "peer_query")
        thr, s2t, f1t, f2t = _route(q, peer_keys1[l], peer_keys2[l])
        u_c = peer_u[l].astype(bf16).reshape(PE_NCH, PE_CH, D)
        vt_c = peer_v[l].astype(bf16).reshape(PE_NCH, PE_CH, D).transpose(0, 2, 1)
        x = _peer(u_c, vt_c, h2t, thr, f1t, s2t, f2t, x1, mod_seg,
                  final_norm_w if l == DEPTH - 1 else None)

        seqs = lambda st: st.reshape((-1,) + st.shape[2:])
        gla_st.append((seqs(gla_p), seqs(gla_s)))
        ssd_st.append((seqs(ssd_p), seqs(ssd_s)))
        conv_st.append((seqs(conv_p), seqs(conv_s)))

    y_prompt = x[:N_P].reshape(B_P, T_P, D)
    y_sample = x[N_P:].reshape(B_S, T_S, D)
    stack = lambda pairs, k: jnp.stack([pr[k] for pr in pairs])
    return (y_prompt, y_sample, stack(gla_st, 0), stack(ssd_st, 0), stack(conv_st, 0),
            stack(gla_st, 1), stack(ssd_st, 1), stack(conv_st, 1))
```

```python
import functools

import jax
import jax.numpy as jnp
import numpy as np
from jax import lax
from jax.experimental import pallas as pl
from jax.experimental.pallas import tpu as pltpu

f32 = jnp.float32
bf16 = jnp.bfloat16

D = 1024
DEPTH = 2
B_P, T_P = 8, 2048
B_S, T_S = 32, 64
N_P = B_P * T_P
N_S = B_S * T_S
N_TOK = N_P + N_S
SEG = 64
NSEG = N_TOK // SEG
NSEG_P = N_P // SEG
SEG_PER_PSEQ = T_P // SEG
EPS = 1e-6

GLA_H, GLA_DK, GLA_DV = 4, 128, 256
GLA_RANK = 16
GLA_GATE_NORM = 16.0
SSD_INNER = 2048
SSD_P = 64
SSD_H = 32
SSD_G = 4
SSD_N = 128
SSD_HG = SSD_H // SSD_G
SSD_GW = SSD_INNER // SSD_G
CONV_W = 4
SSD_CONV_DIM = SSD_INNER + 2 * SSD_G * SSD_N

PEER_H = 8
PEER_NK = 128
PEER_DQ = 256
PEER_TOPK = 16

C_Q, C_K, C_V, C_GO = 0, 512, 1024, 2048
C_XBC, C_Z, C_GATE = 3072, 6144, 8192
C_LOW = 10240
LOW_DT0 = 32
P_COLS = C_LOW + 128

LANE = 128
SUBLANES = 8
VMEM_LIMIT = 56 * 1024 * 1024


def _cparams(sem):
    return pltpu.CompilerParams(dimension_semantics=sem, vmem_limit_bytes=VMEM_LIMIT)


def _dot(a, b):
    return jnp.dot(a.astype(bf16), b.astype(bf16), preferred_element_type=f32)


def _dot_nt(a, b):
    return lax.dot_general(a.astype(bf16), b.astype(bf16), (((1,), (1,)), ((), ())), preferred_element_type=f32)


def _dot_tn(a, b):
    return lax.dot_general(a.astype(bf16), b.astype(bf16), (((0,), (0,)), ((), ())), preferred_element_type=f32)


def _split3(x):
    hi = x.astype(bf16)
    r = x - hi.astype(f32)
    mid = r.astype(bf16)
    lo = (r - mid.astype(f32)).astype(bf16)
    return hi, mid, lo


def _dot3(a, b):
    ah, am, _ = _split3(a)
    bh, bm, _ = _split3(b)
    d = functools.partial(jnp.dot, preferred_element_type=f32)
    return d(ah, bh) + (d(ah, bm) + d(am, bh))


def _dot3_nt(a, b):
    ah, am, _ = _split3(a)
    bh, bm, _ = _split3(b)
    d = functools.partial(lax.dot_general, dimension_numbers=(((1,), (1,)), ((), ())), preferred_element_type=f32)
    return d(ah, bh) + (d(ah, bm) + d(am, bh))


def _cumsum_rows(x):
    n = x.shape[0]
    tri = (lax.broadcasted_iota(jnp.int32, (n, n), 0) >= lax.broadcasted_iota(jnp.int32, (n, n), 1)).astype(bf16)
    hi, mid, lo = _split3(x)
    d = functools.partial(jnp.dot, preferred_element_type=f32)
    return d(tri, hi) + (d(tri, mid) + d(tri, lo))


def _silu(x):
    return x * jax.nn.sigmoid(x)


def _softplus(x):
    return jnp.maximum(x, 0.0) + jnp.log1p(jnp.exp(-jnp.abs(x)))


def _rms(x, w):
    return x * lax.rsqrt(jnp.mean(x * x, axis=-1, keepdims=True) + EPS) * w


def _ada_kernel(c_ref, w_ref, b_ref, o_ref):
    o_ref[0] = _dot3(_silu(c_ref[...]), w_ref[0]) + b_ref[0]


def _ada(c_all, w_ada, b_ada):
    nb = c_all.shape[0]
    tn = 1536
    return pl.pallas_call(
        _ada_kernel,
        grid=(DEPTH, 6 * D // tn),
        in_specs=[pl.BlockSpec((nb, D), lambda l, j: (0, 0)),
                  pl.BlockSpec((1, D, tn), lambda l, j: (l, 0, j)),
                  pl.BlockSpec((1, 1, tn), lambda l, j: (l, 0, j))],
        out_specs=pl.BlockSpec((1, nb, tn), lambda l, j: (l, 0, j)),
        out_shape=jax.ShapeDtypeStruct((DEPTH, nb, 6 * D), f32),
        compiler_params=_cparams(("parallel", "parallel")),
        name="ada_mod",
    )(c_all, w_ada, b_ada.reshape(DEPTH, 1, 6 * D))


NM_TB = 512


def _normmod_kernel(x_ref, w_ref, sc_ref, sh_ref, o_ref):
    for s in range(NM_TB // SEG):
        r = slice(s * SEG, (s + 1) * SEG)
        y = _rms(x_ref[r, :], w_ref[...])
        o_ref[r, :] = (y * (1.0 + sc_ref[s:s + 1, :]) + sh_ref[s:s + 1, :]).astype(bf16)


def _normmod(x, w, mod_seg, sc_col, sh_col):
    nseg_b = NM_TB // SEG
    return pl.pallas_call(
        _normmod_kernel,
        grid=(N_TOK // NM_TB,),
        in_specs=[pl.BlockSpec((NM_TB, D), lambda i: (i, 0)),
                  pl.BlockSpec((1, D), lambda i: (0, 0)),
                  pl.BlockSpec((nseg_b, D), lambda i: (i, sc_col)),
                  pl.BlockSpec((nseg_b, D), lambda i: (i, sh_col))],
        out_specs=pl.BlockSpec((NM_TB, D), lambda i: (i, 0)),
        out_shape=jax.ShapeDtypeStruct((N_TOK, D), bf16),
        compiler_params=_cparams(("parallel",)),
        name="norm_mod",
    )(x, w.reshape(1, D), mod_seg, mod_seg)


def _mm_kernel(a_ref, b_ref, o_ref):
    o_ref[...] = jnp.dot(a_ref[...], b_ref[...], preferred_element_type=f32)


def _mm(a, b, tm, tn, name):
    m, k = a.shape
    n = b.shape[1]
    return pl.pallas_call(
        _mm_kernel,
        grid=(n // tn, m // tm),
        in_specs=[pl.BlockSpec((tm, k), lambda j, i: (i, 0)),
                  pl.BlockSpec((k, tn), lambda j, i: (0, j))],
        out_specs=pl.BlockSpec((tm, tn), lambda j, i: (i, j)),
        out_shape=jax.ShapeDtypeStruct((m, n), f32),
        compiler_params=_cparams(("parallel", "parallel")),
        name=name,
    )(a, b)


NSTREAM = 2
STEPS_P = NSEG_P // NSTREAM
STEPS_S = (NSEG - NSEG_P) // NSTREAM
HALF_TOK = N_TOK // NSTREAM


def _is_prompt(c):
    return c < STEPS_P


def _sample_seq(c):
    return jnp.maximum(c - STEPS_P, 0)


def _prompt_seq(c):
    return jnp.minimum(c // SEG_PER_PSEQ, B_P // NSTREAM - 1)


LOADED, STEP, COMPUTED = "loaded", "step", "computed"


def _segment(c, r):
    return jnp.where(c < STEPS_P, r * STEPS_P + c, NSEG_P + r * STEPS_S + (c - STEPS_P))


def _gla_kernel(r, q_ref, k_ref, v_ref, g_ref, low_ref, w2_ref, b2_ref, nw_ref, s0_ref,
                o_ref, stp_ref, sts_ref, st_ref):
    c = pl.program_id(0)
    is_p = _is_prompt(c)
    pos = c % SEG_PER_PSEQ

    @pl.when(jnp.logical_and(is_p, pos == 0))
    def _():
        st_ref[r] = jnp.zeros(st_ref.shape[1:], f32)

    @pl.when(jnp.logical_not(is_p))
    def _():
        for h in range(GLA_H):
            st_ref[r, h] = s0_ref[r, 0, h].T

    yield LOADED
    pre = _dot3(low_ref[...], w2_ref[...]) + b2_ref[...]
    gk = -_softplus(-pre) / GLA_GATE_NORM
    b = _cumsum_rows(gk)
    b_last = b[SEG - 1:SEG, :]
    e_b = jnp.exp(b)
    e_nb = jnp.exp(-b)
    e_end = jnp.exp(b_last - b)
    e_last = jnp.exp(b_last)
    causal = lax.broadcasted_iota(jnp.int32, (SEG, SEG), 0) >= lax.broadcasted_iota(jnp.int32, (SEG, SEG), 1)
    for h in range(GLA_H):
        ks = slice(h * GLA_DK, (h + 1) * GLA_DK)
        vs = slice(h * GLA_DV, (h + 1) * GLA_DV)
        k = k_ref[:, ks]
        q_t = (q_ref[:, ks] * (GLA_DK ** -0.5)) * e_b[:, ks]
        att = jnp.where(causal, _dot_nt(q_t, k * e_nb[:, ks]), 0.0)
        v = v_ref[:, vs]
        st = st_ref[r, h]
        o = _dot(att, v) + _dot_nt(q_t, st)
        st_ref[r, h] = st * e_last[:, ks] + _dot_tn(v, k * e_end[:, ks])
        o_ref[r, :, vs] = (_rms(o, nw_ref[...]) * _silu(g_ref[:, vs])).astype(bf16)
        yield STEP

    yield COMPUTED
    @pl.when(jnp.logical_and(is_p, pos == SEG_PER_PSEQ - 1))
    def _():
        for h in range(GLA_H):
            stp_ref[r, 0, h] = st_ref[r, h].T

    @pl.when(jnp.logical_not(is_p))
    def _():
        for h in range(GLA_H):
            sts_ref[r, 0, h] = st_ref[r, h].T


def _gla_call(p, w2pad, b2, norm_w, s0):
    st_dims = (GLA_H, GLA_DK, GLA_DV)
    st_block = (NSTREAM, 1) + st_dims
    kd, vd = GLA_H * GLA_DK, GLA_H * GLA_DV

    def stream_specs(r):
        return [pl.BlockSpec((SEG, kd), lambda c: (_segment(c, r), C_Q // kd)),
                pl.BlockSpec((SEG, kd), lambda c: (_segment(c, r), C_K // kd)),
                pl.BlockSpec((SEG, vd), lambda c: (_segment(c, r), C_V // vd)),
                pl.BlockSpec((SEG, vd), lambda c: (_segment(c, r), C_GO // vd)),
                pl.BlockSpec((SEG, LANE), lambda c: (_segment(c, r), C_LOW // LANE))]

    return dict(
        stream_specs=stream_specs, stream_args=(p, p, p, p, p),
        in_specs=[pl.BlockSpec((LANE, kd), lambda c: (0, 0)),
                  pl.BlockSpec((1, kd), lambda c: (0, 0)),
                  pl.BlockSpec((1, GLA_DV), lambda c: (0, 0)),
                  pl.BlockSpec(st_block, lambda c: (0, _sample_seq(c), 0, 0, 0))],
        args=(w2pad, b2.reshape(1, -1), norm_w.reshape(1, -1), s0.reshape((NSTREAM, B_S // NSTREAM) + st_dims)),
        out_specs=[pl.BlockSpec((NSTREAM, SEG, vd), lambda c: (0, c, 0)),
                   pl.BlockSpec(st_block, lambda c: (0, _prompt_seq(c), 0, 0, 0)),
                   pl.BlockSpec(st_block, lambda c: (0, _sample_seq(c), 0, 0, 0))],
        out_shape=[jax.ShapeDtypeStruct((NSTREAM, HALF_TOK, vd), bf16),
                   jax.ShapeDtypeStruct((NSTREAM, B_P // NSTREAM) + st_dims, f32),
                   jax.ShapeDtypeStruct((NSTREAM, B_S // NSTREAM) + st_dims, f32)],
        scratch_shapes=[pltpu.VMEM((NSTREAM, GLA_H, GLA_DV, GLA_DK), f32)])


CARRY = SUBLANES
EXT = SEG + CARRY
CONV_COLS = LANE
assert SSD_CONV_DIM % CONV_COLS == 0


def _ssd_kernel(r, xbc_ref, z_ref, dt_ref, cw_ref, cb_ref, dtb_ref, alog_ref, dvec_ref, nw_ref, cp_all, s0_all,
                o_all, stp_all, sts_all, cvp_all, cvs_all, ext_all, st_all):
    cp_ref, s0_ref, o_ref = cp_all.at[r], s0_all.at[r], o_all.at[r]
    stp_ref, sts_ref, cvp_ref, cvs_ref = stp_all.at[r], sts_all.at[r], cvp_all.at[r], cvs_all.at[r]
    ext, st_ref = ext_all.at[r], st_all.at[r]
    c = pl.program_id(0)
    is_p = _is_prompt(c)
    pos = c % SEG_PER_PSEQ

    @pl.when(jnp.logical_and(is_p, pos == 0))
    def _():
        ext[0:CARRY, :] = jnp.zeros((CARRY, SSD_CONV_DIM), f32)
        st_ref[...] = jnp.zeros_like(st_ref)

    @pl.when(jnp.logical_not(is_p))
    def _():
        ext[0:CARRY, :] = jnp.zeros((CARRY, SSD_CONV_DIM), f32)
        ext[CARRY - (CONV_W - 1):CARRY, :] = cp_ref[0]
        st_ref[...] = s0_ref[0].reshape(st_ref.shape)

    yield LOADED
    ext[CARRY:EXT, :] = xbc_ref[...]
    for j in range(SSD_CONV_DIM // CONV_COLS):
        cs = slice(j * CONV_COLS, (j + 1) * CONV_COLS)
        acc = cb_ref[:, cs] + ext[pl.ds(CARRY - (CONV_W - 1), SEG), cs] * cw_ref[0:1, cs]
        for i in range(1, CONV_W):
            acc = acc + ext[pl.ds(CARRY - (CONV_W - 1) + i, SEG), cs] * cw_ref[i:i + 1, cs]
        ext[0:CARRY, cs] = ext[SEG:EXT, cs]
        ext[CARRY:EXT, cs] = _silu(acc)
    conv = ext.at[pl.ds(CARRY, SEG), :]
    yield STEP

    dt = _softplus(dt_ref[...] + dtb_ref[...])
    a = _cumsum_rows(dt * (-jnp.exp(alog_ref[...])))
    a_t = a.T
    dt_t = dt.T
    a_last = a[SEG - 1:SEG, :]
    e_a = jnp.exp(a)
    w_col = dt * jnp.exp(a_last - a)
    e_last = jnp.exp(a_last)
    causal = lax.broadcasted_iota(jnp.int32, (SEG, SEG), 0) >= lax.broadcasted_iota(jnp.int32, (SEG, SEG), 1)
    lane_lo = lax.broadcasted_iota(jnp.int32, (SEG, 2 * SSD_P), 1) < SSD_P
    sub_lo = lax.broadcasted_iota(jnp.int32, (2 * SSD_P, SSD_N), 0) < SSD_P
    nbc = SSD_G * SSD_N

    def pair_cols(m, ja, jb):
        return jnp.where(lane_lo, jnp.broadcast_to(m[:, ja:ja + 1], (SEG, 2 * SSD_P)),
                         jnp.broadcast_to(m[:, jb:jb + 1], (SEG, 2 * SSD_P)))

    for g in range(SSD_G):
        j0 = LOW_DT0 + g * SSD_HG
        bc = conv[:, SSD_INNER + g * SSD_N:SSD_INNER + (g + 1) * SSD_N]
        cc = conv[:, SSD_INNER + nbc + g * SSD_N:SSD_INNER + nbc + (g + 1) * SSD_N]
        cbm = _dot_nt(cc, bc)

        def w_intra(j):
            seg = a[:, j:j + 1] - a_t[j:j + 1, :]
            return cbm * jnp.exp(jnp.where(causal, seg, -jnp.inf)) * dt_t[j:j + 1, :]

        ys = []
        for p in range(SSD_HG // 2):
            ja, jb = j0 + 2 * p, j0 + 2 * p + 1
            pp = g * (SSD_HG // 2) + p
            xp = conv[:, pp * 2 * SSD_P:(pp + 1) * 2 * SSD_P]
            y_intra = jnp.where(lane_lo, _dot(w_intra(ja), xp), _dot(w_intra(jb), xp))
            sp = st_ref[pp]
            y_inter = _dot_nt(cc, sp) * pair_cols(e_a, ja, jb)
            ds = _dot_tn(xp * pair_cols(w_col, ja, jb), bc)
            rs = jnp.where(sub_lo, jnp.broadcast_to(e_last[:, ja:ja + 1], (2 * SSD_P, SSD_N)),
                           jnp.broadcast_to(e_last[:, jb:jb + 1], (2 * SSD_P, SSD_N)))
            st_ref[pp] = sp * rs + ds
            ys.append(y_intra + y_inter + dvec_ref[:, pp * 2 * SSD_P:(pp + 1) * 2 * SSD_P] * xp)
            yield STEP
        gw = slice(g * SSD_GW, (g + 1) * SSD_GW)
        y = jnp.concatenate(ys, axis=1) * _silu(z_ref[:, gw])
        o_ref[:, gw] = _rms(y, nw_ref[:, gw]).astype(bf16)

    yield COMPUTED
    @pl.when(jnp.logical_and(is_p, pos == SEG_PER_PSEQ - 1))
    def _():
        stp_ref[0] = st_ref[...].reshape(stp_ref.shape[1:])
        cvp_ref[0] = xbc_ref[SEG - (CONV_W - 1):SEG, :]

    @pl.when(jnp.logical_not(is_p))
    def _():
        sts_ref[0] = st_ref[...].reshape(sts_ref.shape[1:])
        cvs_ref[0] = xbc_ref[SEG - (CONV_W - 1):SEG, :]


def _ssd_call(p, conv_prev, conv_w, conv_b, dtb, alog, dvec, norm_w, s0):
    st_dims = (SSD_H, SSD_P, SSD_N)
    cv_dims = (CONV_W - 1, SSD_CONV_DIM)
    st_block = (NSTREAM, 1) + st_dims
    cv_block = (NSTREAM, 1) + cv_dims
    full = lambda c: (0, 0)
    sample_blk = lambda c: (0, _sample_seq(c)) + (0,) * 3
    prompt_blk = lambda c: (0, _prompt_seq(c)) + (0,) * 3

    def stream_specs(r):
        return [pl.BlockSpec((SEG, SSD_CONV_DIM), lambda c: (_segment(c, r), C_XBC // SSD_CONV_DIM)),
                pl.BlockSpec((SEG, SSD_INNER), lambda c: (_segment(c, r), C_Z // SSD_INNER)),
                pl.BlockSpec((SEG, LANE), lambda c: (_segment(c, r), C_LOW // LANE))]

    return dict(
        stream_specs=stream_specs, stream_args=(p, p, p),
        in_specs=[pl.BlockSpec((CONV_W, SSD_CONV_DIM), full),
                  pl.BlockSpec((1, SSD_CONV_DIM), full),
                  pl.BlockSpec((1, LANE), full),
                  pl.BlockSpec((1, LANE), full),
                  pl.BlockSpec((1, SSD_INNER), full),
                  pl.BlockSpec((1, SSD_INNER), full),
                  pl.BlockSpec(cv_block, lambda c: sample_blk(c)[:4]),
                  pl.BlockSpec(st_block, sample_blk)],
        args=(conv_w, conv_b.reshape(1, -1), dtb, alog, dvec, norm_w.reshape(1, -1),
              conv_prev.reshape((NSTREAM, B_S // NSTREAM) + cv_dims), s0.reshape((NSTREAM, B_S // NSTREAM) + st_dims)),
        out_specs=[pl.BlockSpec((NSTREAM, SEG, SSD_INNER), lambda c: (0, c, 0)),
                   pl.BlockSpec(st_block, prompt_blk),
                   pl.BlockSpec(st_block, sample_blk),
                   pl.BlockSpec(cv_block, lambda c: prompt_blk(c)[:4]),
                   pl.BlockSpec(cv_block, lambda c: sample_blk(c)[:4])],
        out_shape=[jax.ShapeDtypeStruct((NSTREAM, HALF_TOK, SSD_INNER), bf16),
                   jax.ShapeDtypeStruct((NSTREAM, B_P // NSTREAM) + st_dims, f32),
                   jax.ShapeDtypeStruct((NSTREAM, B_S // NSTREAM) + st_dims, f32),
                   jax.ShapeDtypeStruct((NSTREAM, B_P // NSTREAM) + cv_dims, f32),
                   jax.ShapeDtypeStruct((NSTREAM, B_S // NSTREAM) + cv_dims, f32)],
        scratch_shapes=[pltpu.VMEM((NSTREAM, EXT, SSD_CONV_DIM), f32),
                        pltpu.VMEM((NSTREAM, SSD_H // 2, 2 * SSD_P, SSD_N), f32)])


def _mixers(gla, ssd):
    halves = (gla, ssd)
    kernels = (_gla_kernel, _ssd_kernel)
    n_str = [len(m["stream_args"]) for m in halves]
    n_in = [len(m["in_specs"]) for m in halves]
    n_out = [len(m["out_specs"]) for m in halves]
    n_scr = [len(m["scratch_shapes"]) for m in halves]

    def take(refs, counts):
        out, at = [], 0
        for n in counts:
            out.append(refs[at:at + n])
            at += n
        return out, refs[at:]

    def body(*refs):
        streams, rest = take(refs, [n for n in n_str for _ in range(NSTREAM)])
        shared, rest = take(rest, n_in)
        outs, rest = take(rest, n_out)
        scr, _ = take(rest, n_scr)
        parts = [kernels[m](r, *streams[m * NSTREAM + r], *shared[m], *outs[m], *scr[m])
                 for m in range(len(halves)) for r in range(NSTREAM)]
        for part in parts:
            assert next(part) == LOADED
        active = list(parts)
        while active:
            active = [part for part in active if next(part) != COMPUTED]
        for part in parts:
            next(part, None)

    outs = pl.pallas_call(
        body,
        grid=(STEPS_P + STEPS_S,),
        in_specs=([s for m in halves for r in range(NSTREAM) for s in m["stream_specs"](r)]
                  + gla["in_specs"] + ssd["in_specs"]),
        out_specs=gla["out_specs"] + ssd["out_specs"],
        out_shape=gla["out_shape"] + ssd["out_shape"],
        scratch_shapes=gla["scratch_shapes"] + ssd["scratch_shapes"],
        compiler_params=_cparams(("arbitrary",)),
        name="token_mixers",
    )(*[a for m in halves for _ in range(NSTREAM) for a in m["stream_args"]], *gla["args"], *ssd["args"])
    return outs[:n_out[0]], outs[n_out[0]:]


MIX_TB = 512


def _mix_kernel(oa_ref, yb_ref, gt_ref, x_ref, ga_ref, sc_ref, sh_ref, nw_ref, wa_ref, wb_ref, wo_ref,
                x1_ref, h2_ref, h2t_ref):
    br_a = jnp.dot(oa_ref[0], wa_ref[...], preferred_element_type=f32)
    br_b = jnp.dot(yb_ref[0], wb_ref[...], preferred_element_type=f32)
    g_a = jax.nn.sigmoid(gt_ref[:, 0:D])
    g_b = jax.nn.sigmoid(gt_ref[:, D:2 * D])
    y = _dot(g_a * br_a + g_b * br_b, wo_ref[...])
    for s in range(MIX_TB // SEG):
        r = slice(s * SEG, (s + 1) * SEG)
        x1 = x_ref[r, :] + ga_ref[s:s + 1, :] * y[r, :]
        x1_ref[r, :] = x1
        h2_ref[r, :] = (_rms(x1, nw_ref[...]) * (1.0 + sc_ref[s:s + 1, :]) + sh_ref[s:s + 1, :]).astype(bf16)
    for g in range(MIX_TB // MXU):
        h2t_ref[g] = h2_ref[g * MXU:(g + 1) * MXU, :].T


def _mix(oa, yb, p, x, mod_seg, norm2_w, wa, wb, wo):
    nseg_b = MIX_TB // SEG
    full = lambda i: (0, 0)
    n_pb = N_P // MIX_TB
    pb_per_stream, sb_per_stream = n_pb // NSTREAM, (N_S // MIX_TB) // NSTREAM

    def mixer_block(i):
        r = jnp.where(i < n_pb, i // pb_per_stream, (i - n_pb) // sb_per_stream)
        blk = jnp.where(i < n_pb, i % pb_per_stream, pb_per_stream + (i - n_pb) % sb_per_stream)
        return r, blk, 0

    return pl.pallas_call(
        _mix_kernel,
        grid=(N_TOK // MIX_TB,),
        in_specs=[pl.BlockSpec((1, MIX_TB, D), mixer_block),
                  pl.BlockSpec((1, MIX_TB, SSD_INNER), mixer_block),
                  pl.BlockSpec((MIX_TB, 2 * D), lambda i: (i, C_GATE // (2 * D))),
                  pl.BlockSpec((MIX_TB, D), lambda i: (i, 0)),
                  pl.BlockSpec((nseg_b, D), lambda i: (i, 2)),
                  pl.BlockSpec((nseg_b, D), lambda i: (i, 4)),
                  pl.BlockSpec((nseg_b, D), lambda i: (i, 3)),
                  pl.BlockSpec((1, D), full),
                  pl.BlockSpec((D, D), full),
                  pl.BlockSpec((SSD_INNER, D), full),
                  pl.BlockSpec((D, D), full)],
        out_specs=[pl.BlockSpec((MIX_TB, D), lambda i: (i, 0)),
                   pl.BlockSpec((MIX_TB, D), lambda i: (i, 0)),
                   pl.BlockSpec((MIX_TB // MXU, D, MXU), lambda i: (i, 0, 0))],
        out_shape=[jax.ShapeDtypeStruct((N_TOK, D), f32),
                   jax.ShapeDtypeStruct((N_TOK, D), bf16),
                   jax.ShapeDtypeStruct((N_TOK // MXU, D, MXU), bf16)],
        compiler_params=_cparams(("parallel",)),
        name="mix_out",
    )(oa, yb, p, x, mod_seg, mod_seg, mod_seg, norm2_w.reshape(1, D), wa, wb, wo)


RT_TL = 128
HALF = PEER_DQ // 2


RT_HPS = 8


def _batcher_sort_net(n):
    def merge(lo, hi, r):
        step = 2 * r
        if step < hi - lo:
            yield from merge(lo, hi, step)
            yield from merge(lo + r, hi, step)
            yield from ((i, i + r) for i in range(lo + r, hi - r, step))
        else:
            yield (lo, lo + r)

    def sort(lo, hi):
        if hi - lo >= 1:
            mid = lo + (hi - lo) // 2
            yield from sort(lo, mid)
            yield from sort(mid + 1, hi)
            yield from merge(lo, hi, 1)

    return tuple(sort(0, n - 1))


def _bitonic_merge_net(n):
    net, d = [], n // 2
    while d >= 1:
        net += [(i, i + d) for i in range(n) if (i // d) % 2 == 0]
        d //= 2
    return tuple(net)


_SORT16 = _batcher_sort_net(PEER_TOPK)
_MERGE16 = _bitonic_merge_net(PEER_TOPK)
N_CAND_VREGS = 10


def _compare_exchange(x, net):
    for i, j in net:
        x[i], x[j] = jnp.maximum(x[i], x[j]), jnp.minimum(x[i], x[j])


def _merge_across_sublanes(x, n_valid):
    for shift in (4, 2, 1):
        y = [pltpu.roll(v, shift, 0) for v in x]
        merged = []
        for k in range(PEER_TOPK):
            a = x[k] if k < n_valid else None
            b = y[PEER_TOPK - 1 - k] if PEER_TOPK - 1 - k < n_valid else None
            merged.append(jnp.maximum(a, b) if (a is not None and b is not None) else (a if b is None else b))
        x = merged
        _compare_exchange(x, _MERGE16)
        n_valid = PEER_TOPK
    return x


def _top16_sorted(s):
    x = [s[SUBLANES * k:SUBLANES * (k + 1), :] for k in range(PEER_NK // SUBLANES)]
    _compare_exchange(x, _SORT16)
    return _merge_across_sublanes(x, PEER_TOPK)


def _last_passing(s1k, v2, tau):
    def passes(t):
        return s1k + t >= tau

    best = v2[0]
    path = []
    for half in (8, 4, 2, 1):
        t = _select_by_path(path, [v2[lo + half] for lo in range(0, PEER_TOPK, 2 * half)])
        c = passes(t)
        best = jnp.where(c, t, best)
        path.append(c)
    return jnp.where(passes(v2[0]), best, jnp.inf)


def _select_by_path(path, leaves):
    if not path:
        return leaves[0]
    n = len(leaves) // 2
    return jnp.where(path[0], _select_by_path(path[1:], leaves[n:]), _select_by_path(path[1:], leaves[:n]))


def _route_kernel(q_ref, k1_ref, k2_ref, thr_ref, s2_ref, f1_ref, f2_ref):
    sub = lax.broadcasted_iota(jnp.int32, (SUBLANES, RT_TL), 0)
    ninf = jnp.float32(-jnp.inf)

    def by_sublane(vs):
        out = vs[0]
        for r in range(1, SUBLANES):
            out = jnp.where(sub == r, vs[r], out)
        return out

    for hh in range(RT_HPS):
        s1 = _dot3_nt(k1_ref[hh], q_ref[:, hh * PEER_DQ:hh * PEER_DQ + HALF])
        s2 = _dot3_nt(k2_ref[hh], q_ref[:, hh * PEER_DQ + HALF:(hh + 1) * PEER_DQ])
        s2_ref[hh] = s2
        v1 = _top16_sorted(s1)
        v2 = _top16_sorted(s2)
        v2lo, v2hi, v1hi = by_sublane(v2[:8]), by_sublane(v2[8:]), by_sublane(v1[8:])
        cands = [v1[0] + v2lo, v1[0] + v2hi, v1[1] + v2lo]
        for i in range(2, 8):
            cands.append(jnp.where(sub < PEER_TOPK // (i + 1), v1[i] + v2lo, ninf))
        cands.append(v1hi + v2[0])
        assert len(cands) == N_CAND_VREGS
        _compare_exchange(cands, tuple((i, j) for i, j in _SORT16 if j < N_CAND_VREGS))
        top = _merge_across_sublanes(cands, N_CAND_VREGS)
        zsum = jnp.ones_like(top[0])
        for k in range(1, PEER_TOPK):
            zsum = zsum + jnp.exp(top[k] - top[0])
        tau = top[PEER_TOPK - 1]
        for k in range(PEER_NK // SUBLANES):
            rows = slice(SUBLANES * k, SUBLANES * (k + 1))
            s1k = s1[rows, :]
            thr_ref[hh, rows, :] = _last_passing(s1k, v2, tau)
        f1_ref[hh] = jnp.exp(s1 - v1[0][0:1, :]) * (0.5 / zsum[0:1, :])
        f2_ref[hh] = jnp.exp(s2 - v2[0][0:1, :])


def _route(q, keys1, keys2):
    tile = pl.BlockSpec((RT_HPS, PEER_NK, RT_TL), lambda i, h: (h, 0, i))
    big = jax.ShapeDtypeStruct((PEER_H, PEER_NK, N_TOK), f32)
    return pl.pallas_call(
        _route_kernel,
        grid=(N_TOK // RT_TL, PEER_H // RT_HPS),
        in_specs=[pl.BlockSpec((RT_TL, RT_HPS * PEER_DQ), lambda i, h: (i, h)),
                  pl.BlockSpec((RT_HPS, PEER_NK, HALF), lambda i, h: (h, 0, 0)),
                  pl.BlockSpec((RT_HPS, PEER_NK, HALF), lambda i, h: (h, 0, 0))],
        out_specs=[tile, tile, tile, tile],
        out_shape=[big, big, big, big],
        compiler_params=_cparams(("parallel", "parallel")),
        name="peer_route",
    )(q, keys1, keys2)


PE_TB = 512
PE_NA = 4
PE_AG = 2
PE_CH = PE_AG * PE_NA * PEER_NK
PE_NCH = PEER_NK * PEER_NK // PE_CH
PE_BH = 32
SQRT_HALF = 0.7071067811865476
MXU = 256


def _peer_kernel(*refs, final):
    if final:
        (u0_ref, u_ref, vt_ref, xt_ref, thr_ref, f1_ref, s2_ref, f2_ref, x1_ref, ga_ref, fw_ref,
         o_ref, acc_ref, pw_ref, gel_ref, rthr_ref, rf1_ref) = refs
    else:
        (u0_ref, u_ref, vt_ref, xt_ref, thr_ref, f1_ref, s2_ref, f2_ref, x1_ref, ga_ref,
         o_ref, acc_ref, pw_ref, gel_ref, rthr_ref, rf1_ref) = refs
    j = pl.program_id(1)

    n_half = PE_TB // MXU

    def two_gelu(act):
        return act * (1.0 + lax.erf(act * SQRT_HALF))

    @pl.when(j == 0)
    def _():
        acc_ref[...] = jnp.zeros_like(acc_ref)
        pw_ref[1] = jnp.zeros(pw_ref.shape[1:], bf16)
        for g in range(n_half):
            gel_ref[0, g] = two_gelu(jnp.dot(u0_ref[0], xt_ref[g], preferred_element_type=f32))

    cur = j % 2
    prev = 1 - cur

    def prep_rows():
        for ia in range(PE_AG * PE_NA):
            for h in range(PEER_H):
                a = j * (PE_AG * PE_NA) + ia
                rthr_ref[ia * PEER_H + h] = jnp.broadcast_to(thr_ref[h, pl.ds(a, 1), :], (SUBLANES, PE_TB))
                rf1_ref[ia * PEER_H + h] = jnp.broadcast_to(f1_ref[h, pl.ds(a, 1), :], (SUBLANES, PE_TB))

    nsub = PE_BH // SUBLANES
    n_bh = PEER_NK // PE_BH

    def weight_tile(g, lq, bh, ag):
        lanes = pl.ds(pl.multiple_of(g * MXU + lq * LANE, LANE), LANE)
        lq_lanes = slice(lq * LANE, (lq + 1) * LANE)
        rows = slice(bh * PE_BH, (bh + 1) * PE_BH)
        ws = [jnp.zeros((nsub, SUBLANES, LANE), f32) for _ in range(PE_NA)]
        for h in range(PEER_H):
            s2t = s2_ref[h, rows, lanes].reshape(nsub, SUBLANES, LANE)
            f2t = f2_ref[h, rows, lanes].reshape(nsub, SUBLANES, LANE)
            for i in range(PE_NA):
                row = (ag * PE_NA + i) * PEER_H + h
                sel = s2t >= rthr_ref[row, :, lanes][None]
                ws[i] = ws[i] + jnp.where(sel, f2t * rf1_ref[row, :, lanes][None], 0.0)
        for i in range(PE_NA):
            e0 = (ag * PE_NA + i) * PEER_NK + bh * PE_BH
            pw_ref[cur, g, e0:e0 + PE_BH, lq_lanes] = (
                ws[i].reshape(PE_BH, LANE) * gel_ref[cur, g, e0:e0 + PE_BH, lq_lanes]).astype(bf16)

    def stage(p, carry, with_act):
        if with_act:
            gel_ref[prev, p] = two_gelu(jnp.dot(u_ref[0], xt_ref[p], preferred_element_type=f32))
        acc_ref[p] += jnp.dot(vt_ref[0], pw_ref[prev, p], preferred_element_type=f32)
        for lq in range(MXU // LANE):
            for bh in range(n_bh):
                for ag in range(PE_AG):
                    weight_tile(p, lq, bh, ag)
        return carry

    @pl.when(j < PE_NCH - 1)
    def _():
        prep_rows()
        lax.fori_loop(0, n_half, functools.partial(stage, with_act=True), 0)

    @pl.when(j == PE_NCH - 1)
    def _():
        prep_rows()
        lax.fori_loop(0, n_half, functools.partial(stage, with_act=False), 0)

    @pl.when(j == PE_NCH)
    def _():
        for g in range(n_half):
            out = (acc_ref[g] + jnp.dot(vt_ref[0], pw_ref[prev, g], preferred_element_type=f32)).T
            for s in range(MXU // SEG):
                r = slice(g * MXU + s * SEG, g * MXU + (s + 1) * SEG)
                seg = g * (MXU // SEG) + s
                x2 = x1_ref[r, :] + ga_ref[seg:seg + 1, :] * out[s * SEG:(s + 1) * SEG, :]
                o_ref[r, :] = _rms(x2, fw_ref[...]) if final else x2


def _peer(u_c, vt_c, h2t, thr, f1t, s2t, f2t, x1, mod_seg, final_w):
    final = final_w is not None
    nseg_b = PE_TB // SEG
    rt = pl.BlockSpec((PEER_H, PEER_NK, PE_TB), lambda i, j: (0, 0, i))
    in_specs = [pl.BlockSpec((1, PE_CH, D), lambda i, j: (0, 0, 0)),
                pl.BlockSpec((1, PE_CH, D), lambda i, j: (jnp.minimum(j + 1, PE_NCH - 1), 0, 0)),
                pl.BlockSpec((1, D, PE_CH), lambda i, j: (jnp.maximum(j - 1, 0), 0, 0)),
                pl.BlockSpec((PE_TB // MXU, D, MXU), lambda i, j: (i, 0, 0)),
                rt, rt, rt, rt,
                pl.BlockSpec((PE_TB, D), lambda i, j: (i, 0)),
                pl.BlockSpec((nseg_b, D), lambda i, j: (i, 5))]
    args = [u_c, u_c, vt_c, h2t, thr, f1t, s2t, f2t, x1, mod_seg]
    if final:
        in_specs.append(pl.BlockSpec((1, D), lambda i, j: (0, 0)))
        args.append(final_w.reshape(1, D))
    return pl.pallas_call(
        functools.partial(_peer_kernel, final=final),
        grid=(N_TOK // PE_TB, PE_NCH + 1),
        in_specs=in_specs,
        out_specs=pl.BlockSpec((PE_TB, D), lambda i, j: (i, 0)),
        out_shape=jax.ShapeDtypeStruct((N_TOK, D), f32),
        scratch_shapes=[pltpu.VMEM((PE_TB // MXU, D, MXU), f32), pltpu.VMEM((2, PE_TB // MXU, PE_CH, MXU), bf16),
                        pltpu.VMEM((2, PE_TB // MXU, PE_CH, MXU), f32),
                        pltpu.VMEM((PE_AG * PE_NA * PEER_H, SUBLANES, PE_TB), f32),
                        pltpu.VMEM((PE_AG * PE_NA * PEER_H, SUBLANES, PE_TB), f32)],
        compiler_params=_cparams(("parallel", "arbitrary")),
        name="peer_experts_final" if final else "peer_experts",
    )(*args)


def _reorder_w_in(w):
    o = np.cumsum([0, 512, 512, 1024, 1024, GLA_RANK, SSD_INNER, SSD_CONV_DIM, SSD_H, 2 * D])
    q_k_v_go = w[:, o[0]:o[4]]
    gk_low = w[:, o[4]:o[5]]
    z = w[:, o[5]:o[6]]
    xbc = w[:, o[6]:o[7]]
    dt = w[:, o[7]:o[8]]
    gates = w[:, o[8]:o[9]]
    pad = lambda m: jnp.pad(m, ((0, 0), (0, LANE - m.shape[1])))
    low = jnp.concatenate([gk_low, jnp.zeros((D, LOW_DT0 - GLA_RANK), w.dtype), dt], axis=1)
    return jnp.concatenate([q_k_v_go, xbc, z, gates, pad(low)], axis=1).astype(bf16)


def kernel(x_prompt, x_sample, state_gla, state_ssd, state_conv, c_prompt, c_sample, w_ada, b_ada, norm1_w, w_in, gla_gk_w2, gla_gk_b, gla_norm_w, gla_proj, ssd_conv_w, ssd_conv_b, ssd_dt_bias, ssd_A_log, ssd_D, ssd_norm_w, ssd_proj, w_out, norm2_w, peer_wq, peer_keys1, peer_keys2, peer_u, peer_v, final_norm_w):
    x = jnp.concatenate([x_prompt.reshape(N_P, D), x_sample.reshape(N_S, D)], axis=0)
    c_all = jnp.concatenate([c_prompt, c_sample], axis=0)
    mod = _ada(c_all, w_ada, b_ada)
    seg2seq = np.concatenate([np.repeat(np.arange(B_P), SEG_PER_PSEQ), B_P + np.arange(B_S)])

    gla_st, ssd_st, conv_st = [], [], []
    for l in range(DEPTH):
        mod_seg = mod[l][seg2seq]
        h = _normmod(x, norm1_w[l], mod_seg, 1, 0)
        p = _mm(h, _reorder_w_in(w_in[l]), 1024, P_COLS // 3, "in_proj")

        w2pad = jnp.pad(gla_gk_w2[l], ((0, LANE - GLA_RANK), (0, 0)))
        low_lanes = lambda m: jnp.pad(m.reshape(1, SSD_H), ((0, 0), (LOW_DT0, LANE - LOW_DT0 - SSD_H)))
        dvec = jnp.repeat(ssd_D[l], SSD_P).reshape(1, SSD_INNER)
        (oa, gla_p, gla_s), (yb, ssd_p, ssd_s, conv_p, conv_s) = _mixers(
            _gla_call(p, w2pad, gla_gk_b[l], gla_norm_w[l], state_gla[l]),
            _ssd_call(p, state_conv[l], ssd_conv_w[l], ssd_conv_b[l], low_lanes(ssd_dt_bias[l]),
                      low_lanes(ssd_A_log[l]), dvec, ssd_norm_w[l], state_ssd[l]))

        x1, h2, h2t = _mix(oa, yb, p, x, mod_seg, norm2_w[l], gla_proj[l].astype(bf16),
                           ssd_proj[l].astype(bf16), w_out[l].astype(bf16))
        q = _mm(h2, peer_wq[l].astype(bf16), 512, PEER_H * PEER_DQ, "peer_query")
        thr, s2t, f1t, f2t = _route(q, peer_keys1[l], peer_keys2[l])
        u_c = peer_u[l].astype(bf16).reshape(PE_NCH, PE_CH, D)
        vt_c = peer_v[l].astype(bf16).reshape(PE_NCH, PE_CH, D).transpose(0, 2, 1)
        x = _peer(u_c, vt_c, h2t, thr, f1t, s2t, f2t, x1, mod_seg,
                  final_norm_w if l == DEPTH - 1 else None)

        seqs = lambda st: st.reshape((-1,) + st.shape[2:])
        gla_st.append((seqs(gla_p), seqs(gla_s)))
        ssd_st.append((seqs(ssd_p), seqs(ssd_s)))
        conv_st.append((seqs(conv_p), seqs(conv_s)))

    y_prompt = x[:N_P].reshape(B_P, T_P, D)
    y_sample = x[N_P:].reshape(B_S, T_S, D)
    stack = lambda pairs, k: jnp.stack([pr[k] for pr in pairs])
    return (y_prompt, y_sample, stack(gla_st, 0), stack(ssd_st, 0), stack(conv_st, 0),
            stack(gla_st, 1), stack(ssd_st, 1), stack(conv_st, 1))
```
